```python
import jax, jax.numpy as jnp
from jax import lax
import numpy as np

D_MODEL = 2048
BATCH = 16
SEQ = 256
DEPTH = 1
DEC_BATCH = 4
DEC_SEQ = 1024
PAST_LEN = 512

GRID_W = 64
N_HEADS_A = 8
HEAD_DIM = 128
D_ATTN = N_HEADS_A * HEAD_DIM
KH_MAX = 8
KW = 16
CHUNK = 128
N_GROUPS_B = 8
D_GMLP = 1024
GROUP_CH = D_GMLP // N_GROUPS_B
D_FF = ((8 * D_MODEL // 3 + 255) // 256) * 256
N_MOD = 6
EPS = 1e-6
ATTN_SCALE = HEAD_DIM ** -0.5
SPLITS = (D_ATTN, 2 * D_ATTN, 3 * D_ATTN, 3 * D_ATTN + D_GMLP,
          3 * D_ATTN + 2 * D_GMLP, 3 * D_ATTN + 2 * D_GMLP + D_MODEL)
D_IN = 3 * D_ATTN + 2 * D_GMLP + 2 * D_MODEL

kernel_name = "hybrid_natten_gmlp_prefix_diffusion_step"


def rms_norm(x, g):
    xf = x.astype(jnp.float32)
    y = xf * lax.rsqrt(jnp.mean(xf * xf, axis=-1, keepdims=True) + EPS)
    return (y * g.astype(jnp.float32)).astype(x.dtype)


def layer_norm(x, g):
    xf = x.astype(jnp.float32)
    mu = jnp.mean(xf, axis=-1, keepdims=True)
    xc = xf - mu
    y = xc * lax.rsqrt(jnp.mean(xc * xc, axis=-1, keepdims=True) + EPS)
    return (y * g.astype(jnp.float32)).astype(x.dtype)


def modulation(cvec, w_ada, b_ada):
    m = (jax.nn.silu(cvec) @ w_ada + b_ada)[..., None, :]
    return jnp.split(m, N_MOD, axis=-1)


def window_start(i, k, n):
    return jnp.clip(i - k // 2, 0, n - k)


def context_attention(q, k, v):
    B, L, H, Dh = q.shape
    qb = q.reshape(B, L // CHUNK, CHUNK, H, Dh).transpose(1, 0, 2, 3, 4)

    def block(qi):
        s = jnp.einsum('bqhd,bkhd->bhqk', qi, k).astype(jnp.float32) * ATTN_SCALE
        p = jax.nn.softmax(s, axis=-1).astype(v.dtype)
        return jnp.einsum('bhqk,bkhd->bqhd', p, v)

    o = lax.map(block, qb)
    return o.transpose(1, 0, 2, 3, 4).reshape(B, L, H * Dh)


def neighbourhood_attention(q, k, v, k_ctx, v_ctx, rpb):
    B, N, H, Dh = q.shape
    rows = N // GRID_W
    kh = min(KH_MAX, rows)
    qg = q.reshape(B, rows, GRID_W, H, Dh)
    kg = k.reshape(B, rows, GRID_W, H, Dh)
    vg = v.reshape(B, rows, GRID_W, H, Dh)
    cols = jnp.arange(GRID_W)
    cs = window_start(cols, KW, GRID_W)
    col_mask = (cols[None, :] >= cs[:, None]) & (cols[None, :] < cs[:, None] + KW)
    dc_idx = jnp.clip(cols[None, :] - cols[:, None] + (KW - 1), 0, 2 * KW - 2)
    rpb_cols = rpb[:, :, dc_idx]

    def row_block(r):
        rs = window_start(r, kh, rows)
        q_r = lax.dynamic_index_in_dim(qg, r, axis=1, keepdims=False)
        k_w = lax.dynamic_slice_in_dim(kg, rs, kh, axis=1)
        v_w = lax.dynamic_slice_in_dim(vg, rs, kh, axis=1)
        dr_idx = rs + jnp.arange(kh) - r + (KH_MAX - 1)
        bias = jnp.take(rpb_cols, dr_idx, axis=1).transpose(0, 2, 1, 3)
        s_w = jnp.einsum('bqhd,bjkhd->bhqjk', q_r, k_w).astype(jnp.float32) * ATTN_SCALE
        s_w = s_w + bias[None].astype(jnp.float32)
        s_w = jnp.where(col_mask[None, None, :, None, :], s_w, -jnp.inf)
        s_w = s_w.reshape(B, H, GRID_W, kh * GRID_W)
        s_c = jnp.einsum('bqhd,bchd->bhqc', q_r, k_ctx).astype(jnp.float32) * ATTN_SCALE
        p = jax.nn.softmax(jnp.concatenate([s_w, s_c], axis=-1), axis=-1).astype(v.dtype)
        p_w = p[..., :kh * GRID_W].reshape(B, H, GRID_W, kh, GRID_W)
        p_c = p[..., kh * GRID_W:]
        return (jnp.einsum('bhqjk,bjkhd->bqhd', p_w, v_w)
                + jnp.einsum('bhqc,bchd->bqhd', p_c, v_ctx))

    o = lax.map(row_block, jnp.arange(rows))
    return o.transpose(1, 0, 2, 3, 4).reshape(B, N, H * Dh)


def spatial_gating(u, vb, ln_g, w_s, b_s):
    B, N, _ = u.shape
    vn = layer_norm(vb, ln_g).reshape(B, N // CHUNK, CHUNK, N_GROUPS_B, GROUP_CH)
    s = jnp.einsum('gpq,bnqgc->bnpgc', w_s, vn) + b_s.T[None, None, :, :, None]
    return u * s.reshape(B, N, D_GMLP)


def in_projection(h, P):
    B, N, _ = h.shape
    q, k, v, u, vb, ga, gb = jnp.split(h @ P['w_in'], SPLITS, axis=-1)
    shp = (B, N, N_HEADS_A, HEAD_DIM)
    return q.reshape(shp), k.reshape(shp), v.reshape(shp), u, vb, ga, gb


def merge_branches(o_a, u, vb, ga, gb, P):
    o_b = spatial_gating(jax.nn.gelu(u), jax.nn.gelu(vb), P['ln_v'], P['w_s'], P['b_s'])
    m = jax.nn.sigmoid(ga) * (o_a @ P['w_pa']) + jax.nn.sigmoid(gb) * (o_b @ P['w_pb'])
    return m @ P['w_o']


def ffn_residual(x, shift, scale, gate, P):
    h = rms_norm(x, P['n_ffn_pre']) * (1 + scale) + shift
    y = (jax.nn.silu(h @ P['w_gate']) * (h @ P['w_up'])) @ P['w_down']
    return x + gate * rms_norm(y, P['n_ffn_post'])


def context_layer(x, mods, P):
    sh1, sc1, g1, sh2, sc2, g2 = mods
    h = rms_norm(x, P['n_mix_pre']) * (1 + sc1) + sh1
    q, k, v, u, vb, ga, gb = in_projection(h, P)
    y = merge_branches(context_attention(q, k, v), u, vb, ga, gb, P)
    x = x + g1 * rms_norm(y, P['n_mix_post'])
    return ffn_residual(x, sh2, sc2, g2, P), k, v


def latent_layer(x, mods, k_ctx, v_ctx, P):
    sh1, sc1, g1, sh2, sc2, g2 = mods
    h = rms_norm(x, P['n_mix_pre']) * (1 + sc1) + sh1
    q, k, v, u, vb, ga, gb = in_projection(h, P)
    o_a = neighbourhood_attention(q, k, v, k_ctx, v_ctx, P['rpb'])
    y = merge_branches(o_a, u, vb, ga, gb, P)
    x = x + g1 * rms_norm(y, P['n_mix_post'])
    return ffn_residual(x, sh2, sc2, g2, P)


def setup_inputs(seed: int = 0) -> dict:
    key = jax.random.key(seed)
    ks = jax.random.split(key, 24)
    nrm = jax.random.normal
    f32 = jnp.float32
    return {
        "x_prompt": nrm(ks[0], (BATCH, SEQ, D_MODEL), f32),
        "x_sample": nrm(ks[1], (DEC_BATCH, DEC_SEQ, D_MODEL), f32),
        "cache_k": nrm(ks[2], (DEC_BATCH, DEPTH, PAST_LEN, N_HEADS_A, HEAD_DIM), f32),
        "cache_v": nrm(ks[3], (DEC_BATCH, DEPTH, PAST_LEN, N_HEADS_A, HEAD_DIM), f32),
        "c": nrm(ks[4], (DEC_BATCH, D_MODEL), f32),
        "c_ctx": nrm(ks[5], (D_MODEL,), f32),
        "w_ada": nrm(ks[6], (DEPTH, D_MODEL, N_MOD * D_MODEL), f32) * (0.5 * D_MODEL ** -0.5),
        "b_ada": nrm(ks[7], (DEPTH, N_MOD * D_MODEL), f32) * 0.02,
        "norm_mix_pre": 1.0 + 0.02 * nrm(ks[8], (DEPTH, D_MODEL), f32),
        "norm_mix_post": 1.0 + 0.02 * nrm(ks[9], (DEPTH, D_MODEL), f32),
        "norm_ffn_pre": 1.0 + 0.02 * nrm(ks[10], (DEPTH, D_MODEL), f32),
        "norm_ffn_post": 1.0 + 0.02 * nrm(ks[11], (DEPTH, D_MODEL), f32),
        "w_in": nrm(ks[12], (DEPTH, D_MODEL, D_IN), f32) * D_MODEL ** -0.5,
        "rpb": nrm(ks[13], (DEPTH, N_HEADS_A, 2 * KH_MAX - 1, 2 * KW - 1), f32) * 0.2,
        "ln_v": 1.0 + 0.02 * nrm(ks[14], (DEPTH, D_GMLP), f32),
        "w_s": nrm(ks[15], (DEPTH, N_GROUPS_B, CHUNK, CHUNK), f32) * CHUNK ** -0.5,
        "b_s": 1.0 + 0.02 * nrm(ks[16], (DEPTH, N_GROUPS_B, CHUNK), f32),
        "w_pa": nrm(ks[17], (DEPTH, D_ATTN, D_MODEL), f32) * D_ATTN ** -0.5,
        "w_pb": nrm(ks[18], (DEPTH, D_GMLP, D_MODEL), f32) * D_GMLP ** -0.5,
        "w_o": nrm(ks[19], (DEPTH, D_MODEL, D_MODEL), f32) * D_MODEL ** -0.5,
        "w_gate": nrm(ks[20], (DEPTH, D_MODEL, D_FF), f32) * D_MODEL ** -0.5,
        "w_up": nrm(ks[21], (DEPTH, D_MODEL, D_FF), f32) * D_MODEL ** -0.5,
        "w_down": nrm(ks[22], (DEPTH, D_FF, D_MODEL), f32) * D_FF ** -0.5,
    }


def reference(x_prompt, x_sample, cache_k, cache_v, c, c_ctx, w_ada, b_ada,
              norm_mix_pre, norm_mix_post, norm_ffn_pre, norm_ffn_post, w_in, rpb,
              ln_v, w_s, b_s, w_pa, w_pb, w_o, w_gate, w_up, w_down):
    y_prompt = x_prompt
    y_sample = x_sample
    new_k = []
    new_v = []
    for l in range(DEPTH):
        P = {
            'n_mix_pre': norm_mix_pre[l], 'n_mix_post': norm_mix_post[l],
            'n_ffn_pre': norm_ffn_pre[l], 'n_ffn_post': norm_ffn_post[l],
            'w_in': w_in[l], 'rpb': rpb[l], 'ln_v': ln_v[l], 'w_s': w_s[l], 'b_s': b_s[l],
            'w_pa': w_pa[l], 'w_pb': w_pb[l], 'w_o': w_o[l],
            'w_gate': w_gate[l], 'w_up': w_up[l], 'w_down': w_down[l],
        }
        mods_ctx = modulation(c_ctx, w_ada[l], b_ada[l])
        mods_lat = modulation(c, w_ada[l], b_ada[l])
        y_prompt, k_l, v_l = context_layer(y_prompt, mods_ctx, P)
        new_k.append(k_l)
        new_v.append(v_l)
        y_sample = latent_layer(y_sample, mods_lat, cache_k[:, l], cache_v[:, l], P)
    state_k = jnp.stack(new_k, axis=1)
    state_v = jnp.stack(new_v, axis=1)
    return (y_prompt, y_sample, state_k, state_v)
```

```python
import functools

import jax
import jax.numpy as jnp
from jax import lax
from jax.experimental import pallas as pl
from jax.experimental.pallas import tpu as pltpu

D_MODEL = 2048
DEPTH = 1
GRID_W = 64
N_HEADS_A = 8
HEAD_DIM = 128
D_ATTN = N_HEADS_A * HEAD_DIM
KH_MAX = 8
KW = 16
CHUNK = 128
N_GROUPS_B = 8
D_GMLP = 1024
GROUP_CH = D_GMLP // N_GROUPS_B
D_FF = ((8 * D_MODEL // 3 + 255) // 256) * 256
N_MOD = 6
EPS = 1e-6
ATTN_SCALE = HEAD_DIM ** -0.5
D_IN = 3 * D_ATTN + 2 * D_GMLP + 2 * D_MODEL

N_DR = 2 * KH_MAX - 1
N_DC = 2 * KW - 1
COL_BLOCK = 1024
MOD_ROWS = 8

F32 = jnp.float32
BF16 = jnp.bfloat16

MIB = 1024 * 1024


def _params(semantics, vmem_mib):
    return pltpu.CompilerParams(dimension_semantics=semantics, vmem_limit_bytes=vmem_mib * MIB)


def _rms_norm(x, g):
    return x * lax.rsqrt(jnp.mean(x * x, axis=-1, keepdims=True) + EPS) * g


def _layer_norm(x, g):
    xc = x - jnp.mean(x, axis=-1, keepdims=True)
    return xc * lax.rsqrt(jnp.mean(xc * xc, axis=-1, keepdims=True) + EPS) * g


def _modulation_kernel(c_ref, w_ref, b_ref, o_ref):
    s = jax.nn.silu(c_ref[...]).astype(BF16)
    o_ref[...] = jnp.dot(s, w_ref[...].astype(BF16), preferred_element_type=F32) + b_ref[...]


def _modulation(cvecs, w_ada, b_ada):
    tn = 1024
    n = N_MOD * D_MODEL
    return pl.pallas_call(
        _modulation_kernel,
        grid=(n // tn,),
        in_specs=[
            pl.BlockSpec((MOD_ROWS, D_MODEL), lambda j: (0, 0)),
            pl.BlockSpec((D_MODEL, tn), lambda j: (0, j)),
            pl.BlockSpec((1, tn), lambda j: (0, j)),
        ],
        out_specs=pl.BlockSpec((MOD_ROWS, tn), lambda j: (0, j)),
        out_shape=jax.ShapeDtypeStruct((MOD_ROWS, n), F32),
        compiler_params=_params(("parallel",), 40),
        name="modulation",
    )(cvecs, w_ada, b_ada.reshape(1, n))


def _bias_table_kernel(rpb_ref, o_ref):
    h = pl.program_id(0)
    qc = lax.broadcasted_iota(jnp.int32, (GRID_W, GRID_W), 0)
    kc = lax.broadcasted_iota(jnp.int32, (GRID_W, GRID_W), 1)
    dc = jnp.clip(kc - qc + (KW - 1), 0, N_DC - 1)
    cs = jnp.clip(qc - KW // 2, 0, GRID_W - KW)
    valid = (kc >= cs) & (kc < cs + KW)
    tiles = []
    for dr in range(N_DR):
        t = jnp.zeros((GRID_W, GRID_W), F32)
        for j in range(N_DC):
            t = jnp.where(dc == j, rpb_ref[(h * N_DR + dr) * N_DC + j], t)
        tiles.append(jnp.where(valid, t, -jnp.inf))
    pad = jnp.zeros((GRID_W, GRID_W), F32)
    o_ref[0, 0] = jnp.concatenate(tiles + [pad], axis=-1)
    o_ref[0, 1] = jnp.concatenate([pad] + tiles, axis=-1)


def _bias_table(rpb):
    width = (N_DR + 1) * GRID_W
    return pl.pallas_call(
        _bias_table_kernel,
        grid_spec=pltpu.PrefetchScalarGridSpec(
            num_scalar_prefetch=1,
            grid=(N_HEADS_A,),
            in_specs=[],
            out_specs=pl.BlockSpec((1, 2, GRID_W, width), lambda h, rpb_ref: (h, 0, 0, 0)),
        ),
        out_shape=jax.ShapeDtypeStruct((N_HEADS_A, 2, GRID_W, width), F32),
        compiler_params=_params(("parallel",), 16),
        name="bias_table",
    )(rpb.reshape(-1))


def _in_proj_kernel(x_ref, mod_ref, g_ref, lnv_ref, w_ref, proj_ref, *rest, emit_kv):
    if emit_kv:
        k_ref, v_ref, h_ref = rest
    else:
        (h_ref,) = rest
    j = pl.program_id(1)

    @pl.when(j == 0)
    def _():
        y = _rms_norm(x_ref[...], g_ref[...])
        h_ref[...] = (y * (1.0 + mod_ref[0, 1:2, :]) + mod_ref[0, 0:1, :]).astype(BF16)

    acc = jnp.dot(h_ref[...], w_ref[...], preferred_element_type=F32)

    @pl.when(j < 3)
    def _():
        proj_ref[...] = acc.astype(BF16)

    if emit_kv:
        @pl.when(j == 1)
        def _():
            k_ref[...] = acc

        @pl.when(j == 2)
        def _():
            v_ref[...] = acc

    @pl.when(j == 3)
    def _():
        proj_ref[...] = jax.nn.gelu(acc).astype(BF16)

    @pl.when(j == 4)
    def _():
        proj_ref[...] = _layer_norm(jax.nn.gelu(acc), lnv_ref[...]).astype(BF16)

    @pl.when(j >= 5)
    def _():
        proj_ref[...] = jax.nn.sigmoid(acc).astype(BF16)


def _in_proj(x, mods, seq, g, ln_v, w_in, *, emit_kv, tm=512):
    m = x.shape[0]
    n_col = D_IN // COL_BLOCK
    out_shape = [jax.ShapeDtypeStruct((m, D_IN), BF16)]
    out_specs = [pl.BlockSpec((tm, COL_BLOCK), lambda i, j: (i, j))]
    if emit_kv:
        out_shape += [jax.ShapeDtypeStruct((m, D_ATTN), F32)] * 2
        out_specs += [pl.BlockSpec((tm, D_ATTN), lambda i, j: (i, 0))] * 2
    return pl.pallas_call(
        functools.partial(_in_proj_kernel, emit_kv=emit_kv),
        grid=(m // tm, n_col),
        in_specs=[
            pl.BlockSpec((tm, D_MODEL), lambda i, j: (i, 0)),
            pl.BlockSpec((1, N_MOD, D_MODEL), lambda i, j: (i * tm // seq, 0, 0)),
            pl.BlockSpec((1, D_MODEL), lambda i, j: (0, 0)),
            pl.BlockSpec((1, D_GMLP), lambda i, j: (0, 0)),
            pl.BlockSpec((D_MODEL, COL_BLOCK), lambda i, j: (0, j)),
        ],
        out_specs=out_specs,
        out_shape=out_shape,
        scratch_shapes=[pltpu.VMEM((tm, D_MODEL), BF16)],
        compiler_params=_params(("parallel", "arbitrary"), 48),
        name="in_proj_ctx" if emit_kv else "in_proj_lat",
    )(x, mods, g, ln_v, w_in)


def _softmax_parts(scores):
    m = functools.reduce(jnp.maximum, [jnp.max(s, axis=-1, keepdims=True) for s in scores])
    es = [jnp.exp(s - m) for s in scores]
    inv = 1.0 / functools.reduce(jnp.add, [jnp.sum(e, axis=-1, keepdims=True) for e in es])
    return [(e * inv).astype(BF16) for e in es]


def _qk(q, k):
    return lax.dot_general(q, k, (((1,), (1,)), ((), ())), preferred_element_type=F32) * ATTN_SCALE


def _ctx_attn_kernel(q_ref, k_ref, v_ref, o_ref):
    for h in range(N_HEADS_A):
        cols = slice(h * HEAD_DIM, (h + 1) * HEAD_DIM)
        (p,) = _softmax_parts([_qk(q_ref[:, cols], k_ref[:, cols])])
        o_ref[:, cols] = jnp.dot(p, v_ref[:, cols], preferred_element_type=F32).astype(BF16)


def _ctx_attention(proj, seq):
    m = proj.shape[0]
    spec = lambda col: pl.BlockSpec((seq, D_ATTN), lambda b: (b, col))
    return pl.pallas_call(
        _ctx_attn_kernel,
        grid=(m // seq,),
        in_specs=[spec(0), spec(1), spec(2)],
        out_specs=spec(0),
        out_shape=jax.ShapeDtypeStruct((m, D_ATTN), BF16),
        compiler_params=_params(("parallel",), 32),
        name="ctx_attention",
    )(proj, proj, proj)


def _lat_attn_kernel(q_ref, k_ref, v_ref, kc_ref, vc_ref, tab_ref, o_ref, *, rows, kh):
    kc = kc_ref[0].astype(BF16)
    vc = vc_ref[0].astype(BF16)
    win = kh * GRID_W
    for r in range(rows):
        rs = min(max(r - kh // 2, 0), rows - kh)
        off = rs - r + (KH_MAX - 1)
        lane0 = (off + off % 2) * GRID_W
        bias = tab_ref[0, off % 2, :, lane0:lane0 + win]
        q = q_ref[r * GRID_W:(r + 1) * GRID_W, :]
        s_w = _qk(q, k_ref[rs * GRID_W:rs * GRID_W + win, :]) + bias
        s_c = _qk(q, kc)
        p_w, p_c = _softmax_parts([s_w, s_c])
        o = (jnp.dot(p_w, v_ref[rs * GRID_W:rs * GRID_W + win, :], preferred_element_type=F32)
             + jnp.dot(p_c, vc, preferred_element_type=F32))
        o_ref[r * GRID_W:(r + 1) * GRID_W, :] = o.astype(BF16)


def _lat_attention(proj, cache_k, cache_v, table, seq):
    m = proj.shape[0]
    past = cache_k.shape[1]
    rows = seq // GRID_W
    kh = min(KH_MAX, rows)
    qkv = lambda part: pl.BlockSpec((seq, HEAD_DIM), lambda b, h: (b, part * N_HEADS_A + h))
    cache = pl.BlockSpec((1, past, HEAD_DIM), lambda b, h: (b, 0, h))
    return pl.pallas_call(
        functools.partial(_lat_attn_kernel, rows=rows, kh=kh),
        grid=(m // seq, N_HEADS_A),
        in_specs=[qkv(0), qkv(1), qkv(2), cache, cache,
                  pl.BlockSpec((1,) + table.shape[1:], lambda b, h: (h, 0, 0, 0))],
        out_specs=pl.BlockSpec((seq, HEAD_DIM), lambda b, h: (b, h)),
        out_shape=jax.ShapeDtypeStruct((m, D_ATTN), BF16),
        compiler_params=_params(("parallel", "parallel"), 32),
        name="lat_attention",
    )(proj, proj, proj, cache_k, cache_v, table)


def _merge_kernel(x_ref, mod_ref, oa_ref, gu_ref, vn_ref, ga0_ref, ga1_ref, gb0_ref, gb1_ref,
                  ws_ref, bst_ref, wpa_ref, wpb_ref, wo_ref, g_ref, o_ref, ob_ref, m_ref, *, tm):
    for c in range(tm // CHUNK):
        rows = slice(c * CHUNK, (c + 1) * CHUNK)
        for g in range(N_GROUPS_B):
            cols = slice(g * GROUP_CH, (g + 1) * GROUP_CH)
            s = jnp.dot(ws_ref[g], vn_ref[rows, cols], preferred_element_type=F32) + bst_ref[:, g:g + 1]
            ob_ref[rows, cols] = (gu_ref[rows, cols].astype(F32) * s).astype(BF16)
    pa = jnp.dot(oa_ref[...], wpa_ref[...], preferred_element_type=F32)
    pb = jnp.dot(ob_ref[...], wpb_ref[...], preferred_element_type=F32)
    half = D_MODEL // 2
    m_ref[:, :half] = (ga0_ref[...].astype(F32) * pa[:, :half]
                       + gb0_ref[...].astype(F32) * pb[:, :half]).astype(BF16)
    m_ref[:, half:] = (ga1_ref[...].astype(F32) * pa[:, half:]
                       + gb1_ref[...].astype(F32) * pb[:, half:]).astype(BF16)
    y = jnp.dot(m_ref[...], wo_ref[...], preferred_element_type=F32)
    o_ref[...] = x_ref[...] + mod_ref[0, 2:3, :] * _rms_norm(y, g_ref[...])


def _merge(x, mods, seq, o_a, proj, w_s, b_s_t, w_pa, w_pb, w_o, g, *, tm=512):
    m = x.shape[0]
    col = lambda blk: pl.BlockSpec((tm, COL_BLOCK), lambda i: (i, blk))
    whole = lambda a: pl.BlockSpec(a.shape, lambda i: (0,) * a.ndim, pipeline_mode=pl.Buffered(1))
    return pl.pallas_call(
        functools.partial(_merge_kernel, tm=tm),
        grid=(m // tm,),
        in_specs=[
            pl.BlockSpec((tm, D_MODEL), lambda i: (i, 0)),
            pl.BlockSpec((1, N_MOD, D_MODEL), lambda i: (i * tm // seq, 0, 0)),
            pl.BlockSpec((tm, D_ATTN), lambda i: (i, 0)),
            col(3), col(4), col(5), col(6), col(7), col(8),
            whole(w_s), whole(b_s_t), whole(w_pa), whole(w_pb), whole(w_o), whole(g),
        ],
        out_specs=pl.BlockSpec((tm, D_MODEL), lambda i: (i, 0)),
        out_shape=jax.ShapeDtypeStruct((m, D_MODEL), F32),
        scratch_shapes=[pltpu.VMEM((tm, D_GMLP), BF16), pltpu.VMEM((tm, D_MODEL), BF16)],
        compiler_params=_params(("parallel",), 56),
        name="merge",
    )(x, mods, o_a, proj, proj, proj, proj, proj, proj, w_s, b_s_t, w_pa, w_pb, w_o, g)


def _ffn_kernel(x_ref, mod_ref, gpre_ref, gpost_ref, wg_ref, wu_ref, wd_ref, o_ref, h_ref, acc_ref):
    f = pl.program_id(1)

    @pl.when(f == 0)
    def _():
        y = _rms_norm(x_ref[...], gpre_ref[...])
        h_ref[...] = (y * (1.0 + mod_ref[0, 4:5, :]) + mod_ref[0, 3:4, :]).astype(BF16)
        acc_ref[...] = jnp.zeros_like(acc_ref)

    h = h_ref[...]
    gate = jnp.dot(h, wg_ref[...], preferred_element_type=F32)
    up = jnp.dot(h, wu_ref[...], preferred_element_type=F32)
    act = (jax.nn.silu(gate) * up).astype(BF16)
    acc_ref[...] += jnp.dot(act, wd_ref[...], preferred_element_type=F32)

    @pl.when(f == pl.num_programs(1) - 1)
    def _():
        o_ref[...] = x_ref[...] + mod_ref[0, 5:6, :] * _rms_norm(acc_ref[...], gpost_ref[...])


def _ffn(x, mods, seq, g_pre, g_post, w_gate, w_up, w_down, *, tm=512, tf=512):
    m = x.shape[0]
    return pl.pallas_call(
        _ffn_kernel,
        grid=(m // tm, D_FF // tf),
        in_specs=[
            pl.BlockSpec((tm, D_MODEL), lambda i, f: (i, 0)),
            pl.BlockSpec((1, N_MOD, D_MODEL), lambda i, f: (i * tm // seq, 0, 0)),
            pl.BlockSpec((1, D_MODEL), lambda i, f: (0, 0)),
            pl.BlockSpec((1, D_MODEL), lambda i, f: (0, 0)),
            pl.BlockSpec((D_MODEL, tf), lambda i, f: (0, f)),
            pl.BlockSpec((D_MODEL, tf), lambda i, f: (0, f)),
            pl.BlockSpec((tf, D_MODEL), lambda i, f: (f, 0)),
        ],
        out_specs=pl.BlockSpec((tm, D_MODEL), lambda i, f: (i, 0)),
        out_shape=jax.ShapeDtypeStruct((m, D_MODEL), F32),
        scratch_shapes=[pltpu.VMEM((tm, D_MODEL), BF16), pltpu.VMEM((tm, D_MODEL), F32)],
        compiler_params=_params(("parallel", "arbitrary"), 48),
        name="ffn",
    )(x, mods, g_pre, g_post, w_gate, w_up, w_down)


def kernel(x_prompt, x_sample, cache_k, cache_v, c, c_ctx, w_ada, b_ada, norm_mix_pre, norm_mix_post,
           norm_ffn_pre, norm_ffn_post, w_in, rpb, ln_v, w_s, b_s, w_pa, w_pb, w_o, w_gate, w_up, w_down):
    assert w_ada.shape[0] == DEPTH == 1
    batch, seq, _ = x_prompt.shape
    dec_batch, dec_seq, _ = x_sample.shape
    past = cache_k.shape[2]

    row = lambda a: a[0].reshape(1, -1)
    bf = lambda a: a[0].astype(BF16)

    cvecs = jnp.concatenate(
        [c, c_ctx[None], jnp.zeros((MOD_ROWS - dec_batch - 1, D_MODEL), F32)], axis=0)
    mods = _modulation(cvecs, w_ada[0], b_ada[0])
    mods_lat = mods[:dec_batch].reshape(dec_batch, N_MOD, D_MODEL)
    mods_ctx = mods[dec_batch:dec_batch + 1].reshape(1, N_MOD, D_MODEL)
    table = _bias_table(rpb[0])

    w_in_b = bf(w_in)
    mix = (bf(w_s), b_s[0].T, bf(w_pa), bf(w_pb), bf(w_o), row(norm_mix_post))
    ffn = (row(norm_ffn_pre), row(norm_ffn_post), bf(w_gate), bf(w_up), bf(w_down))

    xp = x_prompt.reshape(batch * seq, D_MODEL)
    proj_p, k_p, v_p = _in_proj(xp, mods_ctx, batch * seq, row(norm_mix_pre), row(ln_v), w_in_b,
                                emit_kv=True)
    oa_p = _ctx_attention(proj_p, seq)
    xp = _merge(xp, mods_ctx, batch * seq, oa_p, proj_p, *mix)
    y_prompt = _ffn(xp, mods_ctx, batch * seq, *ffn).reshape(batch, seq, D_MODEL)

    xs = x_sample.reshape(dec_batch * dec_seq, D_MODEL)
    (proj_s,) = _in_proj(xs, mods_lat, dec_seq, row(norm_mix_pre), row(ln_v), w_in_b, emit_kv=False)
    oa_s = _lat_attention(proj_s, cache_k.reshape(dec_batch, past, D_ATTN),
                          cache_v.reshape(dec_batch, past, D_ATTN), table, dec_seq)
    xs = _merge(xs, mods_lat, dec_seq, oa_s, proj_s, *mix)
    y_sample = _ffn(xs, mods_lat, dec_seq, *ffn).reshape(dec_batch, dec_seq, D_MODEL)

    state_shape = (batch, DEPTH, seq, N_HEADS_A, HEAD_DIM)
    return y_prompt, y_sample, k_p.reshape(state_shape), v_p.reshape(state_shape)
```

```python
import functools

import jax
import jax.numpy as jnp
from jax import lax
from jax.experimental import pallas as pl
from jax.experimental.pallas import tpu as pltpu

D_MODEL = 2048
DEPTH = 1
GRID_W = 64
N_HEADS_A = 8
HEAD_DIM = 128
D_ATTN = N_HEADS_A * HEAD_DIM
KH_MAX = 8
KW = 16
CHUNK = 128
N_GROUPS_B = 8
D_GMLP = 1024
GROUP_CH = D_GMLP // N_GROUPS_B
D_FF = ((8 * D_MODEL // 3 + 255) // 256) * 256
N_MOD = 6
EPS = 1e-6
ATTN_SCALE = HEAD_DIM ** -0.5
D_IN = 3 * D_ATTN + 2 * D_GMLP + 2 * D_MODEL

N_DR = 2 * KH_MAX - 1
N_DC = 2 * KW - 1
COL_BLOCK = 1024
MOD_ROWS = 8

F32 = jnp.float32
BF16 = jnp.bfloat16

MIB = 1024 * 1024


def _params(semantics, vmem_mib):
    return pltpu.CompilerParams(dimension_semantics=semantics, vmem_limit_bytes=vmem_mib * MIB)


def _rms_norm(x, g):
    return x * lax.rsqrt(jnp.mean(x * x, axis=-1, keepdims=True) + EPS) * g


def _layer_norm(x, g):
    xc = x - jnp.mean(x, axis=-1, keepdims=True)
    return xc * lax.rsqrt(jnp.mean(xc * xc, axis=-1, keepdims=True) + EPS) * g


def _modulation_kernel(c_ref, w_ref, b_ref, o_ref):
    s = jax.nn.silu(c_ref[...]).astype(BF16)
    o_ref[...] = jnp.dot(s, w_ref[...].astype(BF16), preferred_element_type=F32) + b_ref[...]


def _modulation(cvecs, w_ada, b_ada):
    tn = 1024
    n = N_MOD * D_MODEL
    return pl.pallas_call(
        _modulation_kernel,
        grid=(n // tn,),
        in_specs=[
            pl.BlockSpec((MOD_ROWS, D_MODEL), lambda j: (0, 0)),
            pl.BlockSpec((D_MODEL, tn), lambda j: (0, j)),
            pl.BlockSpec((1, tn), lambda j: (0, j)),
        ],
        out_specs=pl.BlockSpec((MOD_ROWS, tn), lambda j: (0, j)),
        out_shape=jax.ShapeDtypeStruct((MOD_ROWS, n), F32),
        compiler_params=_params(("parallel",), 40),
        name="modulation",
    )(cvecs, w_ada, b_ada.reshape(1, n))


def _bias_table_kernel(rpb_ref, o_ref):
    h = pl.program_id(0)
    qc = lax.broadcasted_iota(jnp.int32, (GRID_W, GRID_W), 0)
    kc = lax.broadcasted_iota(jnp.int32, (GRID_W, GRID_W), 1)
    dc = jnp.clip(kc - qc + (KW - 1), 0, N_DC - 1)
    cs = jnp.clip(qc - KW // 2, 0, GRID_W - KW)
    valid = (kc >= cs) & (kc < cs + KW)
    tiles = []
    for dr in range(N_DR):
        t = jnp.zeros((GRID_W, GRID_W), F32)
        for j in range(N_DC):
            t = jnp.where(dc == j, rpb_ref[(h * N_DR + dr) * N_DC + j], t)
        tiles.append(jnp.where(valid, t, -jnp.inf))
    pad = jnp.zeros((GRID_W, GRID_W), F32)
    o_ref[0, 0] = jnp.concatenate(tiles + [pad], axis=-1)
    o_ref[0, 1] = jnp.concatenate([pad] + tiles, axis=-1)


def _bias_table(rpb):
    width = (N_DR + 1) * GRID_W
    return pl.pallas_call(
        _bias_table_kernel,
        grid_spec=pltpu.PrefetchScalarGridSpec(
            num_scalar_prefetch=1,
            grid=(N_HEADS_A,),
            in_specs=[],
            out_specs=pl.BlockSpec((1, 2, GRID_W, width), lambda h, rpb_ref: (h, 0, 0, 0)),
        ),
        out_shape=jax.ShapeDtypeStruct((N_HEADS_A, 2, GRID_W, width), F32),
        compiler_params=_params(("parallel",), 16),
        name="bias_table",
    )(rpb.reshape(-1))


def _in_proj_kernel(x_ref, mod_ref, g_ref, lnv_ref, w_ref, proj_ref, *rest, emit_kv):
    if emit_kv:
        k_ref, v_ref, h_ref = rest
    else:
        (h_ref,) = rest
    j = pl.program_id(1)

    @pl.when(j == 0)
    def _():
        y = _rms_norm(x_ref[...], g_ref[...])
        h_ref[...] = (y * (1.0 + mod_ref[0, 1:2, :]) + mod_ref[0, 0:1, :]).astype(BF16)

    acc = jnp.dot(h_ref[...], w_ref[...], preferred_element_type=F32)

    @pl.when(j < 3)
    def _():
        proj_ref[...] = acc.astype(BF16)

    if emit_kv:
        @pl.when(j == 1)
        def _():
            k_ref[...] = acc

        @pl.when(j == 2)
        def _():
            v_ref[...] = acc

    @pl.when(j == 3)
    def _():
        proj_ref[...] = jax.nn.gelu(acc).astype(BF16)

    @pl.when(j == 4)
    def _():
        proj_ref[...] = _layer_norm(jax.nn.gelu(acc), lnv_ref[...]).astype(BF16)

    @pl.when(j >= 5)
    def _():
        proj_ref[...] = jax.nn.sigmoid(acc).astype(BF16)


def _in_proj(x, mods, seq, g, ln_v, w_in, *, emit_kv, tm=512):
    m = x.shape[0]
    n_col = D_IN // COL_BLOCK
    out_shape = [jax.ShapeDtypeStruct((m, D_IN), BF16)]
    out_specs = [pl.BlockSpec((tm, COL_BLOCK), lambda i, j: (i, j))]
    if emit_kv:
        out_shape += [jax.ShapeDtypeStruct((m, D_ATTN), F32)] * 2
        out_specs += [pl.BlockSpec((tm, D_ATTN), lambda i, j: (i, 0))] * 2
    return pl.pallas_call(
        functools.partial(_in_proj_kernel, emit_kv=emit_kv),
        grid=(m // tm, n_col),
        in_specs=[
            pl.BlockSpec((tm, D_MODEL), lambda i, j: (i, 0)),
            pl.BlockSpec((1, N_MOD, D_MODEL), lambda i, j: (i * tm // seq, 0, 0)),
            pl.BlockSpec((1, D_MODEL), lambda i, j: (0, 0)),
            pl.BlockSpec((1, D_GMLP), lambda i, j: (0, 0)),
            pl.BlockSpec((D_MODEL, COL_BLOCK), lambda i, j: (0, j)),
        ],
        out_specs=out_specs,
        out_shape=out_shape,
        scratch_shapes=[pltpu.VMEM((tm, D_MODEL), BF16)],
        compiler_params=_params(("parallel", "arbitrary"), 48),
        name="in_proj_ctx" if emit_kv else "in_proj_lat",
    )(x, mods, g, ln_v, w_in)


def _softmax_parts(scores):
    m = functools.reduce(jnp.maximum, [jnp.max(s, axis=-1, keepdims=True) for s in scores])
    es = [jnp.exp(s - m) for s in scores]
    inv = 1.0 / functools.reduce(jnp.add, [jnp.sum(e, axis=-1, keepdims=True) for e in es])
    return [(e * inv).astype(BF16) for e in es]


def _qk(q, k):
    return lax.dot_general(q, k, (((1,), (1,)), ((), ())), preferred_element_type=F32) * ATTN_SCALE


def _ctx_attn_kernel(q_ref, k_ref, v_ref, o_ref):
    for h in range(N_HEADS_A):
        cols = slice(h * HEAD_DIM, (h + 1) * HEAD_DIM)
        (p,) = _softmax_parts([_qk(q_ref[:, cols], k_ref[:, cols])])
        o_ref[:, cols] = jnp.dot(p, v_ref[:, cols], preferred_element_type=F32).astype(BF16)


def _ctx_attention(proj, seq):
    m = proj.shape[0]
    spec = lambda col: pl.BlockSpec((seq, D_ATTN), lambda b: (b, col))
    return pl.pallas_call(
        _ctx_attn_kernel,
        grid=(m // seq,),
        in_specs=[spec(0), spec(1), spec(2)],
        out_specs=spec(0),
        out_shape=jax.ShapeDtypeStruct((m, D_ATTN), BF16),
        compiler_params=_params(("parallel",), 32),
        name="ctx_attention",
    )(proj, proj, proj)


def _lat_attn_kernel(q_ref, k_ref, v_ref, kc_ref, vc_ref, tab_ref, o_ref, sw_ref, pw_ref, *, rows, kh):
    h = pl.program_id(1)
    kc = kc_ref[0, :, h, :].astype(BF16)
    vc = vc_ref[0, :, h, :].astype(BF16)
    win = kh * GRID_W
    s_c = _qk(q_ref[...], kc)
    starts = [min(max(r - kh // 2, 0), rows - kh) * GRID_W for r in range(rows)]
    for r, start in enumerate(starts):
        off = start // GRID_W - r + (KH_MAX - 1)
        lane0 = (off + off % 2) * GRID_W
        bias = tab_ref[0, off % 2, :, lane0:lane0 + win]
        q = q_ref[r * GRID_W:(r + 1) * GRID_W, :]
        sw_ref[r * GRID_W:(r + 1) * GRID_W, :] = _qk(q, k_ref[start:start + win, :]) + bias
    p_w, p_c = _softmax_parts([sw_ref[...], s_c])
    pw_ref[...] = p_w
    o_c = jnp.dot(p_c, vc, preferred_element_type=F32)
    for r, start in enumerate(starts):
        q_rows = slice(r * GRID_W, (r + 1) * GRID_W)
        o_w = jnp.dot(pw_ref[q_rows, :], v_ref[start:start + win, :], preferred_element_type=F32)
        o_ref[q_rows, :] = (o_w + o_c[q_rows, :]).astype(BF16)


def _lat_attention(proj, cache_k, cache_v, table, seq):
    m = proj.shape[0]
    past = cache_k.shape[1]
    rows = seq // GRID_W
    kh = min(KH_MAX, rows)
    qkv = lambda part: pl.BlockSpec((seq, HEAD_DIM), lambda b, h: (b, part * N_HEADS_A + h))
    cache = pl.BlockSpec((1, past, N_HEADS_A, HEAD_DIM), lambda b, h: (b, 0, 0, 0))
    return pl.pallas_call(
        functools.partial(_lat_attn_kernel, rows=rows, kh=kh),
        grid=(m // seq, N_HEADS_A),
        in_specs=[qkv(0), qkv(1), qkv(2), cache, cache,
                  pl.BlockSpec((1,) + table.shape[1:], lambda b, h: (h, 0, 0, 0))],
        out_specs=pl.BlockSpec((seq, HEAD_DIM), lambda b, h: (b, h)),
        out_shape=jax.ShapeDtypeStruct((m, D_ATTN), BF16),
        scratch_shapes=[pltpu.VMEM((seq, kh * GRID_W), F32), pltpu.VMEM((seq, kh * GRID_W), BF16)],
        compiler_params=_params(("parallel", "arbitrary"), 40),
        name="lat_attention",
    )(proj, proj, proj, cache_k, cache_v, table)


def _merge_kernel(x_ref, mod_ref, oa_ref, gu_ref, vn_ref, ga0_ref, ga1_ref, gb0_ref, gb1_ref,
                  ws_ref, bst_ref, wpa_ref, wpb_ref, wo_ref, g_ref, gffn_ref, o_ref, h2_ref,
                  ob_ref, m_ref, *, tm):
    for c in range(tm // CHUNK):
        rows = slice(c * CHUNK, (c + 1) * CHUNK)
        for g in range(N_GROUPS_B):
            cols = slice(g * GROUP_CH, (g + 1) * GROUP_CH)
            s = jnp.dot(ws_ref[g], vn_ref[rows, cols], preferred_element_type=F32) + bst_ref[:, g:g + 1]
            ob_ref[rows, cols] = (gu_ref[rows, cols].astype(F32) * s).astype(BF16)
    pa = jnp.dot(oa_ref[...], wpa_ref[...], preferred_element_type=F32)
    pb = jnp.dot(ob_ref[...], wpb_ref[...], preferred_element_type=F32)
    half = D_MODEL // 2
    m_ref[:, :half] = (ga0_ref[...].astype(F32) * pa[:, :half]
                       + gb0_ref[...].astype(F32) * pb[:, :half]).astype(BF16)
    m_ref[:, half:] = (ga1_ref[...].astype(F32) * pa[:, half:]
                       + gb1_ref[...].astype(F32) * pb[:, half:]).astype(BF16)
    y = jnp.dot(m_ref[...], wo_ref[...], preferred_element_type=F32)
    x1 = x_ref[...] + mod_ref[0, 2:3, :] * _rms_norm(y, g_ref[...])
    o_ref[...] = x1
    h2 = _rms_norm(x1, gffn_ref[...]) * (1.0 + mod_ref[0, 4:5, :]) + mod_ref[0, 3:4, :]
    h2_ref[...] = h2.astype(BF16)


def _merge(x, mods, seq, o_a, proj, w_s, b_s_t, w_pa, w_pb, w_o, g, g_ffn, *, tm=512):
    m = x.shape[0]
    col = lambda blk: pl.BlockSpec((tm, COL_BLOCK), lambda i: (i, blk))
    whole = lambda a: pl.BlockSpec(a.shape, lambda i: (0,) * a.ndim, pipeline_mode=pl.Buffered(1))
    return pl.pallas_call(
        functools.partial(_merge_kernel, tm=tm),
        grid=(m // tm,),
        in_specs=[
            pl.BlockSpec((tm, D_MODEL), lambda i: (i, 0)),
            pl.BlockSpec((1, N_MOD, D_MODEL), lambda i: (i * tm // seq, 0, 0)),
            pl.BlockSpec((tm, D_ATTN), lambda i: (i, 0)),
            col(3), col(4), col(5), col(6), col(7), col(8),
            whole(w_s), whole(b_s_t), whole(w_pa), whole(w_pb), whole(w_o), whole(g), whole(g_ffn),
        ],
        out_specs=[pl.BlockSpec((tm, D_MODEL), lambda i: (i, 0))] * 2,
        out_shape=[jax.ShapeDtypeStruct((m, D_MODEL), F32), jax.ShapeDtypeStruct((m, D_MODEL), BF16)],
        scratch_shapes=[pltpu.VMEM((tm, D_GMLP), BF16), pltpu.VMEM((tm, D_MODEL), BF16)],
        compiler_params=_params(("parallel",), 60),
        name="merge",
    )(x, mods, o_a, proj, proj, proj, proj, proj, proj, w_s, b_s_t, w_pa, w_pb, w_o, g, g_ffn)


def _ffn_kernel(x_ref, h_ref, mod_ref, gpost_ref, wg_ref, wu_ref, wd_ref, o_ref, acc_ref):
    f = pl.program_id(1)

    @pl.when(f == 0)
    def _():
        acc_ref[...] = jnp.zeros_like(acc_ref)

    h = h_ref[...]
    gate = jnp.dot(h, wg_ref[...], preferred_element_type=F32)
    up = jnp.dot(h, wu_ref[...], preferred_element_type=F32)
    act = (jax.nn.silu(gate) * up).astype(BF16)
    acc_ref[...] += jnp.dot(act, wd_ref[...], preferred_element_type=F32)

    @pl.when(f == pl.num_programs(1) - 1)
    def _():
        o_ref[...] = x_ref[...] + mod_ref[0, 5:6, :] * _rms_norm(acc_ref[...], gpost_ref[...])


def _ffn(x, h, mods, seq, g_post, w_gate, w_up, w_down, *, tm=512, tf=512):
    m = x.shape[0]
    return pl.pallas_call(
        _ffn_kernel,
        grid=(m // tm, D_FF // tf),
        in_specs=[
            pl.BlockSpec((tm, D_MODEL), lambda i, f: (i, 0)),
            pl.BlockSpec((tm, D_MODEL), lambda i, f: (i, 0)),
            pl.BlockSpec((1, N_MOD, D_MODEL), lambda i, f: (i * tm // seq, 0, 0)),
            pl.BlockSpec((1, D_MODEL), lambda i, f: (0, 0)),
            pl.BlockSpec((D_MODEL, tf), lambda i, f: (0, f)),
            pl.BlockSpec((D_MODEL, tf), lambda i, f: (0, f)),
            pl.BlockSpec((tf, D_MODEL), lambda i, f: (f, 0)),
        ],
        out_specs=pl.BlockSpec((tm, D_MODEL), lambda i, f: (i, 0)),
        out_shape=jax.ShapeDtypeStruct((m, D_MODEL), F32),
        scratch_shapes=[pltpu.VMEM((tm, D_MODEL), F32)],
        compiler_params=_params(("parallel", "arbitrary"), 48),
        name="ffn",
    )(x, h, mods, g_post, w_gate, w_up, w_down)


def kernel(x_prompt, x_sample, cache_k, cache_v, c, c_ctx, w_ada, b_ada, norm_mix_pre, norm_mix_post,
           norm_ffn_pre, norm_ffn_post, w_in, rpb, ln_v, w_s, b_s, w_pa, w_pb, w_o, w_gate, w_up, w_down):
    assert w_ada.shape[0] == DEPTH == 1
    batch, seq, _ = x_prompt.shape
    dec_batch, dec_seq, _ = x_sample.shape
    past = cache_k.shape[2]

    row = lambda a: a[0].reshape(1, -1)
    bf = lambda a: a[0].astype(BF16)

    cvecs = jnp.concatenate(
        [c, c_ctx[None], jnp.zeros((MOD_ROWS - dec_batch - 1, D_MODEL), F32)], axis=0)
    mods = _modulation(cvecs, w_ada[0], b_ada[0])
    mods_lat = mods[:dec_batch].reshape(dec_batch, N_MOD, D_MODEL)
    mods_ctx = mods[dec_batch:dec_batch + 1].reshape(1, N_MOD, D_MODEL)
    table = _bias_table(rpb[0])

    w_in_b = bf(w_in)
    mix = (bf(w_s), b_s[0].T, bf(w_pa), bf(w_pb), bf(w_o), row(norm_mix_post), row(norm_ffn_pre))
    ffn = (row(norm_ffn_post), bf(w_gate), bf(w_up), bf(w_down))

    xp = x_prompt.reshape(batch * seq, D_MODEL)
    proj_p, k_p, v_p = _in_proj(xp, mods_ctx, batch * seq, row(norm_mix_pre), row(ln_v), w_in_b,
                                emit_kv=True)
    oa_p = _ctx_attention(proj_p, seq)
    xp, hp = _merge(xp, mods_ctx, batch * seq, oa_p, proj_p, *mix)
    y_prompt = _ffn(xp, hp, mods_ctx, batch * seq, *ffn).reshape(batch, seq, D_MODEL)

    xs = x_sample.reshape(dec_batch * dec_seq, D_MODEL)
    (proj_s,) = _in_proj(xs, mods_lat, dec_seq, row(norm_mix_pre), row(ln_v), w_in_b, emit_kv=False)
    oa_s = _lat_attention(proj_s, cache_k.reshape(dec_batch, past, N_HEADS_A, HEAD_DIM),
                          cache_v.reshape(dec_batch, past, N_HEADS_A, HEAD_DIM), table, dec_seq)
    xs, hs = _merge(xs, mods_lat, dec_seq, oa_s, proj_s, *mix)
    y_sample = _ffn(xs, hs, mods_lat, dec_seq, *ffn).reshape(dec_batch, dec_seq, D_MODEL)

    state_shape = (batch, DEPTH, seq, N_HEADS_A, HEAD_DIM)
    return y_prompt, y_sample, k_p.reshape(state_shape), v_p.reshape(state_shape)
```

```python
import functools

import jax
import jax.numpy as jnp
from jax import lax
from jax.experimental import pallas as pl
from jax.experimental.pallas import tpu as pltpu

D_MODEL = 2048
DEPTH = 1
GRID_W = 64
N_HEADS_A = 8
HEAD_DIM = 128
D_ATTN = N_HEADS_A * HEAD_DIM
KH_MAX = 8
KW = 16
CHUNK = 128
N_GROUPS_B = 8
D_GMLP = 1024
GROUP_CH = D_GMLP // N_GROUPS_B
D_FF = ((8 * D_MODEL // 3 + 255) // 256) * 256
N_MOD = 6
EPS = 1e-6
ATTN_SCALE = HEAD_DIM ** -0.5
D_IN = 3 * D_ATTN + 2 * D_GMLP + 2 * D_MODEL

N_DR = 2 * KH_MAX - 1
N_DC = 2 * KW - 1
COL_BLOCK = 1024
MOD_ROWS = 8

F32 = jnp.float32
BF16 = jnp.bfloat16

MIB = 1024 * 1024


def _params(semantics, vmem_mib):
    return pltpu.CompilerParams(dimension_semantics=semantics, vmem_limit_bytes=vmem_mib * MIB)


def _rms_norm(x, g):
    return x * lax.rsqrt(jnp.mean(x * x, axis=-1, keepdims=True) + EPS) * g


def _layer_norm(x, g):
    xc = x - jnp.mean(x, axis=-1, keepdims=True)
    return xc * lax.rsqrt(jnp.mean(xc * xc, axis=-1, keepdims=True) + EPS) * g


def _modulation_kernel(c_ref, w_ref, b_ref, o_ref):
    s = jax.nn.silu(c_ref[...]).astype(BF16)
    o_ref[...] = jnp.dot(s, w_ref[...].astype(BF16), preferred_element_type=F32) + b_ref[...]


def _modulation(cvecs, w_ada, b_ada):
    tn = 1024
    n = N_MOD * D_MODEL
    return pl.pallas_call(
        _modulation_kernel,
        grid=(n // tn,),
        in_specs=[
            pl.BlockSpec((MOD_ROWS, D_MODEL), lambda j: (0, 0)),
            pl.BlockSpec((D_MODEL, tn), lambda j: (0, j)),
            pl.BlockSpec((1, tn), lambda j: (0, j)),
        ],
        out_specs=pl.BlockSpec((MOD_ROWS, tn), lambda j: (0, j)),
        out_shape=jax.ShapeDtypeStruct((MOD_ROWS, n), F32),
        compiler_params=_params(("parallel",), 40),
        name="modulation",
    )(cvecs, w_ada, b_ada.reshape(1, n))


def _bias_table_kernel(rpb_ref, o_ref):
    h = pl.program_id(0)
    qc = lax.broadcasted_iota(jnp.int32, (GRID_W, GRID_W), 0)
    kc = lax.broadcasted_iota(jnp.int32, (GRID_W, GRID_W), 1)
    dc = jnp.clip(kc - qc + (KW - 1), 0, N_DC - 1)
    cs = jnp.clip(qc - KW // 2, 0, GRID_W - KW)
    valid = (kc >= cs) & (kc < cs + KW)
    tiles = []
    for dr in range(N_DR):
        t = jnp.zeros((GRID_W, GRID_W), F32)
        for j in range(N_DC):
            t = jnp.where(dc == j, rpb_ref[(h * N_DR + dr) * N_DC + j], t)
        tiles.append(jnp.where(valid, t, -jnp.inf))
    pad = jnp.zeros((GRID_W, GRID_W), F32)
    o_ref[0, 0] = jnp.concatenate(tiles + [pad], axis=-1)
    o_ref[0, 1] = jnp.concatenate([pad] + tiles, axis=-1)


def _bias_table(rpb):
    width = (N_DR + 1) * GRID_W
    return pl.pallas_call(
        _bias_table_kernel,
        grid_spec=pltpu.PrefetchScalarGridSpec(
            num_scalar_prefetch=1,
            grid=(N_HEADS_A,),
            in_specs=[],
            out_specs=pl.BlockSpec((1, 2, GRID_W, width), lambda h, rpb_ref: (h, 0, 0, 0)),
        ),
        out_shape=jax.ShapeDtypeStruct((N_HEADS_A, 2, GRID_W, width), F32),
        compiler_params=_params(("parallel",), 16),
        name="bias_table",
    )(rpb.reshape(-1))


def _proj_qkvu_kernel(x_ref, mod_ref, g_ref, w_ref, h_ref, proj_ref, *kv_refs):
    y = _rms_norm(x_ref[...], g_ref[...])
    h = (y * (1.0 + mod_ref[0, 1:2, :]) + mod_ref[0, 0:1, :]).astype(BF16)
    h_ref[...] = h
    for blk in range(4):
        cols = slice(blk * COL_BLOCK, (blk + 1) * COL_BLOCK)
        acc = jnp.dot(h, w_ref[:, cols], preferred_element_type=F32)
        if kv_refs and blk in (1, 2):
            kv_refs[blk - 1][...] = acc
        if blk == 3:
            acc = jax.nn.gelu(acc)
        proj_ref[:, cols] = acc.astype(BF16)


def _proj_qkvu(x, mods, seq, g, w, *, emit_kv, tm=512):
    m = x.shape[0]
    n = w.shape[1]
    out_shape = [jax.ShapeDtypeStruct((m, D_MODEL), BF16), jax.ShapeDtypeStruct((m, n), BF16)]
    out_specs = [pl.BlockSpec((tm, D_MODEL), lambda i: (i, 0)), pl.BlockSpec((tm, n), lambda i: (i, 0))]
    if emit_kv:
        out_shape += [jax.ShapeDtypeStruct((m, D_ATTN), F32)] * 2
        out_specs += [pl.BlockSpec((tm, D_ATTN), lambda i: (i, 0))] * 2
    return pl.pallas_call(
        _proj_qkvu_kernel,
        grid=(m // tm,),
        in_specs=[
            pl.BlockSpec((tm, D_MODEL), lambda i: (i, 0)),
            pl.BlockSpec((1, N_MOD, D_MODEL), lambda i: (i * tm // seq, 0, 0)),
            pl.BlockSpec((1, D_MODEL), lambda i: (0, 0)),
            pl.BlockSpec(w.shape, lambda i: (0, 0), pipeline_mode=pl.Buffered(1)),
        ],
        out_specs=out_specs,
        out_shape=out_shape,
        compiler_params=_params(("parallel",), 56),
        name="proj_qkvu_ctx" if emit_kv else "proj_qkvu_lat",
    )(x, mods, g, w)


def _proj_gates_kernel(h_ref, lnv_ref, w_ref, proj_ref):
    h = h_ref[...]
    for blk in range(5):
        cols = slice(blk * COL_BLOCK, (blk + 1) * COL_BLOCK)
        acc = jnp.dot(h, w_ref[:, cols], preferred_element_type=F32)
        if blk < 4:
            out = jax.nn.sigmoid(acc)
        else:
            out = _layer_norm(jax.nn.gelu(acc), lnv_ref[...])
        proj_ref[:, cols] = out.astype(BF16)


def _proj_gates(h, ln_v, w, *, tm=512):
    m = h.shape[0]
    n = w.shape[1]
    return pl.pallas_call(
        _proj_gates_kernel,
        grid=(m // tm,),
        in_specs=[
            pl.BlockSpec((tm, D_MODEL), lambda i: (i, 0)),
            pl.BlockSpec((1, D_GMLP), lambda i: (0, 0)),
            pl.BlockSpec(w.shape, lambda i: (0, 0), pipeline_mode=pl.Buffered(1)),
        ],
        out_specs=pl.BlockSpec((tm, n), lambda i: (i, 0)),
        out_shape=jax.ShapeDtypeStruct((m, n), BF16),
        compiler_params=_params(("parallel",), 56),
        name="proj_gates",
    )(h, ln_v, w)


def _softmax_parts(scores):
    m = functools.reduce(jnp.maximum, [jnp.max(s, axis=-1, keepdims=True) for s in scores])
    es = [jnp.exp(s - m) for s in scores]
    inv = 1.0 / functools.reduce(jnp.add, [jnp.sum(e, axis=-1, keepdims=True) for e in es])
    return [(e * inv).astype(BF16) for e in es]


def _qk(q, k):
    return lax.dot_general(q, k, (((1,), (1,)), ((), ())), preferred_element_type=F32) * ATTN_SCALE


def _ctx_attn_kernel(q_ref, k_ref, v_ref, o_ref):
    for h in range(N_HEADS_A):
        cols = slice(h * HEAD_DIM, (h + 1) * HEAD_DIM)
        (p,) = _softmax_parts([_qk(q_ref[:, cols], k_ref[:, cols])])
        o_ref[:, cols] = jnp.dot(p, v_ref[:, cols], preferred_element_type=F32).astype(BF16)


def _ctx_attention(proj, seq):
    m = proj.shape[0]
    spec = lambda col: pl.BlockSpec((seq, D_ATTN), lambda b: (b, col))
    return pl.pallas_call(
        _ctx_attn_kernel,
        grid=(m // seq,),
        in_specs=[spec(0), spec(1), spec(2)],
        out_specs=spec(0),
        out_shape=jax.ShapeDtypeStruct((m, D_ATTN), BF16),
        compiler_params=_params(("parallel",), 32),
        name="ctx_attention",
    )(proj, proj, proj)


def _lat_attn_kernel(q_ref, k_ref, v_ref, kc_ref, vc_ref, tab_ref, o_ref, sw_ref, pw_ref, *, rows, kh):
    h = pl.program_id(1)
    kc = kc_ref[0, :, h, :].astype(BF16)
    vc = vc_ref[0, :, h, :].astype(BF16)
    win = kh * GRID_W
    s_c = _qk(q_ref[...], kc)
    starts = [min(max(r - kh // 2, 0), rows - kh) * GRID_W for r in range(rows)]
    for r, start in enumerate(starts):
        off = start // GRID_W - r + (KH_MAX - 1)
        lane0 = (off + off % 2) * GRID_W
        bias = tab_ref[0, off % 2, :, lane0:lane0 + win]
        q = q_ref[r * GRID_W:(r + 1) * GRID_W, :]
        sw_ref[r * GRID_W:(r + 1) * GRID_W, :] = _qk(q, k_ref[start:start + win, :]) + bias
    p_w, p_c = _softmax_parts([sw_ref[...], s_c])
    pw_ref[...] = p_w
    o_c = jnp.dot(p_c, vc, preferred_element_type=F32)
    for r, start in enumerate(starts):
        q_rows = slice(r * GRID_W, (r + 1) * GRID_W)
        o_w = jnp.dot(pw_ref[q_rows, :], v_ref[start:start + win, :], preferred_element_type=F32)
        o_ref[q_rows, :] = (o_w + o_c[q_rows, :]).astype(BF16)


def _lat_attention(proj, cache_k, cache_v, table, seq):
    m = proj.shape[0]
    past = cache_k.shape[1]
    rows = seq // GRID_W
    kh = min(KH_MAX, rows)
    qkv = lambda part: pl.BlockSpec((seq, HEAD_DIM), lambda b, h: (b, part * N_HEADS_A + h))
    cache = pl.BlockSpec((1, past, N_HEADS_A, HEAD_DIM), lambda b, h: (b, 0, 0, 0))
    return pl.pallas_call(
        functools.partial(_lat_attn_kernel, rows=rows, kh=kh),
        grid=(m // seq, N_HEADS_A),
        in_specs=[qkv(0), qkv(1), qkv(2), cache, cache,
                  pl.BlockSpec((1,) + table.shape[1:], lambda b, h: (h, 0, 0, 0))],
        out_specs=pl.BlockSpec((seq, HEAD_DIM), lambda b, h: (b, h)),
        out_shape=jax.ShapeDtypeStruct((m, D_ATTN), BF16),
        scratch_shapes=[pltpu.VMEM((seq, kh * GRID_W), F32), pltpu.VMEM((seq, kh * GRID_W), BF16)],
        compiler_params=_params(("parallel", "arbitrary"), 40),
        name="lat_attention",
    )(proj, proj, proj, cache_k, cache_v, table)


def _merge_kernel(x_ref, mod_ref, oa_ref, gu_ref, vn_ref, ga_ref, gb_ref,
                  ws_ref, bst_ref, wpa_ref, wpb_ref, wo_ref, g_ref, gffn_ref, o_ref, h2_ref,
                  ob_ref, *, tm):
    for c in range(tm // CHUNK):
        rows = slice(c * CHUNK, (c + 1) * CHUNK)
        for g in range(N_GROUPS_B):
            cols = slice(g * GROUP_CH, (g + 1) * GROUP_CH)
            s = jnp.dot(ws_ref[g], vn_ref[rows, cols], preferred_element_type=F32) + bst_ref[:, g:g + 1]
            ob_ref[rows, cols] = (gu_ref[rows, cols].astype(F32) * s).astype(BF16)
    pa = jnp.dot(oa_ref[...], wpa_ref[...], preferred_element_type=F32)
    pb = jnp.dot(ob_ref[...], wpb_ref[...], preferred_element_type=F32)
    mixed = (ga_ref[...].astype(F32) * pa + gb_ref[...].astype(F32) * pb).astype(BF16)
    y = jnp.dot(mixed, wo_ref[...], preferred_element_type=F32)
    x1 = x_ref[...] + mod_ref[0, 2:3, :] * _rms_norm(y, g_ref[...])
    o_ref[...] = x1
    h2 = _rms_norm(x1, gffn_ref[...]) * (1.0 + mod_ref[0, 4:5, :]) + mod_ref[0, 3:4, :]
    h2_ref[...] = h2.astype(BF16)


def _merge(x, mods, seq, o_a, proj_qkvu, proj_gates, w_s, b_s_t, w_pa, w_pb, w_o, g, g_ffn, *, tm=512):
    m = x.shape[0]
    whole = lambda a: pl.BlockSpec(a.shape, lambda i: (0,) * a.ndim, pipeline_mode=pl.Buffered(1))
    return pl.pallas_call(
        functools.partial(_merge_kernel, tm=tm),
        grid=(m // tm,),
        in_specs=[
            pl.BlockSpec((tm, D_MODEL), lambda i: (i, 0)),
            pl.BlockSpec((1, N_MOD, D_MODEL), lambda i: (i * tm // seq, 0, 0)),
            pl.BlockSpec((tm, D_ATTN), lambda i: (i, 0)),
            pl.BlockSpec((tm, D_GMLP), lambda i: (i, 3)),
            pl.BlockSpec((tm, D_GMLP), lambda i: (i, 4)),
            pl.BlockSpec((tm, D_MODEL), lambda i: (i, 0)),
            pl.BlockSpec((tm, D_MODEL), lambda i: (i, 1)),
            whole(w_s), whole(b_s_t), whole(w_pa), whole(w_pb), whole(w_o), whole(g), whole(g_ffn),
        ],
        out_specs=[pl.BlockSpec((tm, D_MODEL), lambda i: (i, 0))] * 2,
        out_shape=[jax.ShapeDtypeStruct((m, D_MODEL), F32), jax.ShapeDtypeStruct((m, D_MODEL), BF16)],
        scratch_shapes=[pltpu.VMEM((tm, D_GMLP), BF16)],
        compiler_params=_params(("parallel",), 60),
        name="merge",
    )(x, mods, o_a, proj_qkvu, proj_gates, proj_gates, proj_gates,
      w_s, b_s_t, w_pa, w_pb, w_o, g, g_ffn)


def _ffn_kernel(x_ref, h_ref, mod_ref, gpost_ref, wg_ref, wu_ref, wd_ref, o_ref, acc_ref):
    f = pl.program_id(1)

    @pl.when(f == 0)
    def _():
        acc_ref[...] = jnp.zeros_like(acc_ref)

    h = h_ref[...]
    gate = jnp.dot(h, wg_ref[...], preferred_element_type=F32)
    up = jnp.dot(h, wu_ref[...], preferred_element_type=F32)
    act = (jax.nn.silu(gate) * up).astype(BF16)
    acc_ref[...] += jnp.dot(act, wd_ref[...], preferred_element_type=F32)

    @pl.when(f == pl.num_programs(1) - 1)
    def _():
        o_ref[...] = x_ref[...] + mod_ref[0, 5:6, :] * _rms_norm(acc_ref[...], gpost_ref[...])


def _ffn(x, h, mods, seq, g_post, w_gate, w_up, w_down, *, tm=512, tf=512):
    m = x.shape[0]
    return pl.pallas_call(
        _ffn_kernel,
        grid=(m // tm, D_FF // tf),
        in_specs=[
            pl.BlockSpec((tm, D_MODEL), lambda i, f: (i, 0)),
            pl.BlockSpec((tm, D_MODEL), lambda i, f: (i, 0)),
            pl.BlockSpec((1, N_MOD, D_MODEL), lambda i, f: (i * tm // seq, 0, 0)),
            pl.BlockSpec((1, D_MODEL), lambda i, f: (0, 0)),
            pl.BlockSpec((D_MODEL, tf), lambda i, f: (0, f)),
            pl.BlockSpec((D_MODEL, tf), lambda i, f: (0, f)),
            pl.BlockSpec((tf, D_MODEL), lambda i, f: (f, 0)),
        ],
        out_specs=pl.BlockSpec((tm, D_MODEL), lambda i, f: (i, 0)),
        out_shape=jax.ShapeDtypeStruct((m, D_MODEL), F32),
        scratch_shapes=[pltpu.VMEM((tm, D_MODEL), F32)],
        compiler_params=_params(("parallel", "arbitrary"), 48),
        name="ffn",
    )(x, h, mods, g_post, w_gate, w_up, w_down)


def kernel(x_prompt, x_sample, cache_k, cache_v, c, c_ctx, w_ada, b_ada, norm_mix_pre, norm_mix_post,
           norm_ffn_pre, norm_ffn_post, w_in, rpb, ln_v, w_s, b_s, w_pa, w_pb, w_o, w_gate, w_up, w_down):
    assert w_ada.shape[0] == DEPTH == 1
    batch, seq, _ = x_prompt.shape
    dec_batch, dec_seq, _ = x_sample.shape
    past = cache_k.shape[2]

    row = lambda a: a[0].reshape(1, -1)
    bf = lambda a: a[0].astype(BF16)

    cvecs = jnp.concatenate(
        [c, c_ctx[None], jnp.zeros((MOD_ROWS - dec_batch - 1, D_MODEL), F32)], axis=0)
    mods = _modulation(cvecs, w_ada[0], b_ada[0])
    mods_lat = mods[:dec_batch].reshape(dec_batch, N_MOD, D_MODEL)
    mods_ctx = mods[dec_batch:dec_batch + 1].reshape(1, N_MOD, D_MODEL)
    table = _bias_table(rpb[0])

    n_qkvu = 3 * D_ATTN + D_GMLP
    n_vb = n_qkvu + D_GMLP
    w_qkvu = w_in[0, :, :n_qkvu].astype(BF16)
    w_gates = jnp.concatenate([w_in[0, :, n_vb:], w_in[0, :, n_qkvu:n_vb]], axis=1).astype(BF16)
    mix = (bf(w_s), b_s[0].T, bf(w_pa), bf(w_pb), bf(w_o), row(norm_mix_post), row(norm_ffn_pre))
    ffn = (row(norm_ffn_post), bf(w_gate), bf(w_up), bf(w_down))

    xp = x_prompt.reshape(batch * seq, D_MODEL)
    hp, proj_p, k_p, v_p = _proj_qkvu(xp, mods_ctx, batch * seq, row(norm_mix_pre), w_qkvu, emit_kv=True)
    gates_p = _proj_gates(hp, row(ln_v), w_gates)
    oa_p = _ctx_attention(proj_p, seq)
    xp, hp = _merge(xp, mods_ctx, batch * seq, oa_p, proj_p, gates_p, *mix)
    y_prompt = _ffn(xp, hp, mods_ctx, batch * seq, *ffn).reshape(batch, seq, D_MODEL)

    xs = x_sample.reshape(dec_batch * dec_seq, D_MODEL)
    hs, proj_s = _proj_qkvu(xs, mods_lat, dec_seq, row(norm_mix_pre), w_qkvu, emit_kv=False)
    gates_s = _proj_gates(hs, row(ln_v), w_gates)
    oa_s = _lat_attention(proj_s, cache_k.reshape(dec_batch, past, N_HEADS_A, HEAD_DIM),
                          cache_v.reshape(dec_batch, past, N_HEADS_A, HEAD_DIM), table, dec_seq)
    xs, hs = _merge(xs, mods_lat, dec_seq, oa_s, proj_s, gates_s, *mix)
    y_sample = _ffn(xs, hs, mods_lat, dec_seq, *ffn).reshape(dec_batch, dec_seq, D_MODEL)

    state_shape = (batch, DEPTH, seq, N_HEADS_A, HEAD_DIM)
    return y_prompt, y_sample, k_p.reshape(state_shape), v_p.reshape(state_shape)
```

```python
import functools

import jax
import jax.numpy as jnp
from jax import lax
from jax.experimental import pallas as pl
from jax.experimental.pallas import tpu as pltpu

D_MODEL = 2048
DEPTH = 1
GRID_W = 64
N_HEADS_A = 8
HEAD_DIM = 128
D_ATTN = N_HEADS_A * HEAD_DIM
KH_MAX = 8
KW = 16
CHUNK = 128
N_GROUPS_B = 8
D_GMLP = 1024
GROUP_CH = D_GMLP // N_GROUPS_B
D_FF = ((8 * D_MODEL // 3 + 255) // 256) * 256
N_MOD = 6
EPS = 1e-6
ATTN_SCALE = HEAD_DIM ** -0.5
D_IN = 3 * D_ATTN + 2 * D_GMLP + 2 * D_MODEL

N_DR = 2 * KH_MAX - 1
N_DC = 2 * KW - 1
COL_BLOCK = 1024
MOD_ROWS = 8

F32 = jnp.float32
BF16 = jnp.bfloat16

MIB = 1024 * 1024


def _params(semantics, vmem_mib):
    return pltpu.CompilerParams(dimension_semantics=semantics, vmem_limit_bytes=vmem_mib * MIB)


def _rms_norm(x, g):
    return x * lax.rsqrt(jnp.mean(x * x, axis=-1, keepdims=True) + EPS) * g


def _layer_norm(x, g):
    xc = x - jnp.mean(x, axis=-1, keepdims=True)
    return xc * lax.rsqrt(jnp.mean(xc * xc, axis=-1, keepdims=True) + EPS) * g


def _modulation_kernel(c_ref, w_ref, b_ref, o_ref):
    s = jax.nn.silu(c_ref[...]).astype(BF16)
    o_ref[...] = jnp.dot(s, w_ref[...].astype(BF16), preferred_element_type=F32) + b_ref[...]


def _modulation(cvecs, w_ada, b_ada):
    tn = 1024
    n = N_MOD * D_MODEL
    return pl.pallas_call(
        _modulation_kernel,
        grid=(n // tn,),
        in_specs=[
            pl.BlockSpec((MOD_ROWS, D_MODEL), lambda j: (0, 0)),
            pl.BlockSpec((D_MODEL, tn), lambda j: (0, j)),
            pl.BlockSpec((1, tn), lambda j: (0, j)),
        ],
        out_specs=pl.BlockSpec((MOD_ROWS, tn), lambda j: (0, j)),
        out_shape=jax.ShapeDtypeStruct((MOD_ROWS, n), F32),
        compiler_params=_params(("parallel",), 40),
        name="modulation",
    )(cvecs, w_ada, b_ada.reshape(1, n))


def _bias_table_kernel(rpb_ref, o_ref):
    h = pl.program_id(0)
    qc = lax.broadcasted_iota(jnp.int32, (GRID_W, GRID_W), 0)
    kc = lax.broadcasted_iota(jnp.int32, (GRID_W, GRID_W), 1)
    dc = jnp.clip(kc - qc + (KW - 1), 0, N_DC - 1)
    cs = jnp.clip(qc - KW // 2, 0, GRID_W - KW)
    valid = (kc >= cs) & (kc < cs + KW)
    tiles = []
    for dr in range(N_DR):
        t = jnp.zeros((GRID_W, GRID_W), F32)
        for j in range(N_DC):
            t = jnp.where(dc == j, rpb_ref[(h * N_DR + dr) * N_DC + j], t)
        tiles.append(jnp.where(valid, t, -jnp.inf))
    pad = jnp.zeros((GRID_W, GRID_W), F32)
    o_ref[0, 0] = jnp.concatenate(tiles + [pad], axis=-1)
    o_ref[0, 1] = jnp.concatenate([pad] + tiles, axis=-1)


def _bias_table(rpb):
    width = (N_DR + 1) * GRID_W
    return pl.pallas_call(
        _bias_table_kernel,
        grid_spec=pltpu.PrefetchScalarGridSpec(
            num_scalar_prefetch=1,
            grid=(N_HEADS_A,),
            in_specs=[],
            out_specs=pl.BlockSpec((1, 2, GRID_W, width), lambda h, rpb_ref: (h, 0, 0, 0)),
        ),
        out_shape=jax.ShapeDtypeStruct((N_HEADS_A, 2, GRID_W, width), F32),
        compiler_params=_params(("parallel",), 16),
        name="bias_table",
    )(rpb.reshape(-1))


def _proj_qkvu_kernel(x_ref, mod_ref, g_ref, w_ref, h_ref, proj_ref, *kv_refs):
    y = _rms_norm(x_ref[...], g_ref[...])
    h = (y * (1.0 + mod_ref[0, 1:2, :]) + mod_ref[0, 0:1, :]).astype(BF16)
    h_ref[...] = h
    for blk in range(4):
        cols = slice(blk * COL_BLOCK, (blk + 1) * COL_BLOCK)
        acc = jnp.dot(h, w_ref[:, cols], preferred_element_type=F32)
        if kv_refs and blk in (1, 2):
            kv_refs[blk - 1][...] = acc
        if blk == 3:
            acc = jax.nn.gelu(acc)
        proj_ref[:, cols] = acc.astype(BF16)


def _proj_qkvu(x, mods, seq, g, w_in, *, emit_kv, tm=512):
    m = x.shape[0]
    n = 3 * D_ATTN + D_GMLP
    out_shape = [jax.ShapeDtypeStruct((m, D_MODEL), BF16), jax.ShapeDtypeStruct((m, n), BF16)]
    out_specs = [pl.BlockSpec((tm, D_MODEL), lambda i: (i, 0)), pl.BlockSpec((tm, n), lambda i: (i, 0))]
    if emit_kv:
        out_shape += [jax.ShapeDtypeStruct((m, D_ATTN), F32)] * 2
        out_specs += [pl.BlockSpec((tm, D_ATTN), lambda i: (i, 0))] * 2
    return pl.pallas_call(
        _proj_qkvu_kernel,
        grid=(m // tm,),
        in_specs=[
            pl.BlockSpec((tm, D_MODEL), lambda i: (i, 0)),
            pl.BlockSpec((1, N_MOD, D_MODEL), lambda i: (i * tm // seq, 0, 0)),
            pl.BlockSpec((1, D_MODEL), lambda i: (0, 0)),
            pl.BlockSpec((D_MODEL, n), lambda i: (0, 0), pipeline_mode=pl.Buffered(1)),
        ],
        out_specs=out_specs,
        out_shape=out_shape,
        compiler_params=_params(("parallel",), 56),
        name="proj_qkvu_ctx" if emit_kv else "proj_qkvu_lat",
    )(x, mods, g, w_in)


def _proj_gates_kernel(h_ref, lnv_ref, wvb_ref, *rest):
    *wgate_refs, proj_ref = rest
    h = h_ref[...]
    for blk, w_ref in enumerate(wgate_refs):
        acc = jnp.dot(h, w_ref[...], preferred_element_type=F32)
        proj_ref[:, blk * COL_BLOCK:(blk + 1) * COL_BLOCK] = jax.nn.sigmoid(acc).astype(BF16)
    acc = jnp.dot(h, wvb_ref[...], preferred_element_type=F32)
    vn = _layer_norm(jax.nn.gelu(acc), lnv_ref[...])
    proj_ref[:, len(wgate_refs) * COL_BLOCK:] = vn.astype(BF16)


def _proj_gates(h, ln_v, w_in, *, tm=512):
    m = h.shape[0]
    first = (3 * D_ATTN + D_GMLP) // COL_BLOCK
    n_blk = D_IN // COL_BLOCK - first
    w_blk = lambda blk: pl.BlockSpec((D_MODEL, COL_BLOCK), lambda i: (0, blk), pipeline_mode=pl.Buffered(1))
    return pl.pallas_call(
        _proj_gates_kernel,
        grid=(m // tm,),
        in_specs=[
            pl.BlockSpec((tm, D_MODEL), lambda i: (i, 0)),
            pl.BlockSpec((1, D_GMLP), lambda i: (0, 0)),
        ] + [w_blk(first + b) for b in range(n_blk)],
        out_specs=pl.BlockSpec((tm, n_blk * COL_BLOCK), lambda i: (i, 0)),
        out_shape=jax.ShapeDtypeStruct((m, n_blk * COL_BLOCK), BF16),
        compiler_params=_params(("parallel",), 56),
        name="proj_gates",
    )(h, ln_v, *([w_in] * n_blk))


def _softmax_parts(scores):
    m = functools.reduce(jnp.maximum, [jnp.max(s, axis=-1, keepdims=True) for s in scores])
    es = [jnp.exp(s - m) for s in scores]
    inv = 1.0 / functools.reduce(jnp.add, [jnp.sum(e, axis=-1, keepdims=True) for e in es])
    return [(e * inv).astype(BF16) for e in es]


def _qk(q, k):
    return lax.dot_general(q, k, (((1,), (1,)), ((), ())), preferred_element_type=F32) * ATTN_SCALE


def _ctx_attn_kernel(q_ref, k_ref, v_ref, o_ref):
    for h in range(N_HEADS_A):
        cols = slice(h * HEAD_DIM, (h + 1) * HEAD_DIM)
        (p,) = _softmax_parts([_qk(q_ref[:, cols], k_ref[:, cols])])
        o_ref[:, cols] = jnp.dot(p, v_ref[:, cols], preferred_element_type=F32).astype(BF16)


def _ctx_attention(proj, seq):
    m = proj.shape[0]
    spec = lambda col: pl.BlockSpec((seq, D_ATTN), lambda b: (b, col))
    return pl.pallas_call(
        _ctx_attn_kernel,
        grid=(m // seq,),
        in_specs=[spec(0), spec(1), spec(2)],
        out_specs=spec(0),
        out_shape=jax.ShapeDtypeStruct((m, D_ATTN), BF16),
        compiler_params=_params(("parallel",), 32),
        name="ctx_attention",
    )(proj, proj, proj)


def _lat_attn_kernel(q_ref, k_ref, v_ref, kc_ref, vc_ref, tab_ref, o_ref, sw_ref, pw_ref, *, rows, kh):
    h = pl.program_id(1)
    kc = kc_ref[0, :, h, :].astype(BF16)
    vc = vc_ref[0, :, h, :].astype(BF16)
    win = kh * GRID_W
    s_c = _qk(q_ref[...], kc)
    starts = [min(max(r - kh // 2, 0), rows - kh) * GRID_W for r in range(rows)]
    for r, start in enumerate(starts):
        off = start // GRID_W - r + (KH_MAX - 1)
        lane0 = (off + off % 2) * GRID_W
        bias = tab_ref[0, off % 2, :, lane0:lane0 + win]
        q = q_ref[r * GRID_W:(r + 1) * GRID_W, :]
        sw_ref[r * GRID_W:(r + 1) * GRID_W, :] = _qk(q, k_ref[start:start + win, :]) + bias
    p_w, p_c = _softmax_parts([sw_ref[...], s_c])
    pw_ref[...] = p_w
    o_c = jnp.dot(p_c, vc, preferred_element_type=F32)
    for r, start in enumerate(starts):
        q_rows = slice(r * GRID_W, (r + 1) * GRID_W)
        o_w = jnp.dot(pw_ref[q_rows, :], v_ref[start:start + win, :], preferred_element_type=F32)
        o_ref[q_rows, :] = (o_w + o_c[q_rows, :]).astype(BF16)


def _lat_attention(proj, cache_k, cache_v, table, seq):
    m = proj.shape[0]
    past = cache_k.shape[1]
    rows = seq // GRID_W
    kh = min(KH_MAX, rows)
    qkv = lambda part: pl.BlockSpec((seq, HEAD_DIM), lambda b, h: (b, part * N_HEADS_A + h))
    cache = pl.BlockSpec((1, past, N_HEADS_A, HEAD_DIM), lambda b, h: (b, 0, 0, 0))
    return pl.pallas_call(
        functools.partial(_lat_attn_kernel, rows=rows, kh=kh),
        grid=(m // seq, N_HEADS_A),
        in_specs=[qkv(0), qkv(1), qkv(2), cache, cache,
                  pl.BlockSpec((1,) + table.shape[1:], lambda b, h: (h, 0, 0, 0))],
        out_specs=pl.BlockSpec((seq, HEAD_DIM), lambda b, h: (b, h)),
        out_shape=jax.ShapeDtypeStruct((m, D_ATTN), BF16),
        scratch_shapes=[pltpu.VMEM((seq, kh * GRID_W), F32), pltpu.VMEM((seq, kh * GRID_W), BF16)],
        compiler_params=_params(("parallel", "arbitrary"), 40),
        name="lat_attention",
    )(proj, proj, proj, cache_k, cache_v, table)


def _merge_kernel(x_ref, mod_ref, oa_ref, gu_ref, vn_ref, ga_ref, gb_ref,
                  ws_ref, bst_ref, wpa_ref, wpb_ref, wo_ref, g_ref, gffn_ref, o_ref, h2_ref,
                  ob_ref, *, tm):
    for c in range(tm // CHUNK):
        rows = slice(c * CHUNK, (c + 1) * CHUNK)
        for g in range(N_GROUPS_B):
            cols = slice(g * GROUP_CH, (g + 1) * GROUP_CH)
            s = jnp.dot(ws_ref[g], vn_ref[rows, cols], preferred_element_type=F32) + bst_ref[:, g:g + 1]
            ob_ref[rows, cols] = (gu_ref[rows, cols].astype(F32) * s).astype(BF16)
    pa = jnp.dot(oa_ref[...], wpa_ref[...], preferred_element_type=F32)
    pb = jnp.dot(ob_ref[...], wpb_ref[...], preferred_element_type=F32)
    mixed = (ga_ref[...].astype(F32) * pa + gb_ref[...].astype(F32) * pb).astype(BF16)
    y = jnp.dot(mixed, wo_ref[...], preferred_element_type=F32)
    x1 = x_ref[...] + mod_ref[0, 2:3, :] * _rms_norm(y, g_ref[...])
    o_ref[...] = x1
    h2 = _rms_norm(x1, gffn_ref[...]) * (1.0 + mod_ref[0, 4:5, :]) + mod_ref[0, 3:4, :]
    h2_ref[...] = h2.astype(BF16)


def _merge(x, mods, seq, o_a, proj_qkvu, proj_gates, w_s, b_s_t, w_pa, w_pb, w_o, g, g_ffn, *, tm=512):
    m = x.shape[0]
    whole = lambda a: pl.BlockSpec(a.shape, lambda i: (0,) * a.ndim, pipeline_mode=pl.Buffered(1))
    return pl.pallas_call(
        functools.partial(_merge_kernel, tm=tm),
        grid=(m // tm,),
        in_specs=[
            pl.BlockSpec((tm, D_MODEL), lambda i: (i, 0)),
            pl.BlockSpec((1, N_MOD, D_MODEL), lambda i: (i * tm // seq, 0, 0)),
            pl.BlockSpec((tm, D_ATTN), lambda i: (i, 0)),
            pl.BlockSpec((tm, D_GMLP), lambda i: (i, 3)),
            pl.BlockSpec((tm, D_GMLP), lambda i: (i, 4)),
            pl.BlockSpec((tm, D_MODEL), lambda i: (i, 0)),
            pl.BlockSpec((tm, D_MODEL), lambda i: (i, 1)),
            whole(w_s), whole(b_s_t), whole(w_pa), whole(w_pb), whole(w_o), whole(g), whole(g_ffn),
        ],
        out_specs=[pl.BlockSpec((tm, D_MODEL), lambda i: (i, 0))] * 2,
        out_shape=[jax.ShapeDtypeStruct((m, D_MODEL), F32), jax.ShapeDtypeStruct((m, D_MODEL), BF16)],
        scratch_shapes=[pltpu.VMEM((tm, D_GMLP), BF16)],
        compiler_params=_params(("parallel",), 60),
        name="merge",
    )(x, mods, o_a, proj_qkvu, proj_gates, proj_gates, proj_gates,
      w_s, b_s_t, w_pa, w_pb, w_o, g, g_ffn)


def _ffn_x_block(f, n_f, n_x):
    return jnp.clip(f - (n_f - n_x), 0, n_x - 1)


def _ffn_kernel(x_ref, h_ref, mod_ref, gpost_ref, wg_ref, wu_ref, wd_ref, o_ref, acc_ref, *, n_x, row_parts):
    f = pl.program_id(1)
    x_rows = x_ref.shape[0]
    tm = h_ref.shape[0]

    @pl.when(f == 0)
    def _():
        acc_ref[...] = jnp.zeros_like(acc_ref)

    row0 = pl.multiple_of(_ffn_x_block(f, pl.num_programs(1), n_x) * x_rows, x_rows)
    o_ref[pl.ds(row0, x_rows), :] = x_ref[...]

    for part in range(row_parts):
        rows = slice(part * (tm // row_parts), (part + 1) * (tm // row_parts))
        h = h_ref[rows, :]
        gate = jnp.dot(h, wg_ref[...], preferred_element_type=F32)
        up = jnp.dot(h, wu_ref[...], preferred_element_type=F32)
        act = (jax.nn.silu(gate) * up).astype(BF16)
        acc_ref[rows, :] += jnp.dot(act, wd_ref[...], preferred_element_type=F32)

    @pl.when(f == pl.num_programs(1) - 1)
    def _():
        o_ref[...] = o_ref[...] + mod_ref[0, 5:6, :] * _rms_norm(acc_ref[...], gpost_ref[...])


def _ffn(x, h, mods, seq, g_post, w_gate, w_up, w_down, *, tm=1024, tf=512, x_rows=128, row_parts=2):
    m = x.shape[0]
    n_f = D_FF // tf
    n_x = tm // x_rows
    assert n_x <= n_f
    return pl.pallas_call(
        functools.partial(_ffn_kernel, n_x=n_x, row_parts=row_parts),
        grid=(m // tm, n_f),
        in_specs=[
            pl.BlockSpec((x_rows, D_MODEL), lambda i, f: (i * n_x + _ffn_x_block(f, n_f, n_x), 0)),
            pl.BlockSpec((tm, D_MODEL), lambda i, f: (i, 0)),
            pl.BlockSpec((1, N_MOD, D_MODEL), lambda i, f: (i * tm // seq, 0, 0)),
            pl.BlockSpec((1, D_MODEL), lambda i, f: (0, 0)),
            pl.BlockSpec((D_MODEL, tf), lambda i, f: (0, f)),
            pl.BlockSpec((D_MODEL, tf), lambda i, f: (0, f)),
            pl.BlockSpec((tf, D_MODEL), lambda i, f: (f, 0)),
        ],
        out_specs=pl.BlockSpec((tm, D_MODEL), lambda i, f: (i, 0)),
        out_shape=jax.ShapeDtypeStruct((m, D_MODEL), F32),
        scratch_shapes=[pltpu.VMEM((tm, D_MODEL), F32)],
        compiler_params=_params(("parallel", "arbitrary"), 58),
        name="ffn",
    )(x, h, mods, g_post, w_gate, w_up, w_down)


def kernel(x_prompt, x_sample, cache_k, cache_v, c, c_ctx, w_ada, b_ada, norm_mix_pre, norm_mix_post,
           norm_ffn_pre, norm_ffn_post, w_in, rpb, ln_v, w_s, b_s, w_pa, w_pb, w_o, w_gate, w_up, w_down):
    assert w_ada.shape[0] == DEPTH == 1
    batch, seq, _ = x_prompt.shape
    dec_batch, dec_seq, _ = x_sample.shape
    past = cache_k.shape[2]

    row = lambda a: a[0].reshape(1, -1)
    bf = lambda a: a[0].astype(BF16)

    cvecs = jnp.concatenate(
        [c, c_ctx[None], jnp.zeros((MOD_ROWS - dec_batch - 1, D_MODEL), F32)], axis=0)
    mods = _modulation(cvecs, w_ada[0], b_ada[0])
    mods_lat = mods[:dec_batch].reshape(dec_batch, N_MOD, D_MODEL)
    mods_ctx = mods[dec_batch:dec_batch + 1].reshape(1, N_MOD, D_MODEL)
    table = _bias_table(rpb[0])

    w_in_b = bf(w_in)
    mix = (bf(w_s), b_s[0].T, bf(w_pa), bf(w_pb), bf(w_o), row(norm_mix_post), row(norm_ffn_pre))
    ffn = (row(norm_ffn_post), bf(w_gate), bf(w_up), bf(w_down))

    xp = x_prompt.reshape(batch * seq, D_MODEL)
    hp, proj_p, k_p, v_p = _proj_qkvu(xp, mods_ctx, batch * seq, row(norm_mix_pre), w_in_b, emit_kv=True)
    gates_p = _proj_gates(hp, row(ln_v), w_in_b)
    oa_p = _ctx_attention(proj_p, seq)
    xp, hp = _merge(xp, mods_ctx, batch * seq, oa_p, proj_p, gates_p, *mix)
    y_prompt = _ffn(xp, hp, mods_ctx, batch * seq, *ffn).reshape(batch, seq, D_MODEL)

    xs = x_sample.reshape(dec_batch * dec_seq, D_MODEL)
    hs, proj_s = _proj_qkvu(xs, mods_lat, dec_seq, row(norm_mix_pre), w_in_b, emit_kv=False)
    gates_s = _proj_gates(hs, row(ln_v), w_in_b)
    oa_s = _lat_attention(proj_s, cache_k.reshape(dec_batch, past, N_HEADS_A, HEAD_DIM),
                          cache_v.reshape(dec_batch, past, N_HEADS_A, HEAD_DIM), table, dec_seq)
    xs, hs = _merge(xs, mods_lat, dec_seq, oa_s, proj_s, gates_s, *mix)
    y_sample = _ffn(xs, hs, mods_lat, dec_seq, *ffn).reshape(dec_batch, dec_seq, D_MODEL)

    state_shape = (batch, DEPTH, seq, N_HEADS_A, HEAD_DIM)
    return y_prompt, y_sample, k_p.reshape(state_shape), v_p.reshape(state_shape)
```

```python
import functools

import jax
import jax.numpy as jnp
from jax import lax
from jax.experimental import pallas as pl
from jax.experimental.pallas import tpu as pltpu

D_MODEL = 2048
DEPTH = 1
GRID_W = 64
N_HEADS_A = 8
HEAD_DIM = 128
D_ATTN = N_HEADS_A * HEAD_DIM
KH_MAX = 8
KW = 16
CHUNK = 128
N_GROUPS_B = 8
D_GMLP = 1024
GROUP_CH = D_GMLP // N_GROUPS_B
D_FF = ((8 * D_MODEL // 3 + 255) // 256) * 256
N_MOD = 6
EPS = 1e-6
ATTN_SCALE = HEAD_DIM ** -0.5
LOG2E = 1.4426950408889634
D_IN = 3 * D_ATTN + 2 * D_GMLP + 2 * D_MODEL

N_DR = 2 * KH_MAX - 1
N_DC = 2 * KW - 1
COL_BLOCK = 1024
MOD_ROWS = 8

F32 = jnp.float32
BF16 = jnp.bfloat16

MIB = 1024 * 1024


def _params(semantics, vmem_mib):
    return pltpu.CompilerParams(dimension_semantics=semantics, vmem_limit_bytes=vmem_mib * MIB)


def _rms_norm(x, g):
    return x * lax.rsqrt(jnp.mean(x * x, axis=-1, keepdims=True) + EPS) * g


def _layer_norm(x, g):
    xc = x - jnp.mean(x, axis=-1, keepdims=True)
    return xc * lax.rsqrt(jnp.mean(xc * xc, axis=-1, keepdims=True) + EPS) * g


def _modulation_kernel(c_ref, w_ref, b_ref, o_ref):
    s = jax.nn.silu(c_ref[...]).astype(BF16)
    o_ref[...] = jnp.dot(s, w_ref[...].astype(BF16), preferred_element_type=F32) + b_ref[...]


def _modulation(cvecs, w_ada, b_ada):
    tn = 1024
    n = N_MOD * D_MODEL
    return pl.pallas_call(
        _modulation_kernel,
        grid=(n // tn,),
        in_specs=[
            pl.BlockSpec((MOD_ROWS, D_MODEL), lambda j: (0, 0)),
            pl.BlockSpec((D_MODEL, tn), lambda j: (0, j)),
            pl.BlockSpec((1, tn), lambda j: (0, j)),
        ],
        out_specs=pl.BlockSpec((MOD_ROWS, tn), lambda j: (0, j)),
        out_shape=jax.ShapeDtypeStruct((MOD_ROWS, n), F32),
        compiler_params=_params(("parallel",), 40),
        name="modulation",
    )(cvecs, w_ada, b_ada.reshape(1, n))


def _bias_table_kernel(rpb_ref, o_ref):
    h = pl.program_id(0)
    qc = lax.broadcasted_iota(jnp.int32, (GRID_W, GRID_W), 0)
    kc = lax.broadcasted_iota(jnp.int32, (GRID_W, GRID_W), 1)
    dc = jnp.clip(kc - qc + (KW - 1), 0, N_DC - 1)
    cs = jnp.clip(qc - KW // 2, 0, GRID_W - KW)
    valid = (kc >= cs) & (kc < cs + KW)
    tiles = []
    for dr in range(N_DR):
        t = jnp.zeros((GRID_W, GRID_W), F32)
        for j in range(N_DC):
            t = jnp.where(dc == j, rpb_ref[(h * N_DR + dr) * N_DC + j], t)
        tiles.append(jnp.where(valid, t * LOG2E, -jnp.inf))
    pad = jnp.zeros((GRID_W, GRID_W), F32)
    o_ref[0, 0] = jnp.concatenate(tiles + [pad], axis=-1)
    o_ref[0, 1] = jnp.concatenate([pad] + tiles, axis=-1)


def _bias_table(rpb):
    width = (N_DR + 1) * GRID_W
    return pl.pallas_call(
        _bias_table_kernel,
        grid_spec=pltpu.PrefetchScalarGridSpec(
            num_scalar_prefetch=1,
            grid=(N_HEADS_A,),
            in_specs=[],
            out_specs=pl.BlockSpec((1, 2, GRID_W, width), lambda h, rpb_ref: (h, 0, 0, 0)),
        ),
        out_shape=jax.ShapeDtypeStruct((N_HEADS_A, 2, GRID_W, width), F32),
        compiler_params=_params(("parallel",), 16),
        name="bias_table",
    )(rpb.reshape(-1))


def _row_parts(tm, parts):
    return [slice(p * (tm // parts), (p + 1) * (tm // parts)) for p in range(parts)]


def _proj_qkvu_kernel(x_ref, mod_ref, g_ref, w_ref, h_ref, proj_ref, *kv_refs, row_parts):
    for rows in _row_parts(x_ref.shape[0], row_parts):
        y = _rms_norm(x_ref[rows, :], g_ref[...])
        h = (y * (1.0 + mod_ref[0, 1:2, :]) + mod_ref[0, 0:1, :]).astype(BF16)
        h_ref[rows, :] = h
        for blk in (3, 0, 1, 2):
            cols = slice(blk * COL_BLOCK, (blk + 1) * COL_BLOCK)
            acc = jnp.dot(h, w_ref[:, cols], preferred_element_type=F32)
            if kv_refs and blk in (1, 2):
                kv_refs[blk - 1][rows, :] = acc
            if blk == 3:
                acc = jax.nn.gelu(acc)
            proj_ref[rows, cols] = acc.astype(BF16)


def _proj_qkvu(x, mods, seq, g, w_in, *, emit_kv, tm=512, row_parts=2):
    m = x.shape[0]
    n = 3 * D_ATTN + D_GMLP
    out_shape = [jax.ShapeDtypeStruct((m, D_MODEL), BF16), jax.ShapeDtypeStruct((m, n), BF16)]
    out_specs = [pl.BlockSpec((tm, D_MODEL), lambda i: (i, 0)), pl.BlockSpec((tm, n), lambda i: (i, 0))]
    if emit_kv:
        out_shape += [jax.ShapeDtypeStruct((m, D_ATTN), F32)] * 2
        out_specs += [pl.BlockSpec((tm, D_ATTN), lambda i: (i, 0))] * 2
    return pl.pallas_call(
        functools.partial(_proj_qkvu_kernel, row_parts=row_parts),
        grid=(m // tm,),
        in_specs=[
            pl.BlockSpec((tm, D_MODEL), lambda i: (i, 0)),
            pl.BlockSpec((1, N_MOD, D_MODEL), lambda i: (i * tm // seq, 0, 0)),
            pl.BlockSpec((1, D_MODEL), lambda i: (0, 0)),
            pl.BlockSpec((D_MODEL, n), lambda i: (0, 0), pipeline_mode=pl.Buffered(1)),
        ],
        out_specs=out_specs,
        out_shape=out_shape,
        compiler_params=_params(("parallel",), 56),
        name="proj_qkvu_ctx" if emit_kv else "proj_qkvu_lat",
    )(x, mods, g, w_in)


def _proj_gates_kernel(h_ref, lnv_ref, wvb_ref, *rest, row_parts):
    *wgate_refs, proj_ref = rest
    n_gate = len(wgate_refs) * COL_BLOCK
    for rows in _row_parts(h_ref.shape[0], row_parts):
        h = h_ref[rows, :]
        acc = jnp.dot(h, wvb_ref[...], preferred_element_type=F32)
        proj_ref[rows, n_gate:] = _layer_norm(jax.nn.gelu(acc), lnv_ref[...]).astype(BF16)
        for blk, w_ref in enumerate(wgate_refs):
            acc = jnp.dot(h, w_ref[...], preferred_element_type=F32)
            proj_ref[rows, blk * COL_BLOCK:(blk + 1) * COL_BLOCK] = jax.nn.sigmoid(acc).astype(BF16)


def _proj_gates(h, ln_v, w_in, *, tm=512, row_parts=2):
    m = h.shape[0]
    first = (3 * D_ATTN + D_GMLP) // COL_BLOCK
    n_blk = D_IN // COL_BLOCK - first
    w_blk = lambda blk: pl.BlockSpec((D_MODEL, COL_BLOCK), lambda i: (0, blk), pipeline_mode=pl.Buffered(1))
    return pl.pallas_call(
        functools.partial(_proj_gates_kernel, row_parts=row_parts),
        grid=(m // tm,),
        in_specs=[
            pl.BlockSpec((tm, D_MODEL), lambda i: (i, 0)),
            pl.BlockSpec((1, D_GMLP), lambda i: (0, 0)),
        ] + [w_blk(first + b) for b in range(n_blk)],
        out_specs=pl.BlockSpec((tm, n_blk * COL_BLOCK), lambda i: (i, 0)),
        out_shape=jax.ShapeDtypeStruct((m, n_blk * COL_BLOCK), BF16),
        compiler_params=_params(("parallel",), 56),
        name="proj_gates",
    )(h, ln_v, *([w_in] * n_blk))


def _softmax_parts(logits2):
    m = functools.reduce(jnp.maximum, [jnp.max(t, axis=-1, keepdims=True) for t in logits2])
    es = [jnp.exp2(t - m) for t in logits2]
    inv = 1.0 / functools.reduce(jnp.add, [jnp.sum(e, axis=-1, keepdims=True) for e in es])
    return [e.astype(BF16) for e in es], inv


def _qk(q, k):
    return lax.dot_general(q, k, (((1,), (1,)), ((), ())), preferred_element_type=F32) * (ATTN_SCALE * LOG2E)


def _ctx_attn_kernel(q_ref, k_ref, v_ref, o_ref):
    for h in range(N_HEADS_A):
        cols = slice(h * HEAD_DIM, (h + 1) * HEAD_DIM)
        (p,), inv = _softmax_parts([_qk(q_ref[:, cols], k_ref[:, cols])])
        o_ref[:, cols] = (jnp.dot(p, v_ref[:, cols], preferred_element_type=F32) * inv).astype(BF16)


def _ctx_attention(proj, seq):
    m = proj.shape[0]
    spec = lambda col: pl.BlockSpec((seq, D_ATTN), lambda b: (b, col))
    return pl.pallas_call(
        _ctx_attn_kernel,
        grid=(m // seq,),
        in_specs=[spec(0), spec(1), spec(2)],
        out_specs=spec(0),
        out_shape=jax.ShapeDtypeStruct((m, D_ATTN), BF16),
        compiler_params=_params(("parallel",), 32),
        name="ctx_attention",
    )(proj, proj, proj)


def _lat_attn_kernel(q_ref, k_ref, v_ref, kc_ref, vc_ref, tab_ref, o_ref, sw_ref, pw_ref, *, rows, kh):
    h = pl.program_id(1)
    past = kc_ref.shape[1] // N_HEADS_A
    kc = kc_ref[0, pl.ds(h, past, stride=N_HEADS_A), :].astype(BF16)
    vc = vc_ref[0, pl.ds(h, past, stride=N_HEADS_A), :].astype(BF16)
    win = kh * GRID_W
    s_c = _qk(q_ref[...], kc)
    starts = [min(max(r - kh // 2, 0), rows - kh) * GRID_W for r in range(rows)]
    for r, start in enumerate(starts):
        off = start // GRID_W - r + (KH_MAX - 1)
        lane0 = (off + off % 2) * GRID_W
        bias = tab_ref[0, off % 2, :, lane0:lane0 + win]
        q = q_ref[r * GRID_W:(r + 1) * GRID_W, :]
        sw_ref[r * GRID_W:(r + 1) * GRID_W, :] = _qk(q, k_ref[start:start + win, :]) + bias
    (p_w, p_c), inv = _softmax_parts([sw_ref[...], s_c])
    pw_ref[...] = p_w
    o_c = jnp.dot(p_c, vc, preferred_element_type=F32)
    for r, start in enumerate(starts):
        q_rows = slice(r * GRID_W, (r + 1) * GRID_W)
        o_w = jnp.dot(pw_ref[q_rows, :], v_ref[start:start + win, :], preferred_element_type=F32)
        o_ref[q_rows, :] = ((o_w + o_c[q_rows, :]) * inv[q_rows, :]).astype(BF16)


def _lat_attention(proj, cache_k, cache_v, table, seq):
    m = proj.shape[0]
    rows = seq // GRID_W
    kh = min(KH_MAX, rows)
    qkv = lambda part: pl.BlockSpec((seq, HEAD_DIM), lambda b, h: (b, part * N_HEADS_A + h))
    cache = pl.BlockSpec((1,) + cache_k.shape[1:], lambda b, h: (b, 0, 0))
    return pl.pallas_call(
        functools.partial(_lat_attn_kernel, rows=rows, kh=kh),
        grid=(m // seq, N_HEADS_A),
        in_specs=[qkv(0), qkv(1), qkv(2), cache, cache,
                  pl.BlockSpec((1,) + table.shape[1:], lambda b, h: (h, 0, 0, 0))],
        out_specs=pl.BlockSpec((seq, HEAD_DIM), lambda b, h: (b, h)),
        out_shape=jax.ShapeDtypeStruct((m, D_ATTN), BF16),
        scratch_shapes=[pltpu.VMEM((seq, kh * GRID_W), F32), pltpu.VMEM((seq, kh * GRID_W), BF16)],
        compiler_params=_params(("parallel", "arbitrary"), 40),
        name="lat_attention",
    )(proj, proj, proj, cache_k, cache_v, table)


def _merge_kernel(x_ref, mod_ref, oa_ref, gu_ref, vn_ref, ga_ref, gb_ref,
                  ws_ref, bst_ref, wpa_ref, wpb_ref, wo_ref, g_ref, gffn_ref, o_ref, h2_ref,
                  ob_ref, *, row_parts):
    for part in _row_parts(x_ref.shape[0], row_parts):
        for c in range(part.start // CHUNK, part.stop // CHUNK):
            rows = slice(c * CHUNK, (c + 1) * CHUNK)
            for g in range(N_GROUPS_B):
                cols = slice(g * GROUP_CH, (g + 1) * GROUP_CH)
                s = jnp.dot(ws_ref[g], vn_ref[rows, cols], preferred_element_type=F32) + bst_ref[:, g:g + 1]
                ob_ref[rows, cols] = (gu_ref[rows, cols].astype(F32) * s).astype(BF16)
        pa = jnp.dot(oa_ref[part, :], wpa_ref[...], preferred_element_type=F32)
        pb = jnp.dot(ob_ref[part, :], wpb_ref[...], preferred_element_type=F32)
        mixed = (ga_ref[part, :].astype(F32) * pa + gb_ref[part, :].astype(F32) * pb).astype(BF16)
        y = jnp.dot(mixed, wo_ref[...], preferred_element_type=F32)
        x1 = x_ref[part, :] + mod_ref[0, 2:3, :] * _rms_norm(y, g_ref[...])
        o_ref[part, :] = x1
        h2 = _rms_norm(x1, gffn_ref[...]) * (1.0 + mod_ref[0, 4:5, :]) + mod_ref[0, 3:4, :]
        h2_ref[part, :] = h2.astype(BF16)


def _merge(x, mods, seq, o_a, proj_qkvu, proj_gates, w_s, b_s_t, w_pa, w_pb, w_o, g, g_ffn, *,
           tm=512, row_parts=2):
    m = x.shape[0]
    assert (tm // row_parts) % CHUNK == 0
    whole = lambda a: pl.BlockSpec(a.shape, lambda i: (0,) * a.ndim, pipeline_mode=pl.Buffered(1))
    return pl.pallas_call(
        functools.partial(_merge_kernel, row_parts=row_parts),
        grid=(m // tm,),
        in_specs=[
            pl.BlockSpec((tm, D_MODEL), lambda i: (i, 0)),
            pl.BlockSpec((1, N_MOD, D_MODEL), lambda i: (i * tm // seq, 0, 0)),
            pl.BlockSpec((tm, D_ATTN), lambda i: (i, 0)),
            pl.BlockSpec((tm, D_GMLP), lambda i: (i, 3)),
            pl.BlockSpec((tm, D_GMLP), lambda i: (i, 4)),
            pl.BlockSpec((tm, D_MODEL), lambda i: (i, 0)),
            pl.BlockSpec((tm, D_MODEL), lambda i: (i, 1)),
            whole(w_s), whole(b_s_t), whole(w_pa), whole(w_pb), whole(w_o), whole(g), whole(g_ffn),
        ],
        out_specs=[pl.BlockSpec((tm, D_MODEL), lambda i: (i, 0))] * 2,
        out_shape=[jax.ShapeDtypeStruct((m, D_MODEL), F32), jax.ShapeDtypeStruct((m, D_MODEL), BF16)],
        scratch_shapes=[pltpu.VMEM((tm, D_GMLP), BF16)],
        compiler_params=_params(("parallel",), 60),
        name="merge",
    )(x, mods, o_a, proj_qkvu, proj_gates, proj_gates, proj_gates,
      w_s, b_s_t, w_pa, w_pb, w_o, g, g_ffn)


def _ffn_x_block(f, n_f, n_x):
    return jnp.clip(f - (n_f - n_x), 0, n_x - 1)


def _ffn_kernel(x_ref, h_ref, mod_ref, gpost_ref, wg_ref, wu_ref, wd_ref, o_ref, acc_ref, *, n_x, row_parts):
    f = pl.program_id(1)
    x_rows = x_ref.shape[0]
    tm = h_ref.shape[0]

    @pl.when(f == 0)
    def _():
        acc_ref[...] = jnp.zeros_like(acc_ref)

    row0 = pl.multiple_of(_ffn_x_block(f, pl.num_programs(1), n_x) * x_rows, x_rows)
    o_ref[pl.ds(row0, x_rows), :] = x_ref[...]

    for part in range(row_parts):
        rows = slice(part * (tm // row_parts), (part + 1) * (tm // row_parts))
        h = h_ref[rows, :]
        gate = jnp.dot(h, wg_ref[...], preferred_element_type=F32)
        up = jnp.dot(h, wu_ref[...], preferred_element_type=F32)
        act = (jax.nn.silu(gate) * up).astype(BF16)
        acc_ref[rows, :] += jnp.dot(act, wd_ref[...], preferred_element_type=F32)

    @pl.when(f == pl.num_programs(1) - 1)
    def _():
        o_ref[...] = o_ref[...] + mod_ref[0, 5:6, :] * _rms_norm(acc_ref[...], gpost_ref[...])


def _ffn(x, h, mods, seq, g_post, w_gate, w_up, w_down, *, tm=1024, tf=512, x_rows=128, row_parts=2):
    m = x.shape[0]
    n_f = D_FF // tf
    n_x = tm // x_rows
    assert n_x <= n_f
    return pl.pallas_call(
        functools.partial(_ffn_kernel, n_x=n_x, row_parts=row_parts),
        grid=(m // tm, n_f),
        in_specs=[
            pl.BlockSpec((x_rows, D_MODEL), lambda i, f: (i * n_x + _ffn_x_block(f, n_f, n_x), 0)),
            pl.BlockSpec((tm, D_MODEL), lambda i, f: (i, 0)),
            pl.BlockSpec((1, N_MOD, D_MODEL), lambda i, f: (i * tm // seq, 0, 0)),
            pl.BlockSpec((1, D_MODEL), lambda i, f: (0, 0)),
            pl.BlockSpec((D_MODEL, tf), lambda i, f: (0, f)),
            pl.BlockSpec((D_MODEL, tf), lambda i, f: (0, f)),
            pl.BlockSpec((tf, D_MODEL), lambda i, f: (f, 0)),
        ],
        out_specs=pl.BlockSpec((tm, D_MODEL), lambda i, f: (i, 0)),
        out_shape=jax.ShapeDtypeStruct((m, D_MODEL), F32),
        scratch_shapes=[pltpu.VMEM((tm, D_MODEL), F32)],
        compiler_params=_params(("parallel", "arbitrary"), 58),
        name="ffn",
    )(x, h, mods, g_post, w_gate, w_up, w_down)


def kernel(x_prompt, x_sample, cache_k, cache_v, c, c_ctx, w_ada, b_ada, norm_mix_pre, norm_mix_post,
           norm_ffn_pre, norm_ffn_post, w_in, rpb, ln_v, w_s, b_s, w_pa, w_pb, w_o, w_gate, w_up, w_down):
    assert w_ada.shape[0] == DEPTH == 1
    batch, seq, _ = x_prompt.shape
    dec_batch, dec_seq, _ = x_sample.shape
    past = cache_k.shape[2]

    row = lambda a: a[0].reshape(1, -1)
    bf = lambda a: a[0].astype(BF16)

    cvecs = jnp.concatenate(
        [c, c_ctx[None], jnp.zeros((MOD_ROWS - dec_batch - 1, D_MODEL), F32)], axis=0)
    mods = _modulation(cvecs, w_ada[0], b_ada[0])
    mods_lat = mods[:dec_batch].reshape(dec_batch, N_MOD, D_MODEL)
    mods_ctx = mods[dec_batch:dec_batch + 1].reshape(1, N_MOD, D_MODEL)
    table = _bias_table(rpb[0])

    w_in_b = bf(w_in)
    mix = (bf(w_s), b_s[0].T, bf(w_pa), bf(w_pb), bf(w_o), row(norm_mix_post), row(norm_ffn_pre))
    ffn = (row(norm_ffn_post), bf(w_gate), bf(w_up), bf(w_down))

    xp = x_prompt.reshape(batch * seq, D_MODEL)
    hp, proj_p, k_p, v_p = _proj_qkvu(xp, mods_ctx, batch * seq, row(norm_mix_pre), w_in_b, emit_kv=True)
    gates_p = _proj_gates(hp, row(ln_v), w_in_b)
    oa_p = _ctx_attention(proj_p, seq)
    xp, hp = _merge(xp, mods_ctx, batch * seq, oa_p, proj_p, gates_p, *mix)
    y_prompt = _ffn(xp, hp, mods_ctx, batch * seq, *ffn).reshape(batch, seq, D_MODEL)

    xs = x_sample.reshape(dec_batch * dec_seq, D_MODEL)
    hs, proj_s = _proj_qkvu(xs, mods_lat, dec_seq, row(norm_mix_pre), w_in_b, emit_kv=False)
    gates_s = _proj_gates(hs, row(ln_v), w_in_b)
    oa_s = _lat_attention(proj_s, cache_k.reshape(dec_batch, past * N_HEADS_A, HEAD_DIM),
                          cache_v.reshape(dec_batch, past * N_HEADS_A, HEAD_DIM), table, dec_seq)
    xs, hs = _merge(xs, mods_lat, dec_seq, oa_s, proj_s, gates_s, *mix)
    y_sample = _ffn(xs, hs, mods_lat, dec_seq, *ffn).reshape(dec_batch, dec_seq, D_MODEL)

    state_shape = (batch, DEPTH, seq, N_HEADS_A, HEAD_DIM)
    return y_prompt, y_sample, k_p.reshape(state_shape), v_p.reshape(state_shape)
```

```python
import functools

import jax
import jax.numpy as jnp
from jax import lax
from jax.experimental import pallas as pl
from jax.experimental.pallas import tpu as pltpu

D_MODEL = 2048
DEPTH = 1
GRID_W = 64
N_HEADS_A = 8
HEAD_DIM = 128
D_ATTN = N_HEADS_A * HEAD_DIM
KH_MAX = 8
KW = 16
CHUNK = 128
N_GROUPS_B = 8
D_GMLP = 1024
GROUP_CH = D_GMLP // N_GROUPS_B
D_FF = ((8 * D_MODEL // 3 + 255) // 256) * 256
N_MOD = 6
EPS = 1e-6
ATTN_SCALE = HEAD_DIM ** -0.5
LOG2E = 1.4426950408889634
D_IN = 3 * D_ATTN + 2 * D_GMLP + 2 * D_MODEL

N_DR = 2 * KH_MAX - 1
N_DC = 2 * KW - 1
COL_BLOCK = 1024
MOD_ROWS = 8

F32 = jnp.float32
BF16 = jnp.bfloat16

MIB = 1024 * 1024


def _params(semantics, vmem_mib):
    return pltpu.CompilerParams(dimension_semantics=semantics, vmem_limit_bytes=vmem_mib * MIB)


def _rms_norm(x, g):
    return x * lax.rsqrt(jnp.mean(x * x, axis=-1, keepdims=True) + EPS) * g


def _layer_norm(x, g):
    xc = x - jnp.mean(x, axis=-1, keepdims=True)
    return xc * lax.rsqrt(jnp.mean(xc * xc, axis=-1, keepdims=True) + EPS) * g


def _modulation_kernel(c_ref, w_ref, b_ref, o_ref):
    s = jax.nn.silu(c_ref[...]).astype(BF16)
    o_ref[...] = jnp.dot(s, w_ref[...].astype(BF16), preferred_element_type=F32) + b_ref[...]


def _modulation(cvecs, w_ada, b_ada):
    tn = 1024
    n = N_MOD * D_MODEL
    return pl.pallas_call(
        _modulation_kernel,
        grid=(n // tn,),
        in_specs=[
            pl.BlockSpec((MOD_ROWS, D_MODEL), lambda j: (0, 0)),
            pl.BlockSpec((D_MODEL, tn), lambda j: (0, j)),
            pl.BlockSpec((1, tn), lambda j: (0, j)),
        ],
        out_specs=pl.BlockSpec((MOD_ROWS, tn), lambda j: (0, j)),
        out_shape=jax.ShapeDtypeStruct((MOD_ROWS, n), F32),
        compiler_params=_params(("parallel",), 40),
        name="modulation",
    )(cvecs, w_ada, b_ada.reshape(1, n))


def _bias_table_kernel(rpb_ref, o_ref):
    qc = lax.broadcasted_iota(jnp.int32, (GRID_W, GRID_W), 0)
    kc = lax.broadcasted_iota(jnp.int32, (GRID_W, GRID_W), 1)
    cs = jnp.clip(qc - KW // 2, 0, GRID_W - KW)
    valid = (kc >= cs) & (kc < cs + KW)
    lanes = rpb_ref.shape[-1]
    tiles = []
    for dr in range(N_DR):
        row = jnp.broadcast_to(rpb_ref[0, dr:dr + 1, :], (GRID_W, lanes))
        t = pltpu.roll(row, lanes - (KW - 1), 1, stride=1, stride_axis=0)[:, :GRID_W]
        tiles.append(jnp.where(valid, t * LOG2E, -jnp.inf))
    pad = jnp.zeros((GRID_W, GRID_W), F32)
    o_ref[0, 0] = jnp.concatenate(tiles + [pad], axis=-1)
    o_ref[0, 1] = jnp.concatenate([pad] + tiles, axis=-1)


def _bias_table(rpb):
    width = (N_DR + 1) * GRID_W
    lanes = 128
    rpb_rows = jnp.pad(rpb, ((0, 0), (0, 0), (0, lanes - N_DC)))
    return pl.pallas_call(
        _bias_table_kernel,
        grid=(N_HEADS_A,),
        in_specs=[pl.BlockSpec((1, N_DR, lanes), lambda h: (h, 0, 0))],
        out_specs=pl.BlockSpec((1, 2, GRID_W, width), lambda h: (h, 0, 0, 0)),
        out_shape=jax.ShapeDtypeStruct((N_HEADS_A, 2, GRID_W, width), F32),
        compiler_params=_params(("parallel",), 16),
        name="bias_table",
    )(rpb_rows)


def _row_parts(tm, parts):
    return [slice(p * (tm // parts), (p + 1) * (tm // parts)) for p in range(parts)]


def _proj_qkvu_kernel(x_ref, mod_ref, g_ref, w_ref, h_ref, proj_ref, *kv_refs, row_parts):
    for rows in _row_parts(x_ref.shape[0], row_parts):
        y = _rms_norm(x_ref[rows, :], g_ref[...])
        h = (y * (1.0 + mod_ref[0, 1:2, :]) + mod_ref[0, 0:1, :]).astype(BF16)
        h_ref[rows, :] = h
        for blk in (3, 0, 1, 2):
            cols = slice(blk * COL_BLOCK, (blk + 1) * COL_BLOCK)
            acc = jnp.dot(h, w_ref[:, cols], preferred_element_type=F32)
            if kv_refs and blk in (1, 2):
                n_rows = rows.stop - rows.start
                for head in range(N_HEADS_A):
                    dst = pl.ds(rows.start * N_HEADS_A + head, n_rows, stride=N_HEADS_A)
                    kv_refs[blk - 1][dst, :] = acc[:, head * HEAD_DIM:(head + 1) * HEAD_DIM]
            if blk == 3:
                acc = jax.nn.gelu(acc)
            proj_ref[rows, cols] = acc.astype(BF16)


def _proj_qkvu(x, mods, seq, g, w_in, *, emit_kv, tm=512, row_parts=1):
    m = x.shape[0]
    n = 3 * D_ATTN + D_GMLP
    out_shape = [jax.ShapeDtypeStruct((m, D_MODEL), BF16), jax.ShapeDtypeStruct((m, n), BF16)]
    out_specs = [pl.BlockSpec((tm, D_MODEL), lambda i: (i, 0)), pl.BlockSpec((tm, n), lambda i: (i, 0))]
    if emit_kv:
        out_shape += [jax.ShapeDtypeStruct((m * N_HEADS_A, HEAD_DIM), F32)] * 2
        out_specs += [pl.BlockSpec((tm * N_HEADS_A, HEAD_DIM), lambda i: (i, 0))] * 2
    return pl.pallas_call(
        functools.partial(_proj_qkvu_kernel, row_parts=row_parts),
        grid=(m // tm,),
        in_specs=[
            pl.BlockSpec((tm, D_MODEL), lambda i: (i, 0)),
            pl.BlockSpec((1, N_MOD, D_MODEL), lambda i: (i * tm // seq, 0, 0)),
            pl.BlockSpec((1, D_MODEL), lambda i: (0, 0)),
            pl.BlockSpec((D_MODEL, n), lambda i: (0, 0), pipeline_mode=pl.Buffered(1)),
        ],
        out_specs=out_specs,
        out_shape=out_shape,
        compiler_params=_params(("parallel",), 56),
        name="proj_qkvu_ctx" if emit_kv else "proj_qkvu_lat",
    )(x, mods, g, w_in)


def _proj_gates_kernel(h_ref, lnv_ref, wvb_ref, *rest, row_parts):
    *wgate_refs, proj_ref = rest
    n_gate = len(wgate_refs) * COL_BLOCK
    for rows in _row_parts(h_ref.shape[0], row_parts):
        h = h_ref[rows, :]
        acc = jnp.dot(h, wvb_ref[...], preferred_element_type=F32)
        proj_ref[rows, n_gate:] = _layer_norm(jax.nn.gelu(acc), lnv_ref[...]).astype(BF16)
        for blk, w_ref in enumerate(wgate_refs):
            acc = jnp.dot(h, w_ref[...], preferred_element_type=F32)
            proj_ref[rows, blk * COL_BLOCK:(blk + 1) * COL_BLOCK] = jax.nn.sigmoid(acc).astype(BF16)


def _proj_gates(h, ln_v, w_in, *, tm=512, row_parts=2):
    m = h.shape[0]
    first = (3 * D_ATTN + D_GMLP) // COL_BLOCK
    n_blk = D_IN // COL_BLOCK - first
    w_blk = lambda blk: pl.BlockSpec((D_MODEL, COL_BLOCK), lambda i: (0, blk), pipeline_mode=pl.Buffered(1))
    return pl.pallas_call(
        functools.partial(_proj_gates_kernel, row_parts=row_parts),
        grid=(m // tm,),
        in_specs=[
            pl.BlockSpec((tm, D_MODEL), lambda i: (i, 0)),
            pl.BlockSpec((1, D_GMLP), lambda i: (0, 0)),
        ] + [w_blk(first + b) for b in range(n_blk)],
        out_specs=pl.BlockSpec((tm, n_blk * COL_BLOCK), lambda i: (i, 0)),
        out_shape=jax.ShapeDtypeStruct((m, n_blk * COL_BLOCK), BF16),
        compiler_params=_params(("parallel",), 56),
        name="proj_gates",
    )(h, ln_v, *([w_in] * n_blk))


def _softmax_parts(logits2):
    m = functools.reduce(jnp.maximum, [jnp.max(t, axis=-1, keepdims=True) for t in logits2])
    es = [jnp.exp2(t - m) for t in logits2]
    inv = 1.0 / functools.reduce(jnp.add, [jnp.sum(e, axis=-1, keepdims=True) for e in es])
    return [e.astype(BF16) for e in es], inv


def _qk(q, k):
    return lax.dot_general(q, k, (((1,), (1,)), ((), ())), preferred_element_type=F32) * (ATTN_SCALE * LOG2E)


def _ctx_attn_kernel(q_ref, k_ref, v_ref, o_ref):
    for h in range(N_HEADS_A):
        cols = slice(h * HEAD_DIM, (h + 1) * HEAD_DIM)
        (p,), inv = _softmax_parts([_qk(q_ref[:, cols], k_ref[:, cols])])
        o_ref[:, cols] = (jnp.dot(p, v_ref[:, cols], preferred_element_type=F32) * inv).astype(BF16)


def _ctx_attention(proj, seq):
    m = proj.shape[0]
    spec = lambda col: pl.BlockSpec((seq, D_ATTN), lambda b: (b, col))
    return pl.pallas_call(
        _ctx_attn_kernel,
        grid=(m // seq,),
        in_specs=[spec(0), spec(1), spec(2)],
        out_specs=spec(0),
        out_shape=jax.ShapeDtypeStruct((m, D_ATTN), BF16),
        compiler_params=_params(("parallel",), 32),
        name="ctx_attention",
    )(proj, proj, proj)


def _lat_attn_kernel(q_ref, k_ref, v_ref, kc_ref, vc_ref, tab_ref, o_ref, sw_ref, pw_ref, *, rows, kh):
    h = pl.program_id(1)
    past = kc_ref.shape[1] // N_HEADS_A
    kc = kc_ref[0, pl.ds(h, past, stride=N_HEADS_A), :].astype(BF16)
    vc = vc_ref[0, pl.ds(h, past, stride=N_HEADS_A), :].astype(BF16)
    win = kh * GRID_W
    s_c = _qk(q_ref[...], kc)
    starts = [min(max(r - kh // 2, 0), rows - kh) * GRID_W for r in range(rows)]
    for r, start in enumerate(starts):
        off = start // GRID_W - r + (KH_MAX - 1)
        lane0 = (off + off % 2) * GRID_W
        bias = tab_ref[0, off % 2, :, lane0:lane0 + win]
        q = q_ref[r * GRID_W:(r + 1) * GRID_W, :]
        sw_ref[r * GRID_W:(r + 1) * GRID_W, :] = _qk(q, k_ref[start:start + win, :]) + bias
    (p_w, p_c), inv = _softmax_parts([sw_ref[...], s_c])
    pw_ref[...] = p_w
    o_c = jnp.dot(p_c, vc, preferred_element_type=F32)
    for r, start in enumerate(starts):
        q_rows = slice(r * GRID_W, (r + 1) * GRID_W)
        o_w = jnp.dot(pw_ref[q_rows, :], v_ref[start:start + win, :], preferred_element_type=F32)
        o_ref[q_rows, :] = ((o_w + o_c[q_rows, :]) * inv[q_rows, :]).astype(BF16)


def _lat_attention(proj, cache_k, cache_v, table, seq):
    m = proj.shape[0]
    rows = seq // GRID_W
    kh = min(KH_MAX, rows)
    qkv = lambda part: pl.BlockSpec((seq, HEAD_DIM), lambda b, h: (b, part * N_HEADS_A + h))
    cache = pl.BlockSpec((1,) + cache_k.shape[1:], lambda b, h: (b, 0, 0))
    return pl.pallas_call(
        functools.partial(_lat_attn_kernel, rows=rows, kh=kh),
        grid=(m // seq, N_HEADS_A),
        in_specs=[qkv(0), qkv(1), qkv(2), cache, cache,
                  pl.BlockSpec((1,) + table.shape[1:], lambda b, h: (h, 0, 0, 0))],
        out_specs=pl.BlockSpec((seq, HEAD_DIM), lambda b, h: (b, h)),
        out_shape=jax.ShapeDtypeStruct((m, D_ATTN), BF16),
        scratch_shapes=[pltpu.VMEM((seq, kh * GRID_W), F32), pltpu.VMEM((seq, kh * GRID_W), BF16)],
        compiler_params=_params(("parallel", "arbitrary"), 40),
        name="lat_attention",
    )(proj, proj, proj, cache_k, cache_v, table)


def _merge_kernel(x_ref, mod_ref, oa_ref, gu_ref, vn_ref, ga_ref, gb_ref,
                  ws_ref, bst_ref, wpa_ref, wpb_ref, wo_ref, g_ref, gffn_ref, o_ref, h2_ref,
                  ob_ref, *, row_parts):
    for part in _row_parts(x_ref.shape[0], row_parts):
        for c in range(part.start // CHUNK, part.stop // CHUNK):
            rows = slice(c * CHUNK, (c + 1) * CHUNK)
            for g in range(N_GROUPS_B):
                cols = slice(g * GROUP_CH, (g + 1) * GROUP_CH)
                s = jnp.dot(ws_ref[g], vn_ref[rows, cols], preferred_element_type=F32) + bst_ref[:, g:g + 1]
                ob_ref[rows, cols] = (gu_ref[rows, cols].astype(F32) * s).astype(BF16)
        pa = jnp.dot(oa_ref[part, :], wpa_ref[...], preferred_element_type=F32)
        pb = jnp.dot(ob_ref[part, :], wpb_ref[...], preferred_element_type=F32)
        mixed = (ga_ref[part, :].astype(F32) * pa + gb_ref[part, :].astype(F32) * pb).astype(BF16)
        y = jnp.dot(mixed, wo_ref[...], preferred_element_type=F32)
        x1 = x_ref[part, :] + mod_ref[0, 2:3, :] * _rms_norm(y, g_ref[...])
        o_ref[part, :] = x1
        h2 = _rms_norm(x1, gffn_ref[...]) * (1.0 + mod_ref[0, 4:5, :]) + mod_ref[0, 3:4, :]
        h2_ref[part, :] = h2.astype(BF16)


def _merge(x, mods, seq, o_a, proj_qkvu, proj_gates, w_s, b_s_t, w_pa, w_pb, w_o, g, g_ffn, *,
           tm=512, row_parts=2):
    m = x.shape[0]
    assert (tm // row_parts) % CHUNK == 0
    whole = lambda a: pl.BlockSpec(a.shape, lambda i: (0,) * a.ndim, pipeline_mode=pl.Buffered(1))
    return pl.pallas_call(
        functools.partial(_merge_kernel, row_parts=row_parts),
        grid=(m // tm,),
        in_specs=[
            pl.BlockSpec((tm, D_MODEL), lambda i: (i, 0)),
            pl.BlockSpec((1, N_MOD, D_MODEL), lambda i: (i * tm // seq, 0, 0)),
            pl.BlockSpec((tm, D_ATTN), lambda i: (i, 0)),
            pl.BlockSpec((tm, D_GMLP), lambda i: (i, 3)),
            pl.BlockSpec((tm, D_GMLP), lambda i: (i, 4)),
            pl.BlockSpec((tm, D_MODEL), lambda i: (i, 0)),
            pl.BlockSpec((tm, D_MODEL), lambda i: (i, 1)),
            whole(w_s), whole(b_s_t), whole(w_pa), whole(w_pb), whole(w_o), whole(g), whole(g_ffn),
        ],
        out_specs=[pl.BlockSpec((tm, D_MODEL), lambda i: (i, 0))] * 2,
        out_shape=[jax.ShapeDtypeStruct((m, D_MODEL), F32), jax.ShapeDtypeStruct((m, D_MODEL), BF16)],
        scratch_shapes=[pltpu.VMEM((tm, D_GMLP), BF16)],
        compiler_params=_params(("parallel",), 60),
        name="merge",
    )(x, mods, o_a, proj_qkvu, proj_gates, proj_gates, proj_gates,
      w_s, b_s_t, w_pa, w_pb, w_o, g, g_ffn)


def _ffn_x_block(f, n_f, n_x):
    return jnp.clip(f - (n_f - n_x), 0, n_x - 1)


def _ffn_kernel(x_ref, h_ref, mod_ref, gpost_ref, wg_ref, wu_ref, wd_ref, o_ref, acc_ref, *, n_x, row_parts):
    f = pl.program_id(1)
    x_rows = x_ref.shape[0]
    tm = h_ref.shape[0]

    @pl.when(f == 0)
    def _():
        acc_ref[...] = jnp.zeros_like(acc_ref)

    row0 = pl.multiple_of(_ffn_x_block(f, pl.num_programs(1), n_x) * x_rows, x_rows)
    o_ref[pl.ds(row0, x_rows), :] = x_ref[...]

    for part in range(row_parts):
        rows = slice(part * (tm // row_parts), (part + 1) * (tm // row_parts))
        h = h_ref[rows, :]
        gate = jnp.dot(h, wg_ref[...], preferred_element_type=F32)
        up = jnp.dot(h, wu_ref[...], preferred_element_type=F32)
        act = (jax.nn.silu(gate) * up).astype(BF16)
        acc_ref[rows, :] += jnp.dot(act, wd_ref[...], preferred_element_type=F32)

    @pl.when(f == pl.num_programs(1) - 1)
    def _():
        o_ref[...] = o_ref[...] + mod_ref[0, 5:6, :] * _rms_norm(acc_ref[...], gpost_ref[...])


def _ffn(x, h, mods, seq, g_post, w_gate, w_up, w_down, *, tm=1024, tf=512, x_rows=128, row_parts=2):
    m = x.shape[0]
    n_f = D_FF // tf
    n_x = tm // x_rows
    assert n_x <= n_f
    return pl.pallas_call(
        functools.partial(_ffn_kernel, n_x=n_x, row_parts=row_parts),
        grid=(m // tm, n_f),
        in_specs=[
            pl.BlockSpec((x_rows, D_MODEL), lambda i, f: (i * n_x + _ffn_x_block(f, n_f, n_x), 0)),
            pl.BlockSpec((tm, D_MODEL), lambda i, f: (i, 0)),
            pl.BlockSpec((1, N_MOD, D_MODEL), lambda i, f: (i * tm // seq, 0, 0)),
            pl.BlockSpec((1, D_MODEL), lambda i, f: (0, 0)),
            pl.BlockSpec((D_MODEL, tf), lambda i, f: (0, f)),
            pl.BlockSpec((D_MODEL, tf), lambda i, f: (0, f)),
            pl.BlockSpec((tf, D_MODEL), lambda i, f: (f, 0)),
        ],
        out_specs=pl.BlockSpec((tm, D_MODEL), lambda i, f: (i, 0)),
        out_shape=jax.ShapeDtypeStruct((m, D_MODEL), F32),
        scratch_shapes=[pltpu.VMEM((tm, D_MODEL), F32)],
        compiler_params=_params(("parallel", "arbitrary"), 58),
        name="ffn",
    )(x, h, mods, g_post, w_gate, w_up, w_down)


def kernel(x_prompt, x_sample, cache_k, cache_v, c, c_ctx, w_ada, b_ada, norm_mix_pre, norm_mix_post,
           norm_ffn_pre, norm_ffn_post, w_in, rpb, ln_v, w_s, b_s, w_pa, w_pb, w_o, w_gate, w_up, w_down):
    assert w_ada.shape[0] == DEPTH == 1
    batch, seq, _ = x_prompt.shape
    dec_batch, dec_seq, _ = x_sample.shape
    past = cache_k.shape[2]

    row = lambda a: a[0].reshape(1, -1)
    bf = lambda a: a[0].astype(BF16)

    cvecs = jnp.concatenate(
        [c, c_ctx[None], jnp.zeros((MOD_ROWS - dec_batch - 1, D_MODEL), F32)], axis=0)
    mods = _modulation(cvecs, w_ada[0], b_ada[0])
    mods_lat = mods[:dec_batch].reshape(dec_batch, N_MOD, D_MODEL)
    mods_ctx = mods[dec_batch:dec_batch + 1].reshape(1, N_MOD, D_MODEL)
    table = _bias_table(rpb[0])

    w_in_b = bf(w_in)
    mix = (bf(w_s), b_s[0].T, bf(w_pa), bf(w_pb), bf(w_o), row(norm_mix_post), row(norm_ffn_pre))
    ffn = (row(norm_ffn_post), bf(w_gate), bf(w_up), bf(w_down))

    xp = x_prompt.reshape(batch * seq, D_MODEL)
    hp, proj_p, k_p, v_p = _proj_qkvu(xp, mods_ctx, batch * seq, row(norm_mix_pre), w_in_b, emit_kv=True)
    gates_p = _proj_gates(hp, row(ln_v), w_in_b)
    oa_p = _ctx_attention(proj_p, seq)
    xp, hp = _merge(xp, mods_ctx, batch * seq, oa_p, proj_p, gates_p, *mix)
    y_prompt = _ffn(xp, hp, mods_ctx, batch * seq, *ffn).reshape(batch, seq, D_MODEL)

    xs = x_sample.reshape(dec_batch * dec_seq, D_MODEL)
    hs, proj_s = _proj_qkvu(xs, mods_lat, dec_seq, row(norm_mix_pre), w_in_b, emit_kv=False)
    gates_s = _proj_gates(hs, row(ln_v), w_in_b)
    oa_s = _lat_attention(proj_s, cache_k.reshape(dec_batch, past * N_HEADS_A, HEAD_DIM),
                          cache_v.reshape(dec_batch, past * N_HEADS_A, HEAD_DIM), table, dec_seq)
    xs, hs = _merge(xs, mods_lat, dec_seq, oa_s, proj_s, gates_s, *mix)
    y_sample = _ffn(xs, hs, mods_lat, dec_seq, *ffn).reshape(dec_batch, dec_seq, D_MODEL)

    state_shape = (batch, DEPTH, seq, N_HEADS_A, HEAD_DIM)
    return y_prompt, y_sample, k_p.reshape(state_shape), v_p.reshape(state_shape)
```

```python
import functools

import jax
import jax.numpy as jnp
from jax import lax
from jax.experimental import pallas as pl
from jax.experimental.pallas import tpu as pltpu

D_MODEL = 2048
DEPTH = 1
GRID_W = 64
N_HEADS_A = 8
HEAD_DIM = 128
D_ATTN = N_HEADS_A * HEAD_DIM
KH_MAX = 8
KW = 16
CHUNK = 128
N_GROUPS_B = 8
D_GMLP = 1024
GROUP_CH = D_GMLP // N_GROUPS_B
D_FF = ((8 * D_MODEL // 3 + 255) // 256) * 256
N_MOD = 6
EPS = 1e-6
ATTN_SCALE = HEAD_DIM ** -0.5
LOG2E = 1.4426950408889634
D_IN = 3 * D_ATTN + 2 * D_GMLP + 2 * D_MODEL

N_DR = 2 * KH_MAX - 1
N_DC = 2 * KW - 1
COL_BLOCK = 1024
MOD_ROWS = 8

F32 = jnp.float32
BF16 = jnp.bfloat16

MIB = 1024 * 1024


def _params(semantics, vmem_mib):
    return pltpu.CompilerParams(dimension_semantics=semantics, vmem_limit_bytes=vmem_mib * MIB)


def _rms_norm(x, g):
    return x * lax.rsqrt(jnp.mean(x * x, axis=-1, keepdims=True) + EPS) * g


def _layer_norm(x, g):
    xc = x - jnp.mean(x, axis=-1, keepdims=True)
    return xc * lax.rsqrt(jnp.mean(xc * xc, axis=-1, keepdims=True) + EPS) * g


def _modulation_kernel(c_ref, w_ref, b_ref, o_ref):
    s = jax.nn.silu(c_ref[...]).astype(BF16)
    o_ref[...] = jnp.dot(s, w_ref[...].astype(BF16), preferred_element_type=F32) + b_ref[...]


def _modulation(cvecs, w_ada, b_ada):
    tn = 1024
    n = N_MOD * D_MODEL
    return pl.pallas_call(
        _modulation_kernel,
        grid=(n // tn,),
        in_specs=[
            pl.BlockSpec((MOD_ROWS, D_MODEL), lambda j: (0, 0)),
            pl.BlockSpec((D_MODEL, tn), lambda j: (0, j)),
            pl.BlockSpec((1, tn), lambda j: (0, j)),
        ],
        out_specs=pl.BlockSpec((MOD_ROWS, tn), lambda j: (0, j)),
        out_shape=jax.ShapeDtypeStruct((MOD_ROWS, n), F32),
        compiler_params=_params(("parallel",), 40),
        name="modulation",
    )(cvecs, w_ada, b_ada.reshape(1, n))


def _bias_table_kernel(rpb_ref, o_ref):
    qc = lax.broadcasted_iota(jnp.int32, (GRID_W, GRID_W), 0)
    kc = lax.broadcasted_iota(jnp.int32, (GRID_W, GRID_W), 1)
    cs = jnp.clip(qc - KW // 2, 0, GRID_W - KW)
    valid = (kc >= cs) & (kc < cs + KW)
    lanes = rpb_ref.shape[-1]
    tiles = []
    for dr in range(N_DR):
        row = jnp.broadcast_to(rpb_ref[0, dr:dr + 1, :], (GRID_W, lanes))
        t = pltpu.roll(row, lanes - (KW - 1), 1, stride=1, stride_axis=0)[:, :GRID_W]
        tiles.append(jnp.where(valid, t * LOG2E, -jnp.inf))
    pad = jnp.zeros((GRID_W, GRID_W), F32)
    o_ref[0, 0] = jnp.concatenate(tiles + [pad], axis=-1)
    o_ref[0, 1] = jnp.concatenate([pad] + tiles, axis=-1)


def _bias_table(rpb):
    width = (N_DR + 1) * GRID_W
    lanes = 128
    rpb_rows = jnp.pad(rpb, ((0, 0), (0, 0), (0, lanes - N_DC)))
    return pl.pallas_call(
        _bias_table_kernel,
        grid=(N_HEADS_A,),
        in_specs=[pl.BlockSpec((1, N_DR, lanes), lambda h: (h, 0, 0))],
        out_specs=pl.BlockSpec((1, 2, GRID_W, width), lambda h: (h, 0, 0, 0)),
        out_shape=jax.ShapeDtypeStruct((N_HEADS_A, 2, GRID_W, width), F32),
        compiler_params=_params(("parallel",), 16),
        name="bias_table",
    )(rpb_rows)


def _row_parts(tm, parts):
    return [slice(p * (tm // parts), (p + 1) * (tm // parts)) for p in range(parts)]


def _proj_qkvu_kernel(x_ref, mod_ref, g_ref, w_ref, h_ref, proj_ref, *kv_refs, row_parts):
    for rows in _row_parts(x_ref.shape[0], row_parts):
        y = _rms_norm(x_ref[rows, :], g_ref[...])
        h = (y * (1.0 + mod_ref[0, 1:2, :]) + mod_ref[0, 0:1, :]).astype(BF16)
        h_ref[rows, :] = h
        for blk in (3, 0, 1, 2):
            cols = slice(blk * COL_BLOCK, (blk + 1) * COL_BLOCK)
            acc = jnp.dot(h, w_ref[:, cols], preferred_element_type=F32)
            if kv_refs and blk in (1, 2):
                n_rows = rows.stop - rows.start
                for head in range(N_HEADS_A):
                    dst = pl.ds(rows.start * N_HEADS_A + head, n_rows, stride=N_HEADS_A)
                    kv_refs[blk - 1][dst, :] = acc[:, head * HEAD_DIM:(head + 1) * HEAD_DIM]
            if blk == 3:
                acc = jax.nn.gelu(acc)
            proj_ref[rows, cols] = acc.astype(BF16)


def _proj_qkvu(x, mods, seq, g, w_in, *, emit_kv, tm=512, row_parts=1):
    m = x.shape[0]
    n = 3 * D_ATTN + D_GMLP
    out_shape = [jax.ShapeDtypeStruct((m, D_MODEL), BF16), jax.ShapeDtypeStruct((m, n), BF16)]
    out_specs = [pl.BlockSpec((tm, D_MODEL), lambda i: (i, 0)), pl.BlockSpec((tm, n), lambda i: (i, 0))]
    if emit_kv:
        out_shape += [jax.ShapeDtypeStruct((m * N_HEADS_A, HEAD_DIM), F32)] * 2
        out_specs += [pl.BlockSpec((tm * N_HEADS_A, HEAD_DIM), lambda i: (i, 0))] * 2
    return pl.pallas_call(
        functools.partial(_proj_qkvu_kernel, row_parts=row_parts),
        grid=(m // tm,),
        in_specs=[
            pl.BlockSpec((tm, D_MODEL), lambda i: (i, 0)),
            pl.BlockSpec((1, N_MOD, D_MODEL), lambda i: (i * tm // seq, 0, 0)),
            pl.BlockSpec((1, D_MODEL), lambda i: (0, 0)),
            pl.BlockSpec((D_MODEL, n), lambda i: (0, 0), pipeline_mode=pl.Buffered(1)),
        ],
        out_specs=out_specs,
        out_shape=out_shape,
        compiler_params=_params(("parallel",), 56),
        name="proj_qkvu_ctx" if emit_kv else "proj_qkvu_lat",
    )(x, mods, g, w_in)


def _proj_gates_kernel(h_ref, lnv_ref, wvb_ref, *rest, row_parts):
    *wgate_refs, proj_ref = rest
    n_gate = len(wgate_refs) * COL_BLOCK
    for rows in _row_parts(h_ref.shape[0], row_parts):
        h = h_ref[rows, :]
        acc = jnp.dot(h, wvb_ref[...], preferred_element_type=F32)
        proj_ref[rows, n_gate:] = _layer_norm(jax.nn.gelu(acc), lnv_ref[...]).astype(BF16)
        for blk, w_ref in enumerate(wgate_refs):
            acc = jnp.dot(h, w_ref[...], preferred_element_type=F32)
            proj_ref[rows, blk * COL_BLOCK:(blk + 1) * COL_BLOCK] = jax.nn.sigmoid(acc).astype(BF16)


def _proj_gates(h, ln_v, w_in, *, tm=512, row_parts=2):
    m = h.shape[0]
    first = (3 * D_ATTN + D_GMLP) // COL_BLOCK
    n_blk = D_IN // COL_BLOCK - first
    w_blk = lambda blk: pl.BlockSpec((D_MODEL, COL_BLOCK), lambda i: (0, blk), pipeline_mode=pl.Buffered(1))
    return pl.pallas_call(
        functools.partial(_proj_gates_kernel, row_parts=row_parts),
        grid=(m // tm,),
        in_specs=[
            pl.BlockSpec((tm, D_MODEL), lambda i: (i, 0)),
            pl.BlockSpec((1, D_GMLP), lambda i: (0, 0)),
        ] + [w_blk(first + b) for b in range(n_blk)],
        out_specs=pl.BlockSpec((tm, n_blk * COL_BLOCK), lambda i: (i, 0)),
        out_shape=jax.ShapeDtypeStruct((m, n_blk * COL_BLOCK), BF16),
        compiler_params=_params(("parallel",), 56),
        name="proj_gates",
    )(h, ln_v, *([w_in] * n_blk))


def _softmax_parts(logits2):
    m = functools.reduce(jnp.maximum, [jnp.max(t, axis=-1, keepdims=True) for t in logits2])
    es = [jnp.exp2(t - m) for t in logits2]
    inv = 1.0 / functools.reduce(jnp.add, [jnp.sum(e, axis=-1, keepdims=True) for e in es])
    return [e.astype(BF16) for e in es], inv


def _qk(q, k):
    return lax.dot_general(q, k, (((1,), (1,)), ((), ())), preferred_element_type=F32) * (ATTN_SCALE * LOG2E)


def _ctx_attn_kernel(q_ref, k_ref, v_ref, o_ref):
    for h in range(N_HEADS_A):
        cols = slice(h * HEAD_DIM, (h + 1) * HEAD_DIM)
        (p,), inv = _softmax_parts([_qk(q_ref[:, cols], k_ref[:, cols])])
        o_ref[:, cols] = (jnp.dot(p, v_ref[:, cols], preferred_element_type=F32) * inv).astype(BF16)


def _ctx_attention(proj, seq):
    m = proj.shape[0]
    spec = lambda col: pl.BlockSpec((seq, D_ATTN), lambda b: (b, col))
    return pl.pallas_call(
        _ctx_attn_kernel,
        grid=(m // seq,),
        in_specs=[spec(0), spec(1), spec(2)],
        out_specs=spec(0),
        out_shape=jax.ShapeDtypeStruct((m, D_ATTN), BF16),
        compiler_params=_params(("parallel",), 32),
        name="ctx_attention",
    )(proj, proj, proj)


def _lat_attn_kernel(q_ref, k_ref, v_ref, kc_ref, vc_ref, tab_ref, o_ref, sw_ref, pw_ref, *, rows, kh):
    heads = tab_ref.shape[0]
    past = kc_ref.shape[1] // N_HEADS_A
    win = kh * GRID_W
    starts = [min(max(r - kh // 2, 0), rows - kh) * GRID_W for r in range(rows)]
    for j in range(heads):
        head = pl.program_id(1) * heads + j
        cols = slice(j * HEAD_DIM, (j + 1) * HEAD_DIM)
        sw, pw = sw_ref.at[j % 2], pw_ref.at[j % 2]
        kc = kc_ref[0, pl.ds(head, past, stride=N_HEADS_A), :].astype(BF16)
        vc = vc_ref[0, pl.ds(head, past, stride=N_HEADS_A), :].astype(BF16)
        s_c = _qk(q_ref[:, cols], kc)
        for r, start in enumerate(starts):
            off = start // GRID_W - r + (KH_MAX - 1)
            lane0 = (off + off % 2) * GRID_W
            bias = tab_ref[j, off % 2, :, lane0:lane0 + win]
            q = q_ref[r * GRID_W:(r + 1) * GRID_W, cols]
            sw[r * GRID_W:(r + 1) * GRID_W, :] = _qk(q, k_ref[start:start + win, cols]) + bias
        (p_w, p_c), inv = _softmax_parts([sw[...], s_c])
        pw[...] = p_w
        o_c = jnp.dot(p_c, vc, preferred_element_type=F32)
        for r, start in enumerate(starts):
            q_rows = slice(r * GRID_W, (r + 1) * GRID_W)
            o_w = jnp.dot(pw[q_rows, :], v_ref[start:start + win, cols], preferred_element_type=F32)
            o_ref[q_rows, cols] = ((o_w + o_c[q_rows, :]) * inv[q_rows, :]).astype(BF16)


def _lat_attention(proj, cache_k, cache_v, table, seq, *, heads_per_step=4):
    m = proj.shape[0]
    rows = seq // GRID_W
    kh = min(KH_MAX, rows)
    n_hp = N_HEADS_A // heads_per_step
    width = heads_per_step * HEAD_DIM
    qkv = lambda part: pl.BlockSpec((seq, width), lambda b, hp: (b, part * n_hp + hp))
    cache = pl.BlockSpec((1,) + cache_k.shape[1:], lambda b, hp: (b, 0, 0))
    return pl.pallas_call(
        functools.partial(_lat_attn_kernel, rows=rows, kh=kh),
        grid=(m // seq, n_hp),
        in_specs=[qkv(0), qkv(1), qkv(2), cache, cache,
                  pl.BlockSpec((heads_per_step,) + table.shape[1:], lambda b, hp: (hp, 0, 0, 0))],
        out_specs=pl.BlockSpec((seq, width), lambda b, hp: (b, hp)),
        out_shape=jax.ShapeDtypeStruct((m, D_ATTN), BF16),
        scratch_shapes=[pltpu.VMEM((2, seq, kh * GRID_W), F32), pltpu.VMEM((2, seq, kh * GRID_W), BF16)],
        compiler_params=_params(("parallel", "arbitrary"), 48),
        name="lat_attention",
    )(proj, proj, proj, cache_k, cache_v, table)


def _merge_kernel(x_ref, mod_ref, oa_ref, gu_ref, vn_ref, ga_ref, gb_ref,
                  ws_ref, bst_ref, wpa_ref, wpb_ref, wo_ref, g_ref, gffn_ref, o_ref, h2_ref,
                  ob_ref, *, row_parts):
    for part in _row_parts(x_ref.shape[0], row_parts):
        for c in range(part.start // CHUNK, part.stop // CHUNK):
            rows = slice(c * CHUNK, (c + 1) * CHUNK)
            for g in range(N_GROUPS_B):
                cols = slice(g * GROUP_CH, (g + 1) * GROUP_CH)
                s = jnp.dot(ws_ref[g], vn_ref[rows, cols], preferred_element_type=F32) + bst_ref[:, g:g + 1]
                ob_ref[rows, cols] = (gu_ref[rows, cols].astype(F32) * s).astype(BF16)
        pa = jnp.dot(oa_ref[part, :], wpa_ref[...], preferred_element_type=F32)
        pb = jnp.dot(ob_ref[part, :], wpb_ref[...], preferred_element_type=F32)
        mixed = (ga_ref[part, :].astype(F32) * pa + gb_ref[part, :].astype(F32) * pb).astype(BF16)
        y = jnp.dot(mixed, wo_ref[...], preferred_element_type=F32)
        x1 = x_ref[part, :] + mod_ref[0, 2:3, :] * _rms_norm(y, g_ref[...])
        o_ref[part, :] = x1
        h2 = _rms_norm(x1, gffn_ref[...]) * (1.0 + mod_ref[0, 4:5, :]) + mod_ref[0, 3:4, :]
        h2_ref[part, :] = h2.astype(BF16)


def _merge(x, mods, seq, o_a, proj_qkvu, proj_gates, w_s, b_s_t, w_pa, w_pb, w_o, g, g_ffn, *,
           tm=512, row_parts=2):
    m = x.shape[0]
    assert (tm // row_parts) % CHUNK == 0
    whole = lambda a: pl.BlockSpec(a.shape, lambda i: (0,) * a.ndim, pipeline_mode=pl.Buffered(1))
    return pl.pallas_call(
        functools.partial(_merge_kernel, row_parts=row_parts),
        grid=(m // tm,),
        in_specs=[
            pl.BlockSpec((tm, D_MODEL), lambda i: (i, 0)),
            pl.BlockSpec((1, N_MOD, D_MODEL), lambda i: (i * tm // seq, 0, 0)),
            pl.BlockSpec((tm, D_ATTN), lambda i: (i, 0)),
            pl.BlockSpec((tm, D_GMLP), lambda i: (i, 3)),
            pl.BlockSpec((tm, D_GMLP), lambda i: (i, 4)),
            pl.BlockSpec((tm, D_MODEL), lambda i: (i, 0)),
            pl.BlockSpec((tm, D_MODEL), lambda i: (i, 1)),
            whole(w_s), whole(b_s_t), whole(w_pa), whole(w_pb), whole(w_o), whole(g), whole(g_ffn),
        ],
        out_specs=[pl.BlockSpec((tm, D_MODEL), lambda i: (i, 0))] * 2,
        out_shape=[jax.ShapeDtypeStruct((m, D_MODEL), F32), jax.ShapeDtypeStruct((m, D_MODEL), BF16)],
        scratch_shapes=[pltpu.VMEM((tm, D_GMLP), BF16)],
        compiler_params=_params(("parallel",), 60),
        name="merge",
    )(x, mods, o_a, proj_qkvu, proj_gates, proj_gates, proj_gates,
      w_s, b_s_t, w_pa, w_pb, w_o, g, g_ffn)


def _ffn_x_block(f, n_f, n_x):
    return jnp.clip(f - (n_f - n_x), 0, n_x - 1)


def _ffn_kernel(x_ref, h_ref, mod_ref, gpost_ref, wg_ref, wu_ref, wd_ref, o_ref, acc_ref, *, n_x, row_parts):
    f = pl.program_id(1)
    x_rows = x_ref.shape[0]
    tm = h_ref.shape[0]

    @pl.when(f == 0)
    def _():
        acc_ref[...] = jnp.zeros_like(acc_ref)

    row0 = pl.multiple_of(_ffn_x_block(f, pl.num_programs(1), n_x) * x_rows, x_rows)
    o_ref[pl.ds(row0, x_rows), :] = x_ref[...]

    for part in range(row_parts):
        rows = slice(part * (tm // row_parts), (part + 1) * (tm // row_parts))
        h = h_ref[rows, :]
        gate = jnp.dot(h, wg_ref[0], preferred_element_type=F32)
        up = jnp.dot(h, wu_ref[0], preferred_element_type=F32)
        act = (jax.nn.silu(gate) * up).astype(BF16)
        acc_ref[rows, :] += jnp.dot(act, wd_ref[...], preferred_element_type=F32)

    @pl.when(f == pl.num_programs(1) - 1)
    def _():
        o_ref[...] = o_ref[...] + mod_ref[0, 5:6, :] * _rms_norm(acc_ref[...], gpost_ref[...])


def _ffn(x, h, mods, seq, g_post, w_gate, w_up, w_down, *, tm=1024, x_rows=128, row_parts=2):
    m = x.shape[0]
    n_f, _, tf = w_gate.shape
    n_x = tm // x_rows
    assert n_x <= n_f
    return pl.pallas_call(
        functools.partial(_ffn_kernel, n_x=n_x, row_parts=row_parts),
        grid=(m // tm, n_f),
        in_specs=[
            pl.BlockSpec((x_rows, D_MODEL), lambda i, f: (i * n_x + _ffn_x_block(f, n_f, n_x), 0)),
            pl.BlockSpec((tm, D_MODEL), lambda i, f: (i, 0)),
            pl.BlockSpec((1, N_MOD, D_MODEL), lambda i, f: (i * tm // seq, 0, 0)),
            pl.BlockSpec((1, D_MODEL), lambda i, f: (0, 0)),
            pl.BlockSpec((1, D_MODEL, tf), lambda i, f: (f, 0, 0)),
            pl.BlockSpec((1, D_MODEL, tf), lambda i, f: (f, 0, 0)),
            pl.BlockSpec((tf, D_MODEL), lambda i, f: (f, 0)),
        ],
        out_specs=pl.BlockSpec((tm, D_MODEL), lambda i, f: (i, 0)),
        out_shape=jax.ShapeDtypeStruct((m, D_MODEL), F32),
        scratch_shapes=[pltpu.VMEM((tm, D_MODEL), F32)],
        compiler_params=_params(("parallel", "arbitrary"), 58),
        name="ffn",
    )(x, h, mods, g_post, w_gate, w_up, w_down)


def kernel(x_prompt, x_sample, cache_k, cache_v, c, c_ctx, w_ada, b_ada, norm_mix_pre, norm_mix_post,
           norm_ffn_pre, norm_ffn_post, w_in, rpb, ln_v, w_s, b_s, w_pa, w_pb, w_o, w_gate, w_up, w_down):
    assert w_ada.shape[0] == DEPTH == 1
    batch, seq, _ = x_prompt.shape
    dec_batch, dec_seq, _ = x_sample.shape
    past = cache_k.shape[2]

    row = lambda a: a[0].reshape(1, -1)
    bf = lambda a: a[0].astype(BF16)

    cvecs = jnp.concatenate(
        [c, c_ctx[None], jnp.zeros((MOD_ROWS - dec_batch - 1, D_MODEL), F32)], axis=0)
    mods = _modulation(cvecs, w_ada[0], b_ada[0])
    mods_lat = mods[:dec_batch].reshape(dec_batch, N_MOD, D_MODEL)
    mods_ctx = mods[dec_batch:dec_batch + 1].reshape(1, N_MOD, D_MODEL)
    table = _bias_table(rpb[0])

    w_in_b = bf(w_in)
    mix = (bf(w_s), b_s[0].T, bf(w_pa), bf(w_pb), bf(w_o), row(norm_mix_post), row(norm_ffn_pre))
    ffn_tf = 512
    col_blocks = lambda a: a[0].reshape(D_MODEL, D_FF // ffn_tf, ffn_tf).transpose(1, 0, 2).astype(BF16)
    ffn = (row(norm_ffn_post), col_blocks(w_gate), col_blocks(w_up), bf(w_down))

    xp = x_prompt.reshape(batch * seq, D_MODEL)
    hp, proj_p, k_p, v_p = _proj_qkvu(xp, mods_ctx, batch * seq, row(norm_mix_pre), w_in_b, emit_kv=True)
    gates_p = _proj_gates(hp, row(ln_v), w_in_b)
    oa_p = _ctx_attention(proj_p, seq)
    xp, hp = _merge(xp, mods_ctx, batch * seq, oa_p, proj_p, gates_p, *mix)
    y_prompt = _ffn(xp, hp, mods_ctx, batch * seq, *ffn).reshape(batch, seq, D_MODEL)

    xs = x_sample.reshape(dec_batch * dec_seq, D_MODEL)
    hs, proj_s = _proj_qkvu(xs, mods_lat, dec_seq, row(norm_mix_pre), w_in_b, emit_kv=False)
    gates_s = _proj_gates(hs, row(ln_v), w_in_b)
    oa_s = _lat_attention(proj_s, cache_k.reshape(dec_batch, past * N_HEADS_A, HEAD_DIM),
                          cache_v.reshape(dec_batch, past * N_HEADS_A, HEAD_DIM), table, dec_seq)
    xs, hs = _merge(xs, mods_lat, dec_seq, oa_s, proj_s, gates_s, *mix)
    y_sample = _ffn(xs, hs, mods_lat, dec_seq, *ffn).reshape(dec_batch, dec_seq, D_MODEL)

    state_shape = (batch, DEPTH, seq, N_HEADS_A, HEAD_DIM)
    return y_prompt, y_sample, k_p.reshape(state_shape), v_p.reshape(state_shape)
```

```python
import functools

import jax
import jax.numpy as jnp
from jax import lax
from jax.experimental import pallas as pl
from jax.experimental.pallas import tpu as pltpu

D_MODEL = 2048
DEPTH = 1
GRID_W = 64
N_HEADS_A = 8
HEAD_DIM = 128
D_ATTN = N_HEADS_A * HEAD_DIM
KH_MAX = 8
KW = 16
CHUNK = 128
N_GROUPS_B = 8
D_GMLP = 1024
GROUP_CH = D_GMLP // N_GROUPS_B
D_FF = ((8 * D_MODEL // 3 + 255) // 256) * 256
N_MOD = 6
EPS = 1e-6
ATTN_SCALE = HEAD_DIM ** -0.5
LOG2E = 1.4426950408889634
D_IN = 3 * D_ATTN + 2 * D_GMLP + 2 * D_MODEL

N_DR = 2 * KH_MAX - 1
N_DC = 2 * KW - 1
COL_BLOCK = 1024
SUBLANES = 8
MOD_ROWS = SUBLANES

F32 = jnp.float32
BF16 = jnp.bfloat16

MIB = 1024 * 1024


def _params(semantics, vmem_mib):
    return pltpu.CompilerParams(dimension_semantics=semantics, vmem_limit_bytes=vmem_mib * MIB)


def _rms_norm(x, g):
    return x * lax.rsqrt(jnp.mean(x * x, axis=-1, keepdims=True) + EPS) * g


def _layer_norm(x, g):
    xc = x - jnp.mean(x, axis=-1, keepdims=True)
    return xc * lax.rsqrt(jnp.mean(xc * xc, axis=-1, keepdims=True) + EPS) * g


def _modulation_kernel(c_ref, w_ref, b_ref, o_ref):
    s = jax.nn.silu(c_ref[...]).astype(BF16)
    o_ref[...] = jnp.dot(s, w_ref[...].astype(BF16), preferred_element_type=F32) + b_ref[...]


def _modulation(cvecs, w_ada, b_ada):
    tn = 1024
    n = N_MOD * D_MODEL
    return pl.pallas_call(
        _modulation_kernel,
        grid=(n // tn,),
        in_specs=[
            pl.BlockSpec((MOD_ROWS, D_MODEL), lambda j: (0, 0)),
            pl.BlockSpec((D_MODEL, tn), lambda j: (0, j)),
            pl.BlockSpec((1, tn), lambda j: (0, j)),
        ],
        out_specs=pl.BlockSpec((MOD_ROWS, tn), lambda j: (0, j)),
        out_shape=jax.ShapeDtypeStruct((MOD_ROWS, n), F32),
        compiler_params=_params(("parallel",), 40),
        name="modulation",
    )(cvecs, w_ada, b_ada.reshape(1, n))


def _bias_table_kernel(rpb_ref, o_ref):
    qc = lax.broadcasted_iota(jnp.int32, (GRID_W, GRID_W), 0)
    kc = lax.broadcasted_iota(jnp.int32, (GRID_W, GRID_W), 1)
    cs = jnp.clip(qc - KW // 2, 0, GRID_W - KW)
    valid = (kc >= cs) & (kc < cs + KW)
    lanes = rpb_ref.shape[-1]
    tiles = []
    for dr in range(N_DR):
        row = jnp.broadcast_to(rpb_ref[0, dr:dr + 1, :], (GRID_W, lanes))
        t = pltpu.roll(row, lanes - (KW - 1), 1, stride=1, stride_axis=0)[:, :GRID_W]
        tiles.append(jnp.where(valid, t * LOG2E, -jnp.inf))
    pad = jnp.zeros((GRID_W, GRID_W), F32)
    o_ref[0, 0] = jnp.concatenate(tiles + [pad], axis=-1)
    o_ref[0, 1] = jnp.concatenate([pad] + tiles, axis=-1)


def _bias_table(rpb):
    width = (N_DR + 1) * GRID_W
    lanes = 128
    rpb_rows = jnp.pad(rpb, ((0, 0), (0, 0), (0, lanes - N_DC)))
    return pl.pallas_call(
        _bias_table_kernel,
        grid=(N_HEADS_A,),
        in_specs=[pl.BlockSpec((1, N_DR, lanes), lambda h: (h, 0, 0))],
        out_specs=pl.BlockSpec((1, 2, GRID_W, width), lambda h: (h, 0, 0, 0)),
        out_shape=jax.ShapeDtypeStruct((N_HEADS_A, 2, GRID_W, width), F32),
        compiler_params=_params(("parallel",), 16),
        name="bias_table",
    )(rpb_rows)


def _row_parts(tm, parts):
    return [slice(p * (tm // parts), (p + 1) * (tm // parts)) for p in range(parts)]


def _proj_qkvu_kernel(x_ref, mod_ref, g_ref, w_ref, h_ref, proj_ref, *kv_refs, row_parts):
    for rows in _row_parts(x_ref.shape[0], row_parts):
        y = _rms_norm(x_ref[rows, :], g_ref[...])
        h = (y * (1.0 + mod_ref[0, 1:2, :]) + mod_ref[0, 0:1, :]).astype(BF16)
        h_ref[rows, :] = h
        for blk in (3, 0, 1, 2):
            cols = slice(blk * COL_BLOCK, (blk + 1) * COL_BLOCK)
            acc = jnp.dot(h, w_ref[:, cols], preferred_element_type=F32)
            if kv_refs and blk in (1, 2):
                n_rows = rows.stop - rows.start
                for head in range(N_HEADS_A):
                    dst = pl.ds(rows.start * N_HEADS_A + head, n_rows, stride=N_HEADS_A)
                    kv_refs[blk - 1][dst, :] = acc[:, head * HEAD_DIM:(head + 1) * HEAD_DIM]
            if blk == 3:
                acc = jax.nn.gelu(acc)
            proj_ref[rows, cols] = acc.astype(BF16)


def _proj_qkvu(x, mods, seq, g, w_in, *, emit_kv, tm=512, row_parts=1):
    m = x.shape[0]
    n = 3 * D_ATTN + D_GMLP
    out_shape = [jax.ShapeDtypeStruct((m, D_MODEL), BF16), jax.ShapeDtypeStruct((m, n), BF16)]
    out_specs = [pl.BlockSpec((tm, D_MODEL), lambda i: (i, 0)), pl.BlockSpec((tm, n), lambda i: (i, 0))]
    if emit_kv:
        out_shape += [jax.ShapeDtypeStruct((m * N_HEADS_A, HEAD_DIM), F32)] * 2
        out_specs += [pl.BlockSpec((tm * N_HEADS_A, HEAD_DIM), lambda i: (i, 0))] * 2
    return pl.pallas_call(
        functools.partial(_proj_qkvu_kernel, row_parts=row_parts),
        grid=(m // tm,),
        in_specs=[
            pl.BlockSpec((tm, D_MODEL), lambda i: (i, 0)),
            pl.BlockSpec((1, N_MOD, D_MODEL), lambda i: (i * tm // seq, 0, 0)),
            pl.BlockSpec((1, D_MODEL), lambda i: (0, 0)),
            pl.BlockSpec((D_MODEL, n), lambda i: (0, 0), pipeline_mode=pl.Buffered(1)),
        ],
        out_specs=out_specs,
        out_shape=out_shape,
        compiler_params=_params(("parallel",), 56),
        name="proj_qkvu_ctx" if emit_kv else "proj_qkvu_lat",
    )(x, mods, g, w_in)


def _proj_gates_kernel(h_ref, lnv_ref, wvb_ref, *rest, row_parts):
    *wgate_refs, proj_ref = rest
    n_gate = len(wgate_refs) * COL_BLOCK
    for rows in _row_parts(h_ref.shape[0], row_parts):
        h = h_ref[rows, :]
        acc = jnp.dot(h, wvb_ref[...], preferred_element_type=F32)
        proj_ref[rows, n_gate:] = _layer_norm(jax.nn.gelu(acc), lnv_ref[...]).astype(BF16)
        for blk, w_ref in enumerate(wgate_refs):
            acc = jnp.dot(h, w_ref[...], preferred_element_type=F32)
            proj_ref[rows, blk * COL_BLOCK:(blk + 1) * COL_BLOCK] = jax.nn.sigmoid(acc).astype(BF16)


def _proj_gates(h, ln_v, w_in, *, tm=512, row_parts=2):
    m = h.shape[0]
    first = (3 * D_ATTN + D_GMLP) // COL_BLOCK
    n_blk = D_IN // COL_BLOCK - first
    w_blk = lambda blk: pl.BlockSpec((D_MODEL, COL_BLOCK), lambda i: (0, blk), pipeline_mode=pl.Buffered(1))
    return pl.pallas_call(
        functools.partial(_proj_gates_kernel, row_parts=row_parts),
        grid=(m // tm,),
        in_specs=[
            pl.BlockSpec((tm, D_MODEL), lambda i: (i, 0)),
            pl.BlockSpec((1, D_GMLP), lambda i: (0, 0)),
        ] + [w_blk(first + b) for b in range(n_blk)],
        out_specs=pl.BlockSpec((tm, n_blk * COL_BLOCK), lambda i: (i, 0)),
        out_shape=jax.ShapeDtypeStruct((m, n_blk * COL_BLOCK), BF16),
        compiler_params=_params(("parallel",), 56),
        name="proj_gates",
    )(h, ln_v, *([w_in] * n_blk))


def _softmax_parts(logits2):
    m = functools.reduce(jnp.maximum, [jnp.max(t, axis=-1, keepdims=True) for t in logits2])
    es = [jnp.exp2(t - m) for t in logits2]
    inv = 1.0 / functools.reduce(jnp.add, [jnp.sum(e, axis=-1, keepdims=True) for e in es])
    return [e.astype(BF16) for e in es], inv


def _qk(q, k):
    return lax.dot_general(q, k, (((1,), (1,)), ((), ())), preferred_element_type=F32) * (ATTN_SCALE * LOG2E)


def _ctx_attn_kernel(q_ref, k_ref, v_ref, o_ref):
    for h in range(N_HEADS_A):
        cols = slice(h * HEAD_DIM, (h + 1) * HEAD_DIM)
        (p,), inv = _softmax_parts([_qk(q_ref[:, cols], k_ref[:, cols])])
        o_ref[:, cols] = (jnp.dot(p, v_ref[:, cols], preferred_element_type=F32) * inv).astype(BF16)


def _ctx_attention(proj, seq):
    m = proj.shape[0]
    spec = lambda col: pl.BlockSpec((seq, D_ATTN), lambda b: (b, col))
    return pl.pallas_call(
        _ctx_attn_kernel,
        grid=(m // seq,),
        in_specs=[spec(0), spec(1), spec(2)],
        out_specs=spec(0),
        out_shape=jax.ShapeDtypeStruct((m, D_ATTN), BF16),
        compiler_params=_params(("parallel",), 32),
        name="ctx_attention",
    )(proj, proj, proj)


def _lat_attn_kernel(q_ref, k_ref, v_ref, kc_ref, vc_ref, tab_ref, o_ref, sw_ref, pw_ref, *, rows, kh):
    heads = tab_ref.shape[0]
    past = kc_ref.shape[1] // N_HEADS_A
    win = kh * GRID_W
    starts = [min(max(r - kh // 2, 0), rows - kh) * GRID_W for r in range(rows)]
    for j in range(heads):
        head = pl.program_id(1) * heads + j
        cols = slice(j * HEAD_DIM, (j + 1) * HEAD_DIM)
        sw, pw = sw_ref.at[j % 2], pw_ref.at[j % 2]
        kc = kc_ref[0, pl.ds(head, past, stride=N_HEADS_A), :].astype(BF16)
        vc = vc_ref[0, pl.ds(head, past, stride=N_HEADS_A), :].astype(BF16)
        s_c = _qk(q_ref[:, cols], kc)
        for r, start in enumerate(starts):
            off = start // GRID_W - r + (KH_MAX - 1)
            lane0 = (off + off % 2) * GRID_W
            bias = tab_ref[j, off % 2, :, lane0:lane0 + win]
            q = q_ref[r * GRID_W:(r + 1) * GRID_W, cols]
            sw[r * GRID_W:(r + 1) * GRID_W, :] = _qk(q, k_ref[start:start + win, cols]) + bias
        (p_w, p_c), inv = _softmax_parts([sw[...], s_c])
        pw[...] = p_w
        o_c = jnp.dot(p_c, vc, preferred_element_type=F32)
        for r, start in enumerate(starts):
            q_rows = slice(r * GRID_W, (r + 1) * GRID_W)
            o_w = jnp.dot(pw[q_rows, :], v_ref[start:start + win, cols], preferred_element_type=F32)
            o_ref[q_rows, cols] = ((o_w + o_c[q_rows, :]) * inv[q_rows, :]).astype(BF16)


def _lat_attention(proj, cache_k, cache_v, table, seq, *, heads_per_step=4):
    m = proj.shape[0]
    rows = seq // GRID_W
    kh = min(KH_MAX, rows)
    n_hp = N_HEADS_A // heads_per_step
    width = heads_per_step * HEAD_DIM
    qkv = lambda part: pl.BlockSpec((seq, width), lambda b, hp: (b, part * n_hp + hp))
    cache = pl.BlockSpec((1,) + cache_k.shape[1:], lambda b, hp: (b, 0, 0))
    return pl.pallas_call(
        functools.partial(_lat_attn_kernel, rows=rows, kh=kh),
        grid=(m // seq, n_hp),
        in_specs=[qkv(0), qkv(1), qkv(2), cache, cache,
                  pl.BlockSpec((heads_per_step,) + table.shape[1:], lambda b, hp: (hp, 0, 0, 0))],
        out_specs=pl.BlockSpec((seq, width), lambda b, hp: (b, hp)),
        out_shape=jax.ShapeDtypeStruct((m, D_ATTN), BF16),
        scratch_shapes=[pltpu.VMEM((2, seq, kh * GRID_W), F32), pltpu.VMEM((2, seq, kh * GRID_W), BF16)],
        compiler_params=_params(("parallel", "arbitrary"), 48),
        name="lat_attention",
    )(proj, proj, proj, cache_k, cache_v, table)


def _merge_kernel(x_ref, mod_ref, oa_ref, gu_ref, vn_ref, ga_ref, gb_ref,
                  ws_ref, bst_ref, wpa_ref, wpb_ref, wo_ref, g_ref, gffn_ref, o_ref, h2_ref,
                  ob_ref, *, row_parts):
    for part in _row_parts(x_ref.shape[0], row_parts):
        for c in range(part.start // CHUNK, part.stop // CHUNK):
            rows = slice(c * CHUNK, (c + 1) * CHUNK)
            for g in range(N_GROUPS_B):
                cols = slice(g * GROUP_CH, (g + 1) * GROUP_CH)
                s = jnp.dot(ws_ref[g], vn_ref[rows, cols], preferred_element_type=F32) + bst_ref[:, g:g + 1]
                ob_ref[rows, cols] = (gu_ref[rows, cols].astype(F32) * s).astype(BF16)
        pa = jnp.dot(oa_ref[part, :], wpa_ref[...], preferred_element_type=F32)
        pb = jnp.dot(ob_ref[part, :], wpb_ref[...], preferred_element_type=F32)
        mixed = (ga_ref[part, :].astype(F32) * pa + gb_ref[part, :].astype(F32) * pb).astype(BF16)
        y = jnp.dot(mixed, wo_ref[...], preferred_element_type=F32)
        x1 = x_ref[part, :] + mod_ref[0, 2:3, :] * _rms_norm(y, g_ref[...])
        o_ref[part, :] = x1
        h2 = _rms_norm(x1, gffn_ref[...]) * (1.0 + mod_ref[0, 4:5, :]) + mod_ref[0, 3:4, :]
        h2_ref[part, :] = h2.astype(BF16)


def _merge(x, mods, seq, o_a, proj_qkvu, proj_gates, w_s, b_s_t, w_pa, w_pb, w_o, g, g_ffn, *,
           tm=512, row_parts=2):
    m = x.shape[0]
    assert (tm // row_parts) % CHUNK == 0
    whole = lambda a: pl.BlockSpec(a.shape, lambda i: (0,) * a.ndim, pipeline_mode=pl.Buffered(1))
    return pl.pallas_call(
        functools.partial(_merge_kernel, row_parts=row_parts),
        grid=(m // tm,),
        in_specs=[
            pl.BlockSpec((tm, D_MODEL), lambda i: (i, 0)),
            pl.BlockSpec((1, N_MOD, D_MODEL), lambda i: (i * tm // seq, 0, 0)),
            pl.BlockSpec((tm, D_ATTN), lambda i: (i, 0)),
            pl.BlockSpec((tm, D_GMLP), lambda i: (i, 3)),
            pl.BlockSpec((tm, D_GMLP), lambda i: (i, 4)),
            pl.BlockSpec((tm, D_MODEL), lambda i: (i, 0)),
            pl.BlockSpec((tm, D_MODEL), lambda i: (i, 1)),
            whole(w_s), whole(b_s_t), whole(w_pa), whole(w_pb), whole(w_o), whole(g), whole(g_ffn),
        ],
        out_specs=[pl.BlockSpec((tm, D_MODEL), lambda i: (i, 0))] * 2,
        out_shape=[jax.ShapeDtypeStruct((m, D_MODEL), F32), jax.ShapeDtypeStruct((m, D_MODEL), BF16)],
        scratch_shapes=[pltpu.VMEM((tm, D_GMLP), BF16)],
        compiler_params=_params(("parallel",), 60),
        name="merge",
    )(x, mods, o_a, proj_qkvu, proj_gates, proj_gates, proj_gates,
      w_s, b_s_t, w_pa, w_pb, w_o, g, g_ffn)


def _ffn_x_block(f, n_f, n_x):
    return jnp.clip(f - (n_f - n_x), 0, n_x - 1)


def _ffn_kernel(x_ref, h_ref, mod_ref, gpost_ref, wg_ref, wu_ref, wd_ref, o_ref, acc_ref, *, n_x, row_parts):
    f = pl.program_id(1)
    x_rows = x_ref.shape[0]
    tm = h_ref.shape[0]

    @pl.when(f == 0)
    def _():
        acc_ref[...] = jnp.zeros_like(acc_ref)

    row0 = pl.multiple_of(_ffn_x_block(f, pl.num_programs(1), n_x) * x_rows, x_rows)
    o_ref[pl.ds(row0, x_rows), :] = x_ref[...]

    for part in range(row_parts):
        rows = slice(part * (tm // row_parts), (part + 1) * (tm // row_parts))
        h = h_ref[rows, :]
        gate = jnp.dot(h, wg_ref[...], preferred_element_type=F32)
        up = jnp.dot(h, wu_ref[...], preferred_element_type=F32)
        act = (jax.nn.silu(gate) * up).astype(BF16)
        acc_ref[rows, :] += jnp.dot(act, wd_ref[...], preferred_element_type=F32)

    @pl.when(f == pl.num_programs(1) - 1)
    def _():
        scale = mod_ref[0, 5:6, :] * gpost_ref[...]

        def residual_rows(c, carry):
            rows = pl.ds(pl.multiple_of(c * SUBLANES, SUBLANES), SUBLANES)
            a = acc_ref[rows, :]
            r = lax.rsqrt(jnp.mean(a * a, axis=-1, keepdims=True) + EPS)
            o_ref[rows, :] = o_ref[rows, :] + (a * r) * scale
            return carry

        lax.fori_loop(0, tm // SUBLANES, residual_rows, 0, unroll=32)


def _ffn(x, h, mods, seq, g_post, w_gate, w_up, w_down, *, tm=1024, tf=512, x_rows=128, row_parts=2):
    m = x.shape[0]
    n_f = D_FF // tf
    n_x = tm // x_rows
    assert n_x <= n_f
    return pl.pallas_call(
        functools.partial(_ffn_kernel, n_x=n_x, row_parts=row_parts),
        grid=(m // tm, n_f),
        in_specs=[
            pl.BlockSpec((x_rows, D_MODEL), lambda i, f: (i * n_x + _ffn_x_block(f, n_f, n_x), 0)),
            pl.BlockSpec((tm, D_MODEL), lambda i, f: (i, 0)),
            pl.BlockSpec((1, N_MOD, D_MODEL), lambda i, f: (i * tm // seq, 0, 0)),
            pl.BlockSpec((1, D_MODEL), lambda i, f: (0, 0)),
            pl.BlockSpec((D_MODEL, tf), lambda i, f: (0, f)),
            pl.BlockSpec((D_MODEL, tf), lambda i, f: (0, f)),
            pl.BlockSpec((tf, D_MODEL), lambda i, f: (f, 0)),
        ],
        out_specs=pl.BlockSpec((tm, D_MODEL), lambda i, f: (i, 0)),
        out_shape=jax.ShapeDtypeStruct((m, D_MODEL), F32),
        scratch_shapes=[pltpu.VMEM((tm, D_MODEL), F32)],
        compiler_params=_params(("parallel", "arbitrary"), 58),
        name="ffn",
    )(x, h, mods, g_post, w_gate, w_up, w_down)


def kernel(x_prompt, x_sample, cache_k, cache_v, c, c_ctx, w_ada, b_ada, norm_mix_pre, norm_mix_post,
           norm_ffn_pre, norm_ffn_post, w_in, rpb, ln_v, w_s, b_s, w_pa, w_pb, w_o, w_gate, w_up, w_down):
    assert w_ada.shape[0] == DEPTH == 1
    batch, seq, _ = x_prompt.shape
    dec_batch, dec_seq, _ = x_sample.shape
    past = cache_k.shape[2]

    row = lambda a: a[0].reshape(1, -1)
    bf = lambda a: a[0].astype(BF16)

    cvecs = jnp.concatenate(
        [c, c_ctx[None], jnp.zeros((MOD_ROWS - dec_batch - 1, D_MODEL), F32)], axis=0)
    mods = _modulation(cvecs, w_ada[0], b_ada[0])
    mods_lat = mods[:dec_batch].reshape(dec_batch, N_MOD, D_MODEL)
    mods_ctx = mods[dec_batch:dec_batch + 1].reshape(1, N_MOD, D_MODEL)
    table = _bias_table(rpb[0])

    w_in_b = bf(w_in)
    mix = (bf(w_s), b_s[0].T, bf(w_pa), bf(w_pb), bf(w_o), row(norm_mix_post), row(norm_ffn_pre))
    ffn = (row(norm_ffn_post), bf(w_gate), bf(w_up), bf(w_down))

    xp = x_prompt.reshape(batch * seq, D_MODEL)
    hp, proj_p, k_p, v_p = _proj_qkvu(xp, mods_ctx, batch * seq, row(norm_mix_pre), w_in_b, emit_kv=True)
    gates_p = _proj_gates(hp, row(ln_v), w_in_b)
    oa_p = _ctx_attention(proj_p, seq)
    xp, hp = _merge(xp, mods_ctx, batch * seq, oa_p, proj_p, gates_p, *mix)
    y_prompt = _ffn(xp, hp, mods_ctx, batch * seq, *ffn).reshape(batch, seq, D_MODEL)

    xs = x_sample.reshape(dec_batch * dec_seq, D_MODEL)
    hs, proj_s = _proj_qkvu(xs, mods_lat, dec_seq, row(norm_mix_pre), w_in_b, emit_kv=False)
    gates_s = _proj_gates(hs, row(ln_v), w_in_b)
    oa_s = _lat_attention(proj_s, cache_k.reshape(dec_batch, past * N_HEADS_A, HEAD_DIM),
                          cache_v.reshape(dec_batch, past * N_HEADS_A, HEAD_DIM), table, dec_seq)
    xs, hs = _merge(xs, mods_lat, dec_seq, oa_s, proj_s, gates_s, *mix)
    y_sample = _ffn(xs, hs, mods_lat, dec_seq, *ffn).reshape(dec_batch, dec_seq, D_MODEL)

    state_shape = (batch, DEPTH, seq, N_HEADS_A, HEAD_DIM)
    return y_prompt, y_sample, k_p.reshape(state_shape), v_p.reshape(state_shape)
```

```python
import functools

import jax
import jax.numpy as jnp
from jax import lax
from jax.experimental import pallas as pl
from jax.experimental.pallas import tpu as pltpu

D_MODEL = 2048
DEPTH = 1
GRID_W = 64
N_HEADS_A = 8
HEAD_DIM = 128
D_ATTN = N_HEADS_A * HEAD_DIM
KH_MAX = 8
KW = 16
CHUNK = 128
N_GROUPS_B = 8
D_GMLP = 1024
GROUP_CH = D_GMLP // N_GROUPS_B
D_FF = ((8 * D_MODEL // 3 + 255) // 256) * 256
N_MOD = 6
EPS = 1e-6
ATTN_SCALE = HEAD_DIM ** -0.5
LOG2E = 1.4426950408889634
D_IN = 3 * D_ATTN + 2 * D_GMLP + 2 * D_MODEL

N_DR = 2 * KH_MAX - 1
N_DC = 2 * KW - 1
COL_BLOCK = 1024
SUBLANES = 8
MOD_ROWS = SUBLANES

F32 = jnp.float32
BF16 = jnp.bfloat16

MIB = 1024 * 1024


def _params(semantics, vmem_mib):
    return pltpu.CompilerParams(dimension_semantics=semantics, vmem_limit_bytes=vmem_mib * MIB)


def _rms_norm(x, g):
    return x * lax.rsqrt(jnp.mean(x * x, axis=-1, keepdims=True) + EPS) * g


def _layer_norm(x, g):
    xc = x - jnp.mean(x, axis=-1, keepdims=True)
    return xc * lax.rsqrt(jnp.mean(xc * xc, axis=-1, keepdims=True) + EPS) * g


def _modulation_kernel(c_ref, w_ref, b_ref, o_ref):
    s = jax.nn.silu(c_ref[...]).astype(BF16)
    o_ref[...] = jnp.dot(s, w_ref[...].astype(BF16), preferred_element_type=F32) + b_ref[...]


def _modulation(cvecs, w_ada, b_ada):
    tn = 1024
    n = N_MOD * D_MODEL
    return pl.pallas_call(
        _modulation_kernel,
        grid=(n // tn,),
        in_specs=[
            pl.BlockSpec((MOD_ROWS, D_MODEL), lambda j: (0, 0)),
            pl.BlockSpec((D_MODEL, tn), lambda j: (0, j)),
            pl.BlockSpec((1, tn), lambda j: (0, j)),
        ],
        out_specs=pl.BlockSpec((MOD_ROWS, tn), lambda j: (0, j)),
        out_shape=jax.ShapeDtypeStruct((MOD_ROWS, n), F32),
        compiler_params=_params(("parallel",), 40),
        name="modulation",
    )(cvecs, w_ada, b_ada.reshape(1, n))


def _bias_table_kernel(rpb_ref, o_ref):
    qc = lax.broadcasted_iota(jnp.int32, (GRID_W, GRID_W), 0)
    kc = lax.broadcasted_iota(jnp.int32, (GRID_W, GRID_W), 1)
    cs = jnp.clip(qc - KW // 2, 0, GRID_W - KW)
    valid = (kc >= cs) & (kc < cs + KW)
    lanes = rpb_ref.shape[-1]
    tiles = []
    for dr in range(N_DR):
        row = jnp.broadcast_to(rpb_ref[0, dr:dr + 1, :], (GRID_W, lanes))
        t = pltpu.roll(row, lanes - (KW - 1), 1, stride=1, stride_axis=0)[:, :GRID_W]
        tiles.append(jnp.where(valid, t * LOG2E, -jnp.inf))
    pad = jnp.zeros((GRID_W, GRID_W), F32)
    o_ref[0, 0] = jnp.concatenate(tiles + [pad], axis=-1)
    o_ref[0, 1] = jnp.concatenate([pad] + tiles, axis=-1)


def _bias_table(rpb):
    width = (N_DR + 1) * GRID_W
    lanes = 128
    rpb_rows = jnp.pad(rpb, ((0, 0), (0, 0), (0, lanes - N_DC)))
    return pl.pallas_call(
        _bias_table_kernel,
        grid=(N_HEADS_A,),
        in_specs=[pl.BlockSpec((1, N_DR, lanes), lambda h: (h, 0, 0))],
        out_specs=pl.BlockSpec((1, 2, GRID_W, width), lambda h: (h, 0, 0, 0)),
        out_shape=jax.ShapeDtypeStruct((N_HEADS_A, 2, GRID_W, width), F32),
        compiler_params=_params(("parallel",), 16),
        name="bias_table",
    )(rpb_rows)


LOAD_CHUNK = 512
N_QKVU = 3 * D_ATTN + D_GMLP


def _row_parts(tm, parts):
    return [slice(p * (tm // parts), (p + 1) * (tm // parts)) for p in range(parts)]


def _load_cast_weights(w_hbm, col0, w_res, stage, sems):
    n_chunks = w_res.shape[1] // LOAD_CHUNK

    def copy(c):
        src = w_hbm.at[:, pl.ds(col0 + c * LOAD_CHUNK, LOAD_CHUNK)]
        return pltpu.make_async_copy(src, stage.at[c % 2], sems.at[c % 2])

    copy(0).start()
    for c in range(n_chunks):
        if c + 1 < n_chunks:
            copy(c + 1).start()
        copy(c).wait()
        w_res[:, c * LOAD_CHUNK:(c + 1) * LOAD_CHUNK] = stage[c % 2].astype(BF16)


def _proj_qkvu_kernel(x_ref, mod_ref, g_ref, w_ref, h_ref, proj_ref, *rest, is_ctx):
    if is_ctx:
        k_ref, v_ref, wpub_ref, w_res, stage, sems = rest
        step = pl.program_id(0)
        publish = pltpu.make_async_copy(w_res, wpub_ref, sems.at[2])

        @pl.when(step == 0)
        def _():
            _load_cast_weights(w_ref, 0, w_res, stage, sems)
            publish.start()
    else:
        w_res = w_ref
    y = _rms_norm(x_ref[...], g_ref[...])
    h = (y * (1.0 + mod_ref[0, 1:2, :]) + mod_ref[0, 0:1, :]).astype(BF16)
    h_ref[...] = h
    for blk in (3, 0, 1, 2):
        cols = slice(blk * COL_BLOCK, (blk + 1) * COL_BLOCK)
        acc = jnp.dot(h, w_res[:, cols], preferred_element_type=F32)
        if is_ctx and blk in (1, 2):
            kv_ref = (k_ref, v_ref)[blk - 1]
            for head in range(N_HEADS_A):
                dst = pl.ds(head, x_ref.shape[0], stride=N_HEADS_A)
                kv_ref[dst, :] = acc[:, head * HEAD_DIM:(head + 1) * HEAD_DIM]
        if blk == 3:
            acc = jax.nn.gelu(acc)
        proj_ref[:, cols] = acc.astype(BF16)
    if is_ctx:
        @pl.when(step == pl.num_programs(0) - 1)
        def _():
            publish.wait()


def _proj_qkvu(x, mods, seq, g, w, *, is_ctx, tm=512):
    m = x.shape[0]
    out_shape = [jax.ShapeDtypeStruct((m, D_MODEL), BF16), jax.ShapeDtypeStruct((m, N_QKVU), BF16)]
    out_specs = [pl.BlockSpec((tm, D_MODEL), lambda i: (i, 0)), pl.BlockSpec((tm, N_QKVU), lambda i: (i, 0))]
    scratch = []
    if is_ctx:
        out_shape += [jax.ShapeDtypeStruct((m * N_HEADS_A, HEAD_DIM), F32)] * 2
        out_specs += [pl.BlockSpec((tm * N_HEADS_A, HEAD_DIM), lambda i: (i, 0))] * 2
        out_shape += [jax.ShapeDtypeStruct((D_MODEL, N_QKVU), BF16)]
        out_specs += [pl.BlockSpec(memory_space=pl.ANY)]
        w_spec = pl.BlockSpec(memory_space=pl.ANY)
        scratch = [pltpu.VMEM((D_MODEL, N_QKVU), BF16), pltpu.VMEM((2, D_MODEL, LOAD_CHUNK), F32),
                   pltpu.SemaphoreType.DMA((3,))]
    else:
        w_spec = pl.BlockSpec((D_MODEL, N_QKVU), lambda i: (0, 0), pipeline_mode=pl.Buffered(1))
    return pl.pallas_call(
        functools.partial(_proj_qkvu_kernel, is_ctx=is_ctx),
        grid=(m // tm,),
        in_specs=[
            pl.BlockSpec((tm, D_MODEL), lambda i: (i, 0)),
            pl.BlockSpec((1, N_MOD, D_MODEL), lambda i: (i * tm // seq, 0, 0)),
            pl.BlockSpec((1, D_MODEL), lambda i: (0, 0)),
            w_spec,
        ],
        out_specs=out_specs,
        out_shape=out_shape,
        scratch_shapes=scratch,
        compiler_params=_params(("arbitrary",), 58),
        name="proj_qkvu_ctx" if is_ctx else "proj_qkvu_lat",
    )(x, mods, g, w)


def _proj_gates_kernel(hp_ref, hs_ref, lnv_ref, w_ref, proj_ref, w_res, stage, sems, *, n_ctx, row_parts):
    step = pl.program_id(0)

    @pl.when(step == 0)
    def _():
        _load_cast_weights(w_ref, N_QKVU, w_res, stage, sems)

    n_gate = w_res.shape[1] - D_GMLP
    for rows in _row_parts(hp_ref.shape[0], row_parts):
        h = jnp.where(step < n_ctx, hp_ref[rows, :], hs_ref[rows, :])
        acc = jnp.dot(h, w_res[:, :D_GMLP], preferred_element_type=F32)
        proj_ref[rows, n_gate:] = _layer_norm(jax.nn.gelu(acc), lnv_ref[...]).astype(BF16)
        for blk in range(n_gate // COL_BLOCK):
            cols = slice(blk * COL_BLOCK, (blk + 1) * COL_BLOCK)
            acc = jnp.dot(h, w_res[:, D_GMLP + cols.start:D_GMLP + cols.stop], preferred_element_type=F32)
            proj_ref[rows, cols] = jax.nn.sigmoid(acc).astype(BF16)


def _proj_gates(h_ctx, h_lat, ln_v, w_in, *, tm=512, row_parts=2):
    n_ctx, n_lat = h_ctx.shape[0] // tm, h_lat.shape[0] // tm
    n = D_IN - N_QKVU
    return pl.pallas_call(
        functools.partial(_proj_gates_kernel, n_ctx=n_ctx, row_parts=row_parts),
        grid=(n_ctx + n_lat,),
        in_specs=[
            pl.BlockSpec((tm, D_MODEL), lambda i: (jnp.minimum(i, n_ctx - 1), 0)),
            pl.BlockSpec((tm, D_MODEL), lambda i: (jnp.maximum(i - n_ctx, 0), 0)),
            pl.BlockSpec((1, D_GMLP), lambda i: (0, 0)),
            pl.BlockSpec(memory_space=pl.ANY),
        ],
        out_specs=pl.BlockSpec((tm, n), lambda i: (i, 0)),
        out_shape=jax.ShapeDtypeStruct(((n_ctx + n_lat) * tm, n), BF16),
        scratch_shapes=[pltpu.VMEM((D_MODEL, n), BF16), pltpu.VMEM((2, D_MODEL, LOAD_CHUNK), F32),
                        pltpu.SemaphoreType.DMA((2,))],
        compiler_params=_params(("arbitrary",), 56),
        name="proj_gates",
    )(h_ctx, h_lat, ln_v, w_in)


def _softmax_parts(logits2):
    m = functools.reduce(jnp.maximum, [jnp.max(t, axis=-1, keepdims=True) for t in logits2])
    es = [jnp.exp2(t - m) for t in logits2]
    inv = 1.0 / functools.reduce(jnp.add, [jnp.sum(e, axis=-1, keepdims=True) for e in es])
    return [e.astype(BF16) for e in es], inv


def _qk(q, k):
    return lax.dot_general(q, k, (((1,), (1,)), ((), ())), preferred_element_type=F32) * (ATTN_SCALE * LOG2E)


def _ctx_attn_kernel(q_ref, k_ref, v_ref, o_ref):
    for h in range(N_HEADS_A):
        cols = slice(h * HEAD_DIM, (h + 1) * HEAD_DIM)
        (p,), inv = _softmax_parts([_qk(q_ref[:, cols], k_ref[:, cols])])
        o_ref[:, cols] = (jnp.dot(p, v_ref[:, cols], preferred_element_type=F32) * inv).astype(BF16)


def _ctx_attention(proj, seq):
    m = proj.shape[0]
    spec = lambda col: pl.BlockSpec((seq, D_ATTN), lambda b: (b, col))
    return pl.pallas_call(
        _ctx_attn_kernel,
        grid=(m // seq,),
        in_specs=[spec(0), spec(1), spec(2)],
        out_specs=spec(0),
        out_shape=jax.ShapeDtypeStruct((m, D_ATTN), BF16),
        compiler_params=_params(("parallel",), 32),
        name="ctx_attention",
    )(proj, proj, proj)


def _lat_attn_kernel(q_ref, k_ref, v_ref, kc_ref, vc_ref, tab_ref, o_ref, sw_ref, pw_ref, *, rows, kh):
    heads = tab_ref.shape[0]
    past = kc_ref.shape[1] // N_HEADS_A
    win = kh * GRID_W
    starts = [min(max(r - kh // 2, 0), rows - kh) * GRID_W for r in range(rows)]
    for j in range(heads):
        head = pl.program_id(1) * heads + j
        cols = slice(j * HEAD_DIM, (j + 1) * HEAD_DIM)
        sw, pw = sw_ref.at[j % 2], pw_ref.at[j % 2]
        kc = kc_ref[0, pl.ds(head, past, stride=N_HEADS_A), :].astype(BF16)
        vc = vc_ref[0, pl.ds(head, past, stride=N_HEADS_A), :].astype(BF16)
        s_c = _qk(q_ref[:, cols], kc)
        for r, start in enumerate(starts):
            off = start // GRID_W - r + (KH_MAX - 1)
            lane0 = (off + off % 2) * GRID_W
            bias = tab_ref[j, off % 2, :, lane0:lane0 + win]
            q = q_ref[r * GRID_W:(r + 1) * GRID_W, cols]
            sw[r * GRID_W:(r + 1) * GRID_W, :] = _qk(q, k_ref[start:start + win, cols]) + bias
        (p_w, p_c), inv = _softmax_parts([sw[...], s_c])
        pw[...] = p_w
        o_c = jnp.dot(p_c, vc, preferred_element_type=F32)
        for r, start in enumerate(starts):
            q_rows = slice(r * GRID_W, (r + 1) * GRID_W)
            o_w = jnp.dot(pw[q_rows, :], v_ref[start:start + win, cols], preferred_element_type=F32)
            o_ref[q_rows, cols] = ((o_w + o_c[q_rows, :]) * inv[q_rows, :]).astype(BF16)


def _lat_attention(proj, cache_k, cache_v, table, seq, *, heads_per_step=4):
    m = proj.shape[0]
    rows = seq // GRID_W
    kh = min(KH_MAX, rows)
    n_hp = N_HEADS_A // heads_per_step
    width = heads_per_step * HEAD_DIM
    qkv = lambda part: pl.BlockSpec((seq, width), lambda b, hp: (b, part * n_hp + hp))
    cache = pl.BlockSpec((1,) + cache_k.shape[1:], lambda b, hp: (b, 0, 0))
    return pl.pallas_call(
        functools.partial(_lat_attn_kernel, rows=rows, kh=kh),
        grid=(m // seq, n_hp),
        in_specs=[qkv(0), qkv(1), qkv(2), cache, cache,
                  pl.BlockSpec((heads_per_step,) + table.shape[1:], lambda b, hp: (hp, 0, 0, 0))],
        out_specs=pl.BlockSpec((seq, width), lambda b, hp: (b, hp)),
        out_shape=jax.ShapeDtypeStruct((m, D_ATTN), BF16),
        scratch_shapes=[pltpu.VMEM((2, seq, kh * GRID_W), F32), pltpu.VMEM((2, seq, kh * GRID_W), BF16)],
        compiler_params=_params(("parallel", "arbitrary"), 48),
        name="lat_attention",
    )(proj, proj, proj, cache_k, cache_v, table)


def _merge_kernel(x_ref, mod_ref, oa_ref, gu_ref, vn_ref, ga_ref, gb_ref,
                  ws_ref, bst_ref, wpa_ref, wpb_ref, wo_ref, g_ref, gffn_ref, o_ref, h2_ref,
                  ob_ref, *, row_parts):
    for part in _row_parts(x_ref.shape[0], row_parts):
        for c in range(part.start // CHUNK, part.stop // CHUNK):
            rows = slice(c * CHUNK, (c + 1) * CHUNK)
            for g in range(N_GROUPS_B):
                cols = slice(g * GROUP_CH, (g + 1) * GROUP_CH)
                s = jnp.dot(ws_ref[g], vn_ref[rows, cols], preferred_element_type=F32) + bst_ref[:, g:g + 1]
                ob_ref[rows, cols] = (gu_ref[rows, cols].astype(F32) * s).astype(BF16)
        pa = jnp.dot(oa_ref[part, :], wpa_ref[...], preferred_element_type=F32)
        pb = jnp.dot(ob_ref[part, :], wpb_ref[...], preferred_element_type=F32)
        mixed = (ga_ref[part, :].astype(F32) * pa + gb_ref[part, :].astype(F32) * pb).astype(BF16)
        y = jnp.dot(mixed, wo_ref[...], preferred_element_type=F32)
        x1 = x_ref[part, :] + mod_ref[0, 2:3, :] * _rms_norm(y, g_ref[...])
        o_ref[part, :] = x1
        h2 = _rms_norm(x1, gffn_ref[...]) * (1.0 + mod_ref[0, 4:5, :]) + mod_ref[0, 3:4, :]
        h2_ref[part, :] = h2.astype(BF16)


def _merge(x, mods, seq, o_a, proj_qkvu, proj_gates, gates_row0, w_s, b_s_t, w_pa, w_pb, w_o, g, g_ffn, *,
           tm=512, row_parts=2):
    m = x.shape[0]
    assert (tm // row_parts) % CHUNK == 0 and gates_row0 % tm == 0
    g0 = gates_row0 // tm
    whole = lambda a: pl.BlockSpec(a.shape, lambda i: (0,) * a.ndim, pipeline_mode=pl.Buffered(1))
    return pl.pallas_call(
        functools.partial(_merge_kernel, row_parts=row_parts),
        grid=(m // tm,),
        in_specs=[
            pl.BlockSpec((tm, D_MODEL), lambda i: (i, 0)),
            pl.BlockSpec((1, N_MOD, D_MODEL), lambda i: (i * tm // seq, 0, 0)),
            pl.BlockSpec((tm, D_ATTN), lambda i: (i, 0)),
            pl.BlockSpec((tm, D_GMLP), lambda i: (i, 3)),
            pl.BlockSpec((tm, D_GMLP), lambda i: (g0 + i, 4)),
            pl.BlockSpec((tm, D_MODEL), lambda i: (g0 + i, 0)),
            pl.BlockSpec((tm, D_MODEL), lambda i: (g0 + i, 1)),
            whole(w_s), whole(b_s_t), whole(w_pa), whole(w_pb), whole(w_o), whole(g), whole(g_ffn),
        ],
        out_specs=[pl.BlockSpec((tm, D_MODEL), lambda i: (i, 0))] * 2,
        out_shape=[jax.ShapeDtypeStruct((m, D_MODEL), F32), jax.ShapeDtypeStruct((m, D_MODEL), BF16)],
        scratch_shapes=[pltpu.VMEM((tm, D_GMLP), BF16)],
        compiler_params=_params(("parallel",), 60),
        name="merge",
    )(x, mods, o_a, proj_qkvu, proj_gates, proj_gates, proj_gates,
      w_s, b_s_t, w_pa, w_pb, w_o, g, g_ffn)


def _ffn_x_block(f, n_f, n_x):
    return jnp.clip(f - (n_f - n_x), 0, n_x - 1)


def _ffn_kernel(x_ref, h_ref, mod_ref, gpost_ref, wg_ref, wu_ref, wd_ref, o_ref, acc_ref, *, n_x, row_parts):
    f = pl.program_id(1)
    x_rows = x_ref.shape[0]
    tm = h_ref.shape[0]

    @pl.when(f == 0)
    def _():
        acc_ref[...] = jnp.zeros_like(acc_ref)

    row0 = pl.multiple_of(_ffn_x_block(f, pl.num_programs(1), n_x) * x_rows, x_rows)
    o_ref[pl.ds(row0, x_rows), :] = x_ref[...]

    for part in range(row_parts):
        rows = slice(part * (tm // row_parts), (part + 1) * (tm // row_parts))
        h = h_ref[rows, :]
        gate = jnp.dot(h, wg_ref[...], preferred_element_type=F32)
        up = jnp.dot(h, wu_ref[...], preferred_element_type=F32)
        act = (jax.nn.silu(gate) * up).astype(BF16)
        acc_ref[rows, :] += jnp.dot(act, wd_ref[...], preferred_element_type=F32)

    @pl.when(f == pl.num_programs(1) - 1)
    def _():
        scale = mod_ref[0, 5:6, :] * gpost_ref[...]

        def residual_rows(c, carry):
            rows = pl.ds(pl.multiple_of(c * SUBLANES, SUBLANES), SUBLANES)
            a = acc_ref[rows, :]
            r = lax.rsqrt(jnp.mean(a * a, axis=-1, keepdims=True) + EPS)
            o_ref[rows, :] = o_ref[rows, :] + (a * r) * scale
            return carry

        lax.fori_loop(0, tm // SUBLANES, residual_rows, 0, unroll=32)


def _ffn(x, h, mods, seq, g_post, w_gate, w_up, w_down, *, tm=1024, tf=512, x_rows=128, row_parts=2):
    m = x.shape[0]
    n_f = D_FF // tf
    n_x = tm // x_rows
    assert n_x <= n_f
    return pl.pallas_call(
        functools.partial(_ffn_kernel, n_x=n_x, row_parts=row_parts),
        grid=(m // tm, n_f),
        in_specs=[
            pl.BlockSpec((x_rows, D_MODEL), lambda i, f: (i * n_x + _ffn_x_block(f, n_f, n_x), 0)),
            pl.BlockSpec((tm, D_MODEL), lambda i, f: (i, 0)),
            pl.BlockSpec((1, N_MOD, D_MODEL), lambda i, f: (i * tm // seq, 0, 0)),
            pl.BlockSpec((1, D_MODEL), lambda i, f: (0, 0)),
            pl.BlockSpec((D_MODEL, tf), lambda i, f: (0, f)),
            pl.BlockSpec((D_MODEL, tf), lambda i, f: (0, f)),
            pl.BlockSpec((tf, D_MODEL), lambda i, f: (f, 0)),
        ],
        out_specs=pl.BlockSpec((tm, D_MODEL), lambda i, f: (i, 0)),
        out_shape=jax.ShapeDtypeStruct((m, D_MODEL), F32),
        scratch_shapes=[pltpu.VMEM((tm, D_MODEL), F32)],
        compiler_params=_params(("parallel", "arbitrary"), 58),
        name="ffn",
    )(x, h, mods, g_post, w_gate, w_up, w_down)


def kernel(x_prompt, x_sample, cache_k, cache_v, c, c_ctx, w_ada, b_ada, norm_mix_pre, norm_mix_post,
           norm_ffn_pre, norm_ffn_post, w_in, rpb, ln_v, w_s, b_s, w_pa, w_pb, w_o, w_gate, w_up, w_down):
    assert w_ada.shape[0] == DEPTH == 1
    batch, seq, _ = x_prompt.shape
    dec_batch, dec_seq, _ = x_sample.shape
    past = cache_k.shape[2]

    row = lambda a: a[0].reshape(1, -1)
    bf = lambda a: a[0].astype(BF16)

    cvecs = jnp.concatenate(
        [c, c_ctx[None], jnp.zeros((MOD_ROWS - dec_batch - 1, D_MODEL), F32)], axis=0)
    mods = _modulation(cvecs, w_ada[0], b_ada[0])
    mods_lat = mods[:dec_batch].reshape(dec_batch, N_MOD, D_MODEL)
    mods_ctx = mods[dec_batch:dec_batch + 1].reshape(1, N_MOD, D_MODEL)
    table = _bias_table(rpb[0])

    w_in_f = w_in.reshape(D_MODEL, D_IN)
    mix = (bf(w_s), b_s[0].T, bf(w_pa), bf(w_pb), bf(w_o), row(norm_mix_post), row(norm_ffn_pre))
    ffn = (row(norm_ffn_post), bf(w_gate), bf(w_up), bf(w_down))

    xp = x_prompt.reshape(batch * seq, D_MODEL)
    xs = x_sample.reshape(dec_batch * dec_seq, D_MODEL)
    hp, proj_p, k_p, v_p, w_qkvu = _proj_qkvu(xp, mods_ctx, batch * seq, row(norm_mix_pre), w_in_f, is_ctx=True)
    hs, proj_s = _proj_qkvu(xs, mods_lat, dec_seq, row(norm_mix_pre), w_qkvu, is_ctx=False)
    gates = _proj_gates(hp, hs, row(ln_v), w_in_f)

    oa_p = _ctx_attention(proj_p, seq)
    xp, hp = _merge(xp, mods_ctx, batch * seq, oa_p, proj_p, gates, 0, *mix)
    y_prompt = _ffn(xp, hp, mods_ctx, batch * seq, *ffn).reshape(batch, seq, D_MODEL)

    oa_s = _lat_attention(proj_s, cache_k.reshape(dec_batch, past * N_HEADS_A, HEAD_DIM),
                          cache_v.reshape(dec_batch, past * N_HEADS_A, HEAD_DIM), table, dec_seq)
    xs, hs = _merge(xs, mods_lat, dec_seq, oa_s, proj_s, gates, batch * seq, *mix)
    y_sample = _ffn(xs, hs, mods_lat, dec_seq, *ffn).reshape(dec_batch, dec_seq, D_MODEL)

    state_shape = (batch, DEPTH, seq, N_HEADS_A, HEAD_DIM)
    return y_prompt, y_sample, k_p.reshape(state_shape), v_p.reshape(state_shape)
```

```python
import functools

import jax
import jax.numpy as jnp
from jax import lax
from jax.experimental import pallas as pl
from jax.experimental.pallas import tpu as pltpu

D_MODEL = 2048
DEPTH = 1
GRID_W = 64
N_HEADS_A = 8
HEAD_DIM = 128
D_ATTN = N_HEADS_A * HEAD_DIM
KH_MAX = 8
KW = 16
CHUNK = 128
N_GROUPS_B = 8
D_GMLP = 1024
GROUP_CH = D_GMLP // N_GROUPS_B
D_FF = ((8 * D_MODEL // 3 + 255) // 256) * 256
N_MOD = 6
EPS = 1e-6
ATTN_SCALE = HEAD_DIM ** -0.5
LOG2E = 1.4426950408889634
D_IN = 3 * D_ATTN + 2 * D_GMLP + 2 * D_MODEL

N_DR = 2 * KH_MAX - 1
N_DC = 2 * KW - 1
COL_BLOCK = 1024
SUBLANES = 8
MOD_ROWS = SUBLANES

F32 = jnp.float32
BF16 = jnp.bfloat16

MIB = 1024 * 1024


def _params(semantics, vmem_mib):
    return pltpu.CompilerParams(dimension_semantics=semantics, vmem_limit_bytes=vmem_mib * MIB)


def _rms_norm(x, g):
    return x * lax.rsqrt(jnp.mean(x * x, axis=-1, keepdims=True) + EPS) * g


def _layer_norm(x, g):
    xc = x - jnp.mean(x, axis=-1, keepdims=True)
    return xc * lax.rsqrt(jnp.mean(xc * xc, axis=-1, keepdims=True) + EPS) * g


def _modulation_kernel(c_ref, w_ref, b_ref, o_ref):
    s = jax.nn.silu(c_ref[...]).astype(BF16)
    o_ref[...] = jnp.dot(s, w_ref[...].astype(BF16), preferred_element_type=F32) + b_ref[...]


def _modulation(cvecs, w_ada, b_ada):
    tn = 1024
    n = N_MOD * D_MODEL
    return pl.pallas_call(
        _modulation_kernel,
        grid=(n // tn,),
        in_specs=[
            pl.BlockSpec((MOD_ROWS, D_MODEL), lambda j: (0, 0)),
            pl.BlockSpec((D_MODEL, tn), lambda j: (0, j)),
            pl.BlockSpec((1, tn), lambda j: (0, j)),
        ],
        out_specs=pl.BlockSpec((MOD_ROWS, tn), lambda j: (0, j)),
        out_shape=jax.ShapeDtypeStruct((MOD_ROWS, n), F32),
        compiler_params=_params(("parallel",), 40),
        name="modulation",
    )(cvecs, w_ada, b_ada.reshape(1, n))


def _bias_table_kernel(rpb_ref, o_ref):
    qc = lax.broadcasted_iota(jnp.int32, (GRID_W, GRID_W), 0)
    kc = lax.broadcasted_iota(jnp.int32, (GRID_W, GRID_W), 1)
    cs = jnp.clip(qc - KW // 2, 0, GRID_W - KW)
    valid = (kc >= cs) & (kc < cs + KW)
    lanes = rpb_ref.shape[-1]
    tiles = []
    for dr in range(N_DR):
        row = jnp.broadcast_to(rpb_ref[0, dr:dr + 1, :], (GRID_W, lanes))
        t = pltpu.roll(row, lanes - (KW - 1), 1, stride=1, stride_axis=0)[:, :GRID_W]
        tiles.append(jnp.where(valid, t * LOG2E, -jnp.inf))
    pad = jnp.zeros((GRID_W, GRID_W), F32)
    o_ref[0, 0] = jnp.concatenate(tiles + [pad], axis=-1)
    o_ref[0, 1] = jnp.concatenate([pad] + tiles, axis=-1)


def _bias_table(rpb):
    width = (N_DR + 1) * GRID_W
    lanes = 128
    rpb_rows = jnp.pad(rpb, ((0, 0), (0, 0), (0, lanes - N_DC)))
    return pl.pallas_call(
        _bias_table_kernel,
        grid=(N_HEADS_A,),
        in_specs=[pl.BlockSpec((1, N_DR, lanes), lambda h: (h, 0, 0))],
        out_specs=pl.BlockSpec((1, 2, GRID_W, width), lambda h: (h, 0, 0, 0)),
        out_shape=jax.ShapeDtypeStruct((N_HEADS_A, 2, GRID_W, width), F32),
        compiler_params=_params(("parallel",), 16),
        name="bias_table",
    )(rpb_rows)


LOAD_CHUNK = 512
N_QKVU = 3 * D_ATTN + D_GMLP


def _row_parts(tm, parts):
    return [slice(p * (tm // parts), (p + 1) * (tm // parts)) for p in range(parts)]


def _load_cast_weights(w_hbm, col0, w_res, stage, sems):
    rows, width = w_res.shape
    chunk = stage.shape[2]
    n_chunks = width // chunk

    def copy(c):
        src = w_hbm.at[:, pl.ds(col0 + c * chunk, chunk)]
        return pltpu.make_async_copy(src, stage.at[c % 2, pl.ds(0, rows)], sems.at[c % 2])

    copy(0).start()
    for c in range(n_chunks):
        if c + 1 < n_chunks:
            copy(c + 1).start()
        copy(c).wait()
        w_res[:, c * chunk:(c + 1) * chunk] = stage[c % 2, :rows, :].astype(BF16)


def _proj_qkvu_kernel(x_ref, mod_ref, g_ref, w_ref, h_ref, proj_ref, *rest, is_ctx):
    if is_ctx:
        k_ref, v_ref, wpub_ref, w_res, stage, sems = rest
        step = pl.program_id(0)
        publish = pltpu.make_async_copy(w_res, wpub_ref, sems.at[2])

        @pl.when(step == 0)
        def _():
            _load_cast_weights(w_ref, 0, w_res, stage, sems)
            publish.start()
    else:
        w_res = w_ref
    y = _rms_norm(x_ref[...], g_ref[...])
    h = (y * (1.0 + mod_ref[0, 1:2, :]) + mod_ref[0, 0:1, :]).astype(BF16)
    h_ref[...] = h
    for blk in (3, 0, 1, 2):
        cols = slice(blk * COL_BLOCK, (blk + 1) * COL_BLOCK)
        acc = jnp.dot(h, w_res[:, cols], preferred_element_type=F32)
        if is_ctx and blk in (1, 2):
            kv_ref = (k_ref, v_ref)[blk - 1]
            for head in range(N_HEADS_A):
                dst = pl.ds(head, x_ref.shape[0], stride=N_HEADS_A)
                kv_ref[dst, :] = acc[:, head * HEAD_DIM:(head + 1) * HEAD_DIM]
        if blk == 3:
            acc = jax.nn.gelu(acc)
        proj_ref[:, cols] = acc.astype(BF16)
    if is_ctx:
        @pl.when(step == pl.num_programs(0) - 1)
        def _():
            publish.wait()


def _proj_qkvu(x, mods, seq, g, w, *, is_ctx, tm=512):
    m = x.shape[0]
    out_shape = [jax.ShapeDtypeStruct((m, D_MODEL), BF16), jax.ShapeDtypeStruct((m, N_QKVU), BF16)]
    out_specs = [pl.BlockSpec((tm, D_MODEL), lambda i: (i, 0)), pl.BlockSpec((tm, N_QKVU), lambda i: (i, 0))]
    scratch = []
    if is_ctx:
        out_shape += [jax.ShapeDtypeStruct((m * N_HEADS_A, HEAD_DIM), F32)] * 2
        out_specs += [pl.BlockSpec((tm * N_HEADS_A, HEAD_DIM), lambda i: (i, 0))] * 2
        out_shape += [jax.ShapeDtypeStruct((D_MODEL, N_QKVU), BF16)]
        out_specs += [pl.BlockSpec(memory_space=pl.ANY)]
        w_spec = pl.BlockSpec(memory_space=pl.ANY)
        scratch = [pltpu.VMEM((D_MODEL, N_QKVU), BF16), pltpu.VMEM((2, D_MODEL, LOAD_CHUNK), F32),
                   pltpu.SemaphoreType.DMA((3,))]
    else:
        w_spec = pl.BlockSpec((D_MODEL, N_QKVU), lambda i: (0, 0), pipeline_mode=pl.Buffered(1))
    return pl.pallas_call(
        functools.partial(_proj_qkvu_kernel, is_ctx=is_ctx),
        grid=(m // tm,),
        in_specs=[
            pl.BlockSpec((tm, D_MODEL), lambda i: (i, 0)),
            pl.BlockSpec((1, N_MOD, D_MODEL), lambda i: (i * tm // seq, 0, 0)),
            pl.BlockSpec((1, D_MODEL), lambda i: (0, 0)),
            w_spec,
        ],
        out_specs=out_specs,
        out_shape=out_shape,
        scratch_shapes=scratch,
        compiler_params=_params(("arbitrary",), 58),
        name="proj_qkvu_ctx" if is_ctx else "proj_qkvu_lat",
    )(x, mods, g, w)


def _proj_gates_kernel(hp_ref, hs_ref, lnv_ref, w_ref, proj_ref, w_res, stage, sems, *, n_ctx, row_parts):
    step = pl.program_id(0)

    @pl.when(step == 0)
    def _():
        _load_cast_weights(w_ref, N_QKVU, w_res, stage, sems)

    n_gate = w_res.shape[1] - D_GMLP
    for rows in _row_parts(hp_ref.shape[0], row_parts):
        h = jnp.where(step < n_ctx, hp_ref[rows, :], hs_ref[rows, :])
        acc = jnp.dot(h, w_res[:, :D_GMLP], preferred_element_type=F32)
        proj_ref[rows, n_gate:] = _layer_norm(jax.nn.gelu(acc), lnv_ref[...]).astype(BF16)
        for blk in range(n_gate // COL_BLOCK):
            cols = slice(blk * COL_BLOCK, (blk + 1) * COL_BLOCK)
            acc = jnp.dot(h, w_res[:, D_GMLP + cols.start:D_GMLP + cols.stop], preferred_element_type=F32)
            proj_ref[rows, cols] = jax.nn.sigmoid(acc).astype(BF16)


def _proj_gates(h_ctx, h_lat, ln_v, w_in, *, tm=512, row_parts=2):
    n_ctx, n_lat = h_ctx.shape[0] // tm, h_lat.shape[0] // tm
    n = D_IN - N_QKVU
    return pl.pallas_call(
        functools.partial(_proj_gates_kernel, n_ctx=n_ctx, row_parts=row_parts),
        grid=(n_ctx + n_lat,),
        in_specs=[
            pl.BlockSpec((tm, D_MODEL), lambda i: (jnp.minimum(i, n_ctx - 1), 0)),
            pl.BlockSpec((tm, D_MODEL), lambda i: (jnp.maximum(i - n_ctx, 0), 0)),
            pl.BlockSpec((1, D_GMLP), lambda i: (0, 0)),
            pl.BlockSpec(memory_space=pl.ANY),
        ],
        out_specs=pl.BlockSpec((tm, n), lambda i: (i, 0)),
        out_shape=jax.ShapeDtypeStruct(((n_ctx + n_lat) * tm, n), BF16),
        scratch_shapes=[pltpu.VMEM((D_MODEL, n), BF16), pltpu.VMEM((2, D_MODEL, LOAD_CHUNK), F32),
                        pltpu.SemaphoreType.DMA((2,))],
        compiler_params=_params(("arbitrary",), 56),
        name="proj_gates",
    )(h_ctx, h_lat, ln_v, w_in)


def _softmax_parts(logits2):
    m = functools.reduce(jnp.maximum, [jnp.max(t, axis=-1, keepdims=True) for t in logits2])
    es = [jnp.exp2(t - m) for t in logits2]
    inv = 1.0 / functools.reduce(jnp.add, [jnp.sum(e, axis=-1, keepdims=True) for e in es])
    return [e.astype(BF16) for e in es], inv


def _qk(q, k):
    return lax.dot_general(q, k, (((1,), (1,)), ((), ())), preferred_element_type=F32) * (ATTN_SCALE * LOG2E)


def _ctx_attn_kernel(q_ref, k_ref, v_ref, o_ref):
    for h in range(N_HEADS_A):
        cols = slice(h * HEAD_DIM, (h + 1) * HEAD_DIM)
        (p,), inv = _softmax_parts([_qk(q_ref[:, cols], k_ref[:, cols])])
        o_ref[:, cols] = (jnp.dot(p, v_ref[:, cols], preferred_element_type=F32) * inv).astype(BF16)


def _ctx_attention(proj, seq):
    m = proj.shape[0]
    spec = lambda col: pl.BlockSpec((seq, D_ATTN), lambda b: (b, col))
    return pl.pallas_call(
        _ctx_attn_kernel,
        grid=(m // seq,),
        in_specs=[spec(0), spec(1), spec(2)],
        out_specs=spec(0),
        out_shape=jax.ShapeDtypeStruct((m, D_ATTN), BF16),
        compiler_params=_params(("parallel",), 32),
        name="ctx_attention",
    )(proj, proj, proj)


def _lat_attn_kernel(q_ref, k_ref, v_ref, kc_ref, vc_ref, tab_ref, o_ref, sw_ref, pw_ref, *, rows, kh):
    heads = tab_ref.shape[0]
    past = kc_ref.shape[1] // N_HEADS_A
    win = kh * GRID_W
    starts = [min(max(r - kh // 2, 0), rows - kh) * GRID_W for r in range(rows)]
    for j in range(heads):
        head = pl.program_id(1) * heads + j
        cols = slice(j * HEAD_DIM, (j + 1) * HEAD_DIM)
        sw, pw = sw_ref.at[j % 2], pw_ref.at[j % 2]
        kc = kc_ref[0, pl.ds(head, past, stride=N_HEADS_A), :].astype(BF16)
        vc = vc_ref[0, pl.ds(head, past, stride=N_HEADS_A), :].astype(BF16)
        s_c = _qk(q_ref[:, cols], kc)
        for r, start in enumerate(starts):
            off = start // GRID_W - r + (KH_MAX - 1)
            lane0 = (off + off % 2) * GRID_W
            bias = tab_ref[j, off % 2, :, lane0:lane0 + win]
            q = q_ref[r * GRID_W:(r + 1) * GRID_W, cols]
            sw[r * GRID_W:(r + 1) * GRID_W, :] = _qk(q, k_ref[start:start + win, cols]) + bias
        (p_w, p_c), inv = _softmax_parts([sw[...], s_c])
        pw[...] = p_w
        o_c = jnp.dot(p_c, vc, preferred_element_type=F32)
        for r, start in enumerate(starts):
            q_rows = slice(r * GRID_W, (r + 1) * GRID_W)
            o_w = jnp.dot(pw[q_rows, :], v_ref[start:start + win, cols], preferred_element_type=F32)
            o_ref[q_rows, cols] = ((o_w + o_c[q_rows, :]) * inv[q_rows, :]).astype(BF16)


def _lat_attention(proj, cache_k, cache_v, table, seq, *, heads_per_step=4):
    m = proj.shape[0]
    rows = seq // GRID_W
    kh = min(KH_MAX, rows)
    n_hp = N_HEADS_A // heads_per_step
    width = heads_per_step * HEAD_DIM
    qkv = lambda part: pl.BlockSpec((seq, width), lambda b, hp: (b, part * n_hp + hp))
    cache = pl.BlockSpec((1,) + cache_k.shape[1:], lambda b, hp: (b, 0, 0))
    return pl.pallas_call(
        functools.partial(_lat_attn_kernel, rows=rows, kh=kh),
        grid=(m // seq, n_hp),
        in_specs=[qkv(0), qkv(1), qkv(2), cache, cache,
                  pl.BlockSpec((heads_per_step,) + table.shape[1:], lambda b, hp: (hp, 0, 0, 0))],
        out_specs=pl.BlockSpec((seq, width), lambda b, hp: (b, hp)),
        out_shape=jax.ShapeDtypeStruct((m, D_ATTN), BF16),
        scratch_shapes=[pltpu.VMEM((2, seq, kh * GRID_W), F32), pltpu.VMEM((2, seq, kh * GRID_W), BF16)],
        compiler_params=_params(("parallel", "arbitrary"), 48),
        name="lat_attention",
    )(proj, proj, proj, cache_k, cache_v, table)


def _merge_kernel(x_ref, mod_ref, oa_ref, gu_ref, vn_ref, ga_ref, gb_ref,
                  ws_ref, bst_ref, wpa_ref, wpb_ref, wo_ref, g_ref, gffn_ref, o_ref, h2_ref,
                  *rest, row_parts, is_first):
    if is_first:
        w_f32 = (wpa_ref, wpb_ref, wo_ref)
        pubs, w_res, (ob_ref, stage, sems) = rest[:3], rest[3:6], rest[6:]
        wpa_ref, wpb_ref, wo_ref = w_res
        step = pl.program_id(0)
        publish = [pltpu.make_async_copy(res, pub, sems.at[2 + n])
                   for n, (res, pub) in enumerate(zip(w_res, pubs))]

        @pl.when(step == 0)
        def _():
            for src, res, pub in zip(w_f32, w_res, publish):
                _load_cast_weights(src, 0, res, stage, sems)
                pub.start()
    else:
        (ob_ref,) = rest
    for part in _row_parts(x_ref.shape[0], row_parts):
        for c in range(part.start // CHUNK, part.stop // CHUNK):
            rows = slice(c * CHUNK, (c + 1) * CHUNK)
            for g in range(N_GROUPS_B):
                cols = slice(g * GROUP_CH, (g + 1) * GROUP_CH)
                s = jnp.dot(ws_ref[g], vn_ref[rows, cols], preferred_element_type=F32) + bst_ref[:, g:g + 1]
                ob_ref[rows, cols] = (gu_ref[rows, cols].astype(F32) * s).astype(BF16)
        pa = jnp.dot(oa_ref[part, :], wpa_ref[...], preferred_element_type=F32)
        pb = jnp.dot(ob_ref[part, :], wpb_ref[...], preferred_element_type=F32)
        mixed = (ga_ref[part, :].astype(F32) * pa + gb_ref[part, :].astype(F32) * pb).astype(BF16)
        y = jnp.dot(mixed, wo_ref[...], preferred_element_type=F32)
        x1 = x_ref[part, :] + mod_ref[0, 2:3, :] * _rms_norm(y, g_ref[...])
        o_ref[part, :] = x1
        h2 = _rms_norm(x1, gffn_ref[...]) * (1.0 + mod_ref[0, 4:5, :]) + mod_ref[0, 3:4, :]
        h2_ref[part, :] = h2.astype(BF16)
    if is_first:
        @pl.when(step == pl.num_programs(0) - 1)
        def _():
            for pub in publish:
                pub.wait()


MERGE_LOAD_CHUNK = 256


def _merge(x, mods, seq, o_a, proj_qkvu, proj_gates, gates_row0, w_s, b_s_t, w_pa, w_pb, w_o, g, g_ffn, *,
           is_first, tm=512, row_parts=2):
    m = x.shape[0]
    assert (tm // row_parts) % CHUNK == 0 and gates_row0 % tm == 0
    g0 = gates_row0 // tm
    whole = lambda a: pl.BlockSpec(a.shape, lambda i: (0,) * a.ndim, pipeline_mode=pl.Buffered(1))
    big = (w_pa, w_pb, w_o)
    out_specs = [pl.BlockSpec((tm, D_MODEL), lambda i: (i, 0))] * 2
    out_shape = [jax.ShapeDtypeStruct((m, D_MODEL), F32), jax.ShapeDtypeStruct((m, D_MODEL), BF16)]
    scratch = [pltpu.VMEM((tm, D_GMLP), BF16)]
    if is_first:
        big_specs = [pl.BlockSpec(memory_space=pl.ANY)] * len(big)
        out_specs += [pl.BlockSpec(memory_space=pl.ANY)] * len(big)
        out_shape += [jax.ShapeDtypeStruct(w.shape, BF16) for w in big]
        stage_rows = max(w.shape[0] for w in big)
        scratch = ([pltpu.VMEM(w.shape, BF16) for w in big] + scratch
                   + [pltpu.VMEM((2, stage_rows, MERGE_LOAD_CHUNK), F32),
                      pltpu.SemaphoreType.DMA((2 + len(big),))])
    else:
        big_specs = [whole(w) for w in big]
    return pl.pallas_call(
        functools.partial(_merge_kernel, row_parts=row_parts, is_first=is_first),
        grid=(m // tm,),
        in_specs=[
            pl.BlockSpec((tm, D_MODEL), lambda i: (i, 0)),
            pl.BlockSpec((1, N_MOD, D_MODEL), lambda i: (i * tm // seq, 0, 0)),
            pl.BlockSpec((tm, D_ATTN), lambda i: (i, 0)),
            pl.BlockSpec((tm, D_GMLP), lambda i: (i, 3)),
            pl.BlockSpec((tm, D_GMLP), lambda i: (g0 + i, 4)),
            pl.BlockSpec((tm, D_MODEL), lambda i: (g0 + i, 0)),
            pl.BlockSpec((tm, D_MODEL), lambda i: (g0 + i, 1)),
            whole(w_s), whole(b_s_t), *big_specs, whole(g), whole(g_ffn),
        ],
        out_specs=out_specs,
        out_shape=out_shape,
        scratch_shapes=scratch,
        compiler_params=_params(("arbitrary",), 60),
        name="merge_ctx" if is_first else "merge_lat",
    )(x, mods, o_a, proj_qkvu, proj_gates, proj_gates, proj_gates,
      w_s, b_s_t, w_pa, w_pb, w_o, g, g_ffn)


def _ffn_x_block(f, n_f, n_x):
    return jnp.clip(f - (n_f - n_x), 0, n_x - 1)


def _ffn_kernel(x_ref, h_ref, mod_ref, gpost_ref, wg_ref, wu_ref, wd_ref, o_ref, acc_ref, *, n_x, row_parts):
    f = pl.program_id(1)
    x_rows = x_ref.shape[0]
    tm = h_ref.shape[0]

    @pl.when(f == 0)
    def _():
        acc_ref[...] = jnp.zeros_like(acc_ref)

    row0 = pl.multiple_of(_ffn_x_block(f, pl.num_programs(1), n_x) * x_rows, x_rows)
    o_ref[pl.ds(row0, x_rows), :] = x_ref[...]

    for part in range(row_parts):
        rows = slice(part * (tm // row_parts), (part + 1) * (tm // row_parts))
        h = h_ref[rows, :]
        gate = jnp.dot(h, wg_ref[...], preferred_element_type=F32)
        up = jnp.dot(h, wu_ref[...], preferred_element_type=F32)
        act = (jax.nn.silu(gate) * up).astype(BF16)
        acc_ref[rows, :] += jnp.dot(act, wd_ref[...], preferred_element_type=F32)

    @pl.when(f == pl.num_programs(1) - 1)
    def _():
        scale = mod_ref[0, 5:6, :] * gpost_ref[...]

        def residual_rows(c, carry):
            rows = pl.ds(pl.multiple_of(c * SUBLANES, SUBLANES), SUBLANES)
            a = acc_ref[rows, :]
            r = lax.rsqrt(jnp.mean(a * a, axis=-1, keepdims=True) + EPS)
            o_ref[rows, :] = o_ref[rows, :] + (a * r) * scale
            return carry

        lax.fori_loop(0, tm // SUBLANES, residual_rows, 0, unroll=32)


def _ffn(x, h, mods, seq, g_post, w_gate, w_up, w_down, *, tm=1024, tf=512, x_rows=128, row_parts=2):
    m = x.shape[0]
    n_f = D_FF // tf
    n_x = tm // x_rows
    assert n_x <= n_f
    return pl.pallas_call(
        functools.partial(_ffn_kernel, n_x=n_x, row_parts=row_parts),
        grid=(m // tm, n_f),
        in_specs=[
            pl.BlockSpec((x_rows, D_MODEL), lambda i, f: (i * n_x + _ffn_x_block(f, n_f, n_x), 0)),
            pl.BlockSpec((tm, D_MODEL), lambda i, f: (i, 0)),
            pl.BlockSpec((1, N_MOD, D_MODEL), lambda i, f: (i * tm // seq, 0, 0)),
            pl.BlockSpec((1, D_MODEL), lambda i, f: (0, 0)),
            pl.BlockSpec((D_MODEL, tf), lambda i, f: (0, f)),
            pl.BlockSpec((D_MODEL, tf), lambda i, f: (0, f)),
            pl.BlockSpec((tf, D_MODEL), lambda i, f: (f, 0)),
        ],
        out_specs=pl.BlockSpec((tm, D_MODEL), lambda i, f: (i, 0)),
        out_shape=jax.ShapeDtypeStruct((m, D_MODEL), F32),
        scratch_shapes=[pltpu.VMEM((tm, D_MODEL), F32)],
        compiler_params=_params(("parallel", "arbitrary"), 58),
        name="ffn",
    )(x, h, mods, g_post, w_gate, w_up, w_down)


def kernel(x_prompt, x_sample, cache_k, cache_v, c, c_ctx, w_ada, b_ada, norm_mix_pre, norm_mix_post,
           norm_ffn_pre, norm_ffn_post, w_in, rpb, ln_v, w_s, b_s, w_pa, w_pb, w_o, w_gate, w_up, w_down):
    assert w_ada.shape[0] == DEPTH == 1
    batch, seq, _ = x_prompt.shape
    dec_batch, dec_seq, _ = x_sample.shape
    past = cache_k.shape[2]

    row = lambda a: a[0].reshape(1, -1)
    bf = lambda a: a[0].astype(BF16)

    cvecs = jnp.concatenate(
        [c, c_ctx[None], jnp.zeros((MOD_ROWS - dec_batch - 1, D_MODEL), F32)], axis=0)
    mods = _modulation(cvecs, w_ada[0], b_ada[0])
    mods_lat = mods[:dec_batch].reshape(dec_batch, N_MOD, D_MODEL)
    mods_ctx = mods[dec_batch:dec_batch + 1].reshape(1, N_MOD, D_MODEL)
    table = _bias_table(rpb[0])

    w_in_f = w_in.reshape(D_MODEL, D_IN)
    mix_head = (bf(w_s), b_s[0].T)
    mix_tail = (row(norm_mix_post), row(norm_ffn_pre))
    ffn = (row(norm_ffn_post), bf(w_gate), bf(w_up), bf(w_down))

    xp = x_prompt.reshape(batch * seq, D_MODEL)
    xs = x_sample.reshape(dec_batch * dec_seq, D_MODEL)
    hp, proj_p, k_p, v_p, w_qkvu = _proj_qkvu(xp, mods_ctx, batch * seq, row(norm_mix_pre), w_in_f, is_ctx=True)
    hs, proj_s = _proj_qkvu(xs, mods_lat, dec_seq, row(norm_mix_pre), w_qkvu, is_ctx=False)
    gates = _proj_gates(hp, hs, row(ln_v), w_in_f)

    oa_p = _ctx_attention(proj_p, seq)
    xp, hp, *mix_bf = _merge(xp, mods_ctx, batch * seq, oa_p, proj_p, gates, 0,
                             *mix_head, w_pa[0], w_pb[0], w_o[0], *mix_tail, is_first=True)
    y_prompt = _ffn(xp, hp, mods_ctx, batch * seq, *ffn).reshape(batch, seq, D_MODEL)

    oa_s = _lat_attention(proj_s, cache_k.reshape(dec_batch, past * N_HEADS_A, HEAD_DIM),
                          cache_v.reshape(dec_batch, past * N_HEADS_A, HEAD_DIM), table, dec_seq)
    xs, hs = _merge(xs, mods_lat, dec_seq, oa_s, proj_s, gates, batch * seq,
                    *mix_head, *mix_bf, *mix_tail, is_first=False)
    y_sample = _ffn(xs, hs, mods_lat, dec_seq, *ffn).reshape(dec_batch, dec_seq, D_MODEL)

    state_shape = (batch, DEPTH, seq, N_HEADS_A, HEAD_DIM)
    return y_prompt, y_sample, k_p.reshape(state_shape), v_p.reshape(state_shape)
```

```python
import functools

import jax
import jax.numpy as jnp
from jax import lax
from jax.experimental import pallas as pl
from jax.experimental.pallas import tpu as pltpu

D_MODEL = 2048
DEPTH = 1
GRID_W = 64
N_HEADS_A = 8
HEAD_DIM = 128
D_ATTN = N_HEADS_A * HEAD_DIM
KH_MAX = 8
KW = 16
CHUNK = 128
N_GROUPS_B = 8
D_GMLP = 1024
GROUP_CH = D_GMLP // N_GROUPS_B
D_FF = ((8 * D_MODEL // 3 + 255) // 256) * 256
N_MOD = 6
EPS = 1e-6
ATTN_SCALE = HEAD_DIM ** -0.5
LOG2E = 1.4426950408889634
D_IN = 3 * D_ATTN + 2 * D_GMLP + 2 * D_MODEL

N_DR = 2 * KH_MAX - 1
N_DC = 2 * KW - 1
COL_BLOCK = 1024
SUBLANES = 8
MOD_ROWS = SUBLANES

F32 = jnp.float32
BF16 = jnp.bfloat16

MIB = 1024 * 1024


def _params(semantics, vmem_mib):
    return pltpu.CompilerParams(dimension_semantics=semantics, vmem_limit_bytes=vmem_mib * MIB)


def _rms_norm(x, g):
    return x * lax.rsqrt(jnp.mean(x * x, axis=-1, keepdims=True) + EPS) * g


def _layer_norm(x, g):
    xc = x - jnp.mean(x, axis=-1, keepdims=True)
    return xc * lax.rsqrt(jnp.mean(xc * xc, axis=-1, keepdims=True) + EPS) * g


def _modulation_kernel(c_ref, w_ref, b_ref, o_ref):
    s = jax.nn.silu(c_ref[...]).astype(BF16)
    o_ref[...] = jnp.dot(s, w_ref[...].astype(BF16), preferred_element_type=F32) + b_ref[...]


def _modulation(cvecs, w_ada, b_ada):
    tn = 1024
    n = N_MOD * D_MODEL
    return pl.pallas_call(
        _modulation_kernel,
        grid=(n // tn,),
        in_specs=[
            pl.BlockSpec((MOD_ROWS, D_MODEL), lambda j: (0, 0)),
            pl.BlockSpec((D_MODEL, tn), lambda j: (0, j)),
            pl.BlockSpec((1, tn), lambda j: (0, j)),
        ],
        out_specs=pl.BlockSpec((MOD_ROWS, tn), lambda j: (0, j)),
        out_shape=jax.ShapeDtypeStruct((MOD_ROWS, n), F32),
        compiler_params=_params(("parallel",), 40),
        name="modulation",
    )(cvecs, w_ada, b_ada.reshape(1, n))


def _bias_table_kernel(rpb_ref, o_ref):
    qc = lax.broadcasted_iota(jnp.int32, (GRID_W, GRID_W), 0)
    kc = lax.broadcasted_iota(jnp.int32, (GRID_W, GRID_W), 1)
    cs = jnp.clip(qc - KW // 2, 0, GRID_W - KW)
    valid = (kc >= cs) & (kc < cs + KW)
    lanes = rpb_ref.shape[-1]
    tiles = []
    for dr in range(N_DR):
        row = jnp.broadcast_to(rpb_ref[0, dr:dr + 1, :], (GRID_W, lanes))
        t = pltpu.roll(row, lanes - (KW - 1), 1, stride=1, stride_axis=0)[:, :GRID_W]
        tiles.append(jnp.where(valid, t * LOG2E, -jnp.inf))
    pad = jnp.zeros((GRID_W, GRID_W), F32)
    o_ref[0, 0] = jnp.concatenate(tiles + [pad], axis=-1)
    o_ref[0, 1] = jnp.concatenate([pad] + tiles, axis=-1)


def _bias_table(rpb):
    width = (N_DR + 1) * GRID_W
    lanes = 128
    rpb_rows = jnp.pad(rpb, ((0, 0), (0, 0), (0, lanes - N_DC)))
    return pl.pallas_call(
        _bias_table_kernel,
        grid=(N_HEADS_A,),
        in_specs=[pl.BlockSpec((1, N_DR, lanes), lambda h: (h, 0, 0))],
        out_specs=pl.BlockSpec((1, 2, GRID_W, width), lambda h: (h, 0, 0, 0)),
        out_shape=jax.ShapeDtypeStruct((N_HEADS_A, 2, GRID_W, width), F32),
        compiler_params=_params(("parallel",), 16),
        name="bias_table",
    )(rpb_rows)


LOAD_ROWS = 256
N_QKVU = 3 * D_ATTN + D_GMLP


def _row_parts(tm, parts):
    return [slice(p * (tm // parts), (p + 1) * (tm // parts)) for p in range(parts)]


def _load_cast_weights(w_hbm, col0, w_res, stage, sems):
    rows, width = w_res.shape
    chunk = stage.shape[1]
    n_chunks = rows // chunk

    def copy(c):
        src = w_hbm.at[pl.ds(c * chunk, chunk), pl.ds(col0, width)]
        return pltpu.make_async_copy(src, stage.at[c % 2], sems.at[c % 2])

    copy(0).start()
    for c in range(n_chunks):
        if c + 1 < n_chunks:
            copy(c + 1).start()
        copy(c).wait()
        w_res[c * chunk:(c + 1) * chunk, :] = stage[c % 2].astype(BF16)


def _proj_qkvu_kernel(x_ref, mod_ref, g_ref, w_ref, h_ref, proj_ref, *rest, is_ctx):
    if is_ctx:
        k_ref, v_ref, wpub_ref, w_res, stage, sems = rest
        step = pl.program_id(0)
        publish = pltpu.make_async_copy(w_res, wpub_ref, sems.at[2])

        @pl.when(step == 0)
        def _():
            _load_cast_weights(w_ref, 0, w_res, stage, sems)
            publish.start()
    else:
        w_res = w_ref
    y = _rms_norm(x_ref[...], g_ref[...])
    h = (y * (1.0 + mod_ref[0, 1:2, :]) + mod_ref[0, 0:1, :]).astype(BF16)
    h_ref[...] = h
    for blk in (3, 0, 1, 2):
        cols = slice(blk * COL_BLOCK, (blk + 1) * COL_BLOCK)
        acc = jnp.dot(h, w_res[:, cols], preferred_element_type=F32)
        if is_ctx and blk in (1, 2):
            kv_ref = (k_ref, v_ref)[blk - 1]
            for head in range(N_HEADS_A):
                dst = pl.ds(head, x_ref.shape[0], stride=N_HEADS_A)
                kv_ref[dst, :] = acc[:, head * HEAD_DIM:(head + 1) * HEAD_DIM]
        if blk == 3:
            acc = jax.nn.gelu(acc)
        proj_ref[:, cols] = acc.astype(BF16)
    if is_ctx:
        @pl.when(step == pl.num_programs(0) - 1)
        def _():
            publish.wait()


def _proj_qkvu(x, mods, seq, g, w, *, is_ctx, tm=512):
    m = x.shape[0]
    out_shape = [jax.ShapeDtypeStruct((m, D_MODEL), BF16), jax.ShapeDtypeStruct((m, N_QKVU), BF16)]
    out_specs = [pl.BlockSpec((tm, D_MODEL), lambda i: (i, 0)), pl.BlockSpec((tm, N_QKVU), lambda i: (i, 0))]
    scratch = []
    if is_ctx:
        out_shape += [jax.ShapeDtypeStruct((m * N_HEADS_A, HEAD_DIM), F32)] * 2
        out_specs += [pl.BlockSpec((tm * N_HEADS_A, HEAD_DIM), lambda i: (i, 0))] * 2
        out_shape += [jax.ShapeDtypeStruct((D_MODEL, N_QKVU), BF16)]
        out_specs += [pl.BlockSpec(memory_space=pl.ANY)]
        w_spec = pl.BlockSpec(memory_space=pl.ANY)
        scratch = [pltpu.VMEM((D_MODEL, N_QKVU), BF16), pltpu.VMEM((2, LOAD_ROWS, N_QKVU), F32),
                   pltpu.SemaphoreType.DMA((3,))]
    else:
        w_spec = pl.BlockSpec((D_MODEL, N_QKVU), lambda i: (0, 0), pipeline_mode=pl.Buffered(1))
    return pl.pallas_call(
        functools.partial(_proj_qkvu_kernel, is_ctx=is_ctx),
        grid=(m // tm,),
        in_specs=[
            pl.BlockSpec((tm, D_MODEL), lambda i: (i, 0)),
            pl.BlockSpec((1, N_MOD, D_MODEL), lambda i: (i * tm // seq, 0, 0)),
            pl.BlockSpec((1, D_MODEL), lambda i: (0, 0)),
            w_spec,
        ],
        out_specs=out_specs,
        out_shape=out_shape,
        scratch_shapes=scratch,
        compiler_params=_params(("arbitrary",), 58),
        name="proj_qkvu_ctx" if is_ctx else "proj_qkvu_lat",
    )(x, mods, g, w)


def _proj_gates_kernel(hp_ref, hs_ref, lnv_ref, w_ref, proj_ref, w_res, stage, sems, *, n_ctx, row_parts):
    step = pl.program_id(0)

    @pl.when(step == 0)
    def _():
        _load_cast_weights(w_ref, N_QKVU, w_res, stage, sems)

    n_gate = w_res.shape[1] - D_GMLP
    for rows in _row_parts(hp_ref.shape[0], row_parts):
        h = jnp.where(step < n_ctx, hp_ref[rows, :], hs_ref[rows, :])
        acc = jnp.dot(h, w_res[:, :D_GMLP], preferred_element_type=F32)
        proj_ref[rows, n_gate:] = _layer_norm(jax.nn.gelu(acc), lnv_ref[...]).astype(BF16)
        for blk in range(n_gate // COL_BLOCK):
            cols = slice(blk * COL_BLOCK, (blk + 1) * COL_BLOCK)
            acc = jnp.dot(h, w_res[:, D_GMLP + cols.start:D_GMLP + cols.stop], preferred_element_type=F32)
            proj_ref[rows, cols] = jax.nn.sigmoid(acc).astype(BF16)


def _proj_gates(h_ctx, h_lat, ln_v, w_in, *, tm=512, row_parts=2):
    n_ctx, n_lat = h_ctx.shape[0] // tm, h_lat.shape[0] // tm
    n = D_IN - N_QKVU
    return pl.pallas_call(
        functools.partial(_proj_gates_kernel, n_ctx=n_ctx, row_parts=row_parts),
        grid=(n_ctx + n_lat,),
        in_specs=[
            pl.BlockSpec((tm, D_MODEL), lambda i: (jnp.minimum(i, n_ctx - 1), 0)),
            pl.BlockSpec((tm, D_MODEL), lambda i: (jnp.maximum(i - n_ctx, 0), 0)),
            pl.BlockSpec((1, D_GMLP), lambda i: (0, 0)),
            pl.BlockSpec(memory_space=pl.ANY),
        ],
        out_specs=pl.BlockSpec((tm, n), lambda i: (i, 0)),
        out_shape=jax.ShapeDtypeStruct(((n_ctx + n_lat) * tm, n), BF16),
        scratch_shapes=[pltpu.VMEM((D_MODEL, n), BF16), pltpu.VMEM((2, LOAD_ROWS, n), F32),
                        pltpu.SemaphoreType.DMA((2,))],
        compiler_params=_params(("arbitrary",), 56),
        name="proj_gates",
    )(h_ctx, h_lat, ln_v, w_in)


def _softmax_parts(logits2):
    m = functools.reduce(jnp.maximum, [jnp.max(t, axis=-1, keepdims=True) for t in logits2])
    es = [jnp.exp2(t - m) for t in logits2]
    inv = 1.0 / functools.reduce(jnp.add, [jnp.sum(e, axis=-1, keepdims=True) for e in es])
    return [e.astype(BF16) for e in es], inv


def _qk(q, k):
    return lax.dot_general(q, k, (((1,), (1,)), ((), ())), preferred_element_type=F32) * (ATTN_SCALE * LOG2E)


def _ctx_attn_kernel(q_ref, k_ref, v_ref, o_ref):
    for h in range(N_HEADS_A):
        cols = slice(h * HEAD_DIM, (h + 1) * HEAD_DIM)
        (p,), inv = _softmax_parts([_qk(q_ref[:, cols], k_ref[:, cols])])
        o_ref[:, cols] = (jnp.dot(p, v_ref[:, cols], preferred_element_type=F32) * inv).astype(BF16)


def _ctx_attention(proj, seq):
    m = proj.shape[0]
    spec = lambda col: pl.BlockSpec((seq, D_ATTN), lambda b: (b, col))
    return pl.pallas_call(
        _ctx_attn_kernel,
        grid=(m // seq,),
        in_specs=[spec(0), spec(1), spec(2)],
        out_specs=spec(0),
        out_shape=jax.ShapeDtypeStruct((m, D_ATTN), BF16),
        compiler_params=_params(("parallel",), 32),
        name="ctx_attention",
    )(proj, proj, proj)


def _lat_attn_kernel(q_ref, k_ref, v_ref, kc_ref, vc_ref, tab_ref, o_ref, sw_ref, pw_ref, *, rows, kh):
    heads = tab_ref.shape[0]
    past = kc_ref.shape[1] // N_HEADS_A
    win = kh * GRID_W
    starts = [min(max(r - kh // 2, 0), rows - kh) * GRID_W for r in range(rows)]
    for j in range(heads):
        head = pl.program_id(1) * heads + j
        cols = slice(j * HEAD_DIM, (j + 1) * HEAD_DIM)
        sw, pw = sw_ref.at[j % 2], pw_ref.at[j % 2]
        kc = kc_ref[0, pl.ds(head, past, stride=N_HEADS_A), :].astype(BF16)
        vc = vc_ref[0, pl.ds(head, past, stride=N_HEADS_A), :].astype(BF16)
        s_c = _qk(q_ref[:, cols], kc)
        for r, start in enumerate(starts):
            off = start // GRID_W - r + (KH_MAX - 1)
            lane0 = (off + off % 2) * GRID_W
            bias = tab_ref[j, off % 2, :, lane0:lane0 + win]
            q = q_ref[r * GRID_W:(r + 1) * GRID_W, cols]
            sw[r * GRID_W:(r + 1) * GRID_W, :] = _qk(q, k_ref[start:start + win, cols]) + bias
        (p_w, p_c), inv = _softmax_parts([sw[...], s_c])
        pw[...] = p_w
        o_c = jnp.dot(p_c, vc, preferred_element_type=F32)
        for r, start in enumerate(starts):
            q_rows = slice(r * GRID_W, (r + 1) * GRID_W)
            o_w = jnp.dot(pw[q_rows, :], v_ref[start:start + win, cols], preferred_element_type=F32)
            o_ref[q_rows, cols] = ((o_w + o_c[q_rows, :]) * inv[q_rows, :]).astype(BF16)


def _lat_attention(proj, cache_k, cache_v, table, seq, *, heads_per_step=4):
    m = proj.shape[0]
    rows = seq // GRID_W
    kh = min(KH_MAX, rows)
    n_hp = N_HEADS_A // heads_per_step
    width = heads_per_step * HEAD_DIM
    qkv = lambda part: pl.BlockSpec((seq, width), lambda b, hp: (b, part * n_hp + hp))
    cache = pl.BlockSpec((1,) + cache_k.shape[1:], lambda b, hp: (b, 0, 0))
    return pl.pallas_call(
        functools.partial(_lat_attn_kernel, rows=rows, kh=kh),
        grid=(m // seq, n_hp),
        in_specs=[qkv(0), qkv(1), qkv(2), cache, cache,
                  pl.BlockSpec((heads_per_step,) + table.shape[1:], lambda b, hp: (hp, 0, 0, 0))],
        out_specs=pl.BlockSpec((seq, width), lambda b, hp: (b, hp)),
        out_shape=jax.ShapeDtypeStruct((m, D_ATTN), BF16),
        scratch_shapes=[pltpu.VMEM((2, seq, kh * GRID_W), F32), pltpu.VMEM((2, seq, kh * GRID_W), BF16)],
        compiler_params=_params(("parallel", "arbitrary"), 48),
        name="lat_attention",
    )(proj, proj, proj, cache_k, cache_v, table)


def _merge_kernel(x_ref, mod_ref, oa_ref, gu_ref, vn_ref, ga_ref, gb_ref,
                  ws_ref, bst_ref, wpa_ref, wpb_ref, wo_ref, g_ref, gffn_ref, o_ref, h2_ref,
                  *rest, row_parts, is_first):
    if is_first:
        w_f32 = (wpa_ref, wpb_ref, wo_ref)
        pubs, w_res, (ob_ref, stage, sems) = rest[:3], rest[3:6], rest[6:]
        wpa_ref, wpb_ref, wo_ref = w_res
        step = pl.program_id(0)
        publish = [pltpu.make_async_copy(res, pub, sems.at[2 + n])
                   for n, (res, pub) in enumerate(zip(w_res, pubs))]

        @pl.when(step == 0)
        def _():
            for src, res, pub in zip(w_f32, w_res, publish):
                _load_cast_weights(src, 0, res, stage, sems)
                pub.start()
    else:
        (ob_ref,) = rest
    for part in _row_parts(x_ref.shape[0], row_parts):
        for c in range(part.start // CHUNK, part.stop // CHUNK):
            rows = slice(c * CHUNK, (c + 1) * CHUNK)
            for g in range(N_GROUPS_B):
                cols = slice(g * GROUP_CH, (g + 1) * GROUP_CH)
                s = jnp.dot(ws_ref[g], vn_ref[rows, cols], preferred_element_type=F32) + bst_ref[:, g:g + 1]
                ob_ref[rows, cols] = (gu_ref[rows, cols].astype(F32) * s).astype(BF16)
        pa = jnp.dot(oa_ref[part, :], wpa_ref[...], preferred_element_type=F32)
        pb = jnp.dot(ob_ref[part, :], wpb_ref[...], preferred_element_type=F32)
        mixed = (ga_ref[part, :].astype(F32) * pa + gb_ref[part, :].astype(F32) * pb).astype(BF16)
        y = jnp.dot(mixed, wo_ref[...], preferred_element_type=F32)
        x1 = x_ref[part, :] + mod_ref[0, 2:3, :] * _rms_norm(y, g_ref[...])
        o_ref[part, :] = x1
        h2 = _rms_norm(x1, gffn_ref[...]) * (1.0 + mod_ref[0, 4:5, :]) + mod_ref[0, 3:4, :]
        h2_ref[part, :] = h2.astype(BF16)
    if is_first:
        @pl.when(step == pl.num_programs(0) - 1)
        def _():
            for pub in publish:
                pub.wait()


def _merge(x, mods, seq, o_a, proj_qkvu, proj_gates, gates_row0, w_s, b_s_t, w_pa, w_pb, w_o, g, g_ffn, *,
           is_first, tm=512, row_parts=2):
    m = x.shape[0]
    assert (tm // row_parts) % CHUNK == 0 and gates_row0 % tm == 0
    g0 = gates_row0 // tm
    whole = lambda a: pl.BlockSpec(a.shape, lambda i: (0,) * a.ndim, pipeline_mode=pl.Buffered(1))
    big = (w_pa, w_pb, w_o)
    out_specs = [pl.BlockSpec((tm, D_MODEL), lambda i: (i, 0))] * 2
    out_shape = [jax.ShapeDtypeStruct((m, D_MODEL), F32), jax.ShapeDtypeStruct((m, D_MODEL), BF16)]
    scratch = [pltpu.VMEM((tm, D_GMLP), BF16)]
    if is_first:
        big_specs = [pl.BlockSpec(memory_space=pl.ANY)] * len(big)
        out_specs += [pl.BlockSpec(memory_space=pl.ANY)] * len(big)
        out_shape += [jax.ShapeDtypeStruct(w.shape, BF16) for w in big]
        assert all(w.shape[1] == D_MODEL for w in big)
        scratch = ([pltpu.VMEM(w.shape, BF16) for w in big] + scratch
                   + [pltpu.VMEM((2, LOAD_ROWS, D_MODEL), F32), pltpu.SemaphoreType.DMA((2 + len(big),))])
    else:
        big_specs = [whole(w) for w in big]
    return pl.pallas_call(
        functools.partial(_merge_kernel, row_parts=row_parts, is_first=is_first),
        grid=(m // tm,),
        in_specs=[
            pl.BlockSpec((tm, D_MODEL), lambda i: (i, 0)),
            pl.BlockSpec((1, N_MOD, D_MODEL), lambda i: (i * tm // seq, 0, 0)),
            pl.BlockSpec((tm, D_ATTN), lambda i: (i, 0)),
            pl.BlockSpec((tm, D_GMLP), lambda i: (i, 3)),
            pl.BlockSpec((tm, D_GMLP), lambda i: (g0 + i, 4)),
            pl.BlockSpec((tm, D_MODEL), lambda i: (g0 + i, 0)),
            pl.BlockSpec((tm, D_MODEL), lambda i: (g0 + i, 1)),
            whole(w_s), whole(b_s_t), *big_specs, whole(g), whole(g_ffn),
        ],
        out_specs=out_specs,
        out_shape=out_shape,
        scratch_shapes=scratch,
        compiler_params=_params(("arbitrary",), 60),
        name="merge_ctx" if is_first else "merge_lat",
    )(x, mods, o_a, proj_qkvu, proj_gates, proj_gates, proj_gates,
      w_s, b_s_t, w_pa, w_pb, w_o, g, g_ffn)


def _ffn_x_block(f, n_f, n_x):
    return jnp.clip(f - (n_f - n_x), 0, n_x - 1)


def _ffn_kernel(x_ref, h_ref, mod_ref, gpost_ref, wg_ref, wu_ref, wd_ref, o_ref, acc_ref, *, n_x, row_parts):
    f = pl.program_id(1)
    x_rows = x_ref.shape[0]
    tm = h_ref.shape[0]

    @pl.when(f == 0)
    def _():
        acc_ref[...] = jnp.zeros_like(acc_ref)

    row0 = pl.multiple_of(_ffn_x_block(f, pl.num_programs(1), n_x) * x_rows, x_rows)
    o_ref[pl.ds(row0, x_rows), :] = x_ref[...]

    for part in range(row_parts):
        rows = slice(part * (tm // row_parts), (part + 1) * (tm // row_parts))
        h = h_ref[rows, :]
        gate = jnp.dot(h, wg_ref[...], preferred_element_type=F32)
        up = jnp.dot(h, wu_ref[...], preferred_element_type=F32)
        act = (jax.nn.silu(gate) * up).astype(BF16)
        acc_ref[rows, :] += jnp.dot(act, wd_ref[...], preferred_element_type=F32)

    @pl.when(f == pl.num_programs(1) - 1)
    def _():
        scale = mod_ref[0, 5:6, :] * gpost_ref[...]

        def residual_rows(c, carry):
            rows = pl.ds(pl.multiple_of(c * SUBLANES, SUBLANES), SUBLANES)
            a = acc_ref[rows, :]
            r = lax.rsqrt(jnp.mean(a * a, axis=-1, keepdims=True) + EPS)
            o_ref[rows, :] = o_ref[rows, :] + (a * r) * scale
            return carry

        lax.fori_loop(0, tm // SUBLANES, residual_rows, 0, unroll=32)


def _ffn(x, h, mods, seq, g_post, w_gate, w_up, w_down, *, tm=1024, tf=512, x_rows=128, row_parts=2):
    m = x.shape[0]
    n_f = D_FF // tf
    n_x = tm // x_rows
    assert n_x <= n_f
    return pl.pallas_call(
        functools.partial(_ffn_kernel, n_x=n_x, row_parts=row_parts),
        grid=(m // tm, n_f),
        in_specs=[
            pl.BlockSpec((x_rows, D_MODEL), lambda i, f: (i * n_x + _ffn_x_block(f, n_f, n_x), 0)),
            pl.BlockSpec((tm, D_MODEL), lambda i, f: (i, 0)),
            pl.BlockSpec((1, N_MOD, D_MODEL), lambda i, f: (i * tm // seq, 0, 0)),
            pl.BlockSpec((1, D_MODEL), lambda i, f: (0, 0)),
            pl.BlockSpec((D_MODEL, tf), lambda i, f: (0, f)),
            pl.BlockSpec((D_MODEL, tf), lambda i, f: (0, f)),
            pl.BlockSpec((tf, D_MODEL), lambda i, f: (f, 0)),
        ],
        out_specs=pl.BlockSpec((tm, D_MODEL), lambda i, f: (i, 0)),
        out_shape=jax.ShapeDtypeStruct((m, D_MODEL), F32),
        scratch_shapes=[pltpu.VMEM((tm, D_MODEL), F32)],
        compiler_params=_params(("parallel", "arbitrary"), 58),
        name="ffn",
    )(x, h, mods, g_post, w_gate, w_up, w_down)


def kernel(x_prompt, x_sample, cache_k, cache_v, c, c_ctx, w_ada, b_ada, norm_mix_pre, norm_mix_post,
           norm_ffn_pre, norm_ffn_post, w_in, rpb, ln_v, w_s, b_s, w_pa, w_pb, w_o, w_gate, w_up, w_down):
    assert w_ada.shape[0] == DEPTH == 1
    batch, seq, _ = x_prompt.shape
    dec_batch, dec_seq, _ = x_sample.shape
    past = cache_k.shape[2]

    row = lambda a: a[0].reshape(1, -1)
    bf = lambda a: a[0].astype(BF16)

    cvecs = jnp.concatenate(
        [c, c_ctx[None], jnp.zeros((MOD_ROWS - dec_batch - 1, D_MODEL), F32)], axis=0)
    mods = _modulation(cvecs, w_ada[0], b_ada[0])
    mods_lat = mods[:dec_batch].reshape(dec_batch, N_MOD, D_MODEL)
    mods_ctx = mods[dec_batch:dec_batch + 1].reshape(1, N_MOD, D_MODEL)
    table = _bias_table(rpb[0])

    w_in_f = w_in.reshape(D_MODEL, D_IN)
    mix_head = (bf(w_s), b_s[0].T)
    mix_tail = (row(norm_mix_post), row(norm_ffn_pre))
    ffn = (row(norm_ffn_post), bf(w_gate), bf(w_up), bf(w_down))

    xp = x_prompt.reshape(batch * seq, D_MODEL)
    xs = x_sample.reshape(dec_batch * dec_seq, D_MODEL)
    hp, proj_p, k_p, v_p, w_qkvu = _proj_qkvu(xp, mods_ctx, batch * seq, row(norm_mix_pre), w_in_f, is_ctx=True)
    hs, proj_s = _proj_qkvu(xs, mods_lat, dec_seq, row(norm_mix_pre), w_qkvu, is_ctx=False)
    gates = _proj_gates(hp, hs, row(ln_v), w_in_f)

    oa_p = _ctx_attention(proj_p, seq)
    xp, hp, *mix_bf = _merge(xp, mods_ctx, batch * seq, oa_p, proj_p, gates, 0,
                             *mix_head, w_pa[0], w_pb[0], w_o[0], *mix_tail, is_first=True)
    y_prompt = _ffn(xp, hp, mods_ctx, batch * seq, *ffn).reshape(batch, seq, D_MODEL)

    oa_s = _lat_attention(proj_s, cache_k.reshape(dec_batch, past * N_HEADS_A, HEAD_DIM),
                          cache_v.reshape(dec_batch, past * N_HEADS_A, HEAD_DIM), table, dec_seq)
    xs, hs = _merge(xs, mods_lat, dec_seq, oa_s, proj_s, gates, batch * seq,
                    *mix_head, *mix_bf, *mix_tail, is_first=False)
    y_sample = _ffn(xs, hs, mods_lat, dec_seq, *ffn).reshape(dec_batch, dec_seq, D_MODEL)

    state_shape = (batch, DEPTH, seq, N_HEADS_A, HEAD_DIM)
    return y_prompt, y_sample, k_p.reshape(state_shape), v_p.reshape(state_shape)
```

```python
import functools

import jax
import jax.numpy as jnp
from jax import lax
from jax.experimental import pallas as pl
from jax.experimental.pallas import tpu as pltpu

D_MODEL = 2048
DEPTH = 1
GRID_W = 64
N_HEADS_A = 8
HEAD_DIM = 128
D_ATTN = N_HEADS_A * HEAD_DIM
KH_MAX = 8
KW = 16
CHUNK = 128
N_GROUPS_B = 8
D_GMLP = 1024
GROUP_CH = D_GMLP // N_GROUPS_B
D_FF = ((8 * D_MODEL // 3 + 255) // 256) * 256
N_MOD = 6
EPS = 1e-6
ATTN_SCALE = HEAD_DIM ** -0.5
LOG2E = 1.4426950408889634
D_IN = 3 * D_ATTN + 2 * D_GMLP + 2 * D_MODEL

N_DR = 2 * KH_MAX - 1
N_DC = 2 * KW - 1
COL_BLOCK = 1024
SUBLANES = 8
MOD_ROWS = SUBLANES

F32 = jnp.float32
BF16 = jnp.bfloat16

MIB = 1024 * 1024


def _params(semantics, vmem_mib):
    return pltpu.CompilerParams(dimension_semantics=semantics, vmem_limit_bytes=vmem_mib * MIB)


def _rms_norm(x, g):
    return x * lax.rsqrt(jnp.mean(x * x, axis=-1, keepdims=True) + EPS) * g


def _layer_norm(x, g):
    xc = x - jnp.mean(x, axis=-1, keepdims=True)
    return xc * lax.rsqrt(jnp.mean(xc * xc, axis=-1, keepdims=True) + EPS) * g


def _modulation_kernel(c_ref, w_ref, b_ref, o_ref):
    s = jax.nn.silu(c_ref[...]).astype(BF16)
    o_ref[...] = jnp.dot(s, w_ref[...].astype(BF16), preferred_element_type=F32) + b_ref[...]


def _modulation(cvecs, w_ada, b_ada):
    tn = 1024
    n = N_MOD * D_MODEL
    return pl.pallas_call(
        _modulation_kernel,
        grid=(n // tn,),
        in_specs=[
            pl.BlockSpec((MOD_ROWS, D_MODEL), lambda j: (0, 0)),
            pl.BlockSpec((D_MODEL, tn), lambda j: (0, j)),
            pl.BlockSpec((1, tn), lambda j: (0, j)),
        ],
        out_specs=pl.BlockSpec((MOD_ROWS, tn), lambda j: (0, j)),
        out_shape=jax.ShapeDtypeStruct((MOD_ROWS, n), F32),
        compiler_params=_params(("parallel",), 40),
        name="modulation",
    )(cvecs, w_ada, b_ada.reshape(1, n))


def _bias_table_kernel(rpb_ref, o_ref):
    qc = lax.broadcasted_iota(jnp.int32, (GRID_W, GRID_W), 0)
    kc = lax.broadcasted_iota(jnp.int32, (GRID_W, GRID_W), 1)
    cs = jnp.clip(qc - KW // 2, 0, GRID_W - KW)
    valid = (kc >= cs) & (kc < cs + KW)
    lanes = rpb_ref.shape[-1]
    tiles = []
    for dr in range(N_DR):
        row = jnp.broadcast_to(rpb_ref[0, dr:dr + 1, :], (GRID_W, lanes))
        t = pltpu.roll(row, lanes - (KW - 1), 1, stride=1, stride_axis=0)[:, :GRID_W]
        tiles.append(jnp.where(valid, t * LOG2E, -jnp.inf))
    pad = jnp.zeros((GRID_W, GRID_W), F32)
    o_ref[0, 0] = jnp.concatenate(tiles + [pad], axis=-1)
    o_ref[0, 1] = jnp.concatenate([pad] + tiles, axis=-1)


def _bias_table(rpb):
    width = (N_DR + 1) * GRID_W
    lanes = 128
    rpb_rows = jnp.pad(rpb, ((0, 0), (0, 0), (0, lanes - N_DC)))
    return pl.pallas_call(
        _bias_table_kernel,
        grid=(N_HEADS_A,),
        in_specs=[pl.BlockSpec((1, N_DR, lanes), lambda h: (h, 0, 0))],
        out_specs=pl.BlockSpec((1, 2, GRID_W, width), lambda h: (h, 0, 0, 0)),
        out_shape=jax.ShapeDtypeStruct((N_HEADS_A, 2, GRID_W, width), F32),
        compiler_params=_params(("parallel",), 16),
        name="bias_table",
    )(rpb_rows)


LOAD_ROWS = 256
N_QKVU = 3 * D_ATTN + D_GMLP


def _row_parts(tm, parts):
    return [slice(p * (tm // parts), (p + 1) * (tm // parts)) for p in range(parts)]


def _load_cast_weights(w_hbm, col0, w_res, stage, sems):
    rows, width = w_res.shape
    chunk = stage.shape[1]
    n_chunks = rows // chunk

    def copy(c):
        src = w_hbm.at[pl.ds(c * chunk, chunk), pl.ds(col0, width)]
        return pltpu.make_async_copy(src, stage.at[c % 2], sems.at[c % 2])

    copy(0).start()
    for c in range(n_chunks):
        if c + 1 < n_chunks:
            copy(c + 1).start()
        copy(c).wait()
        w_res[c * chunk:(c + 1) * chunk, :] = stage[c % 2].astype(BF16)


def _proj_qkvu_kernel(x_ref, mod_ref, g_ref, w_ref, h_ref, proj_ref, *rest, is_ctx):
    if is_ctx:
        k_ref, v_ref, wpub_ref, w_res, stage, sems = rest
        step = pl.program_id(0)
        publish = pltpu.make_async_copy(w_res, wpub_ref, sems.at[2])

        @pl.when(step == 0)
        def _():
            _load_cast_weights(w_ref, 0, w_res, stage, sems)
            publish.start()
    else:
        w_res = w_ref
    y = _rms_norm(x_ref[...], g_ref[...])
    h = (y * (1.0 + mod_ref[0, 1:2, :]) + mod_ref[0, 0:1, :]).astype(BF16)
    h_ref[...] = h
    for blk in (3, 0, 1, 2):
        cols = slice(blk * COL_BLOCK, (blk + 1) * COL_BLOCK)
        acc = jnp.dot(h, w_res[:, cols], preferred_element_type=F32)
        if is_ctx and blk in (1, 2):
            kv_ref = (k_ref, v_ref)[blk - 1]
            for head in range(N_HEADS_A):
                dst = pl.ds(head, x_ref.shape[0], stride=N_HEADS_A)
                kv_ref[dst, :] = acc[:, head * HEAD_DIM:(head + 1) * HEAD_DIM]
        if blk == 3:
            acc = jax.nn.gelu(acc)
        proj_ref[:, cols] = acc.astype(BF16)
    if is_ctx:
        @pl.when(step == pl.num_programs(0) - 1)
        def _():
            publish.wait()


def _proj_qkvu(x, mods, seq, g, w, *, is_ctx, tm=512):
    m = x.shape[0]
    out_shape = [jax.ShapeDtypeStruct((m, D_MODEL), BF16), jax.ShapeDtypeStruct((m, N_QKVU), BF16)]
    out_specs = [pl.BlockSpec((tm, D_MODEL), lambda i: (i, 0)), pl.BlockSpec((tm, N_QKVU), lambda i: (i, 0))]
    scratch = []
    if is_ctx:
        out_shape += [jax.ShapeDtypeStruct((m * N_HEADS_A, HEAD_DIM), F32)] * 2
        out_specs += [pl.BlockSpec((tm * N_HEADS_A, HEAD_DIM), lambda i: (i, 0))] * 2
        out_shape += [jax.ShapeDtypeStruct((D_MODEL, N_QKVU), BF16)]
        out_specs += [pl.BlockSpec(memory_space=pl.ANY)]
        w_spec = pl.BlockSpec(memory_space=pl.ANY)
        scratch = [pltpu.VMEM((D_MODEL, N_QKVU), BF16), pltpu.VMEM((2, LOAD_ROWS, N_QKVU), F32),
                   pltpu.SemaphoreType.DMA((3,))]
    else:
        w_spec = pl.BlockSpec((D_MODEL, N_QKVU), lambda i: (0, 0), pipeline_mode=pl.Buffered(1))
    return pl.pallas_call(
        functools.partial(_proj_qkvu_kernel, is_ctx=is_ctx),
        grid=(m // tm,),
        in_specs=[
            pl.BlockSpec((tm, D_MODEL), lambda i: (i, 0)),
            pl.BlockSpec((1, N_MOD, D_MODEL), lambda i: (i * tm // seq, 0, 0)),
            pl.BlockSpec((1, D_MODEL), lambda i: (0, 0)),
            w_spec,
        ],
        out_specs=out_specs,
        out_shape=out_shape,
        scratch_shapes=scratch,
        compiler_params=_params(("arbitrary",), 58),
        name="proj_qkvu_ctx" if is_ctx else "proj_qkvu_lat",
    )(x, mods, g, w)


def _proj_gates_kernel(hp_ref, hs_ref, lnv_ref, w_ref, proj_ref, w_res, stage, sems, *, n_ctx, row_parts):
    step = pl.program_id(0)

    @pl.when(step == 0)
    def _():
        _load_cast_weights(w_ref, N_QKVU, w_res, stage, sems)

    n_gate = w_res.shape[1] - D_GMLP
    for rows in _row_parts(hp_ref.shape[0], row_parts):
        h = jnp.where(step < n_ctx, hp_ref[rows, :], hs_ref[rows, :])
        acc = jnp.dot(h, w_res[:, :D_GMLP], preferred_element_type=F32)
        proj_ref[rows, n_gate:] = _layer_norm(jax.nn.gelu(acc), lnv_ref[...]).astype(BF16)
        for blk in range(n_gate // COL_BLOCK):
            cols = slice(blk * COL_BLOCK, (blk + 1) * COL_BLOCK)
            acc = jnp.dot(h, w_res[:, D_GMLP + cols.start:D_GMLP + cols.stop], preferred_element_type=F32)
            proj_ref[rows, cols] = jax.nn.sigmoid(acc).astype(BF16)


def _proj_gates(h_ctx, h_lat, ln_v, w_in, *, tm=512, row_parts=2):
    n_ctx, n_lat = h_ctx.shape[0] // tm, h_lat.shape[0] // tm
    n = D_IN - N_QKVU
    return pl.pallas_call(
        functools.partial(_proj_gates_kernel, n_ctx=n_ctx, row_parts=row_parts),
        grid=(n_ctx + n_lat,),
        in_specs=[
            pl.BlockSpec((tm, D_MODEL), lambda i: (jnp.minimum(i, n_ctx - 1), 0)),
            pl.BlockSpec((tm, D_MODEL), lambda i: (jnp.maximum(i - n_ctx, 0), 0)),
            pl.BlockSpec((1, D_GMLP), lambda i: (0, 0)),
            pl.BlockSpec(memory_space=pl.ANY),
        ],
        out_specs=pl.BlockSpec((tm, n), lambda i: (i, 0)),
        out_shape=jax.ShapeDtypeStruct(((n_ctx + n_lat) * tm, n), BF16),
        scratch_shapes=[pltpu.VMEM((D_MODEL, n), BF16), pltpu.VMEM((2, LOAD_ROWS, n), F32),
                        pltpu.SemaphoreType.DMA((2,))],
        compiler_params=_params(("arbitrary",), 56),
        name="proj_gates",
    )(h_ctx, h_lat, ln_v, w_in)


def _softmax_parts(logits2):
    m = functools.reduce(jnp.maximum, [jnp.max(t, axis=-1, keepdims=True) for t in logits2])
    es = [jnp.exp2(t - m) for t in logits2]
    inv = 1.0 / functools.reduce(jnp.add, [jnp.sum(e, axis=-1, keepdims=True) for e in es])
    return [e.astype(BF16) for e in es], inv


def _qk(q, k):
    return lax.dot_general(q, k, (((1,), (1,)), ((), ())), preferred_element_type=F32) * (ATTN_SCALE * LOG2E)


def _ctx_attn_kernel(q_ref, k_ref, v_ref, o_ref):
    for h in range(N_HEADS_A):
        cols = slice(h * HEAD_DIM, (h + 1) * HEAD_DIM)
        (p,), inv = _softmax_parts([_qk(q_ref[:, cols], k_ref[:, cols])])
        o_ref[:, cols] = (jnp.dot(p, v_ref[:, cols], preferred_element_type=F32) * inv).astype(BF16)


def _ctx_attention(proj, seq):
    m = proj.shape[0]
    spec = lambda col: pl.BlockSpec((seq, D_ATTN), lambda b: (b, col))
    return pl.pallas_call(
        _ctx_attn_kernel,
        grid=(m // seq,),
        in_specs=[spec(0), spec(1), spec(2)],
        out_specs=spec(0),
        out_shape=jax.ShapeDtypeStruct((m, D_ATTN), BF16),
        compiler_params=_params(("parallel",), 32),
        name="ctx_attention",
    )(proj, proj, proj)


def _lat_attn_kernel(q_ref, k_ref, v_ref, kc_ref, vc_ref, tab_ref, o_ref, sw_ref, pw_ref, *, rows, kh):
    heads = tab_ref.shape[0]
    past = kc_ref.shape[1] // N_HEADS_A
    win = kh * GRID_W
    starts = [min(max(r - kh // 2, 0), rows - kh) * GRID_W for r in range(rows)]
    for j in range(heads):
        head = pl.program_id(1) * heads + j
        cols = slice(j * HEAD_DIM, (j + 1) * HEAD_DIM)
        sw, pw = sw_ref.at[j % 2], pw_ref.at[j % 2]
        kc = kc_ref[0, pl.ds(head, past, stride=N_HEADS_A), :].astype(BF16)
        vc = vc_ref[0, pl.ds(head, past, stride=N_HEADS_A), :].astype(BF16)
        s_c = _qk(q_ref[:, cols], kc)
        for r, start in enumerate(starts):
            off = start // GRID_W - r + (KH_MAX - 1)
            lane0 = (off + off % 2) * GRID_W
            bias = tab_ref[j, off % 2, :, lane0:lane0 + win]
            q = q_ref[r * GRID_W:(r + 1) * GRID_W, cols]
            sw[r * GRID_W:(r + 1) * GRID_W, :] = _qk(q, k_ref[start:start + win, cols]) + bias
        (p_w, p_c), inv = _softmax_parts([sw[...], s_c])
        pw[...] = p_w
        o_c = jnp.dot(p_c, vc, preferred_element_type=F32)
        for r, start in enumerate(starts):
            q_rows = slice(r * GRID_W, (r + 1) * GRID_W)
            o_w = jnp.dot(pw[q_rows, :], v_ref[start:start + win, cols], preferred_element_type=F32)
            o_ref[q_rows, cols] = ((o_w + o_c[q_rows, :]) * inv[q_rows, :]).astype(BF16)


def _lat_attention(proj, cache_k, cache_v, table, seq, *, heads_per_step=4):
    m = proj.shape[0]
    rows = seq // GRID_W
    kh = min(KH_MAX, rows)
    n_hp = N_HEADS_A // heads_per_step
    width = heads_per_step * HEAD_DIM
    qkv = lambda part: pl.BlockSpec((seq, width), lambda b, hp: (b, part * n_hp + hp))
    cache = pl.BlockSpec((1,) + cache_k.shape[1:], lambda b, hp: (b, 0, 0))
    return pl.pallas_call(
        functools.partial(_lat_attn_kernel, rows=rows, kh=kh),
        grid=(m // seq, n_hp),
        in_specs=[qkv(0), qkv(1), qkv(2), cache, cache,
                  pl.BlockSpec((heads_per_step,) + table.shape[1:], lambda b, hp: (hp, 0, 0, 0))],
        out_specs=pl.BlockSpec((seq, width), lambda b, hp: (b, hp)),
        out_shape=jax.ShapeDtypeStruct((m, D_ATTN), BF16),
        scratch_shapes=[pltpu.VMEM((2, seq, kh * GRID_W), F32), pltpu.VMEM((2, seq, kh * GRID_W), BF16)],
        compiler_params=_params(("parallel", "arbitrary"), 48),
        name="lat_attention",
    )(proj, proj, proj, cache_k, cache_v, table)


def _merge_kernel(x_ref, mod_ref, oa_ref, gu_ref, vn_ref, ga_ref, gb_ref,
                  ws_ref, bst_ref, wpa_ref, wpb_ref, wo_ref, g_ref, gffn_ref, o_ref, h2_ref,
                  *rest, row_parts, is_first):
    if is_first:
        w_f32 = (wpa_ref, wpb_ref, wo_ref)
        pubs, w_res, (ob_ref, stage, sems) = rest[:3], rest[3:6], rest[6:]
        wpa_ref, wpb_ref, wo_ref = w_res
        step = pl.program_id(0)
        publish = [pltpu.make_async_copy(res, pub, sems.at[2 + n])
                   for n, (res, pub) in enumerate(zip(w_res, pubs))]

        @pl.when(step == 0)
        def _():
            for src, res, pub in zip(w_f32, w_res, publish):
                _load_cast_weights(src, 0, res, stage, sems)
                pub.start()
    else:
        (ob_ref,) = rest
    for part in _row_parts(x_ref.shape[0], row_parts):
        for c in range(part.start // CHUNK, part.stop // CHUNK):
            rows = slice(c * CHUNK, (c + 1) * CHUNK)
            for g in range(N_GROUPS_B):
                cols = slice(g * GROUP_CH, (g + 1) * GROUP_CH)
                s = jnp.dot(ws_ref[g], vn_ref[rows, cols], preferred_element_type=F32) + bst_ref[:, g:g + 1]
                ob_ref[rows, cols] = (gu_ref[rows, cols].astype(F32) * s).astype(BF16)
        pa = jnp.dot(oa_ref[part, :], wpa_ref[...], preferred_element_type=F32)
        pb = jnp.dot(ob_ref[part, :], wpb_ref[...], preferred_element_type=F32)
        mixed = (ga_ref[part, :].astype(F32) * pa + gb_ref[part, :].astype(F32) * pb).astype(BF16)
        y = jnp.dot(mixed, wo_ref[...], preferred_element_type=F32)
        x1 = x_ref[part, :] + mod_ref[0, 2:3, :] * _rms_norm(y, g_ref[...])
        o_ref[part, :] = x1
        h2 = _rms_norm(x1, gffn_ref[...]) * (1.0 + mod_ref[0, 4:5, :]) + mod_ref[0, 3:4, :]
        h2_ref[part, :] = h2.astype(BF16)
    if is_first:
        @pl.when(step == pl.num_programs(0) - 1)
        def _():
            for pub in publish:
                pub.wait()


def _merge(x, mods, seq, o_a, proj_qkvu, proj_gates, gates_row0, w_s, b_s_t, w_pa, w_pb, w_o, g, g_ffn, *,
           is_first, tm=512, row_parts=2):
    m = x.shape[0]
    assert (tm // row_parts) % CHUNK == 0 and gates_row0 % tm == 0
    g0 = gates_row0 // tm
    whole = lambda a: pl.BlockSpec(a.shape, lambda i: (0,) * a.ndim, pipeline_mode=pl.Buffered(1))
    big = (w_pa, w_pb, w_o)
    out_specs = [pl.BlockSpec((tm, D_MODEL), lambda i: (i, 0))] * 2
    out_shape = [jax.ShapeDtypeStruct((m, D_MODEL), F32), jax.ShapeDtypeStruct((m, D_MODEL), BF16)]
    scratch = [pltpu.VMEM((tm, D_GMLP), BF16)]
    if is_first:
        big_specs = [pl.BlockSpec(memory_space=pl.ANY)] * len(big)
        out_specs += [pl.BlockSpec(memory_space=pl.ANY)] * len(big)
        out_shape += [jax.ShapeDtypeStruct(w.shape, BF16) for w in big]
        assert all(w.shape[1] == D_MODEL for w in big)
        scratch = ([pltpu.VMEM(w.shape, BF16) for w in big] + scratch
                   + [pltpu.VMEM((2, LOAD_ROWS, D_MODEL), F32), pltpu.SemaphoreType.DMA((2 + len(big),))])
    else:
        big_specs = [whole(w) for w in big]
    return pl.pallas_call(
        functools.partial(_merge_kernel, row_parts=row_parts, is_first=is_first),
        grid=(m // tm,),
        in_specs=[
            pl.BlockSpec((tm, D_MODEL), lambda i: (i, 0)),
            pl.BlockSpec((1, N_MOD, D_MODEL), lambda i: (i * tm // seq, 0, 0)),
            pl.BlockSpec((tm, D_ATTN), lambda i: (i, 0)),
            pl.BlockSpec((tm, D_GMLP), lambda i: (i, 3)),
            pl.BlockSpec((tm, D_GMLP), lambda i: (g0 + i, 4)),
            pl.BlockSpec((tm, D_MODEL), lambda i: (g0 + i, 0)),
            pl.BlockSpec((tm, D_MODEL), lambda i: (g0 + i, 1)),
            whole(w_s), whole(b_s_t), *big_specs, whole(g), whole(g_ffn),
        ],
        out_specs=out_specs,
        out_shape=out_shape,
        scratch_shapes=scratch,
        compiler_params=_params(("arbitrary",), 60),
        name="merge_ctx" if is_first else "merge_lat",
    )(x, mods, o_a, proj_qkvu, proj_gates, proj_gates, proj_gates,
      w_s, b_s_t, w_pa, w_pb, w_o, g, g_ffn)


def _ffn_x_block(f, n_f, n_x):
    return jnp.clip(f - (n_f - n_x), 0, n_x - 1)


def _ffn_kernel(x_ref, h_ref, mod_ref, gpost_ref, wg_ref, wu_ref, wd_ref, o_ref, acc_ref, *, n_x, row_parts):
    f = pl.program_id(1)
    x_rows = x_ref.shape[0]
    tm = h_ref.shape[0]

    @pl.when(f == 0)
    def _():
        acc_ref[...] = jnp.zeros_like(acc_ref)

    row0 = pl.multiple_of(_ffn_x_block(f, pl.num_programs(1), n_x) * x_rows, x_rows)
    o_ref[pl.ds(row0, x_rows), :] = x_ref[...]

    for part in range(row_parts):
        rows = slice(part * (tm // row_parts), (part + 1) * (tm // row_parts))
        h = h_ref[rows, :]
        gate = jnp.dot(h, wg_ref[...], preferred_element_type=F32)
        up = jnp.dot(h, wu_ref[...], preferred_element_type=F32)
        act = (jax.nn.silu(gate) * up).astype(BF16)
        acc_ref[rows, :] += jnp.dot(act, wd_ref[...], preferred_element_type=F32)

    @pl.when(f == pl.num_programs(1) - 1)
    def _():
        scale = mod_ref[0, 5:6, :] * gpost_ref[...]

        def residual_rows(c, carry):
            rows = pl.ds(pl.multiple_of(c * SUBLANES, SUBLANES), SUBLANES)
            a = acc_ref[rows, :]
            r = lax.rsqrt(jnp.mean(a * a, axis=-1, keepdims=True) + EPS)
            o_ref[rows, :] = o_ref[rows, :] + (a * r) * scale
            return carry

        lax.fori_loop(0, tm // SUBLANES, residual_rows, 0, unroll=32)


def _ffn(x, h, mods, seq, g_post, w_gate, w_up, w_down, *, tm=1024, tf=512, x_rows=128, row_parts=1):
    m = x.shape[0]
    n_f = D_FF // tf
    n_x = tm // x_rows
    assert n_x <= n_f
    return pl.pallas_call(
        functools.partial(_ffn_kernel, n_x=n_x, row_parts=row_parts),
        grid=(m // tm, n_f),
        in_specs=[
            pl.BlockSpec((x_rows, D_MODEL), lambda i, f: (i * n_x + _ffn_x_block(f, n_f, n_x), 0)),
            pl.BlockSpec((tm, D_MODEL), lambda i, f: (i, 0)),
            pl.BlockSpec((1, N_MOD, D_MODEL), lambda i, f: (i * tm // seq, 0, 0)),
            pl.BlockSpec((1, D_MODEL), lambda i, f: (0, 0)),
            pl.BlockSpec((D_MODEL, tf), lambda i, f: (0, f)),
            pl.BlockSpec((D_MODEL, tf), lambda i, f: (0, f)),
            pl.BlockSpec((tf, D_MODEL), lambda i, f: (f, 0)),
        ],
        out_specs=pl.BlockSpec((tm, D_MODEL), lambda i, f: (i, 0)),
        out_shape=jax.ShapeDtypeStruct((m, D_MODEL), F32),
        scratch_shapes=[pltpu.VMEM((tm, D_MODEL), F32)],
        compiler_params=_params(("parallel", "arbitrary"), 58),
        name="ffn",
    )(x, h, mods, g_post, w_gate, w_up, w_down)


def kernel(x_prompt, x_sample, cache_k, cache_v, c, c_ctx, w_ada, b_ada, norm_mix_pre, norm_mix_post,
           norm_ffn_pre, norm_ffn_post, w_in, rpb, ln_v, w_s, b_s, w_pa, w_pb, w_o, w_gate, w_up, w_down):
    assert w_ada.shape[0] == DEPTH == 1
    batch, seq, _ = x_prompt.shape
    dec_batch, dec_seq, _ = x_sample.shape
    past = cache_k.shape[2]

    row = lambda a: a[0].reshape(1, -1)
    bf = lambda a: a[0].astype(BF16)

    cvecs = jnp.concatenate(
        [c, c_ctx[None], jnp.zeros((MOD_ROWS - dec_batch - 1, D_MODEL), F32)], axis=0)
    mods = _modulation(cvecs, w_ada[0], b_ada[0])
    mods_lat = mods[:dec_batch].reshape(dec_batch, N_MOD, D_MODEL)
    mods_ctx = mods[dec_batch:dec_batch + 1].reshape(1, N_MOD, D_MODEL)
    table = _bias_table(rpb[0])

    w_in_f = w_in.reshape(D_MODEL, D_IN)
    mix_head = (bf(w_s), b_s[0].T)
    mix_tail = (row(norm_mix_post), row(norm_ffn_pre))
    ffn = (row(norm_ffn_post), bf(w_gate), bf(w_up), bf(w_down))

    xp = x_prompt.reshape(batch * seq, D_MODEL)
    xs = x_sample.reshape(dec_batch * dec_seq, D_MODEL)
    hp, proj_p, k_p, v_p, w_qkvu = _proj_qkvu(xp, mods_ctx, batch * seq, row(norm_mix_pre), w_in_f, is_ctx=True)
    hs, proj_s = _proj_qkvu(xs, mods_lat, dec_seq, row(norm_mix_pre), w_qkvu, is_ctx=False)
    gates = _proj_gates(hp, hs, row(ln_v), w_in_f)

    oa_p = _ctx_attention(proj_p, seq)
    xp, hp, *mix_bf = _merge(xp, mods_ctx, batch * seq, oa_p, proj_p, gates, 0,
                             *mix_head, w_pa[0], w_pb[0], w_o[0], *mix_tail, is_first=True)
    y_prompt = _ffn(xp, hp, mods_ctx, batch * seq, *ffn).reshape(batch, seq, D_MODEL)

    oa_s = _lat_attention(proj_s, cache_k.reshape(dec_batch, past * N_HEADS_A, HEAD_DIM),
                          cache_v.reshape(dec_batch, past * N_HEADS_A, HEAD_DIM), table, dec_seq)
    xs, hs = _merge(xs, mods_lat, dec_seq, oa_s, proj_s, gates, batch * seq,
                    *mix_head, *mix_bf, *mix_tail, is_first=False)
    y_sample = _ffn(xs, hs, mods_lat, dec_seq, *ffn).reshape(dec_batch, dec_seq, D_MODEL)

    state_shape = (batch, DEPTH, seq, N_HEADS_A, HEAD_DIM)
    return y_prompt, y_sample, k_p.reshape(state_shape), v_p.reshape(state_shape)
```

```python
import functools

import jax
import jax.numpy as jnp
from jax import lax
from jax.experimental import pallas as pl
from jax.experimental.pallas import tpu as pltpu
from jax.experimental.pallas import tpu_sc as plsc

D_MODEL = 2048
DEPTH = 1
GRID_W = 64
N_HEADS_A = 8
HEAD_DIM = 128
D_ATTN = N_HEADS_A * HEAD_DIM
KH_MAX = 8
KW = 16
CHUNK = 128
N_GROUPS_B = 8
D_GMLP = 1024
GROUP_CH = D_GMLP // N_GROUPS_B
D_FF = ((8 * D_MODEL // 3 + 255) // 256) * 256
N_MOD = 6
EPS = 1e-6
ATTN_SCALE = HEAD_DIM ** -0.5
LOG2E = 1.4426950408889634
D_IN = 3 * D_ATTN + 2 * D_GMLP + 2 * D_MODEL

N_DR = 2 * KH_MAX - 1
N_DC = 2 * KW - 1
COL_BLOCK = 1024
SUBLANES = 8
MOD_ROWS = SUBLANES

F32 = jnp.float32
BF16 = jnp.bfloat16

MIB = 1024 * 1024


def _params(semantics, vmem_mib):
    return pltpu.CompilerParams(dimension_semantics=semantics, vmem_limit_bytes=vmem_mib * MIB)


def _rms_norm(x, g):
    return x * lax.rsqrt(jnp.mean(x * x, axis=-1, keepdims=True) + EPS) * g


def _layer_norm(x, g):
    xc = x - jnp.mean(x, axis=-1, keepdims=True)
    return xc * lax.rsqrt(jnp.mean(xc * xc, axis=-1, keepdims=True) + EPS) * g


def _modulation_kernel(c_ref, w_ref, b_ref, o_ref):
    s = jax.nn.silu(c_ref[...]).astype(BF16)
    o_ref[...] = jnp.dot(s, w_ref[...].astype(BF16), preferred_element_type=F32) + b_ref[...]


def _modulation(cvecs, w_ada, b_ada):
    tn = 1024
    n = N_MOD * D_MODEL
    return pl.pallas_call(
        _modulation_kernel,
        grid=(n // tn,),
        in_specs=[
            pl.BlockSpec((MOD_ROWS, D_MODEL), lambda j: (0, 0)),
            pl.BlockSpec((D_MODEL, tn), lambda j: (0, j)),
            pl.BlockSpec((1, tn), lambda j: (0, j)),
        ],
        out_specs=pl.BlockSpec((MOD_ROWS, tn), lambda j: (0, j)),
        out_shape=jax.ShapeDtypeStruct((MOD_ROWS, n), F32),
        compiler_params=_params(("parallel",), 40),
        name="modulation",
    )(cvecs, w_ada, b_ada.reshape(1, n))


def _bias_table_kernel(rpb_ref, o_ref):
    qc = lax.broadcasted_iota(jnp.int32, (GRID_W, GRID_W), 0)
    kc = lax.broadcasted_iota(jnp.int32, (GRID_W, GRID_W), 1)
    cs = jnp.clip(qc - KW // 2, 0, GRID_W - KW)
    valid = (kc >= cs) & (kc < cs + KW)
    lanes = rpb_ref.shape[-1]
    tiles = []
    for dr in range(N_DR):
        row = jnp.broadcast_to(rpb_ref[0, dr:dr + 1, :], (GRID_W, lanes))
        t = pltpu.roll(row, lanes - (KW - 1), 1, stride=1, stride_axis=0)[:, :GRID_W]
        tiles.append(jnp.where(valid, t * LOG2E, -jnp.inf))
    pad = jnp.zeros((GRID_W, GRID_W), F32)
    o_ref[0, 0] = jnp.concatenate(tiles + [pad], axis=-1)
    o_ref[0, 1] = jnp.concatenate([pad] + tiles, axis=-1)


def _bias_table(rpb):
    width = (N_DR + 1) * GRID_W
    lanes = 128
    rpb_rows = jnp.pad(rpb, ((0, 0), (0, 0), (0, lanes - N_DC)))
    return pl.pallas_call(
        _bias_table_kernel,
        grid=(N_HEADS_A,),
        in_specs=[pl.BlockSpec((1, N_DR, lanes), lambda h: (h, 0, 0))],
        out_specs=pl.BlockSpec((1, 2, GRID_W, width), lambda h: (h, 0, 0, 0)),
        out_shape=jax.ShapeDtypeStruct((N_HEADS_A, 2, GRID_W, width), F32),
        compiler_params=_params(("parallel",), 16),
        name="bias_table",
    )(rpb_rows)


LOAD_ROWS = 256
N_QKVU = 3 * D_ATTN + D_GMLP


def _row_parts(tm, parts):
    return [slice(p * (tm // parts), (p + 1) * (tm // parts)) for p in range(parts)]


def _load_cast_weights(w_hbm, col0, w_res, stage, sems):
    rows, width = w_res.shape
    chunk = stage.shape[1]
    n_chunks = rows // chunk

    def copy(c):
        src = w_hbm.at[pl.ds(c * chunk, chunk), pl.ds(col0, width)]
        return pltpu.make_async_copy(src, stage.at[c % 2], sems.at[c % 2])

    copy(0).start()
    for c in range(n_chunks):
        if c + 1 < n_chunks:
            copy(c + 1).start()
        copy(c).wait()
        w_res[c * chunk:(c + 1) * chunk, :] = stage[c % 2].astype(BF16)


def _proj_qkvu_kernel(x_ref, mod_ref, g_ref, w_ref, h_ref, proj_ref, *rest, is_ctx):
    if is_ctx:
        k_ref, v_ref, wpub_ref, w_res, stage, sems = rest
        step = pl.program_id(0)
        publish = pltpu.make_async_copy(w_res, wpub_ref, sems.at[2])

        @pl.when(step == 0)
        def _():
            _load_cast_weights(w_ref, 0, w_res, stage, sems)
            publish.start()
    else:
        w_res = w_ref
    y = _rms_norm(x_ref[...], g_ref[...])
    h = (y * (1.0 + mod_ref[0, 1:2, :]) + mod_ref[0, 0:1, :]).astype(BF16)
    h_ref[...] = h
    for blk in (3, 0, 1, 2):
        cols = slice(blk * COL_BLOCK, (blk + 1) * COL_BLOCK)
        acc = jnp.dot(h, w_res[:, cols], preferred_element_type=F32)
        if is_ctx and blk in (1, 2):
            kv_ref = (k_ref, v_ref)[blk - 1]
            for head in range(N_HEADS_A):
                dst = pl.ds(head, x_ref.shape[0], stride=N_HEADS_A)
                kv_ref[dst, :] = acc[:, head * HEAD_DIM:(head + 1) * HEAD_DIM]
        if blk == 3:
            acc = jax.nn.gelu(acc)
        proj_ref[:, cols] = acc.astype(BF16)
    if is_ctx:
        @pl.when(step == pl.num_programs(0) - 1)
        def _():
            publish.wait()


def _proj_qkvu(x, mods, seq, g, w, *, is_ctx, tm=512):
    m = x.shape[0]
    out_shape = [jax.ShapeDtypeStruct((m, D_MODEL), BF16), jax.ShapeDtypeStruct((m, N_QKVU), BF16)]
    out_specs = [pl.BlockSpec((tm, D_MODEL), lambda i: (i, 0)), pl.BlockSpec((tm, N_QKVU), lambda i: (i, 0))]
    scratch = []
    if is_ctx:
        out_shape += [jax.ShapeDtypeStruct((m * N_HEADS_A, HEAD_DIM), F32)] * 2
        out_specs += [pl.BlockSpec((tm * N_HEADS_A, HEAD_DIM), lambda i: (i, 0))] * 2
        out_shape += [jax.ShapeDtypeStruct((D_MODEL, N_QKVU), BF16)]
        out_specs += [pl.BlockSpec(memory_space=pl.ANY)]
        w_spec = pl.BlockSpec(memory_space=pl.ANY)
        scratch = [pltpu.VMEM((D_MODEL, N_QKVU), BF16), pltpu.VMEM((2, LOAD_ROWS, N_QKVU), F32),
                   pltpu.SemaphoreType.DMA((3,))]
    else:
        w_spec = pl.BlockSpec((D_MODEL, N_QKVU), lambda i: (0, 0), pipeline_mode=pl.Buffered(1))
    return pl.pallas_call(
        functools.partial(_proj_qkvu_kernel, is_ctx=is_ctx),
        grid=(m // tm,),
        in_specs=[
            pl.BlockSpec((tm, D_MODEL), lambda i: (i, 0)),
            pl.BlockSpec((1, N_MOD, D_MODEL), lambda i: (i * tm // seq, 0, 0)),
            pl.BlockSpec((1, D_MODEL), lambda i: (0, 0)),
            w_spec,
        ],
        out_specs=out_specs,
        out_shape=out_shape,
        scratch_shapes=scratch,
        compiler_params=_params(("arbitrary",), 58),
        name="proj_qkvu_ctx" if is_ctx else "proj_qkvu_lat",
    )(x, mods, g, w)


def _proj_gates_kernel(hp_ref, hs_ref, lnv_ref, w_ref, proj_ref, w_res, stage, sems, *, n_ctx, row_parts):
    step = pl.program_id(0)

    @pl.when(step == 0)
    def _():
        _load_cast_weights(w_ref, N_QKVU, w_res, stage, sems)

    n_gate = w_res.shape[1] - D_GMLP
    for rows in _row_parts(hp_ref.shape[0], row_parts):
        h = jnp.where(step < n_ctx, hp_ref[rows, :], hs_ref[rows, :])
        acc = jnp.dot(h, w_res[:, :D_GMLP], preferred_element_type=F32)
        proj_ref[rows, n_gate:] = _layer_norm(jax.nn.gelu(acc), lnv_ref[...]).astype(BF16)
        for blk in range(n_gate // COL_BLOCK):
            cols = slice(blk * COL_BLOCK, (blk + 1) * COL_BLOCK)
            acc = jnp.dot(h, w_res[:, D_GMLP + cols.start:D_GMLP + cols.stop], preferred_element_type=F32)
            proj_ref[rows, cols] = jax.nn.sigmoid(acc).astype(BF16)


def _proj_gates(h_ctx, h_lat, ln_v, w_in, *, tm=512, row_parts=2):
    n_ctx, n_lat = h_ctx.shape[0] // tm, h_lat.shape[0] // tm
    n = D_IN - N_QKVU
    return pl.pallas_call(
        functools.partial(_proj_gates_kernel, n_ctx=n_ctx, row_parts=row_parts),
        grid=(n_ctx + n_lat,),
        in_specs=[
            pl.BlockSpec((tm, D_MODEL), lambda i: (jnp.minimum(i, n_ctx - 1), 0)),
            pl.BlockSpec((tm, D_MODEL), lambda i: (jnp.maximum(i - n_ctx, 0), 0)),
            pl.BlockSpec((1, D_GMLP), lambda i: (0, 0)),
            pl.BlockSpec(memory_space=pl.ANY),
        ],
        out_specs=pl.BlockSpec((tm, n), lambda i: (i, 0)),
        out_shape=jax.ShapeDtypeStruct(((n_ctx + n_lat) * tm, n), BF16),
        scratch_shapes=[pltpu.VMEM((D_MODEL, n), BF16), pltpu.VMEM((2, LOAD_ROWS, n), F32),
                        pltpu.SemaphoreType.DMA((2,))],
        compiler_params=_params(("arbitrary",), 56),
        name="proj_gates",
    )(h_ctx, h_lat, ln_v, w_in)


def _softmax_parts(logits2):
    m = functools.reduce(jnp.maximum, [jnp.max(t, axis=-1, keepdims=True) for t in logits2])
    es = [jnp.exp2(t - m) for t in logits2]
    inv = 1.0 / functools.reduce(jnp.add, [jnp.sum(e, axis=-1, keepdims=True) for e in es])
    return [e.astype(BF16) for e in es], inv


def _qk(q, k):
    return lax.dot_general(q, k, (((1,), (1,)), ((), ())), preferred_element_type=F32) * (ATTN_SCALE * LOG2E)


def _ctx_attn_kernel(q_ref, k_ref, v_ref, o_ref):
    for h in range(N_HEADS_A):
        cols = slice(h * HEAD_DIM, (h + 1) * HEAD_DIM)
        (p,), inv = _softmax_parts([_qk(q_ref[:, cols], k_ref[:, cols])])
        o_ref[:, cols] = (jnp.dot(p, v_ref[:, cols], preferred_element_type=F32) * inv).astype(BF16)


def _ctx_attention(proj, seq):
    m = proj.shape[0]
    spec = lambda col: pl.BlockSpec((seq, D_ATTN), lambda b: (b, col))
    return pl.pallas_call(
        _ctx_attn_kernel,
        grid=(m // seq,),
        in_specs=[spec(0), spec(1), spec(2)],
        out_specs=spec(0),
        out_shape=jax.ShapeDtypeStruct((m, D_ATTN), BF16),
        compiler_params=_params(("parallel",), 32),
        name="ctx_attention",
    )(proj, proj, proj)


def _lat_attn_kernel(q_ref, k_ref, v_ref, kc_ref, vc_ref, tab_ref, o_ref, sw_ref, pw_ref, *, rows, kh):
    heads = tab_ref.shape[0]
    past = kc_ref.shape[1] // N_HEADS_A
    win = kh * GRID_W
    starts = [min(max(r - kh // 2, 0), rows - kh) * GRID_W for r in range(rows)]
    for j in range(heads):
        head = pl.program_id(1) * heads + j
        cols = slice(j * HEAD_DIM, (j + 1) * HEAD_DIM)
        sw, pw = sw_ref.at[j % 2], pw_ref.at[j % 2]
        kc = kc_ref[0, pl.ds(head, past, stride=N_HEADS_A), :].astype(BF16)
        vc = vc_ref[0, pl.ds(head, past, stride=N_HEADS_A), :].astype(BF16)
        s_c = _qk(q_ref[:, cols], kc)
        for r, start in enumerate(starts):
            off = start // GRID_W - r + (KH_MAX - 1)
            lane0 = (off + off % 2) * GRID_W
            bias = tab_ref[j, off % 2, :, lane0:lane0 + win]
            q = q_ref[r * GRID_W:(r + 1) * GRID_W, cols]
            sw[r * GRID_W:(r + 1) * GRID_W, :] = _qk(q, k_ref[start:start + win, cols]) + bias
        (p_w, p_c), inv = _softmax_parts([sw[...], s_c])
        pw[...] = p_w
        o_c = jnp.dot(p_c, vc, preferred_element_type=F32)
        for r, start in enumerate(starts):
            q_rows = slice(r * GRID_W, (r + 1) * GRID_W)
            o_w = jnp.dot(pw[q_rows, :], v_ref[start:start + win, cols], preferred_element_type=F32)
            o_ref[q_rows, cols] = ((o_w + o_c[q_rows, :]) * inv[q_rows, :]).astype(BF16)


def _lat_attention(proj, cache_k, cache_v, table, seq, *, heads_per_step=4):
    m = proj.shape[0]
    rows = seq // GRID_W
    kh = min(KH_MAX, rows)
    n_hp = N_HEADS_A // heads_per_step
    width = heads_per_step * HEAD_DIM
    qkv = lambda part: pl.BlockSpec((seq, width), lambda b, hp: (b, part * n_hp + hp))
    cache = pl.BlockSpec((1,) + cache_k.shape[1:], lambda b, hp: (b, 0, 0))
    return pl.pallas_call(
        functools.partial(_lat_attn_kernel, rows=rows, kh=kh),
        grid=(m // seq, n_hp),
        in_specs=[qkv(0), qkv(1), qkv(2), cache, cache,
                  pl.BlockSpec((heads_per_step,) + table.shape[1:], lambda b, hp: (hp, 0, 0, 0))],
        out_specs=pl.BlockSpec((seq, width), lambda b, hp: (b, hp)),
        out_shape=jax.ShapeDtypeStruct((m, D_ATTN), BF16),
        scratch_shapes=[pltpu.VMEM((2, seq, kh * GRID_W), F32), pltpu.VMEM((2, seq, kh * GRID_W), BF16)],
        compiler_params=_params(("parallel", "arbitrary"), 48),
        name="lat_attention",
    )(proj, proj, proj, cache_k, cache_v, table)


def _merge_kernel(x_ref, mod_ref, oa_ref, gu_ref, vn_ref, ga_ref, gb_ref,
                  ws_ref, bst_ref, wpa_ref, wpb_ref, wo_ref, g_ref, gffn_ref, o_ref, h2_ref,
                  *rest, row_parts, is_first):
    if is_first:
        w_f32 = (wpa_ref, wpb_ref, wo_ref)
        pubs, w_res, (ob_ref, stage, sems) = rest[:3], rest[3:6], rest[6:]
        wpa_ref, wpb_ref, wo_ref = w_res
        step = pl.program_id(0)
        publish = [pltpu.make_async_copy(res, pub, sems.at[2 + n])
                   for n, (res, pub) in enumerate(zip(w_res, pubs))]

        @pl.when(step == 0)
        def _():
            for src, res, pub in zip(w_f32, w_res, publish):
                _load_cast_weights(src, 0, res, stage, sems)
                pub.start()
    else:
        (ob_ref,) = rest
    for part in _row_parts(x_ref.shape[0], row_parts):
        for c in range(part.start // CHUNK, part.stop // CHUNK):
            rows = slice(c * CHUNK, (c + 1) * CHUNK)
            for g in range(N_GROUPS_B):
                cols = slice(g * GROUP_CH, (g + 1) * GROUP_CH)
                s = jnp.dot(ws_ref[g], vn_ref[rows, cols], preferred_element_type=F32) + bst_ref[:, g:g + 1]
                ob_ref[rows, cols] = (gu_ref[rows, cols].astype(F32) * s).astype(BF16)
        pa = jnp.dot(oa_ref[part, :], wpa_ref[...], preferred_element_type=F32)
        pb = jnp.dot(ob_ref[part, :], wpb_ref[...], preferred_element_type=F32)
        mixed = (ga_ref[part, :].astype(F32) * pa + gb_ref[part, :].astype(F32) * pb).astype(BF16)
        y = jnp.dot(mixed, wo_ref[...], preferred_element_type=F32)
        x1 = x_ref[part, :] + mod_ref[0, 2:3, :] * _rms_norm(y, g_ref[...])
        o_ref[part, :] = x1
        h2 = _rms_norm(x1, gffn_ref[...]) * (1.0 + mod_ref[0, 4:5, :]) + mod_ref[0, 3:4, :]
        h2_ref[part, :] = h2.astype(BF16)
    if is_first:
        @pl.when(step == pl.num_programs(0) - 1)
        def _():
            for pub in publish:
                pub.wait()


def _merge(x, mods, seq, o_a, proj_qkvu, proj_gates, gates_row0, w_s, b_s_t, w_pa, w_pb, w_o, g, g_ffn, *,
           is_first, tm=512, row_parts=2):
    m = x.shape[0]
    assert (tm // row_parts) % CHUNK == 0 and gates_row0 % tm == 0
    g0 = gates_row0 // tm
    whole = lambda a: pl.BlockSpec(a.shape, lambda i: (0,) * a.ndim, pipeline_mode=pl.Buffered(1))
    big = (w_pa, w_pb, w_o)
    out_specs = [pl.BlockSpec((tm, D_MODEL), lambda i: (i, 0))] * 2
    out_shape = [jax.ShapeDtypeStruct((m, D_MODEL), F32), jax.ShapeDtypeStruct((m, D_MODEL), BF16)]
    scratch = [pltpu.VMEM((tm, D_GMLP), BF16)]
    if is_first:
        big_specs = [pl.BlockSpec(memory_space=pl.ANY)] * len(big)
        out_specs += [pl.BlockSpec(memory_space=pl.ANY)] * len(big)
        out_shape += [jax.ShapeDtypeStruct(w.shape, BF16) for w in big]
        assert all(w.shape[1] == D_MODEL for w in big)
        scratch = ([pltpu.VMEM(w.shape, BF16) for w in big] + scratch
                   + [pltpu.VMEM((2, LOAD_ROWS, D_MODEL), F32), pltpu.SemaphoreType.DMA((2 + len(big),))])
    else:
        big_specs = [whole(w) for w in big]
    return pl.pallas_call(
        functools.partial(_merge_kernel, row_parts=row_parts, is_first=is_first),
        grid=(m // tm,),
        in_specs=[
            pl.BlockSpec((tm, D_MODEL), lambda i: (i, 0)),
            pl.BlockSpec((1, N_MOD, D_MODEL), lambda i: (i * tm // seq, 0, 0)),
            pl.BlockSpec((tm, D_ATTN), lambda i: (i, 0)),
            pl.BlockSpec((tm, D_GMLP), lambda i: (i, 3)),
            pl.BlockSpec((tm, D_GMLP), lambda i: (g0 + i, 4)),
            pl.BlockSpec((tm, D_MODEL), lambda i: (g0 + i, 0)),
            pl.BlockSpec((tm, D_MODEL), lambda i: (g0 + i, 1)),
            whole(w_s), whole(b_s_t), *big_specs, whole(g), whole(g_ffn),
        ],
        out_specs=out_specs,
        out_shape=out_shape,
        scratch_shapes=scratch,
        compiler_params=_params(("arbitrary",), 60),
        name="merge_ctx" if is_first else "merge_lat",
    )(x, mods, o_a, proj_qkvu, proj_gates, proj_gates, proj_gates,
      w_s, b_s_t, w_pa, w_pb, w_o, g, g_ffn)


SC_LANES = 16
SC_BLOCK = (32, 512)


def _bf16_bits(u):
    return lax.shift_right_logical(u + 0x7FFF + (lax.shift_right_logical(u, 16) & 1), 16)


def _sc_pack_bf16(w):
    rows, cols = w.shape
    blk_r, blk_c = SC_BLOCK
    unroll = 8
    assert rows % blk_r == 0 and cols % blk_c == 0 and blk_c % (unroll * SC_LANES) == 0
    mesh = plsc.VectorSubcoreMesh(core_axis_name="core", subcore_axis_name="subcore")

    def body(in_vmem, out_vmem):
        @pl.loop(0, blk_r // 2)
        def _(r):
            @pl.loop(0, blk_c, step=unroll * SC_LANES)
            def _(c0):
                for k in range(unroll):
                    lanes = pl.ds(c0 + k * SC_LANES, SC_LANES)
                    lo = in_vmem[2 * r, lanes]
                    hi = in_vmem[2 * r + 1, lanes]
                    out_vmem[r, lanes] = _bf16_bits(lo) | lax.shift_left(_bf16_bits(hi), 16)

    @pl.kernel(out_type=jax.ShapeDtypeStruct((rows // 2, cols), jnp.int32), mesh=mesh, scratch_types=[])
    def pack_kernel(w_hbm, o_hbm):
        pltpu.emit_pipeline(
            body,
            grid=(rows // blk_r, cols // blk_c),
            in_specs=[pl.BlockSpec((blk_r, blk_c), lambda i, j: (i, j))],
            out_specs=[pl.BlockSpec((blk_r // 2, blk_c), lambda i, j: (i, j))],
            core_axis_name=("core", "subcore"),
            dimension_semantics=(pltpu.PARALLEL, pltpu.PARALLEL),
        )(w_hbm, o_hbm)

    return pack_kernel(lax.bitcast_convert_type(w, jnp.int32))


def _ffn_x_block(f, n_f, n_x):
    return jnp.clip(f - (n_f - n_x), 0, n_x - 1)


def _ffn_kernel(x_ref, h_ref, mod_ref, gpost_ref, wg_ref, wu_ref, wd_ref, o_ref, acc_ref, *, n_x, row_parts):
    f = pl.program_id(1)
    x_rows = x_ref.shape[0]
    tm = h_ref.shape[0]

    @pl.when(f == 0)
    def _():
        acc_ref[...] = jnp.zeros_like(acc_ref)

    row0 = pl.multiple_of(_ffn_x_block(f, pl.num_programs(1), n_x) * x_rows, x_rows)
    o_ref[pl.ds(row0, x_rows), :] = x_ref[...]

    for part in range(row_parts):
        rows = slice(part * (tm // row_parts), (part + 1) * (tm // row_parts))
        h = h_ref[rows, :]
        gate = jnp.dot(h, pltpu.bitcast(wg_ref[...], BF16), preferred_element_type=F32)
        up = jnp.dot(h, pltpu.bitcast(wu_ref[...], BF16), preferred_element_type=F32)
        act = (jax.nn.silu(gate) * up).astype(BF16)
        acc_ref[rows, :] += jnp.dot(act, pltpu.bitcast(wd_ref[...], BF16), preferred_element_type=F32)

    @pl.when(f == pl.num_programs(1) - 1)
    def _():
        scale = mod_ref[0, 5:6, :] * gpost_ref[...]

        def residual_rows(c, carry):
            rows = pl.ds(pl.multiple_of(c * SUBLANES, SUBLANES), SUBLANES)
            a = acc_ref[rows, :]
            r = lax.rsqrt(jnp.mean(a * a, axis=-1, keepdims=True) + EPS)
            o_ref[rows, :] = o_ref[rows, :] + (a * r) * scale
            return carry

        lax.fori_loop(0, tm // SUBLANES, residual_rows, 0, unroll=32)


def _ffn(x, h, mods, seq, g_post, w_gate, w_up, w_down, *, tm=1024, tf=512, x_rows=128, row_parts=1):
    m = x.shape[0]
    n_f = D_FF // tf
    n_x = tm // x_rows
    assert n_x <= n_f
    return pl.pallas_call(
        functools.partial(_ffn_kernel, n_x=n_x, row_parts=row_parts),
        grid=(m // tm, n_f),
        in_specs=[
            pl.BlockSpec((x_rows, D_MODEL), lambda i, f: (i * n_x + _ffn_x_block(f, n_f, n_x), 0)),
            pl.BlockSpec((tm, D_MODEL), lambda i, f: (i, 0)),
            pl.BlockSpec((1, N_MOD, D_MODEL), lambda i, f: (i * tm // seq, 0, 0)),
            pl.BlockSpec((1, D_MODEL), lambda i, f: (0, 0)),
            pl.BlockSpec((D_MODEL // 2, tf), lambda i, f: (0, f)),
            pl.BlockSpec((D_MODEL // 2, tf), lambda i, f: (0, f)),
            pl.BlockSpec((tf // 2, D_MODEL), lambda i, f: (f, 0)),
        ],
        out_specs=pl.BlockSpec((tm, D_MODEL), lambda i, f: (i, 0)),
        out_shape=jax.ShapeDtypeStruct((m, D_MODEL), F32),
        scratch_shapes=[pltpu.VMEM((tm, D_MODEL), F32)],
        compiler_params=_params(("parallel", "arbitrary"), 58),
        name="ffn",
    )(x, h, mods, g_post, w_gate, w_up, w_down)


def kernel(x_prompt, x_sample, cache_k, cache_v, c, c_ctx, w_ada, b_ada, norm_mix_pre, norm_mix_post,
           norm_ffn_pre, norm_ffn_post, w_in, rpb, ln_v, w_s, b_s, w_pa, w_pb, w_o, w_gate, w_up, w_down):
    assert w_ada.shape[0] == DEPTH == 1
    batch, seq, _ = x_prompt.shape
    dec_batch, dec_seq, _ = x_sample.shape
    past = cache_k.shape[2]

    row = lambda a: a[0].reshape(1, -1)
    bf = lambda a: a[0].astype(BF16)

    cvecs = jnp.concatenate(
        [c, c_ctx[None], jnp.zeros((MOD_ROWS - dec_batch - 1, D_MODEL), F32)], axis=0)
    mods = _modulation(cvecs, w_ada[0], b_ada[0])
    mods_lat = mods[:dec_batch].reshape(dec_batch, N_MOD, D_MODEL)
    mods_ctx = mods[dec_batch:dec_batch + 1].reshape(1, N_MOD, D_MODEL)
    table = _bias_table(rpb[0])

    w_in_f = w_in.reshape(D_MODEL, D_IN)
    mix_head = (bf(w_s), b_s[0].T)
    mix_tail = (row(norm_mix_post), row(norm_ffn_pre))
    ffn = (row(norm_ffn_post), _sc_pack_bf16(w_gate[0]), _sc_pack_bf16(w_up[0]), _sc_pack_bf16(w_down[0]))

    xp = x_prompt.reshape(batch * seq, D_MODEL)
    xs = x_sample.reshape(dec_batch * dec_seq, D_MODEL)
    hp, proj_p, k_p, v_p, w_qkvu = _proj_qkvu(xp, mods_ctx, batch * seq, row(norm_mix_pre), w_in_f, is_ctx=True)
    hs, proj_s = _proj_qkvu(xs, mods_lat, dec_seq, row(norm_mix_pre), w_qkvu, is_ctx=False)
    gates = _proj_gates(hp, hs, row(ln_v), w_in_f)

    oa_p = _ctx_attention(proj_p, seq)
    xp, hp, *mix_bf = _merge(xp, mods_ctx, batch * seq, oa_p, proj_p, gates, 0,
                             *mix_head, w_pa[0], w_pb[0], w_o[0], *mix_tail, is_first=True)
    y_prompt = _ffn(xp, hp, mods_ctx, batch * seq, *ffn).reshape(batch, seq, D_MODEL)

    oa_s = _lat_attention(proj_s, cache_k.reshape(dec_batch, past * N_HEADS_A, HEAD_DIM),
                          cache_v.reshape(dec_batch, past * N_HEADS_A, HEAD_DIM), table, dec_seq)
    xs, hs = _merge(xs, mods_lat, dec_seq, oa_s, proj_s, gates, batch * seq,
                    *mix_head, *mix_bf, *mix_tail, is_first=False)
    y_sample = _ffn(xs, hs, mods_lat, dec_seq, *ffn).reshape(dec_batch, dec_seq, D_MODEL)

    state_shape = (batch, DEPTH, seq, N_HEADS_A, HEAD_DIM)
    return y_prompt, y_sample, k_p.reshape(state_shape), v_p.reshape(state_shape)
```

```python
import functools

import jax
import jax.numpy as jnp
from jax import lax
from jax.experimental import pallas as pl
from jax.experimental.pallas import tpu as pltpu
from jax.experimental.pallas import tpu_sc as plsc

D_MODEL = 2048
DEPTH = 1
GRID_W = 64
N_HEADS_A = 8
HEAD_DIM = 128
D_ATTN = N_HEADS_A * HEAD_DIM
KH_MAX = 8
KW = 16
CHUNK = 128
N_GROUPS_B = 8
D_GMLP = 1024
GROUP_CH = D_GMLP // N_GROUPS_B
D_FF = ((8 * D_MODEL // 3 + 255) // 256) * 256
N_MOD = 6
EPS = 1e-6
ATTN_SCALE = HEAD_DIM ** -0.5
LOG2E = 1.4426950408889634
D_IN = 3 * D_ATTN + 2 * D_GMLP + 2 * D_MODEL

N_DR = 2 * KH_MAX - 1
N_DC = 2 * KW - 1
COL_BLOCK = 1024
SUBLANES = 8
MOD_ROWS = SUBLANES

F32 = jnp.float32
BF16 = jnp.bfloat16

MIB = 1024 * 1024


def _params(semantics, vmem_mib):
    return pltpu.CompilerParams(dimension_semantics=semantics, vmem_limit_bytes=vmem_mib * MIB)


def _rms_norm(x, g):
    return x * lax.rsqrt(jnp.mean(x * x, axis=-1, keepdims=True) + EPS) * g


def _layer_norm(x, g):
    xc = x - jnp.mean(x, axis=-1, keepdims=True)
    return xc * lax.rsqrt(jnp.mean(xc * xc, axis=-1, keepdims=True) + EPS) * g


def _modulation_kernel(c_ref, w_ref, b_ref, o_ref):
    s = jax.nn.silu(c_ref[...]).astype(BF16)
    o_ref[...] = jnp.dot(s, w_ref[...].astype(BF16), preferred_element_type=F32) + b_ref[...]


def _modulation(cvecs, w_ada, b_ada):
    tn = 1024
    n = N_MOD * D_MODEL
    return pl.pallas_call(
        _modulation_kernel,
        grid=(n // tn,),
        in_specs=[
            pl.BlockSpec((MOD_ROWS, D_MODEL), lambda j: (0, 0)),
            pl.BlockSpec((D_MODEL, tn), lambda j: (0, j)),
            pl.BlockSpec((1, tn), lambda j: (0, j)),
        ],
        out_specs=pl.BlockSpec((MOD_ROWS, tn), lambda j: (0, j)),
        out_shape=jax.ShapeDtypeStruct((MOD_ROWS, n), F32),
        compiler_params=_params(("parallel",), 40),
        name="modulation",
    )(cvecs, w_ada, b_ada.reshape(1, n))


def _bias_table_kernel(rpb_ref, o_ref):
    qc = lax.broadcasted_iota(jnp.int32, (GRID_W, GRID_W), 0)
    kc = lax.broadcasted_iota(jnp.int32, (GRID_W, GRID_W), 1)
    cs = jnp.clip(qc - KW // 2, 0, GRID_W - KW)
    valid = (kc >= cs) & (kc < cs + KW)
    lanes = rpb_ref.shape[-1]
    tiles = []
    for dr in range(N_DR):
        row = jnp.broadcast_to(rpb_ref[0, dr:dr + 1, :], (GRID_W, lanes))
        t = pltpu.roll(row, lanes - (KW - 1), 1, stride=1, stride_axis=0)[:, :GRID_W]
        tiles.append(jnp.where(valid, t * LOG2E, -jnp.inf))
    pad = jnp.zeros((GRID_W, GRID_W), F32)
    o_ref[0, 0] = jnp.concatenate(tiles + [pad], axis=-1)
    o_ref[0, 1] = jnp.concatenate([pad] + tiles, axis=-1)


def _bias_table(rpb):
    width = (N_DR + 1) * GRID_W
    lanes = 128
    rpb_rows = jnp.pad(rpb, ((0, 0), (0, 0), (0, lanes - N_DC)))
    return pl.pallas_call(
        _bias_table_kernel,
        grid=(N_HEADS_A,),
        in_specs=[pl.BlockSpec((1, N_DR, lanes), lambda h: (h, 0, 0))],
        out_specs=pl.BlockSpec((1, 2, GRID_W, width), lambda h: (h, 0, 0, 0)),
        out_shape=jax.ShapeDtypeStruct((N_HEADS_A, 2, GRID_W, width), F32),
        compiler_params=_params(("parallel",), 16),
        name="bias_table",
    )(rpb_rows)


LOAD_ROWS = 256
N_QKVU = 3 * D_ATTN + D_GMLP


def _row_parts(tm, parts):
    return [slice(p * (tm // parts), (p + 1) * (tm // parts)) for p in range(parts)]


def _load_cast_weights(w_hbm, col0, w_res, stage, sems):
    rows, width = w_res.shape
    chunk = stage.shape[1]
    n_chunks = rows // chunk

    def copy(c):
        src = w_hbm.at[pl.ds(c * chunk, chunk), pl.ds(col0, width)]
        return pltpu.make_async_copy(src, stage.at[c % 2], sems.at[c % 2])

    copy(0).start()
    for c in range(n_chunks):
        if c + 1 < n_chunks:
            copy(c + 1).start()
        copy(c).wait()
        w_res[c * chunk:(c + 1) * chunk, :] = stage[c % 2].astype(BF16)


def _proj_qkvu_kernel(x_ref, mod_ref, g_ref, w_ref, h_ref, proj_ref, *rest, is_ctx):
    if is_ctx:
        k_ref, v_ref, wpub_ref, w_res, stage, sems = rest
        step = pl.program_id(0)
        publish = pltpu.make_async_copy(w_res, wpub_ref, sems.at[2])

        @pl.when(step == 0)
        def _():
            _load_cast_weights(w_ref, 0, w_res, stage, sems)
            publish.start()
    else:
        w_res = w_ref
    y = _rms_norm(x_ref[...], g_ref[...])
    h = (y * (1.0 + mod_ref[0, 1:2, :]) + mod_ref[0, 0:1, :]).astype(BF16)
    h_ref[...] = h
    for blk in (3, 0, 1, 2):
        cols = slice(blk * COL_BLOCK, (blk + 1) * COL_BLOCK)
        acc = jnp.dot(h, w_res[:, cols], preferred_element_type=F32)
        if is_ctx and blk in (1, 2):
            kv_ref = (k_ref, v_ref)[blk - 1]
            for head in range(N_HEADS_A):
                dst = pl.ds(head, x_ref.shape[0], stride=N_HEADS_A)
                kv_ref[dst, :] = acc[:, head * HEAD_DIM:(head + 1) * HEAD_DIM]
        if blk == 3:
            acc = jax.nn.gelu(acc)
        proj_ref[:, cols] = acc.astype(BF16)
    if is_ctx:
        @pl.when(step == pl.num_programs(0) - 1)
        def _():
            publish.wait()


def _proj_qkvu(x, mods, seq, g, w, *, is_ctx, tm=512):
    m = x.shape[0]
    out_shape = [jax.ShapeDtypeStruct((m, D_MODEL), BF16), jax.ShapeDtypeStruct((m, N_QKVU), BF16)]
    out_specs = [pl.BlockSpec((tm, D_MODEL), lambda i: (i, 0)), pl.BlockSpec((tm, N_QKVU), lambda i: (i, 0))]
    scratch = []
    if is_ctx:
        out_shape += [jax.ShapeDtypeStruct((m * N_HEADS_A, HEAD_DIM), F32)] * 2
        out_specs += [pl.BlockSpec((tm * N_HEADS_A, HEAD_DIM), lambda i: (i, 0))] * 2
        out_shape += [jax.ShapeDtypeStruct((D_MODEL, N_QKVU), BF16)]
        out_specs += [pl.BlockSpec(memory_space=pl.ANY)]
        w_spec = pl.BlockSpec(memory_space=pl.ANY)
        scratch = [pltpu.VMEM((D_MODEL, N_QKVU), BF16), pltpu.VMEM((2, LOAD_ROWS, N_QKVU), F32),
                   pltpu.SemaphoreType.DMA((3,))]
    else:
        w_spec = pl.BlockSpec((D_MODEL, N_QKVU), lambda i: (0, 0), pipeline_mode=pl.Buffered(1))
    return pl.pallas_call(
        functools.partial(_proj_qkvu_kernel, is_ctx=is_ctx),
        grid=(m // tm,),
        in_specs=[
            pl.BlockSpec((tm, D_MODEL), lambda i: (i, 0)),
            pl.BlockSpec((1, N_MOD, D_MODEL), lambda i: (i * tm // seq, 0, 0)),
            pl.BlockSpec((1, D_MODEL), lambda i: (0, 0)),
            w_spec,
        ],
        out_specs=out_specs,
        out_shape=out_shape,
        scratch_shapes=scratch,
        compiler_params=_params(("arbitrary",), 58),
        name="proj_qkvu_ctx" if is_ctx else "proj_qkvu_lat",
    )(x, mods, g, w)


def _proj_gates_kernel(hp_ref, hs_ref, lnv_ref, w_ref, proj_ref, w_res, stage, sems, *, n_ctx, row_parts):
    step = pl.program_id(0)

    @pl.when(step == 0)
    def _():
        _load_cast_weights(w_ref, N_QKVU, w_res, stage, sems)

    n_gate = w_res.shape[1] - D_GMLP
    for rows in _row_parts(hp_ref.shape[0], row_parts):
        h = jnp.where(step < n_ctx, hp_ref[rows, :], hs_ref[rows, :])
        acc = jnp.dot(h, w_res[:, :D_GMLP], preferred_element_type=F32)
        proj_ref[rows, n_gate:] = _layer_norm(jax.nn.gelu(acc), lnv_ref[...]).astype(BF16)
        for blk in range(n_gate // COL_BLOCK):
            cols = slice(blk * COL_BLOCK, (blk + 1) * COL_BLOCK)
            acc = jnp.dot(h, w_res[:, D_GMLP + cols.start:D_GMLP + cols.stop], preferred_element_type=F32)
            proj_ref[rows, cols] = jax.nn.sigmoid(acc).astype(BF16)


def _proj_gates(h_ctx, h_lat, ln_v, w_in, *, tm=512, row_parts=2):
    n_ctx, n_lat = h_ctx.shape[0] // tm, h_lat.shape[0] // tm
    n = D_IN - N_QKVU
    return pl.pallas_call(
        functools.partial(_proj_gates_kernel, n_ctx=n_ctx, row_parts=row_parts),
        grid=(n_ctx + n_lat,),
        in_specs=[
            pl.BlockSpec((tm, D_MODEL), lambda i: (jnp.minimum(i, n_ctx - 1), 0)),
            pl.BlockSpec((tm, D_MODEL), lambda i: (jnp.maximum(i - n_ctx, 0), 0)),
            pl.BlockSpec((1, D_GMLP), lambda i: (0, 0)),
            pl.BlockSpec(memory_space=pl.ANY),
        ],
        out_specs=pl.BlockSpec((tm, n), lambda i: (i, 0)),
        out_shape=jax.ShapeDtypeStruct(((n_ctx + n_lat) * tm, n), BF16),
        scratch_shapes=[pltpu.VMEM((D_MODEL, n), BF16), pltpu.VMEM((2, LOAD_ROWS, n), F32),
                        pltpu.SemaphoreType.DMA((2,))],
        compiler_params=_params(("arbitrary",), 56),
        name="proj_gates",
    )(h_ctx, h_lat, ln_v, w_in)


def _softmax_parts(logits2):
    m = functools.reduce(jnp.maximum, [jnp.max(t, axis=-1, keepdims=True) for t in logits2])
    es = [jnp.exp2(t - m) for t in logits2]
    inv = 1.0 / functools.reduce(jnp.add, [jnp.sum(e, axis=-1, keepdims=True) for e in es])
    return [e.astype(BF16) for e in es], inv


def _qk(q, k):
    return lax.dot_general(q, k, (((1,), (1,)), ((), ())), preferred_element_type=F32) * (ATTN_SCALE * LOG2E)


def _ctx_attn_kernel(q_ref, k_ref, v_ref, o_ref):
    for h in range(N_HEADS_A):
        cols = slice(h * HEAD_DIM, (h + 1) * HEAD_DIM)
        (p,), inv = _softmax_parts([_qk(q_ref[:, cols], k_ref[:, cols])])
        o_ref[:, cols] = (jnp.dot(p, v_ref[:, cols], preferred_element_type=F32) * inv).astype(BF16)


def _ctx_attention(proj, seq):
    m = proj.shape[0]
    spec = lambda col: pl.BlockSpec((seq, D_ATTN), lambda b: (b, col))
    return pl.pallas_call(
        _ctx_attn_kernel,
        grid=(m // seq,),
        in_specs=[spec(0), spec(1), spec(2)],
        out_specs=spec(0),
        out_shape=jax.ShapeDtypeStruct((m, D_ATTN), BF16),
        compiler_params=_params(("parallel",), 32),
        name="ctx_attention",
    )(proj, proj, proj)


def _lat_attn_kernel(q_ref, k_ref, v_ref, kc_ref, vc_ref, tab_ref, o_ref, sw_ref, pw_ref, *, rows, kh):
    heads = tab_ref.shape[0]
    past = kc_ref.shape[1] // N_HEADS_A
    win = kh * GRID_W
    starts = [min(max(r - kh // 2, 0), rows - kh) * GRID_W for r in range(rows)]
    for j in range(heads):
        head = pl.program_id(1) * heads + j
        cols = slice(j * HEAD_DIM, (j + 1) * HEAD_DIM)
        sw, pw = sw_ref.at[j % 2], pw_ref.at[j % 2]
        kc = kc_ref[0, pl.ds(head, past, stride=N_HEADS_A), :].astype(BF16)
        vc = vc_ref[0, pl.ds(head, past, stride=N_HEADS_A), :].astype(BF16)
        s_c = _qk(q_ref[:, cols], kc)
        for r, start in enumerate(starts):
            off = start // GRID_W - r + (KH_MAX - 1)
            lane0 = (off + off % 2) * GRID_W
            bias = tab_ref[j, off % 2, :, lane0:lane0 + win]
            q = q_ref[r * GRID_W:(r + 1) * GRID_W, cols]
            sw[r * GRID_W:(r + 1) * GRID_W, :] = _qk(q, k_ref[start:start + win, cols]) + bias
        (p_w, p_c), inv = _softmax_parts([sw[...], s_c])
        pw[...] = p_w
        o_c = jnp.dot(p_c, vc, preferred_element_type=F32)
        for r, start in enumerate(starts):
            q_rows = slice(r * GRID_W, (r + 1) * GRID_W)
            o_w = jnp.dot(pw[q_rows, :], v_ref[start:start + win, cols], preferred_element_type=F32)
            o_ref[q_rows, cols] = ((o_w + o_c[q_rows, :]) * inv[q_rows, :]).astype(BF16)


def _lat_attention(proj, cache_k, cache_v, table, seq, *, heads_per_step=4):
    m = proj.shape[0]
    rows = seq // GRID_W
    kh = min(KH_MAX, rows)
    n_hp = N_HEADS_A // heads_per_step
    width = heads_per_step * HEAD_DIM
    qkv = lambda part: pl.BlockSpec((seq, width), lambda b, hp: (b, part * n_hp + hp))
    cache = pl.BlockSpec((1,) + cache_k.shape[1:], lambda b, hp: (b, 0, 0))
    return pl.pallas_call(
        functools.partial(_lat_attn_kernel, rows=rows, kh=kh),
        grid=(m // seq, n_hp),
        in_specs=[qkv(0), qkv(1), qkv(2), cache, cache,
                  pl.BlockSpec((heads_per_step,) + table.shape[1:], lambda b, hp: (hp, 0, 0, 0))],
        out_specs=pl.BlockSpec((seq, width), lambda b, hp: (b, hp)),
        out_shape=jax.ShapeDtypeStruct((m, D_ATTN), BF16),
        scratch_shapes=[pltpu.VMEM((2, seq, kh * GRID_W), F32), pltpu.VMEM((2, seq, kh * GRID_W), BF16)],
        compiler_params=_params(("parallel", "arbitrary"), 48),
        name="lat_attention",
    )(proj, proj, proj, cache_k, cache_v, table)


def _merge_kernel(x_ref, mod_ref, oa_ref, gu_ref, vn_ref, ga_ref, gb_ref,
                  ws_ref, bst_ref, wpa_ref, wpb_ref, wo_ref, g_ref, gffn_ref, o_ref, h2_ref,
                  *rest, row_parts, is_first):
    if is_first:
        w_f32 = (wpa_ref, wpb_ref, wo_ref)
        pubs, w_res, (ob_ref, stage, sems) = rest[:3], rest[3:6], rest[6:]
        wpa_ref, wpb_ref, wo_ref = w_res
        step = pl.program_id(0)
        publish = [pltpu.make_async_copy(res, pub, sems.at[2 + n])
                   for n, (res, pub) in enumerate(zip(w_res, pubs))]

        @pl.when(step == 0)
        def _():
            for src, res, pub in zip(w_f32, w_res, publish):
                _load_cast_weights(src, 0, res, stage, sems)
                pub.start()
    else:
        (ob_ref,) = rest
    for part in _row_parts(x_ref.shape[0], row_parts):
        for c in range(part.start // CHUNK, part.stop // CHUNK):
            rows = slice(c * CHUNK, (c + 1) * CHUNK)
            for g in range(N_GROUPS_B):
                cols = slice(g * GROUP_CH, (g + 1) * GROUP_CH)
                s = jnp.dot(ws_ref[g], vn_ref[rows, cols], preferred_element_type=F32) + bst_ref[:, g:g + 1]
                ob_ref[rows, cols] = (gu_ref[rows, cols].astype(F32) * s).astype(BF16)
        pa = jnp.dot(oa_ref[part, :], wpa_ref[...], preferred_element_type=F32)
        pb = jnp.dot(ob_ref[part, :], wpb_ref[...], preferred_element_type=F32)
        mixed = (ga_ref[part, :].astype(F32) * pa + gb_ref[part, :].astype(F32) * pb).astype(BF16)
        y = jnp.dot(mixed, wo_ref[...], preferred_element_type=F32)
        x1 = x_ref[part, :] + mod_ref[0, 2:3, :] * _rms_norm(y, g_ref[...])
        o_ref[part, :] = x1
        h2 = _rms_norm(x1, gffn_ref[...]) * (1.0 + mod_ref[0, 4:5, :]) + mod_ref[0, 3:4, :]
        h2_ref[part, :] = h2.astype(BF16)
    if is_first:
        @pl.when(step == pl.num_programs(0) - 1)
        def _():
            for pub in publish:
                pub.wait()


def _merge(x, mods, seq, o_a, proj_qkvu, proj_gates, gates_row0, w_s, b_s_t, w_pa, w_pb, w_o, g, g_ffn, *,
           is_first, tm=512, row_parts=2):
    m = x.shape[0]
    assert (tm // row_parts) % CHUNK == 0 and gates_row0 % tm == 0
    g0 = gates_row0 // tm
    whole = lambda a: pl.BlockSpec(a.shape, lambda i: (0,) * a.ndim, pipeline_mode=pl.Buffered(1))
    big = (w_pa, w_pb, w_o)
    out_specs = [pl.BlockSpec((tm, D_MODEL), lambda i: (i, 0))] * 2
    out_shape = [jax.ShapeDtypeStruct((m, D_MODEL), F32), jax.ShapeDtypeStruct((m, D_MODEL), BF16)]
    scratch = [pltpu.VMEM((tm, D_GMLP), BF16)]
    if is_first:
        big_specs = [pl.BlockSpec(memory_space=pl.ANY)] * len(big)
        out_specs += [pl.BlockSpec(memory_space=pl.ANY)] * len(big)
        out_shape += [jax.ShapeDtypeStruct(w.shape, BF16) for w in big]
        assert all(w.shape[1] == D_MODEL for w in big)
        scratch = ([pltpu.VMEM(w.shape, BF16) for w in big] + scratch
                   + [pltpu.VMEM((2, LOAD_ROWS, D_MODEL), F32), pltpu.SemaphoreType.DMA((2 + len(big),))])
    else:
        big_specs = [whole(w) for w in big]
    return pl.pallas_call(
        functools.partial(_merge_kernel, row_parts=row_parts, is_first=is_first),
        grid=(m // tm,),
        in_specs=[
            pl.BlockSpec((tm, D_MODEL), lambda i: (i, 0)),
            pl.BlockSpec((1, N_MOD, D_MODEL), lambda i: (i * tm // seq, 0, 0)),
            pl.BlockSpec((tm, D_ATTN), lambda i: (i, 0)),
            pl.BlockSpec((tm, D_GMLP), lambda i: (i, 3)),
            pl.BlockSpec((tm, D_GMLP), lambda i: (g0 + i, 4)),
            pl.BlockSpec((tm, D_MODEL), lambda i: (g0 + i, 0)),
            pl.BlockSpec((tm, D_MODEL), lambda i: (g0 + i, 1)),
            whole(w_s), whole(b_s_t), *big_specs, whole(g), whole(g_ffn),
        ],
        out_specs=out_specs,
        out_shape=out_shape,
        scratch_shapes=scratch,
        compiler_params=_params(("arbitrary",), 60),
        name="merge_ctx" if is_first else "merge_lat",
    )(x, mods, o_a, proj_qkvu, proj_gates, proj_gates, proj_gates,
      w_s, b_s_t, w_pa, w_pb, w_o, g, g_ffn)


SC_LANES = 16
SC_BLOCK = (32, 512)


def _bf16_bits(u):
    return lax.shift_right_logical(u + 0x7FFF + (lax.shift_right_logical(u, 16) & 1), 16)


def _sc_pack_bf16(w):
    rows, cols = w.shape
    blk_r, blk_c = SC_BLOCK
    unroll = 8
    assert rows % blk_r == 0 and cols % blk_c == 0 and blk_c % (unroll * SC_LANES) == 0
    mesh = plsc.VectorSubcoreMesh(core_axis_name="core", subcore_axis_name="subcore")

    def body(in_vmem, out_vmem):
        @pl.loop(0, blk_r // 2)
        def _(r):
            @pl.loop(0, blk_c, step=unroll * SC_LANES)
            def _(c0):
                for k in range(unroll):
                    lanes = pl.ds(c0 + k * SC_LANES, SC_LANES)
                    lo = in_vmem[2 * r, lanes]
                    hi = in_vmem[2 * r + 1, lanes]
                    out_vmem[r, lanes] = _bf16_bits(lo) | lax.shift_left(_bf16_bits(hi), 16)

    @pl.kernel(out_type=jax.ShapeDtypeStruct((rows // 2, cols), jnp.int32), mesh=mesh, scratch_types=[],
               compiler_params=pltpu.CompilerParams(use_tc_tiling_on_sc=True))
    def pack_kernel(w_hbm, o_hbm):
        pltpu.emit_pipeline(
            body,
            grid=(rows // blk_r, cols // blk_c),
            in_specs=[pl.BlockSpec((blk_r, blk_c), lambda i, j: (i, j))],
            out_specs=[pl.BlockSpec((blk_r // 2, blk_c), lambda i, j: (i, j))],
            core_axis_name=("core", "subcore"),
            dimension_semantics=(pltpu.PARALLEL, pltpu.PARALLEL),
        )(w_hbm, o_hbm)

    return pack_kernel(lax.bitcast_convert_type(w, jnp.int32))


def _ffn_x_block(f, n_f, n_x):
    return jnp.clip(f - (n_f - n_x), 0, n_x - 1)


def _ffn_kernel(x_ref, h_ref, mod_ref, gpost_ref, wg_ref, wu_ref, wd_ref, o_ref, acc_ref, *, n_x, row_parts):
    f = pl.program_id(1)
    x_rows = x_ref.shape[0]
    tm = h_ref.shape[0]

    @pl.when(f == 0)
    def _():
        acc_ref[...] = jnp.zeros_like(acc_ref)

    row0 = pl.multiple_of(_ffn_x_block(f, pl.num_programs(1), n_x) * x_rows, x_rows)
    o_ref[pl.ds(row0, x_rows), :] = x_ref[...]

    for part in range(row_parts):
        rows = slice(part * (tm // row_parts), (part + 1) * (tm // row_parts))
        h = h_ref[rows, :]
        gate = jnp.dot(h, pltpu.bitcast(wg_ref[...], BF16), preferred_element_type=F32)
        up = jnp.dot(h, pltpu.bitcast(wu_ref[...], BF16), preferred_element_type=F32)
        act = (jax.nn.silu(gate) * up).astype(BF16)
        acc_ref[rows, :] += jnp.dot(act, pltpu.bitcast(wd_ref[...], BF16), preferred_element_type=F32)

    @pl.when(f == pl.num_programs(1) - 1)
    def _():
        scale = mod_ref[0, 5:6, :] * gpost_ref[...]

        def residual_rows(c, carry):
            rows = pl.ds(pl.multiple_of(c * SUBLANES, SUBLANES), SUBLANES)
            a = acc_ref[rows, :]
            r = lax.rsqrt(jnp.mean(a * a, axis=-1, keepdims=True) + EPS)
            o_ref[rows, :] = o_ref[rows, :] + (a * r) * scale
            return carry

        lax.fori_loop(0, tm // SUBLANES, residual_rows, 0, unroll=32)


def _ffn(x, h, mods, seq, g_post, w_gate, w_up, w_down, *, tm=1024, tf=512, x_rows=128, row_parts=1):
    m = x.shape[0]
    n_f = D_FF // tf
    n_x = tm // x_rows
    assert n_x <= n_f
    return pl.pallas_call(
        functools.partial(_ffn_kernel, n_x=n_x, row_parts=row_parts),
        grid=(m // tm, n_f),
        in_specs=[
            pl.BlockSpec((x_rows, D_MODEL), lambda i, f: (i * n_x + _ffn_x_block(f, n_f, n_x), 0)),
            pl.BlockSpec((tm, D_MODEL), lambda i, f: (i, 0)),
            pl.BlockSpec((1, N_MOD, D_MODEL), lambda i, f: (i * tm // seq, 0, 0)),
            pl.BlockSpec((1, D_MODEL), lambda i, f: (0, 0)),
            pl.BlockSpec((D_MODEL // 2, tf), lambda i, f: (0, f)),
            pl.BlockSpec((D_MODEL // 2, tf), lambda i, f: (0, f)),
            pl.BlockSpec((tf // 2, D_MODEL), lambda i, f: (f, 0)),
        ],
        out_specs=pl.BlockSpec((tm, D_MODEL), lambda i, f: (i, 0)),
        out_shape=jax.ShapeDtypeStruct((m, D_MODEL), F32),
        scratch_shapes=[pltpu.VMEM((tm, D_MODEL), F32)],
        compiler_params=_params(("parallel", "arbitrary"), 58),
        name="ffn",
    )(x, h, mods, g_post, w_gate, w_up, w_down)


def kernel(x_prompt, x_sample, cache_k, cache_v, c, c_ctx, w_ada, b_ada, norm_mix_pre, norm_mix_post,
           norm_ffn_pre, norm_ffn_post, w_in, rpb, ln_v, w_s, b_s, w_pa, w_pb, w_o, w_gate, w_up, w_down):
    assert w_ada.shape[0] == DEPTH == 1
    batch, seq, _ = x_prompt.shape
    dec_batch, dec_seq, _ = x_sample.shape
    past = cache_k.shape[2]

    row = lambda a: a[0].reshape(1, -1)
    bf = lambda a: a[0].astype(BF16)

    cvecs = jnp.concatenate(
        [c, c_ctx[None], jnp.zeros((MOD_ROWS - dec_batch - 1, D_MODEL), F32)], axis=0)
    mods = _modulation(cvecs, w_ada[0], b_ada[0])
    mods_lat = mods[:dec_batch].reshape(dec_batch, N_MOD, D_MODEL)
    mods_ctx = mods[dec_batch:dec_batch + 1].reshape(1, N_MOD, D_MODEL)
    table = _bias_table(rpb[0])

    w_in_f = w_in.reshape(D_MODEL, D_IN)
    mix_head = (bf(w_s), b_s[0].T)
    mix_tail = (row(norm_mix_post), row(norm_ffn_pre))
    ffn = (row(norm_ffn_post), _sc_pack_bf16(w_gate[0]), _sc_pack_bf16(w_up[0]), _sc_pack_bf16(w_down[0]))

    xp = x_prompt.reshape(batch * seq, D_MODEL)
    xs = x_sample.reshape(dec_batch * dec_seq, D_MODEL)
    hp, proj_p, k_p, v_p, w_qkvu = _proj_qkvu(xp, mods_ctx, batch * seq, row(norm_mix_pre), w_in_f, is_ctx=True)
    hs, proj_s = _proj_qkvu(xs, mods_lat, dec_seq, row(norm_mix_pre), w_qkvu, is_ctx=False)
    gates = _proj_gates(hp, hs, row(ln_v), w_in_f)

    oa_p = _ctx_attention(proj_p, seq)
    xp, hp, *mix_bf = _merge(xp, mods_ctx, batch * seq, oa_p, proj_p, gates, 0,
                             *mix_head, w_pa[0], w_pb[0], w_o[0], *mix_tail, is_first=True)
    y_prompt = _ffn(xp, hp, mods_ctx, batch * seq, *ffn).reshape(batch, seq, D_MODEL)

    oa_s = _lat_attention(proj_s, cache_k.reshape(dec_batch, past * N_HEADS_A, HEAD_DIM),
                          cache_v.reshape(dec_batch, past * N_HEADS_A, HEAD_DIM), table, dec_seq)
    xs, hs = _merge(xs, mods_lat, dec_seq, oa_s, proj_s, gates, batch * seq,
                    *mix_head, *mix_bf, *mix_tail, is_first=False)
    y_sample = _ffn(xs, hs, mods_lat, dec_seq, *ffn).reshape(dec_batch, dec_seq, D_MODEL)

    state_shape = (batch, DEPTH, seq, N_HEADS_A, HEAD_DIM)
    return y_prompt, y_sample, k_p.reshape(state_shape), v_p.reshape(state_shape)
```

```python
import functools

import jax
import jax.numpy as jnp
from jax import lax
from jax.experimental import pallas as pl
from jax.experimental.pallas import tpu as pltpu
from jax.experimental.pallas import tpu_sc as plsc

D_MODEL = 2048
DEPTH = 1
GRID_W = 64
N_HEADS_A = 8
HEAD_DIM = 128
D_ATTN = N_HEADS_A * HEAD_DIM
KH_MAX = 8
KW = 16
CHUNK = 128
N_GROUPS_B = 8
D_GMLP = 1024
GROUP_CH = D_GMLP // N_GROUPS_B
D_FF = ((8 * D_MODEL // 3 + 255) // 256) * 256
N_MOD = 6
EPS = 1e-6
ATTN_SCALE = HEAD_DIM ** -0.5
LOG2E = 1.4426950408889634
D_IN = 3 * D_ATTN + 2 * D_GMLP + 2 * D_MODEL

N_DR = 2 * KH_MAX - 1
N_DC = 2 * KW - 1
COL_BLOCK = 1024
SUBLANES = 8
MOD_ROWS = SUBLANES

F32 = jnp.float32
BF16 = jnp.bfloat16

MIB = 1024 * 1024


def _params(semantics, vmem_mib):
    return pltpu.CompilerParams(dimension_semantics=semantics, vmem_limit_bytes=vmem_mib * MIB)


def _rms_norm(x, g):
    return x * lax.rsqrt(jnp.mean(x * x, axis=-1, keepdims=True) + EPS) * g


def _layer_norm(x, g):
    xc = x - jnp.mean(x, axis=-1, keepdims=True)
    return xc * lax.rsqrt(jnp.mean(xc * xc, axis=-1, keepdims=True) + EPS) * g


def _modulation_kernel(c_ref, w_ref, b_ref, o_ref):
    s = jax.nn.silu(c_ref[...]).astype(BF16)
    o_ref[...] = jnp.dot(s, w_ref[...].astype(BF16), preferred_element_type=F32) + b_ref[...]


def _modulation(cvecs, w_ada, b_ada):
    tn = 1024
    n = N_MOD * D_MODEL
    return pl.pallas_call(
        _modulation_kernel,
        grid=(n // tn,),
        in_specs=[
            pl.BlockSpec((MOD_ROWS, D_MODEL), lambda j: (0, 0)),
            pl.BlockSpec((D_MODEL, tn), lambda j: (0, j)),
            pl.BlockSpec((1, tn), lambda j: (0, j)),
        ],
        out_specs=pl.BlockSpec((MOD_ROWS, tn), lambda j: (0, j)),
        out_shape=jax.ShapeDtypeStruct((MOD_ROWS, n), F32),
        compiler_params=_params(("parallel",), 40),
        name="modulation",
    )(cvecs, w_ada, b_ada.reshape(1, n))


def _bias_table_kernel(rpb_ref, o_ref):
    qc = lax.broadcasted_iota(jnp.int32, (GRID_W, GRID_W), 0)
    kc = lax.broadcasted_iota(jnp.int32, (GRID_W, GRID_W), 1)
    cs = jnp.clip(qc - KW // 2, 0, GRID_W - KW)
    valid = (kc >= cs) & (kc < cs + KW)
    lanes = rpb_ref.shape[-1]
    tiles = []
    for dr in range(N_DR):
        row = jnp.broadcast_to(rpb_ref[0, dr:dr + 1, :], (GRID_W, lanes))
        t = pltpu.roll(row, lanes - (KW - 1), 1, stride=1, stride_axis=0)[:, :GRID_W]
        tiles.append(jnp.where(valid, t * LOG2E, -jnp.inf))
    pad = jnp.zeros((GRID_W, GRID_W), F32)
    o_ref[0, 0] = jnp.concatenate(tiles + [pad], axis=-1)
    o_ref[0, 1] = jnp.concatenate([pad] + tiles, axis=-1)


def _bias_table(rpb):
    width = (N_DR + 1) * GRID_W
    lanes = 128
    rpb_rows = jnp.pad(rpb, ((0, 0), (0, 0), (0, lanes - N_DC)))
    return pl.pallas_call(
        _bias_table_kernel,
        grid=(N_HEADS_A,),
        in_specs=[pl.BlockSpec((1, N_DR, lanes), lambda h: (h, 0, 0))],
        out_specs=pl.BlockSpec((1, 2, GRID_W, width), lambda h: (h, 0, 0, 0)),
        out_shape=jax.ShapeDtypeStruct((N_HEADS_A, 2, GRID_W, width), F32),
        compiler_params=_params(("parallel",), 16),
        name="bias_table",
    )(rpb_rows)


LOAD_ROWS = 256
N_QKVU = 3 * D_ATTN + D_GMLP


def _row_parts(tm, parts):
    return [slice(p * (tm // parts), (p + 1) * (tm // parts)) for p in range(parts)]


def _load_cast_weights(w_hbm, col0, w_res, stage, sems):
    rows, width = w_res.shape
    chunk = stage.shape[1]
    n_chunks = rows // chunk

    def copy(c):
        src = w_hbm.at[pl.ds(c * chunk, chunk), pl.ds(col0, width)]
        return pltpu.make_async_copy(src, stage.at[c % 2], sems.at[c % 2])

    copy(0).start()
    for c in range(n_chunks):
        if c + 1 < n_chunks:
            copy(c + 1).start()
        copy(c).wait()
        w_res[c * chunk:(c + 1) * chunk, :] = stage[c % 2].astype(BF16)


def _proj_qkvu_kernel(x_ref, mod_ref, g_ref, w_ref, h_ref, proj_ref, *rest, is_ctx):
    if is_ctx:
        k_ref, v_ref, wpub_ref, w_res, stage, sems = rest
        step = pl.program_id(0)
        publish = pltpu.make_async_copy(w_res, wpub_ref, sems.at[2])

        @pl.when(step == 0)
        def _():
            _load_cast_weights(w_ref, 0, w_res, stage, sems)
            publish.start()
    else:
        w_res = w_ref
    y = _rms_norm(x_ref[...], g_ref[...])
    h = (y * (1.0 + mod_ref[0, 1:2, :]) + mod_ref[0, 0:1, :]).astype(BF16)
    h_ref[...] = h
    for blk in (3, 0, 1, 2):
        cols = slice(blk * COL_BLOCK, (blk + 1) * COL_BLOCK)
        acc = jnp.dot(h, w_res[:, cols], preferred_element_type=F32)
        if is_ctx and blk in (1, 2):
            kv_ref = (k_ref, v_ref)[blk - 1]
            for head in range(N_HEADS_A):
                dst = pl.ds(head, x_ref.shape[0], stride=N_HEADS_A)
                kv_ref[dst, :] = acc[:, head * HEAD_DIM:(head + 1) * HEAD_DIM]
        if blk == 3:
            acc = jax.nn.gelu(acc)
        proj_ref[:, cols] = acc.astype(BF16)
    if is_ctx:
        @pl.when(step == pl.num_programs(0) - 1)
        def _():
            publish.wait()


def _proj_qkvu(x, mods, seq, g, w, *, is_ctx, tm=512):
    m = x.shape[0]
    out_shape = [jax.ShapeDtypeStruct((m, D_MODEL), BF16), jax.ShapeDtypeStruct((m, N_QKVU), BF16)]
    out_specs = [pl.BlockSpec((tm, D_MODEL), lambda i: (i, 0)), pl.BlockSpec((tm, N_QKVU), lambda i: (i, 0))]
    scratch = []
    if is_ctx:
        out_shape += [jax.ShapeDtypeStruct((m * N_HEADS_A, HEAD_DIM), F32)] * 2
        out_specs += [pl.BlockSpec((tm * N_HEADS_A, HEAD_DIM), lambda i: (i, 0))] * 2
        out_shape += [jax.ShapeDtypeStruct((D_MODEL, N_QKVU), BF16)]
        out_specs += [pl.BlockSpec(memory_space=pl.ANY)]
        w_spec = pl.BlockSpec(memory_space=pl.ANY)
        scratch = [pltpu.VMEM((D_MODEL, N_QKVU), BF16), pltpu.VMEM((2, LOAD_ROWS, N_QKVU), F32),
                   pltpu.SemaphoreType.DMA((3,))]
    else:
        w_spec = pl.BlockSpec((D_MODEL, N_QKVU), lambda i: (0, 0), pipeline_mode=pl.Buffered(1))
    return pl.pallas_call(
        functools.partial(_proj_qkvu_kernel, is_ctx=is_ctx),
        grid=(m // tm,),
        in_specs=[
            pl.BlockSpec((tm, D_MODEL), lambda i: (i, 0)),
            pl.BlockSpec((1, N_MOD, D_MODEL), lambda i: (i * tm // seq, 0, 0)),
            pl.BlockSpec((1, D_MODEL), lambda i: (0, 0)),
            w_spec,
        ],
        out_specs=out_specs,
        out_shape=out_shape,
        scratch_shapes=scratch,
        compiler_params=_params(("arbitrary",), 58),
        name="proj_qkvu_ctx" if is_ctx else "proj_qkvu_lat",
    )(x, mods, g, w)


def _proj_gates_kernel(hp_ref, hs_ref, lnv_ref, w_ref, proj_ref, w_res, stage, sems, *, n_ctx, row_parts):
    step = pl.program_id(0)

    @pl.when(step == 0)
    def _():
        _load_cast_weights(w_ref, N_QKVU, w_res, stage, sems)

    n_gate = w_res.shape[1] - D_GMLP
    for rows in _row_parts(hp_ref.shape[0], row_parts):
        h = jnp.where(step < n_ctx, hp_ref[rows, :], hs_ref[rows, :])
        acc = jnp.dot(h, w_res[:, :D_GMLP], preferred_element_type=F32)
        proj_ref[rows, n_gate:] = _layer_norm(jax.nn.gelu(acc), lnv_ref[...]).astype(BF16)
        for blk in range(n_gate // COL_BLOCK):
            cols = slice(blk * COL_BLOCK, (blk + 1) * COL_BLOCK)
            acc = jnp.dot(h, w_res[:, D_GMLP + cols.start:D_GMLP + cols.stop], preferred_element_type=F32)
            proj_ref[rows, cols] = jax.nn.sigmoid(acc).astype(BF16)


def _proj_gates(h_ctx, h_lat, ln_v, w_in, *, tm=512, row_parts=2):
    n_ctx, n_lat = h_ctx.shape[0] // tm, h_lat.shape[0] // tm
    n = D_IN - N_QKVU
    return pl.pallas_call(
        functools.partial(_proj_gates_kernel, n_ctx=n_ctx, row_parts=row_parts),
        grid=(n_ctx + n_lat,),
        in_specs=[
            pl.BlockSpec((tm, D_MODEL), lambda i: (jnp.minimum(i, n_ctx - 1), 0)),
            pl.BlockSpec((tm, D_MODEL), lambda i: (jnp.maximum(i - n_ctx, 0), 0)),
            pl.BlockSpec((1, D_GMLP), lambda i: (0, 0)),
            pl.BlockSpec(memory_space=pl.ANY),
        ],
        out_specs=pl.BlockSpec((tm, n), lambda i: (i, 0)),
        out_shape=jax.ShapeDtypeStruct(((n_ctx + n_lat) * tm, n), BF16),
        scratch_shapes=[pltpu.VMEM((D_MODEL, n), BF16), pltpu.VMEM((2, LOAD_ROWS, n), F32),
                        pltpu.SemaphoreType.DMA((2,))],
        compiler_params=_params(("arbitrary",), 56),
        name="proj_gates",
    )(h_ctx, h_lat, ln_v, w_in)


def _softmax_parts(logits2):
    m = functools.reduce(jnp.maximum, [jnp.max(t, axis=-1, keepdims=True) for t in logits2])
    es = [jnp.exp2(t - m) for t in logits2]
    inv = 1.0 / functools.reduce(jnp.add, [jnp.sum(e, axis=-1, keepdims=True) for e in es])
    return [e.astype(BF16) for e in es], inv


def _qk(q, k):
    return lax.dot_general(q, k, (((1,), (1,)), ((), ())), preferred_element_type=F32) * (ATTN_SCALE * LOG2E)


def _ctx_attn_kernel(q_ref, k_ref, v_ref, o_ref):
    for h in range(N_HEADS_A):
        cols = slice(h * HEAD_DIM, (h + 1) * HEAD_DIM)
        (p,), inv = _softmax_parts([_qk(q_ref[:, cols], k_ref[:, cols])])
        o_ref[:, cols] = (jnp.dot(p, v_ref[:, cols], preferred_element_type=F32) * inv).astype(BF16)


def _ctx_attention(proj, seq):
    m = proj.shape[0]
    spec = lambda col: pl.BlockSpec((seq, D_ATTN), lambda b: (b, col))
    return pl.pallas_call(
        _ctx_attn_kernel,
        grid=(m // seq,),
        in_specs=[spec(0), spec(1), spec(2)],
        out_specs=spec(0),
        out_shape=jax.ShapeDtypeStruct((m, D_ATTN), BF16),
        compiler_params=_params(("parallel",), 32),
        name="ctx_attention",
    )(proj, proj, proj)


def _lat_attn_kernel(q_ref, k_ref, v_ref, kc_ref, vc_ref, tab_ref, o_ref, sw_ref, pw_ref, *, rows, kh):
    heads = tab_ref.shape[0]
    past = kc_ref.shape[1] // N_HEADS_A
    win = kh * GRID_W
    starts = [min(max(r - kh // 2, 0), rows - kh) * GRID_W for r in range(rows)]
    for j in range(heads):
        head = pl.program_id(1) * heads + j
        cols = slice(j * HEAD_DIM, (j + 1) * HEAD_DIM)
        sw, pw = sw_ref.at[j % 2], pw_ref.at[j % 2]
        kc = kc_ref[0, pl.ds(head, past, stride=N_HEADS_A), :].astype(BF16)
        vc = vc_ref[0, pl.ds(head, past, stride=N_HEADS_A), :].astype(BF16)
        s_c = _qk(q_ref[:, cols], kc)
        for r, start in enumerate(starts):
            off = start // GRID_W - r + (KH_MAX - 1)
            lane0 = (off + off % 2) * GRID_W
            bias = tab_ref[j, off % 2, :, lane0:lane0 + win]
            q = q_ref[r * GRID_W:(r + 1) * GRID_W, cols]
            sw[r * GRID_W:(r + 1) * GRID_W, :] = _qk(q, k_ref[start:start + win, cols]) + bias
        (p_w, p_c), inv = _softmax_parts([sw[...], s_c])
        pw[...] = p_w
        o_c = jnp.dot(p_c, vc, preferred_element_type=F32)
        for r, start in enumerate(starts):
            q_rows = slice(r * GRID_W, (r + 1) * GRID_W)
            o_w = jnp.dot(pw[q_rows, :], v_ref[start:start + win, cols], preferred_element_type=F32)
            o_ref[q_rows, cols] = ((o_w + o_c[q_rows, :]) * inv[q_rows, :]).astype(BF16)


def _lat_attention(proj, cache_k, cache_v, table, seq, *, heads_per_step=4):
    m = proj.shape[0]
    rows = seq // GRID_W
    kh = min(KH_MAX, rows)
    n_hp = N_HEADS_A // heads_per_step
    width = heads_per_step * HEAD_DIM
    qkv = lambda part: pl.BlockSpec((seq, width), lambda b, hp: (b, part * n_hp + hp))
    cache = pl.BlockSpec((1,) + cache_k.shape[1:], lambda b, hp: (b, 0, 0))
    return pl.pallas_call(
        functools.partial(_lat_attn_kernel, rows=rows, kh=kh),
        grid=(m // seq, n_hp),
        in_specs=[qkv(0), qkv(1), qkv(2), cache, cache,
                  pl.BlockSpec((heads_per_step,) + table.shape[1:], lambda b, hp: (hp, 0, 0, 0))],
        out_specs=pl.BlockSpec((seq, width), lambda b, hp: (b, hp)),
        out_shape=jax.ShapeDtypeStruct((m, D_ATTN), BF16),
        scratch_shapes=[pltpu.VMEM((2, seq, kh * GRID_W), F32), pltpu.VMEM((2, seq, kh * GRID_W), BF16)],
        compiler_params=_params(("parallel", "arbitrary"), 48),
        name="lat_attention",
    )(proj, proj, proj, cache_k, cache_v, table)


def _merge_kernel(x_ref, mod_ref, oa_ref, gu_ref, vn_ref, ga_ref, gb_ref,
                  ws_ref, bst_ref, wpa_ref, wpb_ref, wo_ref, g_ref, gffn_ref, o_ref, h2_ref,
                  *rest, row_parts, is_first):
    if is_first:
        w_f32 = (wpa_ref, wpb_ref, wo_ref)
        pubs, w_res, (ob_ref, stage, sems) = rest[:3], rest[3:6], rest[6:]
        wpa_ref, wpb_ref, wo_ref = w_res
        step = pl.program_id(0)
        publish = [pltpu.make_async_copy(res, pub, sems.at[2 + n])
                   for n, (res, pub) in enumerate(zip(w_res, pubs))]

        @pl.when(step == 0)
        def _():
            for src, res, pub in zip(w_f32, w_res, publish):
                _load_cast_weights(src, 0, res, stage, sems)
                pub.start()
    else:
        (ob_ref,) = rest
    for part in _row_parts(x_ref.shape[0], row_parts):
        for c in range(part.start // CHUNK, part.stop // CHUNK):
            rows = slice(c * CHUNK, (c + 1) * CHUNK)
            for g in range(N_GROUPS_B):
                cols = slice(g * GROUP_CH, (g + 1) * GROUP_CH)
                s = jnp.dot(ws_ref[g], vn_ref[rows, cols], preferred_element_type=F32) + bst_ref[:, g:g + 1]
                ob_ref[rows, cols] = (gu_ref[rows, cols].astype(F32) * s).astype(BF16)
        pa = jnp.dot(oa_ref[part, :], wpa_ref[...], preferred_element_type=F32)
        pb = jnp.dot(ob_ref[part, :], wpb_ref[...], preferred_element_type=F32)
        mixed = (ga_ref[part, :].astype(F32) * pa + gb_ref[part, :].astype(F32) * pb).astype(BF16)
        y = jnp.dot(mixed, wo_ref[...], preferred_element_type=F32)
        x1 = x_ref[part, :] + mod_ref[0, 2:3, :] * _rms_norm(y, g_ref[...])
        o_ref[part, :] = x1
        h2 = _rms_norm(x1, gffn_ref[...]) * (1.0 + mod_ref[0, 4:5, :]) + mod_ref[0, 3:4, :]
        h2_ref[part, :] = h2.astype(BF16)
    if is_first:
        @pl.when(step == pl.num_programs(0) - 1)
        def _():
            for pub in publish:
                pub.wait()


def _merge(x, mods, seq, o_a, proj_qkvu, proj_gates, gates_row0, w_s, b_s_t, w_pa, w_pb, w_o, g, g_ffn, *,
           is_first, tm=512, row_parts=2):
    m = x.shape[0]
    assert (tm // row_parts) % CHUNK == 0 and gates_row0 % tm == 0
    g0 = gates_row0 // tm
    whole = lambda a: pl.BlockSpec(a.shape, lambda i: (0,) * a.ndim, pipeline_mode=pl.Buffered(1))
    big = (w_pa, w_pb, w_o)
    out_specs = [pl.BlockSpec((tm, D_MODEL), lambda i: (i, 0))] * 2
    out_shape = [jax.ShapeDtypeStruct((m, D_MODEL), F32), jax.ShapeDtypeStruct((m, D_MODEL), BF16)]
    scratch = [pltpu.VMEM((tm, D_GMLP), BF16)]
    if is_first:
        big_specs = [pl.BlockSpec(memory_space=pl.ANY)] * len(big)
        out_specs += [pl.BlockSpec(memory_space=pl.ANY)] * len(big)
        out_shape += [jax.ShapeDtypeStruct(w.shape, BF16) for w in big]
        assert all(w.shape[1] == D_MODEL for w in big)
        scratch = ([pltpu.VMEM(w.shape, BF16) for w in big] + scratch
                   + [pltpu.VMEM((2, LOAD_ROWS, D_MODEL), F32), pltpu.SemaphoreType.DMA((2 + len(big),))])
    else:
        big_specs = [whole(w) for w in big]
    return pl.pallas_call(
        functools.partial(_merge_kernel, row_parts=row_parts, is_first=is_first),
        grid=(m // tm,),
        in_specs=[
            pl.BlockSpec((tm, D_MODEL), lambda i: (i, 0)),
            pl.BlockSpec((1, N_MOD, D_MODEL), lambda i: (i * tm // seq, 0, 0)),
            pl.BlockSpec((tm, D_ATTN), lambda i: (i, 0)),
            pl.BlockSpec((tm, D_GMLP), lambda i: (i, 3)),
            pl.BlockSpec((tm, D_GMLP), lambda i: (g0 + i, 4)),
            pl.BlockSpec((tm, D_MODEL), lambda i: (g0 + i, 0)),
            pl.BlockSpec((tm, D_MODEL), lambda i: (g0 + i, 1)),
            whole(w_s), whole(b_s_t), *big_specs, whole(g), whole(g_ffn),
        ],
        out_specs=out_specs,
        out_shape=out_shape,
        scratch_shapes=scratch,
        compiler_params=_params(("arbitrary",), 60),
        name="merge_ctx" if is_first else "merge_lat",
    )(x, mods, o_a, proj_qkvu, proj_gates, proj_gates, proj_gates,
      w_s, b_s_t, w_pa, w_pb, w_o, g, g_ffn)


SC_LANES = 16
SC_BLOCK = (32, 512)


def _bf16_bits(u):
    return lax.shift_right_logical(u + 0x7FFF + (lax.shift_right_logical(u, 16) & 1), 16)


def _sc_pack_bf16(w):
    rows, cols = w.shape
    blk_r, blk_c = SC_BLOCK
    unroll = 8
    assert rows % blk_r == 0 and cols % blk_c == 0 and blk_c % (unroll * SC_LANES) == 0
    mesh = plsc.VectorSubcoreMesh(core_axis_name="core", subcore_axis_name="subcore")

    def body(in_vmem, out_vmem):
        in_vmem = in_vmem.bitcast(jnp.int32)

        @pl.loop(0, blk_r // 2)
        def _(r):
            @pl.loop(0, blk_c, step=unroll * SC_LANES)
            def _(c0):
                for k in range(unroll):
                    lanes = pl.ds(c0 + k * SC_LANES, SC_LANES)
                    lo = in_vmem[2 * r, lanes]
                    hi = in_vmem[2 * r + 1, lanes]
                    out_vmem[r, lanes] = _bf16_bits(lo) | lax.shift_left(_bf16_bits(hi), 16)

    @pl.kernel(out_type=jax.ShapeDtypeStruct((rows // 2, cols), jnp.int32), mesh=mesh, scratch_types=[],
               compiler_params=pltpu.CompilerParams(use_tc_tiling_on_sc=True))
    def pack_kernel(w_hbm, o_hbm):
        pltpu.emit_pipeline(
            body,
            grid=(rows // blk_r, cols // blk_c),
            in_specs=[pl.BlockSpec((blk_r, blk_c), lambda i, j: (i, j))],
            out_specs=[pl.BlockSpec((blk_r // 2, blk_c), lambda i, j: (i, j))],
            core_axis_name=("core", "subcore"),
            dimension_semantics=(pltpu.PARALLEL, pltpu.PARALLEL),
        )(w_hbm, o_hbm)

    return pack_kernel(w)


def _ffn_x_block(f, n_f, n_x):
    return jnp.clip(f - (n_f - n_x), 0, n_x - 1)


def _ffn_kernel(x_ref, h_ref, mod_ref, gpost_ref, wg_ref, wu_ref, wd_ref, o_ref, acc_ref, *, n_x, row_parts):
    f = pl.program_id(1)
    x_rows = x_ref.shape[0]
    tm = h_ref.shape[0]

    @pl.when(f == 0)
    def _():
        acc_ref[...] = jnp.zeros_like(acc_ref)

    row0 = pl.multiple_of(_ffn_x_block(f, pl.num_programs(1), n_x) * x_rows, x_rows)
    o_ref[pl.ds(row0, x_rows), :] = x_ref[...]

    for part in range(row_parts):
        rows = slice(part * (tm // row_parts), (part + 1) * (tm // row_parts))
        h = h_ref[rows, :]
        gate = jnp.dot(h, pltpu.bitcast(wg_ref[...], BF16), preferred_element_type=F32)
        up = jnp.dot(h, pltpu.bitcast(wu_ref[...], BF16), preferred_element_type=F32)
        act = (jax.nn.silu(gate) * up).astype(BF16)
        acc_ref[rows, :] += jnp.dot(act, pltpu.bitcast(wd_ref[...], BF16), preferred_element_type=F32)

    @pl.when(f == pl.num_programs(1) - 1)
    def _():
        scale = mod_ref[0, 5:6, :] * gpost_ref[...]

        def residual_rows(c, carry):
            rows = pl.ds(pl.multiple_of(c * SUBLANES, SUBLANES), SUBLANES)
            a = acc_ref[rows, :]
            r = lax.rsqrt(jnp.mean(a * a, axis=-1, keepdims=True) + EPS)
            o_ref[rows, :] = o_ref[rows, :] + (a * r) * scale
            return carry

        lax.fori_loop(0, tm // SUBLANES, residual_rows, 0, unroll=32)


def _ffn(x, h, mods, seq, g_post, w_gate, w_up, w_down, *, tm=1024, tf=512, x_rows=128, row_parts=1):
    m = x.shape[0]
    n_f = D_FF // tf
    n_x = tm // x_rows
    assert n_x <= n_f
    return pl.pallas_call(
        functools.partial(_ffn_kernel, n_x=n_x, row_parts=row_parts),
        grid=(m // tm, n_f),
        in_specs=[
            pl.BlockSpec((x_rows, D_MODEL), lambda i, f: (i * n_x + _ffn_x_block(f, n_f, n_x), 0)),
            pl.BlockSpec((tm, D_MODEL), lambda i, f: (i, 0)),
            pl.BlockSpec((1, N_MOD, D_MODEL), lambda i, f: (i * tm // seq, 0, 0)),
            pl.BlockSpec((1, D_MODEL), lambda i, f: (0, 0)),
            pl.BlockSpec((D_MODEL // 2, tf), lambda i, f: (0, f)),
            pl.BlockSpec((D_MODEL // 2, tf), lambda i, f: (0, f)),
            pl.BlockSpec((tf // 2, D_MODEL), lambda i, f: (f, 0)),
        ],
        out_specs=pl.BlockSpec((tm, D_MODEL), lambda i, f: (i, 0)),
        out_shape=jax.ShapeDtypeStruct((m, D_MODEL), F32),
        scratch_shapes=[pltpu.VMEM((tm, D_MODEL), F32)],
        compiler_params=_params(("parallel", "arbitrary"), 58),
        name="ffn",
    )(x, h, mods, g_post, w_gate, w_up, w_down)


def kernel(x_prompt, x_sample, cache_k, cache_v, c, c_ctx, w_ada, b_ada, norm_mix_pre, norm_mix_post,
           norm_ffn_pre, norm_ffn_post, w_in, rpb, ln_v, w_s, b_s, w_pa, w_pb, w_o, w_gate, w_up, w_down):
    assert w_ada.shape[0] == DEPTH == 1
    batch, seq, _ = x_prompt.shape
    dec_batch, dec_seq, _ = x_sample.shape
    past = cache_k.shape[2]

    row = lambda a: a[0].reshape(1, -1)
    bf = lambda a: a[0].astype(BF16)

    cvecs = jnp.concatenate(
        [c, c_ctx[None], jnp.zeros((MOD_ROWS - dec_batch - 1, D_MODEL), F32)], axis=0)
    mods = _modulation(cvecs, w_ada[0], b_ada[0])
    mods_lat = mods[:dec_batch].reshape(dec_batch, N_MOD, D_MODEL)
    mods_ctx = mods[dec_batch:dec_batch + 1].reshape(1, N_MOD, D_MODEL)
    table = _bias_table(rpb[0])

    w_in_f = w_in.reshape(D_MODEL, D_IN)
    mix_head = (bf(w_s), b_s[0].T)
    mix_tail = (row(norm_mix_post), row(norm_ffn_pre))
    ffn = (row(norm_ffn_post), _sc_pack_bf16(w_gate[0]), _sc_pack_bf16(w_up[0]), _sc_pack_bf16(w_down[0]))

    xp = x_prompt.reshape(batch * seq, D_MODEL)
    xs = x_sample.reshape(dec_batch * dec_seq, D_MODEL)
    hp, proj_p, k_p, v_p, w_qkvu = _proj_qkvu(xp, mods_ctx, batch * seq, row(norm_mix_pre), w_in_f, is_ctx=True)
    hs, proj_s = _proj_qkvu(xs, mods_lat, dec_seq, row(norm_mix_pre), w_qkvu, is_ctx=False)
    gates = _proj_gates(hp, hs, row(ln_v), w_in_f)

    oa_p = _ctx_attention(proj_p, seq)
    xp, hp, *mix_bf = _merge(xp, mods_ctx, batch * seq, oa_p, proj_p, gates, 0,
                             *mix_head, w_pa[0], w_pb[0], w_o[0], *mix_tail, is_first=True)
    y_prompt = _ffn(xp, hp, mods_ctx, batch * seq, *ffn).reshape(batch, seq, D_MODEL)

    oa_s = _lat_attention(proj_s, cache_k.reshape(dec_batch, past * N_HEADS_A, HEAD_DIM),
                          cache_v.reshape(dec_batch, past * N_HEADS_A, HEAD_DIM), table, dec_seq)
    xs, hs = _merge(xs, mods_lat, dec_seq, oa_s, proj_s, gates, batch * seq,
                    *mix_head, *mix_bf, *mix_tail, is_first=False)
    y_sample = _ffn(xs, hs, mods_lat, dec_seq, *ffn).reshape(dec_batch, dec_seq, D_MODEL)

    state_shape = (batch, DEPTH, seq, N_HEADS_A, HEAD_DIM)
    return y_prompt, y_sample, k_p.reshape(state_shape), v_p.reshape(state_shape)
```

```python
import functools

import jax
import jax.numpy as jnp
from jax import lax
from jax.experimental import pallas as pl
from jax.experimental.pallas import tpu as pltpu
from jax.experimental.pallas import tpu_sc as plsc

D_MODEL = 2048
DEPTH = 1
GRID_W = 64
N_HEADS_A = 8
HEAD_DIM = 128
D_ATTN = N_HEADS_A * HEAD_DIM
KH_MAX = 8
KW = 16
CHUNK = 128
N_GROUPS_B = 8
D_GMLP = 1024
GROUP_CH = D_GMLP // N_GROUPS_B
D_FF = ((8 * D_MODEL // 3 + 255) // 256) * 256
N_MOD = 6
EPS = 1e-6
ATTN_SCALE = HEAD_DIM ** -0.5
LOG2E = 1.4426950408889634
D_IN = 3 * D_ATTN + 2 * D_GMLP + 2 * D_MODEL

N_DR = 2 * KH_MAX - 1
N_DC = 2 * KW - 1
COL_BLOCK = 1024
SUBLANES = 8
MOD_ROWS = SUBLANES

F32 = jnp.float32
BF16 = jnp.bfloat16

MIB = 1024 * 1024


def _params(semantics, vmem_mib):
    return pltpu.CompilerParams(dimension_semantics=semantics, vmem_limit_bytes=vmem_mib * MIB)


def _rms_norm(x, g):
    return x * lax.rsqrt(jnp.mean(x * x, axis=-1, keepdims=True) + EPS) * g


def _layer_norm(x, g):
    xc = x - jnp.mean(x, axis=-1, keepdims=True)
    return xc * lax.rsqrt(jnp.mean(xc * xc, axis=-1, keepdims=True) + EPS) * g


def _modulation_kernel(c_ref, w_ref, b_ref, o_ref):
    s = jax.nn.silu(c_ref[...]).astype(BF16)
    o_ref[...] = jnp.dot(s, w_ref[...].astype(BF16), preferred_element_type=F32) + b_ref[...]


def _modulation(cvecs, w_ada, b_ada):
    tn = 1024
    n = N_MOD * D_MODEL
    return pl.pallas_call(
        _modulation_kernel,
        grid=(n // tn,),
        in_specs=[
            pl.BlockSpec((MOD_ROWS, D_MODEL), lambda j: (0, 0)),
            pl.BlockSpec((D_MODEL, tn), lambda j: (0, j)),
            pl.BlockSpec((1, tn), lambda j: (0, j)),
        ],
        out_specs=pl.BlockSpec((MOD_ROWS, tn), lambda j: (0, j)),
        out_shape=jax.ShapeDtypeStruct((MOD_ROWS, n), F32),
        compiler_params=_params(("parallel",), 40),
        name="modulation",
    )(cvecs, w_ada, b_ada.reshape(1, n))


def _bias_table_kernel(rpb_ref, o_ref):
    qc = lax.broadcasted_iota(jnp.int32, (GRID_W, GRID_W), 0)
    kc = lax.broadcasted_iota(jnp.int32, (GRID_W, GRID_W), 1)
    cs = jnp.clip(qc - KW // 2, 0, GRID_W - KW)
    valid = (kc >= cs) & (kc < cs + KW)
    lanes = rpb_ref.shape[-1]
    tiles = []
    for dr in range(N_DR):
        row = jnp.broadcast_to(rpb_ref[0, dr:dr + 1, :], (GRID_W, lanes))
        t = pltpu.roll(row, lanes - (KW - 1), 1, stride=1, stride_axis=0)[:, :GRID_W]
        tiles.append(jnp.where(valid, t * LOG2E, -jnp.inf))
    pad = jnp.zeros((GRID_W, GRID_W), F32)
    o_ref[0, 0] = jnp.concatenate(tiles + [pad], axis=-1)
    o_ref[0, 1] = jnp.concatenate([pad] + tiles, axis=-1)


def _bias_table(rpb):
    width = (N_DR + 1) * GRID_W
    lanes = 128
    rpb_rows = jnp.pad(rpb, ((0, 0), (0, 0), (0, lanes - N_DC)))
    return pl.pallas_call(
        _bias_table_kernel,
        grid=(N_HEADS_A,),
        in_specs=[pl.BlockSpec((1, N_DR, lanes), lambda h: (h, 0, 0))],
        out_specs=pl.BlockSpec((1, 2, GRID_W, width), lambda h: (h, 0, 0, 0)),
        out_shape=jax.ShapeDtypeStruct((N_HEADS_A, 2, GRID_W, width), F32),
        compiler_params=_params(("parallel",), 16),
        name="bias_table",
    )(rpb_rows)


LOAD_ROWS = 256
N_QKVU = 3 * D_ATTN + D_GMLP


def _row_parts(tm, parts):
    return [slice(p * (tm // parts), (p + 1) * (tm // parts)) for p in range(parts)]


def _load_cast_weights(w_hbm, col0, w_res, stage, sems):
    rows, width = w_res.shape
    chunk = stage.shape[1]
    n_chunks = rows // chunk

    def copy(c):
        src = w_hbm.at[pl.ds(c * chunk, chunk), pl.ds(col0, width)]
        return pltpu.make_async_copy(src, stage.at[c % 2], sems.at[c % 2])

    copy(0).start()
    for c in range(n_chunks):
        if c + 1 < n_chunks:
            copy(c + 1).start()
        copy(c).wait()
        w_res[c * chunk:(c + 1) * chunk, :] = stage[c % 2].astype(BF16)


def _proj_qkvu_kernel(x_ref, mod_ref, g_ref, w_ref, h_ref, proj_ref, *rest, is_ctx):
    if is_ctx:
        k_ref, v_ref, wpub_ref, w_res, stage, sems = rest
        step = pl.program_id(0)
        publish = pltpu.make_async_copy(w_res, wpub_ref, sems.at[2])

        @pl.when(step == 0)
        def _():
            _load_cast_weights(w_ref, 0, w_res, stage, sems)
            publish.start()
    else:
        w_res = w_ref
    y = _rms_norm(x_ref[...], g_ref[...])
    h = (y * (1.0 + mod_ref[0, 1:2, :]) + mod_ref[0, 0:1, :]).astype(BF16)
    h_ref[...] = h
    for blk in (3, 0, 1, 2):
        cols = slice(blk * COL_BLOCK, (blk + 1) * COL_BLOCK)
        acc = jnp.dot(h, w_res[:, cols], preferred_element_type=F32)
        if is_ctx and blk in (1, 2):
            kv_ref = (k_ref, v_ref)[blk - 1]
            for head in range(N_HEADS_A):
                dst = pl.ds(head, x_ref.shape[0], stride=N_HEADS_A)
                kv_ref[dst, :] = acc[:, head * HEAD_DIM:(head + 1) * HEAD_DIM]
        if blk == 3:
            acc = jax.nn.gelu(acc)
        proj_ref[:, cols] = acc.astype(BF16)
    if is_ctx:
        @pl.when(step == pl.num_programs(0) - 1)
        def _():
            publish.wait()


def _proj_qkvu(x, mods, seq, g, w, *, is_ctx, tm=512):
    m = x.shape[0]
    out_shape = [jax.ShapeDtypeStruct((m, D_MODEL), BF16), jax.ShapeDtypeStruct((m, N_QKVU), BF16)]
    out_specs = [pl.BlockSpec((tm, D_MODEL), lambda i: (i, 0)), pl.BlockSpec((tm, N_QKVU), lambda i: (i, 0))]
    scratch = []
    if is_ctx:
        out_shape += [jax.ShapeDtypeStruct((m * N_HEADS_A, HEAD_DIM), F32)] * 2
        out_specs += [pl.BlockSpec((tm * N_HEADS_A, HEAD_DIM), lambda i: (i, 0))] * 2
        out_shape += [jax.ShapeDtypeStruct((D_MODEL, N_QKVU), BF16)]
        out_specs += [pl.BlockSpec(memory_space=pl.ANY)]
        w_spec = pl.BlockSpec(memory_space=pl.ANY)
        scratch = [pltpu.VMEM((D_MODEL, N_QKVU), BF16), pltpu.VMEM((2, LOAD_ROWS, N_QKVU), F32),
                   pltpu.SemaphoreType.DMA((3,))]
    else:
        w_spec = pl.BlockSpec((D_MODEL, N_QKVU), lambda i: (0, 0), pipeline_mode=pl.Buffered(1))
    return pl.pallas_call(
        functools.partial(_proj_qkvu_kernel, is_ctx=is_ctx),
        grid=(m // tm,),
        in_specs=[
            pl.BlockSpec((tm, D_MODEL), lambda i: (i, 0)),
            pl.BlockSpec((1, N_MOD, D_MODEL), lambda i: (i * tm // seq, 0, 0)),
            pl.BlockSpec((1, D_MODEL), lambda i: (0, 0)),
            w_spec,
        ],
        out_specs=out_specs,
        out_shape=out_shape,
        scratch_shapes=scratch,
        compiler_params=_params(("arbitrary",), 58),
        name="proj_qkvu_ctx" if is_ctx else "proj_qkvu_lat",
    )(x, mods, g, w)


def _proj_gates_kernel(hp_ref, hs_ref, lnv_ref, w_ref, proj_ref, w_res, stage, sems, *, n_ctx, row_parts):
    step = pl.program_id(0)

    @pl.when(step == 0)
    def _():
        _load_cast_weights(w_ref, N_QKVU, w_res, stage, sems)

    n_gate = w_res.shape[1] - D_GMLP
    for rows in _row_parts(hp_ref.shape[0], row_parts):
        h = jnp.where(step < n_ctx, hp_ref[rows, :], hs_ref[rows, :])
        acc = jnp.dot(h, w_res[:, :D_GMLP], preferred_element_type=F32)
        proj_ref[rows, n_gate:] = _layer_norm(jax.nn.gelu(acc), lnv_ref[...]).astype(BF16)
        for blk in range(n_gate // COL_BLOCK):
            cols = slice(blk * COL_BLOCK, (blk + 1) * COL_BLOCK)
            acc = jnp.dot(h, w_res[:, D_GMLP + cols.start:D_GMLP + cols.stop], preferred_element_type=F32)
            proj_ref[rows, cols] = jax.nn.sigmoid(acc).astype(BF16)


def _proj_gates(h_ctx, h_lat, ln_v, w_in, *, tm=512, row_parts=2):
    n_ctx, n_lat = h_ctx.shape[0] // tm, h_lat.shape[0] // tm
    n = D_IN - N_QKVU
    return pl.pallas_call(
        functools.partial(_proj_gates_kernel, n_ctx=n_ctx, row_parts=row_parts),
        grid=(n_ctx + n_lat,),
        in_specs=[
            pl.BlockSpec((tm, D_MODEL), lambda i: (jnp.minimum(i, n_ctx - 1), 0)),
            pl.BlockSpec((tm, D_MODEL), lambda i: (jnp.maximum(i - n_ctx, 0), 0)),
            pl.BlockSpec((1, D_GMLP), lambda i: (0, 0)),
            pl.BlockSpec(memory_space=pl.ANY),
        ],
        out_specs=pl.BlockSpec((tm, n), lambda i: (i, 0)),
        out_shape=jax.ShapeDtypeStruct(((n_ctx + n_lat) * tm, n), BF16),
        scratch_shapes=[pltpu.VMEM((D_MODEL, n), BF16), pltpu.VMEM((2, LOAD_ROWS, n), F32),
                        pltpu.SemaphoreType.DMA((2,))],
        compiler_params=_params(("arbitrary",), 56),
        name="proj_gates",
    )(h_ctx, h_lat, ln_v, w_in)


def _softmax_parts(logits2):
    m = functools.reduce(jnp.maximum, [jnp.max(t, axis=-1, keepdims=True) for t in logits2])
    es = [jnp.exp2(t - m) for t in logits2]
    inv = 1.0 / functools.reduce(jnp.add, [jnp.sum(e, axis=-1, keepdims=True) for e in es])
    return [e.astype(BF16) for e in es], inv


def _qk(q, k):
    return lax.dot_general(q, k, (((1,), (1,)), ((), ())), preferred_element_type=F32) * (ATTN_SCALE * LOG2E)


def _ctx_attn_kernel(q_ref, k_ref, v_ref, o_ref):
    for h in range(N_HEADS_A):
        cols = slice(h * HEAD_DIM, (h + 1) * HEAD_DIM)
        (p,), inv = _softmax_parts([_qk(q_ref[:, cols], k_ref[:, cols])])
        o_ref[:, cols] = (jnp.dot(p, v_ref[:, cols], preferred_element_type=F32) * inv).astype(BF16)


def _ctx_attention(proj, seq):
    m = proj.shape[0]
    spec = lambda col: pl.BlockSpec((seq, D_ATTN), lambda b: (b, col))
    return pl.pallas_call(
        _ctx_attn_kernel,
        grid=(m // seq,),
        in_specs=[spec(0), spec(1), spec(2)],
        out_specs=spec(0),
        out_shape=jax.ShapeDtypeStruct((m, D_ATTN), BF16),
        compiler_params=_params(("parallel",), 32),
        name="ctx_attention",
    )(proj, proj, proj)


def _lat_attn_kernel(q_ref, k_ref, v_ref, kc_ref, vc_ref, tab_ref, o_ref, sw_ref, pw_ref, *, rows, kh):
    heads = tab_ref.shape[0]
    past = kc_ref.shape[1] // N_HEADS_A
    win = kh * GRID_W
    starts = [min(max(r - kh // 2, 0), rows - kh) * GRID_W for r in range(rows)]
    for j in range(heads):
        head = pl.program_id(1) * heads + j
        cols = slice(j * HEAD_DIM, (j + 1) * HEAD_DIM)
        sw, pw = sw_ref.at[j % 2], pw_ref.at[j % 2]
        kc = kc_ref[0, pl.ds(head, past, stride=N_HEADS_A), :].astype(BF16)
        vc = vc_ref[0, pl.ds(head, past, stride=N_HEADS_A), :].astype(BF16)
        s_c = _qk(q_ref[:, cols], kc)
        for r, start in enumerate(starts):
            off = start // GRID_W - r + (KH_MAX - 1)
            lane0 = (off + off % 2) * GRID_W
            bias = tab_ref[j, off % 2, :, lane0:lane0 + win]
            q = q_ref[r * GRID_W:(r + 1) * GRID_W, cols]
            sw[r * GRID_W:(r + 1) * GRID_W, :] = _qk(q, k_ref[start:start + win, cols]) + bias
        (p_w, p_c), inv = _softmax_parts([sw[...], s_c])
        pw[...] = p_w
        o_c = jnp.dot(p_c, vc, preferred_element_type=F32)
        for r, start in enumerate(starts):
            q_rows = slice(r * GRID_W, (r + 1) * GRID_W)
            o_w = jnp.dot(pw[q_rows, :], v_ref[start:start + win, cols], preferred_element_type=F32)
            o_ref[q_rows, cols] = ((o_w + o_c[q_rows, :]) * inv[q_rows, :]).astype(BF16)


def _lat_attention(proj, cache_k, cache_v, table, seq, *, heads_per_step=4):
    m = proj.shape[0]
    rows = seq // GRID_W
    kh = min(KH_MAX, rows)
    n_hp = N_HEADS_A // heads_per_step
    width = heads_per_step * HEAD_DIM
    qkv = lambda part: pl.BlockSpec((seq, width), lambda b, hp: (b, part * n_hp + hp))
    cache = pl.BlockSpec((1,) + cache_k.shape[1:], lambda b, hp: (b, 0, 0))
    return pl.pallas_call(
        functools.partial(_lat_attn_kernel, rows=rows, kh=kh),
        grid=(m // seq, n_hp),
        in_specs=[qkv(0), qkv(1), qkv(2), cache, cache,
                  pl.BlockSpec((heads_per_step,) + table.shape[1:], lambda b, hp: (hp, 0, 0, 0))],
        out_specs=pl.BlockSpec((seq, width), lambda b, hp: (b, hp)),
        out_shape=jax.ShapeDtypeStruct((m, D_ATTN), BF16),
        scratch_shapes=[pltpu.VMEM((2, seq, kh * GRID_W), F32), pltpu.VMEM((2, seq, kh * GRID_W), BF16)],
        compiler_params=_params(("parallel", "arbitrary"), 48),
        name="lat_attention",
    )(proj, proj, proj, cache_k, cache_v, table)


def _merge_kernel(x_ref, mod_ref, oa_ref, gu_ref, vn_ref, ga_ref, gb_ref,
                  ws_ref, bst_ref, wpa_ref, wpb_ref, wo_ref, g_ref, gffn_ref, o_ref, h2_ref,
                  *rest, row_parts, is_first):
    if is_first:
        w_f32 = (wpa_ref, wpb_ref, wo_ref)
        pubs, w_res, (ob_ref, stage, sems) = rest[:3], rest[3:6], rest[6:]
        wpa_ref, wpb_ref, wo_ref = w_res
        step = pl.program_id(0)
        publish = [pltpu.make_async_copy(res, pub, sems.at[2 + n])
                   for n, (res, pub) in enumerate(zip(w_res, pubs))]

        @pl.when(step == 0)
        def _():
            for src, res, pub in zip(w_f32, w_res, publish):
                _load_cast_weights(src, 0, res, stage, sems)
                pub.start()
    else:
        (ob_ref,) = rest
    for part in _row_parts(x_ref.shape[0], row_parts):
        for c in range(part.start // CHUNK, part.stop // CHUNK):
            rows = slice(c * CHUNK, (c + 1) * CHUNK)
            for g in range(N_GROUPS_B):
                cols = slice(g * GROUP_CH, (g + 1) * GROUP_CH)
                s = jnp.dot(ws_ref[g], vn_ref[rows, cols], preferred_element_type=F32) + bst_ref[:, g:g + 1]
                ob_ref[rows, cols] = (gu_ref[rows, cols].astype(F32) * s).astype(BF16)
        pa = jnp.dot(oa_ref[part, :], wpa_ref[...], preferred_element_type=F32)
        pb = jnp.dot(ob_ref[part, :], wpb_ref[...], preferred_element_type=F32)
        mixed = (ga_ref[part, :].astype(F32) * pa + gb_ref[part, :].astype(F32) * pb).astype(BF16)
        y = jnp.dot(mixed, wo_ref[...], preferred_element_type=F32)
        x1 = x_ref[part, :] + mod_ref[0, 2:3, :] * _rms_norm(y, g_ref[...])
        o_ref[part, :] = x1
        h2 = _rms_norm(x1, gffn_ref[...]) * (1.0 + mod_ref[0, 4:5, :]) + mod_ref[0, 3:4, :]
        h2_ref[part, :] = h2.astype(BF16)
    if is_first:
        @pl.when(step == pl.num_programs(0) - 1)
        def _():
            for pub in publish:
                pub.wait()


def _merge(x, mods, seq, o_a, proj_qkvu, proj_gates, gates_row0, w_s, b_s_t, w_pa, w_pb, w_o, g, g_ffn, *,
           is_first, tm=512, row_parts=2):
    m = x.shape[0]
    assert (tm // row_parts) % CHUNK == 0 and gates_row0 % tm == 0
    g0 = gates_row0 // tm
    whole = lambda a: pl.BlockSpec(a.shape, lambda i: (0,) * a.ndim, pipeline_mode=pl.Buffered(1))
    big = (w_pa, w_pb, w_o)
    out_specs = [pl.BlockSpec((tm, D_MODEL), lambda i: (i, 0))] * 2
    out_shape = [jax.ShapeDtypeStruct((m, D_MODEL), F32), jax.ShapeDtypeStruct((m, D_MODEL), BF16)]
    scratch = [pltpu.VMEM((tm, D_GMLP), BF16)]
    if is_first:
        big_specs = [pl.BlockSpec(memory_space=pl.ANY)] * len(big)
        out_specs += [pl.BlockSpec(memory_space=pl.ANY)] * len(big)
        out_shape += [jax.ShapeDtypeStruct(w.shape, BF16) for w in big]
        assert all(w.shape[1] == D_MODEL for w in big)
        scratch = ([pltpu.VMEM(w.shape, BF16) for w in big] + scratch
                   + [pltpu.VMEM((2, LOAD_ROWS, D_MODEL), F32), pltpu.SemaphoreType.DMA((2 + len(big),))])
    else:
        big_specs = [whole(w) for w in big]
    return pl.pallas_call(
        functools.partial(_merge_kernel, row_parts=row_parts, is_first=is_first),
        grid=(m // tm,),
        in_specs=[
            pl.BlockSpec((tm, D_MODEL), lambda i: (i, 0)),
            pl.BlockSpec((1, N_MOD, D_MODEL), lambda i: (i * tm // seq, 0, 0)),
            pl.BlockSpec((tm, D_ATTN), lambda i: (i, 0)),
            pl.BlockSpec((tm, D_GMLP), lambda i: (i, 3)),
            pl.BlockSpec((tm, D_GMLP), lambda i: (g0 + i, 4)),
            pl.BlockSpec((tm, D_MODEL), lambda i: (g0 + i, 0)),
            pl.BlockSpec((tm, D_MODEL), lambda i: (g0 + i, 1)),
            whole(w_s), whole(b_s_t), *big_specs, whole(g), whole(g_ffn),
        ],
        out_specs=out_specs,
        out_shape=out_shape,
        scratch_shapes=scratch,
        compiler_params=_params(("arbitrary",), 60),
        name="merge_ctx" if is_first else "merge_lat",
    )(x, mods, o_a, proj_qkvu, proj_gates, proj_gates, proj_gates,
      w_s, b_s_t, w_pa, w_pb, w_o, g, g_ffn)


SC_LANES = 16
SC_BLOCK = (32, 512)


def _bf16_bits(u):
    return lax.shift_right_logical(u + 0x7FFF + (lax.shift_right_logical(u, 16) & 1), 16)


def _sc_pack_bf16(w, after):
    rows, cols = w.shape
    blk_r, blk_c = SC_BLOCK
    unroll = 8
    assert rows % blk_r == 0 and cols % blk_c == 0 and blk_c % (unroll * SC_LANES) == 0
    mesh = plsc.VectorSubcoreMesh(core_axis_name="core", subcore_axis_name="subcore")

    def body(in_vmem, out_vmem):
        in_vmem = in_vmem.bitcast(jnp.int32)

        @pl.loop(0, blk_r // 2)
        def _(r):
            @pl.loop(0, blk_c, step=unroll * SC_LANES)
            def _(c0):
                for k in range(unroll):
                    lanes = pl.ds(c0 + k * SC_LANES, SC_LANES)
                    lo = in_vmem[2 * r, lanes]
                    hi = in_vmem[2 * r + 1, lanes]
                    out_vmem[r, lanes] = _bf16_bits(lo) | lax.shift_left(_bf16_bits(hi), 16)

    @pl.kernel(out_type=jax.ShapeDtypeStruct((rows // 2, cols), jnp.int32), mesh=mesh, scratch_types=[],
               compiler_params=pltpu.CompilerParams(use_tc_tiling_on_sc=True))
    def pack_kernel(w_hbm, after_hbm, o_hbm):
        del after_hbm
        pltpu.emit_pipeline(
            body,
            grid=(rows // blk_r, cols // blk_c),
            in_specs=[pl.BlockSpec((blk_r, blk_c), lambda i, j: (i, j))],
            out_specs=[pl.BlockSpec((blk_r // 2, blk_c), lambda i, j: (i, j))],
            core_axis_name=("core", "subcore"),
            dimension_semantics=(pltpu.PARALLEL, pltpu.PARALLEL),
        )(w_hbm, o_hbm)

    return pack_kernel(w, after)


def _ffn_x_block(f, n_f, n_x):
    return jnp.clip(f - (n_f - n_x), 0, n_x - 1)


def _ffn_kernel(x_ref, h_ref, mod_ref, gpost_ref, wg_ref, wu_ref, wd_ref, o_ref, acc_ref, *, n_x, row_parts):
    f = pl.program_id(1)
    x_rows = x_ref.shape[0]
    tm = h_ref.shape[0]

    @pl.when(f == 0)
    def _():
        acc_ref[...] = jnp.zeros_like(acc_ref)

    row0 = pl.multiple_of(_ffn_x_block(f, pl.num_programs(1), n_x) * x_rows, x_rows)
    o_ref[pl.ds(row0, x_rows), :] = x_ref[...]

    for part in range(row_parts):
        rows = slice(part * (tm // row_parts), (part + 1) * (tm // row_parts))
        h = h_ref[rows, :]
        gate = jnp.dot(h, pltpu.bitcast(wg_ref[...], BF16), preferred_element_type=F32)
        up = jnp.dot(h, pltpu.bitcast(wu_ref[...], BF16), preferred_element_type=F32)
        act = (jax.nn.silu(gate) * up).astype(BF16)
        acc_ref[rows, :] += jnp.dot(act, pltpu.bitcast(wd_ref[...], BF16), preferred_element_type=F32)

    @pl.when(f == pl.num_programs(1) - 1)
    def _():
        scale = mod_ref[0, 5:6, :] * gpost_ref[...]

        def residual_rows(c, carry):
            rows = pl.ds(pl.multiple_of(c * SUBLANES, SUBLANES), SUBLANES)
            a = acc_ref[rows, :]
            r = lax.rsqrt(jnp.mean(a * a, axis=-1, keepdims=True) + EPS)
            o_ref[rows, :] = o_ref[rows, :] + (a * r) * scale
            return carry

        lax.fori_loop(0, tm // SUBLANES, residual_rows, 0, unroll=32)


def _ffn(x, h, mods, seq, g_post, w_gate, w_up, w_down, *, tm=1024, tf=512, x_rows=128, row_parts=1):
    m = x.shape[0]
    n_f = D_FF // tf
    n_x = tm // x_rows
    assert n_x <= n_f
    return pl.pallas_call(
        functools.partial(_ffn_kernel, n_x=n_x, row_parts=row_parts),
        grid=(m // tm, n_f),
        in_specs=[
            pl.BlockSpec((x_rows, D_MODEL), lambda i, f: (i * n_x + _ffn_x_block(f, n_f, n_x), 0)),
            pl.BlockSpec((tm, D_MODEL), lambda i, f: (i, 0)),
            pl.BlockSpec((1, N_MOD, D_MODEL), lambda i, f: (i * tm // seq, 0, 0)),
            pl.BlockSpec((1, D_MODEL), lambda i, f: (0, 0)),
            pl.BlockSpec((D_MODEL // 2, tf), lambda i, f: (0, f)),
            pl.BlockSpec((D_MODEL // 2, tf), lambda i, f: (0, f)),
            pl.BlockSpec((tf // 2, D_MODEL), lambda i, f: (f, 0)),
        ],
        out_specs=pl.BlockSpec((tm, D_MODEL), lambda i, f: (i, 0)),
        out_shape=jax.ShapeDtypeStruct((m, D_MODEL), F32),
        scratch_shapes=[pltpu.VMEM((tm, D_MODEL), F32)],
        compiler_params=_params(("parallel", "arbitrary"), 58),
        name="ffn",
    )(x, h, mods, g_post, w_gate, w_up, w_down)


def kernel(x_prompt, x_sample, cache_k, cache_v, c, c_ctx, w_ada, b_ada, norm_mix_pre, norm_mix_post,
           norm_ffn_pre, norm_ffn_post, w_in, rpb, ln_v, w_s, b_s, w_pa, w_pb, w_o, w_gate, w_up, w_down):
    assert w_ada.shape[0] == DEPTH == 1
    batch, seq, _ = x_prompt.shape
    dec_batch, dec_seq, _ = x_sample.shape
    past = cache_k.shape[2]

    row = lambda a: a[0].reshape(1, -1)
    bf = lambda a: a[0].astype(BF16)

    cvecs = jnp.concatenate(
        [c, c_ctx[None], jnp.zeros((MOD_ROWS - dec_batch - 1, D_MODEL), F32)], axis=0)
    mods = _modulation(cvecs, w_ada[0], b_ada[0])
    mods_lat = mods[:dec_batch].reshape(dec_batch, N_MOD, D_MODEL)
    mods_ctx = mods[dec_batch:dec_batch + 1].reshape(1, N_MOD, D_MODEL)
    table = _bias_table(rpb[0])

    w_in_f = w_in.reshape(D_MODEL, D_IN)
    mix_head = (bf(w_s), b_s[0].T)
    mix_tail = (row(norm_mix_post), row(norm_ffn_pre))

    xp = x_prompt.reshape(batch * seq, D_MODEL)
    xs = x_sample.reshape(dec_batch * dec_seq, D_MODEL)
    hp, proj_p, k_p, v_p, w_qkvu = _proj_qkvu(xp, mods_ctx, batch * seq, row(norm_mix_pre), w_in_f, is_ctx=True)
    hs, proj_s = _proj_qkvu(xs, mods_lat, dec_seq, row(norm_mix_pre), w_qkvu, is_ctx=False)
    gates = _proj_gates(hp, hs, row(ln_v), w_in_f)

    wg_p = _sc_pack_bf16(w_gate[0], hp)
    wu_p = _sc_pack_bf16(w_up[0], wg_p)
    wd_p = _sc_pack_bf16(w_down[0], wu_p)
    ffn = (row(norm_ffn_post), wg_p, wu_p, wd_p)

    oa_p = _ctx_attention(proj_p, seq)
    xp, hp, *mix_bf = _merge(xp, mods_ctx, batch * seq, oa_p, proj_p, gates, 0,
                             *mix_head, w_pa[0], w_pb[0], w_o[0], *mix_tail, is_first=True)
    y_prompt = _ffn(xp, hp, mods_ctx, batch * seq, *ffn).reshape(batch, seq, D_MODEL)

    oa_s = _lat_attention(proj_s, cache_k.reshape(dec_batch, past * N_HEADS_A, HEAD_DIM),
                          cache_v.reshape(dec_batch, past * N_HEADS_A, HEAD_DIM), table, dec_seq)
    xs, hs = _merge(xs, mods_lat, dec_seq, oa_s, proj_s, gates, batch * seq,
                    *mix_head, *mix_bf, *mix_tail, is_first=False)
    y_sample = _ffn(xs, hs, mods_lat, dec_seq, *ffn).reshape(dec_batch, dec_seq, D_MODEL)

    state_shape = (batch, DEPTH, seq, N_HEADS_A, HEAD_DIM)
    return y_prompt, y_sample, k_p.reshape(state_shape), v_p.reshape(state_shape)
```

```python
import functools

import jax
import jax.numpy as jnp
from jax import lax
from jax.experimental import pallas as pl
from jax.experimental.pallas import tpu as pltpu
from jax.experimental.pallas import tpu_sc as plsc

D_MODEL = 2048
DEPTH = 1
GRID_W = 64
N_HEADS_A = 8
HEAD_DIM = 128
D_ATTN = N_HEADS_A * HEAD_DIM
KH_MAX = 8
KW = 16
CHUNK = 128
N_GROUPS_B = 8
D_GMLP = 1024
GROUP_CH = D_GMLP // N_GROUPS_B
D_FF = ((8 * D_MODEL // 3 + 255) // 256) * 256
N_MOD = 6
EPS = 1e-6
ATTN_SCALE = HEAD_DIM ** -0.5
LOG2E = 1.4426950408889634
D_IN = 3 * D_ATTN + 2 * D_GMLP + 2 * D_MODEL

N_DR = 2 * KH_MAX - 1
N_DC = 2 * KW - 1
COL_BLOCK = 1024
SUBLANES = 8
MOD_ROWS = SUBLANES

F32 = jnp.float32
BF16 = jnp.bfloat16

MIB = 1024 * 1024


def _params(semantics, vmem_mib):
    return pltpu.CompilerParams(dimension_semantics=semantics, vmem_limit_bytes=vmem_mib * MIB)


def _rms_norm(x, g):
    return x * lax.rsqrt(jnp.mean(x * x, axis=-1, keepdims=True) + EPS) * g


def _layer_norm(x, g):
    xc = x - jnp.mean(x, axis=-1, keepdims=True)
    return xc * lax.rsqrt(jnp.mean(xc * xc, axis=-1, keepdims=True) + EPS) * g


def _modulation_kernel(c_ref, w_ref, b_ref, o_ref):
    s = jax.nn.silu(c_ref[...]).astype(BF16)
    o_ref[...] = jnp.dot(s, w_ref[...].astype(BF16), preferred_element_type=F32) + b_ref[...]


def _modulation(cvecs, w_ada, b_ada):
    tn = 1024
    n = N_MOD * D_MODEL
    return pl.pallas_call(
        _modulation_kernel,
        grid=(n // tn,),
        in_specs=[
            pl.BlockSpec((MOD_ROWS, D_MODEL), lambda j: (0, 0)),
            pl.BlockSpec((D_MODEL, tn), lambda j: (0, j)),
            pl.BlockSpec((1, tn), lambda j: (0, j)),
        ],
        out_specs=pl.BlockSpec((MOD_ROWS, tn), lambda j: (0, j)),
        out_shape=jax.ShapeDtypeStruct((MOD_ROWS, n), F32),
        compiler_params=_params(("parallel",), 40),
        name="modulation",
    )(cvecs, w_ada, b_ada.reshape(1, n))


def _bias_table_kernel(rpb_ref, o_ref):
    qc = lax.broadcasted_iota(jnp.int32, (GRID_W, GRID_W), 0)
    kc = lax.broadcasted_iota(jnp.int32, (GRID_W, GRID_W), 1)
    cs = jnp.clip(qc - KW // 2, 0, GRID_W - KW)
    valid = (kc >= cs) & (kc < cs + KW)
    lanes = rpb_ref.shape[-1]
    tiles = []
    for dr in range(N_DR):
        row = jnp.broadcast_to(rpb_ref[0, dr:dr + 1, :], (GRID_W, lanes))
        t = pltpu.roll(row, lanes - (KW - 1), 1, stride=1, stride_axis=0)[:, :GRID_W]
        tiles.append(jnp.where(valid, t * LOG2E, -jnp.inf))
    pad = jnp.zeros((GRID_W, GRID_W), F32)
    o_ref[0, 0] = jnp.concatenate(tiles + [pad], axis=-1)
    o_ref[0, 1] = jnp.concatenate([pad] + tiles, axis=-1)


def _bias_table(rpb):
    width = (N_DR + 1) * GRID_W
    lanes = 128
    rpb_rows = jnp.pad(rpb, ((0, 0), (0, 0), (0, lanes - N_DC)))
    return pl.pallas_call(
        _bias_table_kernel,
        grid=(N_HEADS_A,),
        in_specs=[pl.BlockSpec((1, N_DR, lanes), lambda h: (h, 0, 0))],
        out_specs=pl.BlockSpec((1, 2, GRID_W, width), lambda h: (h, 0, 0, 0)),
        out_shape=jax.ShapeDtypeStruct((N_HEADS_A, 2, GRID_W, width), F32),
        compiler_params=_params(("parallel",), 16),
        name="bias_table",
    )(rpb_rows)


LOAD_ROWS = 256
N_QKVU = 3 * D_ATTN + D_GMLP


def _row_parts(tm, parts):
    return [slice(p * (tm // parts), (p + 1) * (tm // parts)) for p in range(parts)]


def _load_cast_weights(w_hbm, col0, w_res, stage, sems):
    rows, width = w_res.shape
    chunk = stage.shape[1]
    n_chunks = rows // chunk

    def copy(c):
        src = w_hbm.at[pl.ds(c * chunk, chunk), pl.ds(col0, width)]
        return pltpu.make_async_copy(src, stage.at[c % 2], sems.at[c % 2])

    copy(0).start()
    for c in range(n_chunks):
        if c + 1 < n_chunks:
            copy(c + 1).start()
        copy(c).wait()
        w_res[c * chunk:(c + 1) * chunk, :] = stage[c % 2].astype(BF16)


def _proj_qkvu_kernel(x_ref, mod_ref, g_ref, w_ref, h_ref, proj_ref, *rest, is_ctx):
    if is_ctx:
        k_ref, v_ref, wpub_ref, w_res, stage, sems = rest
        step = pl.program_id(0)
        publish = pltpu.make_async_copy(w_res, wpub_ref, sems.at[2])

        @pl.when(step == 0)
        def _():
            _load_cast_weights(w_ref, 0, w_res, stage, sems)
            publish.start()
    else:
        w_res = w_ref
    y = _rms_norm(x_ref[...], g_ref[...])
    h = (y * (1.0 + mod_ref[0, 1:2, :]) + mod_ref[0, 0:1, :]).astype(BF16)
    h_ref[...] = h
    for blk in (3, 0, 1, 2):
        cols = slice(blk * COL_BLOCK, (blk + 1) * COL_BLOCK)
        acc = jnp.dot(h, w_res[:, cols], preferred_element_type=F32)
        if is_ctx and blk in (1, 2):
            kv_ref = (k_ref, v_ref)[blk - 1]
            for head in range(N_HEADS_A):
                dst = pl.ds(head, x_ref.shape[0], stride=N_HEADS_A)
                kv_ref[dst, :] = acc[:, head * HEAD_DIM:(head + 1) * HEAD_DIM]
        if blk == 3:
            acc = jax.nn.gelu(acc)
        proj_ref[:, cols] = acc.astype(BF16)
    if is_ctx:
        @pl.when(step == pl.num_programs(0) - 1)
        def _():
            publish.wait()


def _proj_qkvu(x, mods, seq, g, w, *, is_ctx, tm=512):
    m = x.shape[0]
    out_shape = [jax.ShapeDtypeStruct((m, D_MODEL), BF16), jax.ShapeDtypeStruct((m, N_QKVU), BF16)]
    out_specs = [pl.BlockSpec((tm, D_MODEL), lambda i: (i, 0)), pl.BlockSpec((tm, N_QKVU), lambda i: (i, 0))]
    scratch = []
    if is_ctx:
        out_shape += [jax.ShapeDtypeStruct((m * N_HEADS_A, HEAD_DIM), F32)] * 2
        out_specs += [pl.BlockSpec((tm * N_HEADS_A, HEAD_DIM), lambda i: (i, 0))] * 2
        out_shape += [jax.ShapeDtypeStruct((D_MODEL, N_QKVU), BF16)]
        out_specs += [pl.BlockSpec(memory_space=pl.ANY)]
        w_spec = pl.BlockSpec(memory_space=pl.ANY)
        scratch = [pltpu.VMEM((D_MODEL, N_QKVU), BF16), pltpu.VMEM((2, LOAD_ROWS, N_QKVU), F32),
                   pltpu.SemaphoreType.DMA((3,))]
    else:
        w_spec = pl.BlockSpec((D_MODEL, N_QKVU), lambda i: (0, 0), pipeline_mode=pl.Buffered(1))
    return pl.pallas_call(
        functools.partial(_proj_qkvu_kernel, is_ctx=is_ctx),
        grid=(m // tm,),
        in_specs=[
            pl.BlockSpec((tm, D_MODEL), lambda i: (i, 0)),
            pl.BlockSpec((1, N_MOD, D_MODEL), lambda i: (i * tm // seq, 0, 0)),
            pl.BlockSpec((1, D_MODEL), lambda i: (0, 0)),
            w_spec,
        ],
        out_specs=out_specs,
        out_shape=out_shape,
        scratch_shapes=scratch,
        compiler_params=_params(("arbitrary",), 58),
        name="proj_qkvu_ctx" if is_ctx else "proj_qkvu_lat",
    )(x, mods, g, w)


def _proj_gates_kernel(hp_ref, hs_ref, lnv_ref, w_ref, proj_ref, w_res, stage, sems, *, n_ctx, row_parts):
    step = pl.program_id(0)

    @pl.when(step == 0)
    def _():
        _load_cast_weights(w_ref, N_QKVU, w_res, stage, sems)

    n_gate = w_res.shape[1] - D_GMLP
    for rows in _row_parts(hp_ref.shape[0], row_parts):
        h = jnp.where(step < n_ctx, hp_ref[rows, :], hs_ref[rows, :])
        acc = jnp.dot(h, w_res[:, :D_GMLP], preferred_element_type=F32)
        proj_ref[rows, n_gate:] = _layer_norm(jax.nn.gelu(acc), lnv_ref[...]).astype(BF16)
        for blk in range(n_gate // COL_BLOCK):
            cols = slice(blk * COL_BLOCK, (blk + 1) * COL_BLOCK)
            acc = jnp.dot(h, w_res[:, D_GMLP + cols.start:D_GMLP + cols.stop], preferred_element_type=F32)
            proj_ref[rows, cols] = jax.nn.sigmoid(acc).astype(BF16)


def _proj_gates(h_ctx, h_lat, ln_v, w_in, *, tm=512, row_parts=2):
    n_ctx, n_lat = h_ctx.shape[0] // tm, h_lat.shape[0] // tm
    n = D_IN - N_QKVU
    return pl.pallas_call(
        functools.partial(_proj_gates_kernel, n_ctx=n_ctx, row_parts=row_parts),
        grid=(n_ctx + n_lat,),
        in_specs=[
            pl.BlockSpec((tm, D_MODEL), lambda i: (jnp.minimum(i, n_ctx - 1), 0)),
            pl.BlockSpec((tm, D_MODEL), lambda i: (jnp.maximum(i - n_ctx, 0), 0)),
            pl.BlockSpec((1, D_GMLP), lambda i: (0, 0)),
            pl.BlockSpec(memory_space=pl.ANY),
        ],
        out_specs=pl.BlockSpec((tm, n), lambda i: (i, 0)),
        out_shape=jax.ShapeDtypeStruct(((n_ctx + n_lat) * tm, n), BF16),
        scratch_shapes=[pltpu.VMEM((D_MODEL, n), BF16), pltpu.VMEM((2, LOAD_ROWS, n), F32),
                        pltpu.SemaphoreType.DMA((2,))],
        compiler_params=_params(("arbitrary",), 56),
        name="proj_gates",
    )(h_ctx, h_lat, ln_v, w_in)


def _softmax_parts(logits2):
    m = functools.reduce(jnp.maximum, [jnp.max(t, axis=-1, keepdims=True) for t in logits2])
    es = [jnp.exp2(t - m) for t in logits2]
    inv = 1.0 / functools.reduce(jnp.add, [jnp.sum(e, axis=-1, keepdims=True) for e in es])
    return [e.astype(BF16) for e in es], inv


def _qk(q, k):
    return lax.dot_general(q, k, (((1,), (1,)), ((), ())), preferred_element_type=F32) * (ATTN_SCALE * LOG2E)


def _ctx_attn_kernel(q_ref, k_ref, v_ref, o_ref):
    for h in range(N_HEADS_A):
        cols = slice(h * HEAD_DIM, (h + 1) * HEAD_DIM)
        (p,), inv = _softmax_parts([_qk(q_ref[:, cols], k_ref[:, cols])])
        o_ref[:, cols] = (jnp.dot(p, v_ref[:, cols], preferred_element_type=F32) * inv).astype(BF16)


def _ctx_attention(proj, seq):
    m = proj.shape[0]
    spec = lambda col: pl.BlockSpec((seq, D_ATTN), lambda b: (b, col))
    return pl.pallas_call(
        _ctx_attn_kernel,
        grid=(m // seq,),
        in_specs=[spec(0), spec(1), spec(2)],
        out_specs=spec(0),
        out_shape=jax.ShapeDtypeStruct((m, D_ATTN), BF16),
        compiler_params=_params(("parallel",), 32),
        name="ctx_attention",
    )(proj, proj, proj)


def _lat_attn_kernel(q_ref, k_ref, v_ref, kc_ref, vc_ref, tab_ref, o_ref, sw_ref, pw_ref, *, rows, kh):
    heads = tab_ref.shape[0]
    past = kc_ref.shape[1] // N_HEADS_A
    win = kh * GRID_W
    starts = [min(max(r - kh // 2, 0), rows - kh) * GRID_W for r in range(rows)]
    for j in range(heads):
        head = pl.program_id(1) * heads + j
        cols = slice(j * HEAD_DIM, (j + 1) * HEAD_DIM)
        sw, pw = sw_ref.at[j % 2], pw_ref.at[j % 2]
        kc = kc_ref[0, pl.ds(head, past, stride=N_HEADS_A), :].astype(BF16)
        vc = vc_ref[0, pl.ds(head, past, stride=N_HEADS_A), :].astype(BF16)
        s_c = _qk(q_ref[:, cols], kc)
        for r, start in enumerate(starts):
            off = start // GRID_W - r + (KH_MAX - 1)
            lane0 = (off + off % 2) * GRID_W
            bias = tab_ref[j, off % 2, :, lane0:lane0 + win]
            q = q_ref[r * GRID_W:(r + 1) * GRID_W, cols]
            sw[r * GRID_W:(r + 1) * GRID_W, :] = _qk(q, k_ref[start:start + win, cols]) + bias
        (p_w, p_c), inv = _softmax_parts([sw[...], s_c])
        pw[...] = p_w
        o_c = jnp.dot(p_c, vc, preferred_element_type=F32)
        for r, start in enumerate(starts):
            q_rows = slice(r * GRID_W, (r + 1) * GRID_W)
            o_w = jnp.dot(pw[q_rows, :], v_ref[start:start + win, cols], preferred_element_type=F32)
            o_ref[q_rows, cols] = ((o_w + o_c[q_rows, :]) * inv[q_rows, :]).astype(BF16)


def _lat_attention(proj, cache_k, cache_v, table, seq, *, heads_per_step=4):
    m = proj.shape[0]
    rows = seq // GRID_W
    kh = min(KH_MAX, rows)
    n_hp = N_HEADS_A // heads_per_step
    width = heads_per_step * HEAD_DIM
    qkv = lambda part: pl.BlockSpec((seq, width), lambda b, hp: (b, part * n_hp + hp))
    cache = pl.BlockSpec((1,) + cache_k.shape[1:], lambda b, hp: (b, 0, 0))
    return pl.pallas_call(
        functools.partial(_lat_attn_kernel, rows=rows, kh=kh),
        grid=(m // seq, n_hp),
        in_specs=[qkv(0), qkv(1), qkv(2), cache, cache,
                  pl.BlockSpec((heads_per_step,) + table.shape[1:], lambda b, hp: (hp, 0, 0, 0))],
        out_specs=pl.BlockSpec((seq, width), lambda b, hp: (b, hp)),
        out_shape=jax.ShapeDtypeStruct((m, D_ATTN), BF16),
        scratch_shapes=[pltpu.VMEM((2, seq, kh * GRID_W), F32), pltpu.VMEM((2, seq, kh * GRID_W), BF16)],
        compiler_params=_params(("parallel", "arbitrary"), 48),
        name="lat_attention",
    )(proj, proj, proj, cache_k, cache_v, table)


def _merge_kernel(x_ref, mod_ref, oa_ref, gu_ref, vn_ref, ga_ref, gb_ref,
                  ws_ref, bst_ref, wpa_ref, wpb_ref, wo_ref, g_ref, gffn_ref, o_ref, h2_ref,
                  *rest, row_parts, is_first):
    if is_first:
        w_f32 = (wpa_ref, wpb_ref, wo_ref)
        pubs, w_res, (ob_ref, stage, sems) = rest[:3], rest[3:6], rest[6:]
        wpa_ref, wpb_ref, wo_ref = w_res
        step = pl.program_id(0)
        publish = [pltpu.make_async_copy(res, pub, sems.at[2 + n])
                   for n, (res, pub) in enumerate(zip(w_res, pubs))]

        @pl.when(step == 0)
        def _():
            for src, res, pub in zip(w_f32, w_res, publish):
                _load_cast_weights(src, 0, res, stage, sems)
                pub.start()
    else:
        (ob_ref,) = rest
    for part in _row_parts(x_ref.shape[0], row_parts):
        for c in range(part.start // CHUNK, part.stop // CHUNK):
            rows = slice(c * CHUNK, (c + 1) * CHUNK)
            for g in range(N_GROUPS_B):
                cols = slice(g * GROUP_CH, (g + 1) * GROUP_CH)
                s = jnp.dot(ws_ref[g], vn_ref[rows, cols], preferred_element_type=F32) + bst_ref[:, g:g + 1]
                ob_ref[rows, cols] = (gu_ref[rows, cols].astype(F32) * s).astype(BF16)
        pa = jnp.dot(oa_ref[part, :], wpa_ref[...], preferred_element_type=F32)
        pb = jnp.dot(ob_ref[part, :], wpb_ref[...], preferred_element_type=F32)
        mixed = (ga_ref[part, :].astype(F32) * pa + gb_ref[part, :].astype(F32) * pb).astype(BF16)
        y = jnp.dot(mixed, wo_ref[...], preferred_element_type=F32)
        x1 = x_ref[part, :] + mod_ref[0, 2:3, :] * _rms_norm(y, g_ref[...])
        o_ref[part, :] = x1
        h2 = _rms_norm(x1, gffn_ref[...]) * (1.0 + mod_ref[0, 4:5, :]) + mod_ref[0, 3:4, :]
        h2_ref[part, :] = h2.astype(BF16)
    if is_first:
        @pl.when(step == pl.num_programs(0) - 1)
        def _():
            for pub in publish:
                pub.wait()


def _merge(x, mods, seq, o_a, proj_qkvu, proj_gates, gates_row0, w_s, b_s_t, w_pa, w_pb, w_o, g, g_ffn, *,
           is_first, tm=512, row_parts=2):
    m = x.shape[0]
    assert (tm // row_parts) % CHUNK == 0 and gates_row0 % tm == 0
    g0 = gates_row0 // tm
    whole = lambda a: pl.BlockSpec(a.shape, lambda i: (0,) * a.ndim, pipeline_mode=pl.Buffered(1))
    big = (w_pa, w_pb, w_o)
    out_specs = [pl.BlockSpec((tm, D_MODEL), lambda i: (i, 0))] * 2
    out_shape = [jax.ShapeDtypeStruct((m, D_MODEL), F32), jax.ShapeDtypeStruct((m, D_MODEL), BF16)]
    scratch = [pltpu.VMEM((tm, D_GMLP), BF16)]
    if is_first:
        big_specs = [pl.BlockSpec(memory_space=pl.ANY)] * len(big)
        out_specs += [pl.BlockSpec(memory_space=pl.ANY)] * len(big)
        out_shape += [jax.ShapeDtypeStruct(w.shape, BF16) for w in big]
        assert all(w.shape[1] == D_MODEL for w in big)
        scratch = ([pltpu.VMEM(w.shape, BF16) for w in big] + scratch
                   + [pltpu.VMEM((2, LOAD_ROWS, D_MODEL), F32), pltpu.SemaphoreType.DMA((2 + len(big),))])
    else:
        big_specs = [whole(w) for w in big]
    return pl.pallas_call(
        functools.partial(_merge_kernel, row_parts=row_parts, is_first=is_first),
        grid=(m // tm,),
        in_specs=[
            pl.BlockSpec((tm, D_MODEL), lambda i: (i, 0)),
            pl.BlockSpec((1, N_MOD, D_MODEL), lambda i: (i * tm // seq, 0, 0)),
            pl.BlockSpec((tm, D_ATTN), lambda i: (i, 0)),
            pl.BlockSpec((tm, D_GMLP), lambda i: (i, 3)),
            pl.BlockSpec((tm, D_GMLP), lambda i: (g0 + i, 4)),
            pl.BlockSpec((tm, D_MODEL), lambda i: (g0 + i, 0)),
            pl.BlockSpec((tm, D_MODEL), lambda i: (g0 + i, 1)),
            whole(w_s), whole(b_s_t), *big_specs, whole(g), whole(g_ffn),
        ],
        out_specs=out_specs,
        out_shape=out_shape,
        scratch_shapes=scratch,
        compiler_params=_params(("arbitrary",), 60),
        name="merge_ctx" if is_first else "merge_lat",
    )(x, mods, o_a, proj_qkvu, proj_gates, proj_gates, proj_gates,
      w_s, b_s_t, w_pa, w_pb, w_o, g, g_ffn)


SC_LANES = 16
SC_BLOCK = (32, 512)


def _bf16_bits(u):
    return lax.shift_right_logical(u + 0x7FFF + (lax.shift_right_logical(u, 16) & 1), 16)


def _sc_pack_bf16(w, after):
    rows, cols = w.shape
    blk_r, blk_c = SC_BLOCK
    unroll = 8
    assert rows % blk_r == 0 and cols % blk_c == 0 and blk_c % (unroll * SC_LANES) == 0
    mesh = plsc.VectorSubcoreMesh(core_axis_name="core", subcore_axis_name="subcore")

    def body(in_vmem, out_vmem):
        in_vmem = in_vmem.bitcast(jnp.int32)

        @pl.loop(0, blk_r // 2)
        def _(r):
            @pl.loop(0, blk_c, step=unroll * SC_LANES)
            def _(c0):
                for k in range(unroll):
                    lanes = pl.ds(c0 + k * SC_LANES, SC_LANES)
                    lo = in_vmem[2 * r, lanes]
                    hi = in_vmem[2 * r + 1, lanes]
                    out_vmem[r, lanes] = _bf16_bits(lo) | lax.shift_left(_bf16_bits(hi), 16)

    @pl.kernel(out_type=jax.ShapeDtypeStruct((rows // 2, cols), jnp.int32), mesh=mesh, scratch_types=[],
               compiler_params=pltpu.CompilerParams(use_tc_tiling_on_sc=True))
    def pack_kernel(w_hbm, after_hbm, o_hbm):
        del after_hbm
        pltpu.emit_pipeline(
            body,
            grid=(rows // blk_r, cols // blk_c),
            in_specs=[pl.BlockSpec((blk_r, blk_c), lambda i, j: (i, j))],
            out_specs=[pl.BlockSpec((blk_r // 2, blk_c), lambda i, j: (i, j))],
            core_axis_name=("core", "subcore"),
            dimension_semantics=(pltpu.PARALLEL, pltpu.PARALLEL),
        )(w_hbm, o_hbm)

    return pack_kernel(w, after)


def _ffn_x_block(f, n_f, n_x):
    return jnp.clip(f - (n_f - n_x), 0, n_x - 1)


def _ffn_kernel(x_ref, h_ref, mod_ref, gpost_ref, wg_ref, wu_ref, wd_ref, o_ref, acc_ref, *, n_x, row_parts):
    f = pl.program_id(1)
    x_rows = x_ref.shape[0]
    tm = h_ref.shape[0]

    @pl.when(f == 0)
    def _():
        acc_ref[...] = jnp.zeros_like(acc_ref)

    row0 = pl.multiple_of(_ffn_x_block(f, pl.num_programs(1), n_x) * x_rows, x_rows)
    o_ref[pl.ds(row0, x_rows), :] = x_ref[...]

    for part in range(row_parts):
        rows = slice(part * (tm // row_parts), (part + 1) * (tm // row_parts))
        h = h_ref[rows, :]
        gate = jnp.dot(h, pltpu.bitcast(wg_ref[...], BF16), preferred_element_type=F32)
        up = jnp.dot(h, pltpu.bitcast(wu_ref[...], BF16), preferred_element_type=F32)
        act = (jax.nn.silu(gate) * up).astype(BF16)
        acc_ref[rows, :] += jnp.dot(act, pltpu.bitcast(wd_ref[...], BF16), preferred_element_type=F32)

    @pl.when(f == pl.num_programs(1) - 1)
    def _():
        scale = mod_ref[0, 5:6, :] * gpost_ref[...]

        def residual_rows(c, carry):
            rows = pl.ds(pl.multiple_of(c * SUBLANES, SUBLANES), SUBLANES)
            a = acc_ref[rows, :]
            r = lax.rsqrt(jnp.mean(a * a, axis=-1, keepdims=True) + EPS)
            o_ref[rows, :] = o_ref[rows, :] + (a * r) * scale
            return carry

        lax.fori_loop(0, tm // SUBLANES, residual_rows, 0, unroll=32)


def _ffn(x, h, mods, seq, g_post, w_gate, w_up, w_down, *, tm=1024, tf=512, x_rows=128, row_parts=1):
    m = x.shape[0]
    n_f = D_FF // tf
    n_x = tm // x_rows
    assert n_x <= n_f
    return pl.pallas_call(
        functools.partial(_ffn_kernel, n_x=n_x, row_parts=row_parts),
        grid=(m // tm, n_f),
        in_specs=[
            pl.BlockSpec((x_rows, D_MODEL), lambda i, f: (i * n_x + _ffn_x_block(f, n_f, n_x), 0)),
            pl.BlockSpec((tm, D_MODEL), lambda i, f: (i, 0)),
            pl.BlockSpec((1, N_MOD, D_MODEL), lambda i, f: (i * tm // seq, 0, 0)),
            pl.BlockSpec((1, D_MODEL), lambda i, f: (0, 0)),
            pl.BlockSpec((D_MODEL // 2, tf), lambda i, f: (0, f)),
            pl.BlockSpec((D_MODEL // 2, tf), lambda i, f: (0, f)),
            pl.BlockSpec((tf // 2, D_MODEL), lambda i, f: (f, 0)),
        ],
        out_specs=pl.BlockSpec((tm, D_MODEL), lambda i, f: (i, 0)),
        out_shape=jax.ShapeDtypeStruct((m, D_MODEL), F32),
        scratch_shapes=[pltpu.VMEM((tm, D_MODEL), F32)],
        compiler_params=_params(("parallel", "arbitrary"), 58),
        name="ffn",
    )(x, h, mods, g_post, w_gate, w_up, w_down)


def kernel(x_prompt, x_sample, cache_k, cache_v, c, c_ctx, w_ada, b_ada, norm_mix_pre, norm_mix_post,
           norm_ffn_pre, norm_ffn_post, w_in, rpb, ln_v, w_s, b_s, w_pa, w_pb, w_o, w_gate, w_up, w_down):
    assert w_ada.shape[0] == DEPTH == 1
    batch, seq, _ = x_prompt.shape
    dec_batch, dec_seq, _ = x_sample.shape
    past = cache_k.shape[2]

    row = lambda a: a[0].reshape(1, -1)
    bf = lambda a: a[0].astype(BF16)

    cvecs = jnp.concatenate(
        [c, c_ctx[None], jnp.zeros((MOD_ROWS - dec_batch - 1, D_MODEL), F32)], axis=0)
    mods = _modulation(cvecs, w_ada[0], b_ada[0])
    mods_lat = mods[:dec_batch].reshape(dec_batch, N_MOD, D_MODEL)
    mods_ctx = mods[dec_batch:dec_batch + 1].reshape(1, N_MOD, D_MODEL)
    table = _bias_table(rpb[0])

    w_in_f = w_in.reshape(D_MODEL, D_IN)
    mix_head = (bf(w_s), b_s[0].T)
    mix_tail = (row(norm_mix_post), row(norm_ffn_pre))

    xp = x_prompt.reshape(batch * seq, D_MODEL)
    xs = x_sample.reshape(dec_batch * dec_seq, D_MODEL)
    hp, proj_p, k_p, v_p, w_qkvu = _proj_qkvu(xp, mods_ctx, batch * seq, row(norm_mix_pre), w_in_f, is_ctx=True)
    hs, proj_s = _proj_qkvu(xs, mods_lat, dec_seq, row(norm_mix_pre), w_qkvu, is_ctx=False)
    gates = _proj_gates(hp, hs, row(ln_v), w_in_f)

    ffn = (row(norm_ffn_post), *[_sc_pack_bf16(w[0], hp) for w in (w_gate, w_up, w_down)])

    oa_p = _ctx_attention(proj_p, seq)
    xp, hp, *mix_bf = _merge(xp, mods_ctx, batch * seq, oa_p, proj_p, gates, 0,
                             *mix_head, w_pa[0], w_pb[0], w_o[0], *mix_tail, is_first=True)
    y_prompt = _ffn(xp, hp, mods_ctx, batch * seq, *ffn).reshape(batch, seq, D_MODEL)

    oa_s = _lat_attention(proj_s, cache_k.reshape(dec_batch, past * N_HEADS_A, HEAD_DIM),
                          cache_v.reshape(dec_batch, past * N_HEADS_A, HEAD_DIM), table, dec_seq)
    xs, hs = _merge(xs, mods_lat, dec_seq, oa_s, proj_s, gates, batch * seq,
                    *mix_head, *mix_bf, *mix_tail, is_first=False)
    y_sample = _ffn(xs, hs, mods_lat, dec_seq, *ffn).reshape(dec_batch, dec_seq, D_MODEL)

    state_shape = (batch, DEPTH, seq, N_HEADS_A, HEAD_DIM)
    return y_prompt, y_sample, k_p.reshape(state_shape), v_p.reshape(state_shape)
```

```python
import functools

import jax
import jax.numpy as jnp
from jax import lax
from jax.experimental import pallas as pl
from jax.experimental.pallas import tpu as pltpu
from jax.experimental.pallas import tpu_sc as plsc

D_MODEL = 2048
DEPTH = 1
GRID_W = 64
N_HEADS_A = 8
HEAD_DIM = 128
D_ATTN = N_HEADS_A * HEAD_DIM
KH_MAX = 8
KW = 16
CHUNK = 128
N_GROUPS_B = 8
D_GMLP = 1024
GROUP_CH = D_GMLP // N_GROUPS_B
D_FF = ((8 * D_MODEL // 3 + 255) // 256) * 256
N_MOD = 6
EPS = 1e-6
ATTN_SCALE = HEAD_DIM ** -0.5
LOG2E = 1.4426950408889634
D_IN = 3 * D_ATTN + 2 * D_GMLP + 2 * D_MODEL

N_DR = 2 * KH_MAX - 1
N_DC = 2 * KW - 1
COL_BLOCK = 1024
SUBLANES = 8
MOD_ROWS = SUBLANES

F32 = jnp.float32
BF16 = jnp.bfloat16

MIB = 1024 * 1024


def _params(semantics, vmem_mib):
    return pltpu.CompilerParams(dimension_semantics=semantics, vmem_limit_bytes=vmem_mib * MIB)


def _rms_norm(x, g):
    return x * lax.rsqrt(jnp.mean(x * x, axis=-1, keepdims=True) + EPS) * g


def _layer_norm(x, g):
    xc = x - jnp.mean(x, axis=-1, keepdims=True)
    return xc * lax.rsqrt(jnp.mean(xc * xc, axis=-1, keepdims=True) + EPS) * g


def _modulation_kernel(c_ref, w_ref, b_ref, o_ref):
    s = jax.nn.silu(c_ref[...]).astype(BF16)
    o_ref[...] = jnp.dot(s, w_ref[...].astype(BF16), preferred_element_type=F32) + b_ref[...]


def _modulation(cvecs, w_ada, b_ada):
    tn = 1024
    n = N_MOD * D_MODEL
    return pl.pallas_call(
        _modulation_kernel,
        grid=(n // tn,),
        in_specs=[
            pl.BlockSpec((MOD_ROWS, D_MODEL), lambda j: (0, 0)),
            pl.BlockSpec((D_MODEL, tn), lambda j: (0, j)),
            pl.BlockSpec((1, tn), lambda j: (0, j)),
        ],
        out_specs=pl.BlockSpec((MOD_ROWS, tn), lambda j: (0, j)),
        out_shape=jax.ShapeDtypeStruct((MOD_ROWS, n), F32),
        compiler_params=_params(("parallel",), 40),
        name="modulation",
    )(cvecs, w_ada, b_ada.reshape(1, n))


def _bias_table_kernel(rpb_ref, o_ref):
    qc = lax.broadcasted_iota(jnp.int32, (GRID_W, GRID_W), 0)
    kc = lax.broadcasted_iota(jnp.int32, (GRID_W, GRID_W), 1)
    cs = jnp.clip(qc - KW // 2, 0, GRID_W - KW)
    valid = (kc >= cs) & (kc < cs + KW)
    lanes = rpb_ref.shape[-1]
    tiles = []
    for dr in range(N_DR):
        row = jnp.broadcast_to(rpb_ref[0, dr:dr + 1, :], (GRID_W, lanes))
        t = pltpu.roll(row, lanes - (KW - 1), 1, stride=1, stride_axis=0)[:, :GRID_W]
        tiles.append(jnp.where(valid, t * LOG2E, -jnp.inf))
    pad = jnp.zeros((GRID_W, GRID_W), F32)
    o_ref[0, 0] = jnp.concatenate(tiles + [pad], axis=-1)
    o_ref[0, 1] = jnp.concatenate([pad] + tiles, axis=-1)


def _bias_table(rpb):
    width = (N_DR + 1) * GRID_W
    lanes = 128
    rpb_rows = jnp.pad(rpb, ((0, 0), (0, 0), (0, lanes - N_DC)))
    return pl.pallas_call(
        _bias_table_kernel,
        grid=(N_HEADS_A,),
        in_specs=[pl.BlockSpec((1, N_DR, lanes), lambda h: (h, 0, 0))],
        out_specs=pl.BlockSpec((1, 2, GRID_W, width), lambda h: (h, 0, 0, 0)),
        out_shape=jax.ShapeDtypeStruct((N_HEADS_A, 2, GRID_W, width), F32),
        compiler_params=_params(("parallel",), 16),
        name="bias_table",
    )(rpb_rows)


LOAD_ROWS = 256
N_QKVU = 3 * D_ATTN + D_GMLP


def _row_parts(tm, parts):
    return [slice(p * (tm // parts), (p + 1) * (tm // parts)) for p in range(parts)]


def _load_cast_weights(w_hbm, col0, w_res, stage, sems):
    rows, width = w_res.shape
    chunk = stage.shape[1]
    n_chunks = rows // chunk

    def copy(c):
        src = w_hbm.at[pl.ds(c * chunk, chunk), pl.ds(col0, width)]
        return pltpu.make_async_copy(src, stage.at[c % 2], sems.at[c % 2])

    copy(0).start()
    for c in range(n_chunks):
        if c + 1 < n_chunks:
            copy(c + 1).start()
        copy(c).wait()
        w_res[c * chunk:(c + 1) * chunk, :] = stage[c % 2].astype(BF16)


def _proj_qkvu_kernel(x_ref, mod_ref, g_ref, w_ref, h_ref, proj_ref, *rest, is_ctx):
    if is_ctx:
        k_ref, v_ref, wpub_ref, w_res, stage, sems = rest
        step = pl.program_id(0)
        publish = pltpu.make_async_copy(w_res, wpub_ref, sems.at[2])

        @pl.when(step == 0)
        def _():
            _load_cast_weights(w_ref, 0, w_res, stage, sems)
            publish.start()
    else:
        w_res = w_ref
    y = _rms_norm(x_ref[...], g_ref[...])
    h = (y * (1.0 + mod_ref[0, 1:2, :]) + mod_ref[0, 0:1, :]).astype(BF16)
    h_ref[...] = h
    for blk in (3, 0, 1, 2):
        cols = slice(blk * COL_BLOCK, (blk + 1) * COL_BLOCK)
        acc = jnp.dot(h, w_res[:, cols], preferred_element_type=F32)
        if is_ctx and blk in (1, 2):
            kv_ref = (k_ref, v_ref)[blk - 1]
            for head in range(N_HEADS_A):
                dst = pl.ds(head, x_ref.shape[0], stride=N_HEADS_A)
                kv_ref[dst, :] = acc[:, head * HEAD_DIM:(head + 1) * HEAD_DIM]
        if blk == 3:
            acc = jax.nn.gelu(acc)
        proj_ref[:, cols] = acc.astype(BF16)
    if is_ctx:
        @pl.when(step == pl.num_programs(0) - 1)
        def _():
            publish.wait()


def _proj_qkvu(x, mods, seq, g, w, *, is_ctx, tm=512):
    m = x.shape[0]
    out_shape = [jax.ShapeDtypeStruct((m, D_MODEL), BF16), jax.ShapeDtypeStruct((m, N_QKVU), BF16)]
    out_specs = [pl.BlockSpec((tm, D_MODEL), lambda i: (i, 0)), pl.BlockSpec((tm, N_QKVU), lambda i: (i, 0))]
    scratch = []
    if is_ctx:
        out_shape += [jax.ShapeDtypeStruct((m * N_HEADS_A, HEAD_DIM), F32)] * 2
        out_specs += [pl.BlockSpec((tm * N_HEADS_A, HEAD_DIM), lambda i: (i, 0))] * 2
        out_shape += [jax.ShapeDtypeStruct((D_MODEL, N_QKVU), BF16)]
        out_specs += [pl.BlockSpec(memory_space=pl.ANY)]
        w_spec = pl.BlockSpec(memory_space=pl.ANY)
        scratch = [pltpu.VMEM((D_MODEL, N_QKVU), BF16), pltpu.VMEM((2, LOAD_ROWS, N_QKVU), F32),
                   pltpu.SemaphoreType.DMA((3,))]
    else:
        w_spec = pl.BlockSpec((D_MODEL, N_QKVU), lambda i: (0, 0), pipeline_mode=pl.Buffered(1))
    return pl.pallas_call(
        functools.partial(_proj_qkvu_kernel, is_ctx=is_ctx),
        grid=(m // tm,),
        in_specs=[
            pl.BlockSpec((tm, D_MODEL), lambda i: (i, 0)),
            pl.BlockSpec((1, N_MOD, D_MODEL), lambda i: (i * tm // seq, 0, 0)),
            pl.BlockSpec((1, D_MODEL), lambda i: (0, 0)),
            w_spec,
        ],
        out_specs=out_specs,
        out_shape=out_shape,
        scratch_shapes=scratch,
        compiler_params=_params(("arbitrary",), 58),
        name="proj_qkvu_ctx" if is_ctx else "proj_qkvu_lat",
    )(x, mods, g, w)


def _proj_gates_kernel(hp_ref, hs_ref, lnv_ref, w_ref, proj_ref, w_res, stage, sems, *, n_ctx, row_parts):
    step = pl.program_id(0)

    @pl.when(step == 0)
    def _():
        _load_cast_weights(w_ref, N_QKVU, w_res, stage, sems)

    n_gate = w_res.shape[1] - D_GMLP
    for rows in _row_parts(hp_ref.shape[0], row_parts):
        h = jnp.where(step < n_ctx, hp_ref[rows, :], hs_ref[rows, :])
        acc = jnp.dot(h, w_res[:, :D_GMLP], preferred_element_type=F32)
        proj_ref[rows, n_gate:] = _layer_norm(jax.nn.gelu(acc), lnv_ref[...]).astype(BF16)
        for blk in range(n_gate // COL_BLOCK):
            cols = slice(blk * COL_BLOCK, (blk + 1) * COL_BLOCK)
            acc = jnp.dot(h, w_res[:, D_GMLP + cols.start:D_GMLP + cols.stop], preferred_element_type=F32)
            proj_ref[rows, cols] = jax.nn.sigmoid(acc).astype(BF16)


def _proj_gates(h_ctx, h_lat, ln_v, w_in, *, tm=512, row_parts=2):
    n_ctx, n_lat = h_ctx.shape[0] // tm, h_lat.shape[0] // tm
    n = D_IN - N_QKVU
    return pl.pallas_call(
        functools.partial(_proj_gates_kernel, n_ctx=n_ctx, row_parts=row_parts),
        grid=(n_ctx + n_lat,),
        in_specs=[
            pl.BlockSpec((tm, D_MODEL), lambda i: (jnp.minimum(i, n_ctx - 1), 0)),
            pl.BlockSpec((tm, D_MODEL), lambda i: (jnp.maximum(i - n_ctx, 0), 0)),
            pl.BlockSpec((1, D_GMLP), lambda i: (0, 0)),
            pl.BlockSpec(memory_space=pl.ANY),
        ],
        out_specs=pl.BlockSpec((tm, n), lambda i: (i, 0)),
        out_shape=jax.ShapeDtypeStruct(((n_ctx + n_lat) * tm, n), BF16),
        scratch_shapes=[pltpu.VMEM((D_MODEL, n), BF16), pltpu.VMEM((2, LOAD_ROWS, n), F32),
                        pltpu.SemaphoreType.DMA((2,))],
        compiler_params=_params(("arbitrary",), 56),
        name="proj_gates",
    )(h_ctx, h_lat, ln_v, w_in)


def _softmax_parts(logits2):
    m = functools.reduce(jnp.maximum, [jnp.max(t, axis=-1, keepdims=True) for t in logits2])
    es = [jnp.exp2(t - m) for t in logits2]
    inv = 1.0 / functools.reduce(jnp.add, [jnp.sum(e, axis=-1, keepdims=True) for e in es])
    return [e.astype(BF16) for e in es], inv


def _qk(q, k):
    return lax.dot_general(q, k, (((1,), (1,)), ((), ())), preferred_element_type=F32) * (ATTN_SCALE * LOG2E)


def _ctx_attn_kernel(q_ref, k_ref, v_ref, o_ref):
    for h in range(N_HEADS_A):
        cols = slice(h * HEAD_DIM, (h + 1) * HEAD_DIM)
        (p,), inv = _softmax_parts([_qk(q_ref[:, cols], k_ref[:, cols])])
        o_ref[:, cols] = (jnp.dot(p, v_ref[:, cols], preferred_element_type=F32) * inv).astype(BF16)


def _ctx_attention(proj, seq):
    m = proj.shape[0]
    spec = lambda col: pl.BlockSpec((seq, D_ATTN), lambda b: (b, col))
    return pl.pallas_call(
        _ctx_attn_kernel,
        grid=(m // seq,),
        in_specs=[spec(0), spec(1), spec(2)],
        out_specs=spec(0),
        out_shape=jax.ShapeDtypeStruct((m, D_ATTN), BF16),
        compiler_params=_params(("parallel",), 32),
        name="ctx_attention",
    )(proj, proj, proj)


def _lat_attn_kernel(q_ref, k_ref, v_ref, kc_ref, vc_ref, tab_ref, o_ref, sw_ref, pw_ref, *, rows, kh):
    heads = tab_ref.shape[0]
    past = kc_ref.shape[1] // N_HEADS_A
    win = kh * GRID_W
    starts = [min(max(r - kh // 2, 0), rows - kh) * GRID_W for r in range(rows)]
    for j in range(heads):
        head = pl.program_id(1) * heads + j
        cols = slice(j * HEAD_DIM, (j + 1) * HEAD_DIM)
        sw, pw = sw_ref.at[j % 2], pw_ref.at[j % 2]
        kc = kc_ref[0, pl.ds(head, past, stride=N_HEADS_A), :].astype(BF16)
        vc = vc_ref[0, pl.ds(head, past, stride=N_HEADS_A), :].astype(BF16)
        s_c = _qk(q_ref[:, cols], kc)
        for r, start in enumerate(starts):
            off = start // GRID_W - r + (KH_MAX - 1)
            lane0 = (off + off % 2) * GRID_W
            bias = tab_ref[j, off % 2, :, lane0:lane0 + win]
            q = q_ref[r * GRID_W:(r + 1) * GRID_W, cols]
            sw[r * GRID_W:(r + 1) * GRID_W, :] = _qk(q, k_ref[start:start + win, cols]) + bias
        (p_w, p_c), inv = _softmax_parts([sw[...], s_c])
        pw[...] = p_w
        o_c = jnp.dot(p_c, vc, preferred_element_type=F32)
        for r, start in enumerate(starts):
            q_rows = slice(r * GRID_W, (r + 1) * GRID_W)
            o_w = jnp.dot(pw[q_rows, :], v_ref[start:start + win, cols], preferred_element_type=F32)
            o_ref[q_rows, cols] = ((o_w + o_c[q_rows, :]) * inv[q_rows, :]).astype(BF16)


def _lat_attention(proj, cache_k, cache_v, table, seq, *, heads_per_step=4):
    m = proj.shape[0]
    rows = seq // GRID_W
    kh = min(KH_MAX, rows)
    n_hp = N_HEADS_A // heads_per_step
    width = heads_per_step * HEAD_DIM
    qkv = lambda part: pl.BlockSpec((seq, width), lambda b, hp: (b, part * n_hp + hp))
    cache = pl.BlockSpec((1,) + cache_k.shape[1:], lambda b, hp: (b, 0, 0))
    return pl.pallas_call(
        functools.partial(_lat_attn_kernel, rows=rows, kh=kh),
        grid=(m // seq, n_hp),
        in_specs=[qkv(0), qkv(1), qkv(2), cache, cache,
                  pl.BlockSpec((heads_per_step,) + table.shape[1:], lambda b, hp: (hp, 0, 0, 0))],
        out_specs=pl.BlockSpec((seq, width), lambda b, hp: (b, hp)),
        out_shape=jax.ShapeDtypeStruct((m, D_ATTN), BF16),
        scratch_shapes=[pltpu.VMEM((2, seq, kh * GRID_W), F32), pltpu.VMEM((2, seq, kh * GRID_W), BF16)],
        compiler_params=_params(("parallel", "arbitrary"), 48),
        name="lat_attention",
    )(proj, proj, proj, cache_k, cache_v, table)


def _merge_kernel(x_ref, mod_ref, oa_ref, gu_ref, vn_ref, ga_ref, gb_ref,
                  ws_ref, bst_ref, wpa_ref, wpb_ref, wo_ref, g_ref, gffn_ref, o_ref, h2_ref,
                  ob_ref, *, row_parts):
    w_pa, w_pb, w_o = (pltpu.bitcast(w[...], BF16) for w in (wpa_ref, wpb_ref, wo_ref))
    for part in _row_parts(x_ref.shape[0], row_parts):
        for c in range(part.start // CHUNK, part.stop // CHUNK):
            rows = slice(c * CHUNK, (c + 1) * CHUNK)
            for g in range(N_GROUPS_B):
                cols = slice(g * GROUP_CH, (g + 1) * GROUP_CH)
                s = jnp.dot(ws_ref[g], vn_ref[rows, cols], preferred_element_type=F32) + bst_ref[:, g:g + 1]
                ob_ref[rows, cols] = (gu_ref[rows, cols].astype(F32) * s).astype(BF16)
        pa = jnp.dot(oa_ref[part, :], w_pa, preferred_element_type=F32)
        pb = jnp.dot(ob_ref[part, :], w_pb, preferred_element_type=F32)
        mixed = (ga_ref[part, :].astype(F32) * pa + gb_ref[part, :].astype(F32) * pb).astype(BF16)
        y = jnp.dot(mixed, w_o, preferred_element_type=F32)
        x1 = x_ref[part, :] + mod_ref[0, 2:3, :] * _rms_norm(y, g_ref[...])
        o_ref[part, :] = x1
        h2 = _rms_norm(x1, gffn_ref[...]) * (1.0 + mod_ref[0, 4:5, :]) + mod_ref[0, 3:4, :]
        h2_ref[part, :] = h2.astype(BF16)


def _merge(x, mods, seq, o_a, proj_qkvu, proj_gates, gates_row0, w_s, b_s_t, w_pa, w_pb, w_o, g, g_ffn, *,
           tm=512, row_parts=2):
    m = x.shape[0]
    assert (tm // row_parts) % CHUNK == 0 and gates_row0 % tm == 0
    g0 = gates_row0 // tm
    whole = lambda a: pl.BlockSpec(a.shape, lambda i: (0,) * a.ndim, pipeline_mode=pl.Buffered(1))
    return pl.pallas_call(
        functools.partial(_merge_kernel, row_parts=row_parts),
        grid=(m // tm,),
        in_specs=[
            pl.BlockSpec((tm, D_MODEL), lambda i: (i, 0)),
            pl.BlockSpec((1, N_MOD, D_MODEL), lambda i: (i * tm // seq, 0, 0)),
            pl.BlockSpec((tm, D_ATTN), lambda i: (i, 0)),
            pl.BlockSpec((tm, D_GMLP), lambda i: (i, 3)),
            pl.BlockSpec((tm, D_GMLP), lambda i: (g0 + i, 4)),
            pl.BlockSpec((tm, D_MODEL), lambda i: (g0 + i, 0)),
            pl.BlockSpec((tm, D_MODEL), lambda i: (g0 + i, 1)),
            whole(w_s), whole(b_s_t), whole(w_pa), whole(w_pb), whole(w_o), whole(g), whole(g_ffn),
        ],
        out_specs=[pl.BlockSpec((tm, D_MODEL), lambda i: (i, 0))] * 2,
        out_shape=[jax.ShapeDtypeStruct((m, D_MODEL), F32), jax.ShapeDtypeStruct((m, D_MODEL), BF16)],
        scratch_shapes=[pltpu.VMEM((tm, D_GMLP), BF16)],
        compiler_params=_params(("parallel",), 60),
        name="merge",
    )(x, mods, o_a, proj_qkvu, proj_gates, proj_gates, proj_gates,
      w_s, b_s_t, w_pa, w_pb, w_o, g, g_ffn)


SC_LANES = 16
SC_BLOCK = (32, 512)


def _bf16_bits(u):
    return lax.shift_right_logical(u + 0x7FFF + (lax.shift_right_logical(u, 16) & 1), 16)


def _sc_pack_bf16(w, after):
    rows, cols = w.shape
    blk_r, blk_c = SC_BLOCK
    unroll = 8
    assert rows % blk_r == 0 and cols % blk_c == 0 and blk_c % (unroll * SC_LANES) == 0
    mesh = plsc.VectorSubcoreMesh(core_axis_name="core", subcore_axis_name="subcore")

    def body(in_vmem, out_vmem):
        in_vmem = in_vmem.bitcast(jnp.int32)

        @pl.loop(0, blk_r // 2)
        def _(r):
            @pl.loop(0, blk_c, step=unroll * SC_LANES)
            def _(c0):
                for k in range(unroll):
                    lanes = pl.ds(c0 + k * SC_LANES, SC_LANES)
                    lo = in_vmem[2 * r, lanes]
                    hi = in_vmem[2 * r + 1, lanes]
                    out_vmem[r, lanes] = _bf16_bits(lo) | lax.shift_left(_bf16_bits(hi), 16)

    @pl.kernel(out_type=jax.ShapeDtypeStruct((rows // 2, cols), jnp.int32), mesh=mesh, scratch_types=[],
               compiler_params=pltpu.CompilerParams(use_tc_tiling_on_sc=True))
    def pack_kernel(w_hbm, after_hbm, o_hbm):
        del after_hbm
        pltpu.emit_pipeline(
            body,
            grid=(rows // blk_r, cols // blk_c),
            in_specs=[pl.BlockSpec((blk_r, blk_c), lambda i, j: (i, j))],
            out_specs=[pl.BlockSpec((blk_r // 2, blk_c), lambda i, j: (i, j))],
            core_axis_name=("core", "subcore"),
            dimension_semantics=(pltpu.PARALLEL, pltpu.PARALLEL),
        )(w_hbm, o_hbm)

    return pack_kernel(w, after)


def _ffn_x_block(f, n_f, n_x):
    return jnp.clip(f - (n_f - n_x), 0, n_x - 1)


def _ffn_kernel(x_ref, h_ref, mod_ref, gpost_ref, wg_ref, wu_ref, wd_ref, o_ref, acc_ref, *, n_x, row_parts):
    f = pl.program_id(1)
    x_rows = x_ref.shape[0]
    tm = h_ref.shape[0]

    @pl.when(f == 0)
    def _():
        acc_ref[...] = jnp.zeros_like(acc_ref)

    row0 = pl.multiple_of(_ffn_x_block(f, pl.num_programs(1), n_x) * x_rows, x_rows)
    o_ref[pl.ds(row0, x_rows), :] = x_ref[...]

    for part in range(row_parts):
        rows = slice(part * (tm // row_parts), (part + 1) * (tm // row_parts))
        h = h_ref[rows, :]
        gate = jnp.dot(h, pltpu.bitcast(wg_ref[...], BF16), preferred_element_type=F32)
        up = jnp.dot(h, pltpu.bitcast(wu_ref[...], BF16), preferred_element_type=F32)
        act = (jax.nn.silu(gate) * up).astype(BF16)
        acc_ref[rows, :] += jnp.dot(act, pltpu.bitcast(wd_ref[...], BF16), preferred_element_type=F32)

    @pl.when(f == pl.num_programs(1) - 1)
    def _():
        scale = mod_ref[0, 5:6, :] * gpost_ref[...]

        def residual_rows(c, carry):
            rows = pl.ds(pl.multiple_of(c * SUBLANES, SUBLANES), SUBLANES)
            a = acc_ref[rows, :]
            r = lax.rsqrt(jnp.mean(a * a, axis=-1, keepdims=True) + EPS)
            o_ref[rows, :] = o_ref[rows, :] + (a * r) * scale
            return carry

        lax.fori_loop(0, tm // SUBLANES, residual_rows, 0, unroll=32)


def _ffn(x, h, mods, seq, g_post, w_gate, w_up, w_down, *, tm=1024, tf=512, x_rows=128, row_parts=1):
    m = x.shape[0]
    n_f = D_FF // tf
    n_x = tm // x_rows
    assert n_x <= n_f
    return pl.pallas_call(
        functools.partial(_ffn_kernel, n_x=n_x, row_parts=row_parts),
        grid=(m // tm, n_f),
        in_specs=[
            pl.BlockSpec((x_rows, D_MODEL), lambda i, f: (i * n_x + _ffn_x_block(f, n_f, n_x), 0)),
            pl.BlockSpec((tm, D_MODEL), lambda i, f: (i, 0)),
            pl.BlockSpec((1, N_MOD, D_MODEL), lambda i, f: (i * tm // seq, 0, 0)),
            pl.BlockSpec((1, D_MODEL), lambda i, f: (0, 0)),
            pl.BlockSpec((D_MODEL // 2, tf), lambda i, f: (0, f)),
            pl.BlockSpec((D_MODEL // 2, tf), lambda i, f: (0, f)),
            pl.BlockSpec((tf // 2, D_MODEL), lambda i, f: (f, 0)),
        ],
        out_specs=pl.BlockSpec((tm, D_MODEL), lambda i, f: (i, 0)),
        out_shape=jax.ShapeDtypeStruct((m, D_MODEL), F32),
        scratch_shapes=[pltpu.VMEM((tm, D_MODEL), F32)],
        compiler_params=_params(("parallel", "arbitrary"), 58),
        name="ffn",
    )(x, h, mods, g_post, w_gate, w_up, w_down)


def kernel(x_prompt, x_sample, cache_k, cache_v, c, c_ctx, w_ada, b_ada, norm_mix_pre, norm_mix_post,
           norm_ffn_pre, norm_ffn_post, w_in, rpb, ln_v, w_s, b_s, w_pa, w_pb, w_o, w_gate, w_up, w_down):
    assert w_ada.shape[0] == DEPTH == 1
    batch, seq, _ = x_prompt.shape
    dec_batch, dec_seq, _ = x_sample.shape
    past = cache_k.shape[2]

    row = lambda a: a[0].reshape(1, -1)
    bf = lambda a: a[0].astype(BF16)

    cvecs = jnp.concatenate(
        [c, c_ctx[None], jnp.zeros((MOD_ROWS - dec_batch - 1, D_MODEL), F32)], axis=0)
    mods = _modulation(cvecs, w_ada[0], b_ada[0])
    mods_lat = mods[:dec_batch].reshape(dec_batch, N_MOD, D_MODEL)
    mods_ctx = mods[dec_batch:dec_batch + 1].reshape(1, N_MOD, D_MODEL)
    table = _bias_table(rpb[0])

    w_in_f = w_in.reshape(D_MODEL, D_IN)
    mix_head = (bf(w_s), b_s[0].T)
    mix_tail = (row(norm_mix_post), row(norm_ffn_pre))

    xp = x_prompt.reshape(batch * seq, D_MODEL)
    xs = x_sample.reshape(dec_batch * dec_seq, D_MODEL)
    hp, proj_p, k_p, v_p, w_qkvu = _proj_qkvu(xp, mods_ctx, batch * seq, row(norm_mix_pre), w_in_f, is_ctx=True)
    hs, proj_s = _proj_qkvu(xs, mods_lat, dec_seq, row(norm_mix_pre), w_qkvu, is_ctx=False)
    gates = _proj_gates(hp, hs, row(ln_v), w_in_f)

    mix_w = [_sc_pack_bf16(w[0], hp) for w in (w_pa, w_pb, w_o)]
    ffn = (row(norm_ffn_post), *[_sc_pack_bf16(w[0], hp) for w in (w_gate, w_up, w_down)])

    oa_p = _ctx_attention(proj_p, seq)
    xp, hp = _merge(xp, mods_ctx, batch * seq, oa_p, proj_p, gates, 0, *mix_head, *mix_w, *mix_tail)
    y_prompt = _ffn(xp, hp, mods_ctx, batch * seq, *ffn).reshape(batch, seq, D_MODEL)

    oa_s = _lat_attention(proj_s, cache_k.reshape(dec_batch, past * N_HEADS_A, HEAD_DIM),
                          cache_v.reshape(dec_batch, past * N_HEADS_A, HEAD_DIM), table, dec_seq)
    xs, hs = _merge(xs, mods_lat, dec_seq, oa_s, proj_s, gates, batch * seq, *mix_head, *mix_w, *mix_tail)
    y_sample = _ffn(xs, hs, mods_lat, dec_seq, *ffn).reshape(dec_batch, dec_seq, D_MODEL)

    state_shape = (batch, DEPTH, seq, N_HEADS_A, HEAD_DIM)
    return y_prompt, y_sample, k_p.reshape(state_shape), v_p.reshape(state_shape)
```

```python
import functools

import jax
import jax.numpy as jnp
from jax import lax
from jax.experimental import pallas as pl
from jax.experimental.pallas import tpu as pltpu
from jax.experimental.pallas import tpu_sc as plsc

D_MODEL = 2048
DEPTH = 1
GRID_W = 64
N_HEADS_A = 8
HEAD_DIM = 128
D_ATTN = N_HEADS_A * HEAD_DIM
KH_MAX = 8
KW = 16
CHUNK = 128
N_GROUPS_B = 8
D_GMLP = 1024
GROUP_CH = D_GMLP // N_GROUPS_B
D_FF = ((8 * D_MODEL // 3 + 255) // 256) * 256
N_MOD = 6
EPS = 1e-6
ATTN_SCALE = HEAD_DIM ** -0.5
LOG2E = 1.4426950408889634
D_IN = 3 * D_ATTN + 2 * D_GMLP + 2 * D_MODEL

N_DR = 2 * KH_MAX - 1
N_DC = 2 * KW - 1
COL_BLOCK = 1024
SUBLANES = 8
MOD_ROWS = SUBLANES

F32 = jnp.float32
BF16 = jnp.bfloat16

MIB = 1024 * 1024


def _params(semantics, vmem_mib):
    return pltpu.CompilerParams(dimension_semantics=semantics, vmem_limit_bytes=vmem_mib * MIB)


def _rms_norm(x, g):
    return x * lax.rsqrt(jnp.mean(x * x, axis=-1, keepdims=True) + EPS) * g


def _layer_norm(x, g):
    xc = x - jnp.mean(x, axis=-1, keepdims=True)
    return xc * lax.rsqrt(jnp.mean(xc * xc, axis=-1, keepdims=True) + EPS) * g


def _modulation_kernel(c_ref, w_ref, b_ref, o_ref):
    s = jax.nn.silu(c_ref[...]).astype(BF16)
    o_ref[...] = jnp.dot(s, w_ref[...].astype(BF16), preferred_element_type=F32) + b_ref[...]


def _modulation(cvecs, w_ada, b_ada):
    tn = 1024
    n = N_MOD * D_MODEL
    return pl.pallas_call(
        _modulation_kernel,
        grid=(n // tn,),
        in_specs=[
            pl.BlockSpec((MOD_ROWS, D_MODEL), lambda j: (0, 0)),
            pl.BlockSpec((D_MODEL, tn), lambda j: (0, j)),
            pl.BlockSpec((1, tn), lambda j: (0, j)),
        ],
        out_specs=pl.BlockSpec((MOD_ROWS, tn), lambda j: (0, j)),
        out_shape=jax.ShapeDtypeStruct((MOD_ROWS, n), F32),
        compiler_params=_params(("parallel",), 40),
        name="modulation",
    )(cvecs, w_ada, b_ada.reshape(1, n))


def _bias_table_kernel(rpb_ref, o_ref):
    qc = lax.broadcasted_iota(jnp.int32, (GRID_W, GRID_W), 0)
    kc = lax.broadcasted_iota(jnp.int32, (GRID_W, GRID_W), 1)
    cs = jnp.clip(qc - KW // 2, 0, GRID_W - KW)
    valid = (kc >= cs) & (kc < cs + KW)
    lanes = rpb_ref.shape[-1]
    tiles = []
    for dr in range(N_DR):
        row = jnp.broadcast_to(rpb_ref[0, dr:dr + 1, :], (GRID_W, lanes))
        t = pltpu.roll(row, lanes - (KW - 1), 1, stride=1, stride_axis=0)[:, :GRID_W]
        tiles.append(jnp.where(valid, t * LOG2E, -jnp.inf))
    pad = jnp.zeros((GRID_W, GRID_W), F32)
    o_ref[0, 0] = jnp.concatenate(tiles + [pad], axis=-1)
    o_ref[0, 1] = jnp.concatenate([pad] + tiles, axis=-1)


def _bias_table(rpb):
    width = (N_DR + 1) * GRID_W
    lanes = 128
    rpb_rows = jnp.pad(rpb, ((0, 0), (0, 0), (0, lanes - N_DC)))
    return pl.pallas_call(
        _bias_table_kernel,
        grid=(N_HEADS_A,),
        in_specs=[pl.BlockSpec((1, N_DR, lanes), lambda h: (h, 0, 0))],
        out_specs=pl.BlockSpec((1, 2, GRID_W, width), lambda h: (h, 0, 0, 0)),
        out_shape=jax.ShapeDtypeStruct((N_HEADS_A, 2, GRID_W, width), F32),
        compiler_params=_params(("parallel",), 16),
        name="bias_table",
    )(rpb_rows)


LOAD_ROWS = 256
N_QKVU = 3 * D_ATTN + D_GMLP


def _row_parts(tm, parts):
    return [slice(p * (tm // parts), (p + 1) * (tm // parts)) for p in range(parts)]


def _load_cast_weights(w_hbm, col0, w_res, stage, sems):
    rows, width = w_res.shape
    chunk = stage.shape[1]
    n_chunks = rows // chunk

    def copy(c):
        src = w_hbm.at[pl.ds(c * chunk, chunk), pl.ds(col0, width)]
        return pltpu.make_async_copy(src, stage.at[c % 2], sems.at[c % 2])

    copy(0).start()
    for c in range(n_chunks):
        if c + 1 < n_chunks:
            copy(c + 1).start()
        copy(c).wait()
        w_res[c * chunk:(c + 1) * chunk, :] = stage[c % 2].astype(BF16)


def _proj_qkvu_kernel(x_ref, mod_ref, g_ref, w_ref, h_ref, proj_ref, *rest, is_ctx):
    if is_ctx:
        k_ref, v_ref, wpub_ref, w_res, stage, sems = rest
        step = pl.program_id(0)
        publish = pltpu.make_async_copy(w_res, wpub_ref, sems.at[2])

        @pl.when(step == 0)
        def _():
            _load_cast_weights(w_ref, 0, w_res, stage, sems)
            publish.start()
    else:
        w_res = w_ref
    y = _rms_norm(x_ref[...], g_ref[...])
    h = (y * (1.0 + mod_ref[0, 1:2, :]) + mod_ref[0, 0:1, :]).astype(BF16)
    h_ref[...] = h
    for blk in (3, 0, 1, 2):
        cols = slice(blk * COL_BLOCK, (blk + 1) * COL_BLOCK)
        acc = jnp.dot(h, w_res[:, cols], preferred_element_type=F32)
        if is_ctx and blk in (1, 2):
            kv_ref = (k_ref, v_ref)[blk - 1]
            for head in range(N_HEADS_A):
                dst = pl.ds(head, x_ref.shape[0], stride=N_HEADS_A)
                kv_ref[dst, :] = acc[:, head * HEAD_DIM:(head + 1) * HEAD_DIM]
        if blk == 3:
            acc = jax.nn.gelu(acc)
        proj_ref[:, cols] = acc.astype(BF16)
    if is_ctx:
        @pl.when(step == pl.num_programs(0) - 1)
        def _():
            publish.wait()


def _proj_qkvu(x, mods, seq, g, w, *, is_ctx, tm=512):
    m = x.shape[0]
    out_shape = [jax.ShapeDtypeStruct((m, D_MODEL), BF16), jax.ShapeDtypeStruct((m, N_QKVU), BF16)]
    out_specs = [pl.BlockSpec((tm, D_MODEL), lambda i: (i, 0)), pl.BlockSpec((tm, N_QKVU), lambda i: (i, 0))]
    scratch = []
    if is_ctx:
        out_shape += [jax.ShapeDtypeStruct((m * N_HEADS_A, HEAD_DIM), F32)] * 2
        out_specs += [pl.BlockSpec((tm * N_HEADS_A, HEAD_DIM), lambda i: (i, 0))] * 2
        out_shape += [jax.ShapeDtypeStruct((D_MODEL, N_QKVU), BF16)]
        out_specs += [pl.BlockSpec(memory_space=pl.ANY)]
        w_spec = pl.BlockSpec(memory_space=pl.ANY)
        scratch = [pltpu.VMEM((D_MODEL, N_QKVU), BF16), pltpu.VMEM((2, LOAD_ROWS, N_QKVU), F32),
                   pltpu.SemaphoreType.DMA((3,))]
    else:
        w_spec = pl.BlockSpec((D_MODEL, N_QKVU), lambda i: (0, 0), pipeline_mode=pl.Buffered(1))
    return pl.pallas_call(
        functools.partial(_proj_qkvu_kernel, is_ctx=is_ctx),
        grid=(m // tm,),
        in_specs=[
            pl.BlockSpec((tm, D_MODEL), lambda i: (i, 0)),
            pl.BlockSpec((1, N_MOD, D_MODEL), lambda i: (i * tm // seq, 0, 0)),
            pl.BlockSpec((1, D_MODEL), lambda i: (0, 0)),
            w_spec,
        ],
        out_specs=out_specs,
        out_shape=out_shape,
        scratch_shapes=scratch,
        compiler_params=_params(("arbitrary",), 58),
        name="proj_qkvu_ctx" if is_ctx else "proj_qkvu_lat",
    )(x, mods, g, w)


def _proj_gates_kernel(hp_ref, hs_ref, lnv_ref, w_ref, proj_ref, *, n_ctx, row_parts):
    step = pl.program_id(0)
    w_res = pltpu.bitcast(w_ref[...], BF16)
    n_gate = w_res.shape[1] - D_GMLP
    for rows in _row_parts(hp_ref.shape[0], row_parts):
        h = jnp.where(step < n_ctx, hp_ref[rows, :], hs_ref[rows, :])
        acc = jnp.dot(h, w_res[:, :D_GMLP], preferred_element_type=F32)
        proj_ref[rows, n_gate:] = _layer_norm(jax.nn.gelu(acc), lnv_ref[...]).astype(BF16)
        for blk in range(n_gate // COL_BLOCK):
            cols = slice(blk * COL_BLOCK, (blk + 1) * COL_BLOCK)
            acc = jnp.dot(h, w_res[:, D_GMLP + cols.start:D_GMLP + cols.stop], preferred_element_type=F32)
            proj_ref[rows, cols] = jax.nn.sigmoid(acc).astype(BF16)


def _proj_gates(h_ctx, h_lat, ln_v, w, *, tm=512, row_parts=2):
    n_ctx, n_lat = h_ctx.shape[0] // tm, h_lat.shape[0] // tm
    n = w.shape[1]
    return pl.pallas_call(
        functools.partial(_proj_gates_kernel, n_ctx=n_ctx, row_parts=row_parts),
        grid=(n_ctx + n_lat,),
        in_specs=[
            pl.BlockSpec((tm, D_MODEL), lambda i: (jnp.minimum(i, n_ctx - 1), 0)),
            pl.BlockSpec((tm, D_MODEL), lambda i: (jnp.maximum(i - n_ctx, 0), 0)),
            pl.BlockSpec((1, D_GMLP), lambda i: (0, 0)),
            pl.BlockSpec(w.shape, lambda i: (0, 0), pipeline_mode=pl.Buffered(1)),
        ],
        out_specs=pl.BlockSpec((tm, n), lambda i: (i, 0)),
        out_shape=jax.ShapeDtypeStruct(((n_ctx + n_lat) * tm, n), BF16),
        compiler_params=_params(("parallel",), 56),
        name="proj_gates",
    )(h_ctx, h_lat, ln_v, w)


def _softmax_parts(logits2):
    m = functools.reduce(jnp.maximum, [jnp.max(t, axis=-1, keepdims=True) for t in logits2])
    es = [jnp.exp2(t - m) for t in logits2]
    inv = 1.0 / functools.reduce(jnp.add, [jnp.sum(e, axis=-1, keepdims=True) for e in es])
    return [e.astype(BF16) for e in es], inv


def _qk(q, k):
    return lax.dot_general(q, k, (((1,), (1,)), ((), ())), preferred_element_type=F32) * (ATTN_SCALE * LOG2E)


def _ctx_attn_kernel(q_ref, k_ref, v_ref, o_ref):
    for h in range(N_HEADS_A):
        cols = slice(h * HEAD_DIM, (h + 1) * HEAD_DIM)
        (p,), inv = _softmax_parts([_qk(q_ref[:, cols], k_ref[:, cols])])
        o_ref[:, cols] = (jnp.dot(p, v_ref[:, cols], preferred_element_type=F32) * inv).astype(BF16)


def _ctx_attention(proj, seq):
    m = proj.shape[0]
    spec = lambda col: pl.BlockSpec((seq, D_ATTN), lambda b: (b, col))
    return pl.pallas_call(
        _ctx_attn_kernel,
        grid=(m // seq,),
        in_specs=[spec(0), spec(1), spec(2)],
        out_specs=spec(0),
        out_shape=jax.ShapeDtypeStruct((m, D_ATTN), BF16),
        compiler_params=_params(("parallel",), 32),
        name="ctx_attention",
    )(proj, proj, proj)


def _lat_attn_kernel(q_ref, k_ref, v_ref, kc_ref, vc_ref, tab_ref, o_ref, sw_ref, pw_ref, *, rows, kh):
    heads = tab_ref.shape[0]
    past = kc_ref.shape[1] // N_HEADS_A
    win = kh * GRID_W
    starts = [min(max(r - kh // 2, 0), rows - kh) * GRID_W for r in range(rows)]
    for j in range(heads):
        head = pl.program_id(1) * heads + j
        cols = slice(j * HEAD_DIM, (j + 1) * HEAD_DIM)
        sw, pw = sw_ref.at[j % 2], pw_ref.at[j % 2]
        kc = kc_ref[0, pl.ds(head, past, stride=N_HEADS_A), :].astype(BF16)
        vc = vc_ref[0, pl.ds(head, past, stride=N_HEADS_A), :].astype(BF16)
        s_c = _qk(q_ref[:, cols], kc)
        for r, start in enumerate(starts):
            off = start // GRID_W - r + (KH_MAX - 1)
            lane0 = (off + off % 2) * GRID_W
            bias = tab_ref[j, off % 2, :, lane0:lane0 + win]
            q = q_ref[r * GRID_W:(r + 1) * GRID_W, cols]
            sw[r * GRID_W:(r + 1) * GRID_W, :] = _qk(q, k_ref[start:start + win, cols]) + bias
        (p_w, p_c), inv = _softmax_parts([sw[...], s_c])
        pw[...] = p_w
        o_c = jnp.dot(p_c, vc, preferred_element_type=F32)
        for r, start in enumerate(starts):
            q_rows = slice(r * GRID_W, (r + 1) * GRID_W)
            o_w = jnp.dot(pw[q_rows, :], v_ref[start:start + win, cols], preferred_element_type=F32)
            o_ref[q_rows, cols] = ((o_w + o_c[q_rows, :]) * inv[q_rows, :]).astype(BF16)


def _lat_attention(proj, cache_k, cache_v, table, seq, *, heads_per_step=4):
    m = proj.shape[0]
    rows = seq // GRID_W
    kh = min(KH_MAX, rows)
    n_hp = N_HEADS_A // heads_per_step
    width = heads_per_step * HEAD_DIM
    qkv = lambda part: pl.BlockSpec((seq, width), lambda b, hp: (b, part * n_hp + hp))
    cache = pl.BlockSpec((1,) + cache_k.shape[1:], lambda b, hp: (b, 0, 0))
    return pl.pallas_call(
        functools.partial(_lat_attn_kernel, rows=rows, kh=kh),
        grid=(m // seq, n_hp),
        in_specs=[qkv(0), qkv(1), qkv(2), cache, cache,
                  pl.BlockSpec((heads_per_step,) + table.shape[1:], lambda b, hp: (hp, 0, 0, 0))],
        out_specs=pl.BlockSpec((seq, width), lambda b, hp: (b, hp)),
        out_shape=jax.ShapeDtypeStruct((m, D_ATTN), BF16),
        scratch_shapes=[pltpu.VMEM((2, seq, kh * GRID_W), F32), pltpu.VMEM((2, seq, kh * GRID_W), BF16)],
        compiler_params=_params(("parallel", "arbitrary"), 48),
        name="lat_attention",
    )(proj, proj, proj, cache_k, cache_v, table)


def _merge_kernel(x_ref, mod_ref, oa_ref, gu_ref, vn_ref, ga_ref, gb_ref,
                  ws_ref, bst_ref, wpa_ref, wpb_ref, wo_ref, g_ref, gffn_ref, o_ref, h2_ref,
                  ob_ref, *, row_parts):
    w_pa, w_pb, w_o = (pltpu.bitcast(w[...], BF16) for w in (wpa_ref, wpb_ref, wo_ref))
    for part in _row_parts(x_ref.shape[0], row_parts):
        for c in range(part.start // CHUNK, part.stop // CHUNK):
            rows = slice(c * CHUNK, (c + 1) * CHUNK)
            for g in range(N_GROUPS_B):
                cols = slice(g * GROUP_CH, (g + 1) * GROUP_CH)
                s = jnp.dot(ws_ref[g], vn_ref[rows, cols], preferred_element_type=F32) + bst_ref[:, g:g + 1]
                ob_ref[rows, cols] = (gu_ref[rows, cols].astype(F32) * s).astype(BF16)
        pa = jnp.dot(oa_ref[part, :], w_pa, preferred_element_type=F32)
        pb = jnp.dot(ob_ref[part, :], w_pb, preferred_element_type=F32)
        mixed = (ga_ref[part, :].astype(F32) * pa + gb_ref[part, :].astype(F32) * pb).astype(BF16)
        y = jnp.dot(mixed, w_o, preferred_element_type=F32)
        x1 = x_ref[part, :] + mod_ref[0, 2:3, :] * _rms_norm(y, g_ref[...])
        o_ref[part, :] = x1
        h2 = _rms_norm(x1, gffn_ref[...]) * (1.0 + mod_ref[0, 4:5, :]) + mod_ref[0, 3:4, :]
        h2_ref[part, :] = h2.astype(BF16)


def _merge(x, mods, seq, o_a, proj_qkvu, proj_gates, gates_row0, w_s, b_s_t, w_pa, w_pb, w_o, g, g_ffn, *,
           tm=512, row_parts=2):
    m = x.shape[0]
    assert (tm // row_parts) % CHUNK == 0 and gates_row0 % tm == 0
    g0 = gates_row0 // tm
    whole = lambda a: pl.BlockSpec(a.shape, lambda i: (0,) * a.ndim, pipeline_mode=pl.Buffered(1))
    return pl.pallas_call(
        functools.partial(_merge_kernel, row_parts=row_parts),
        grid=(m // tm,),
        in_specs=[
            pl.BlockSpec((tm, D_MODEL), lambda i: (i, 0)),
            pl.BlockSpec((1, N_MOD, D_MODEL), lambda i: (i * tm // seq, 0, 0)),
            pl.BlockSpec((tm, D_ATTN), lambda i: (i, 0)),
            pl.BlockSpec((tm, D_GMLP), lambda i: (i, 3)),
            pl.BlockSpec((tm, D_GMLP), lambda i: (g0 + i, 4)),
            pl.BlockSpec((tm, D_MODEL), lambda i: (g0 + i, 0)),
            pl.BlockSpec((tm, D_MODEL), lambda i: (g0 + i, 1)),
            whole(w_s), whole(b_s_t), whole(w_pa), whole(w_pb), whole(w_o), whole(g), whole(g_ffn),
        ],
        out_specs=[pl.BlockSpec((tm, D_MODEL), lambda i: (i, 0))] * 2,
        out_shape=[jax.ShapeDtypeStruct((m, D_MODEL), F32), jax.ShapeDtypeStruct((m, D_MODEL), BF16)],
        scratch_shapes=[pltpu.VMEM((tm, D_GMLP), BF16)],
        compiler_params=_params(("parallel",), 60),
        name="merge",
    )(x, mods, o_a, proj_qkvu, proj_gates, proj_gates, proj_gates,
      w_s, b_s_t, w_pa, w_pb, w_o, g, g_ffn)


SC_LANES = 16
SC_BLOCK = (32, 512)


def _bf16_bits(u):
    return lax.shift_right_logical(u + 0x7FFF + (lax.shift_right_logical(u, 16) & 1), 16)


def _sc_pack_bf16(w, after, col0=0, cols=None):
    rows = w.shape[0]
    cols = w.shape[1] if cols is None else cols
    blk_r, blk_c = SC_BLOCK
    unroll = 8
    assert rows % blk_r == 0 and cols % blk_c == 0 and col0 % blk_c == 0 and blk_c % (unroll * SC_LANES) == 0
    j0 = col0 // blk_c
    mesh = plsc.VectorSubcoreMesh(core_axis_name="core", subcore_axis_name="subcore")

    def body(in_vmem, out_vmem):
        in_vmem = in_vmem.bitcast(jnp.int32)

        @pl.loop(0, blk_r // 2)
        def _(r):
            @pl.loop(0, blk_c, step=unroll * SC_LANES)
            def _(c0):
                for k in range(unroll):
                    lanes = pl.ds(c0 + k * SC_LANES, SC_LANES)
                    lo = in_vmem[2 * r, lanes]
                    hi = in_vmem[2 * r + 1, lanes]
                    out_vmem[r, lanes] = _bf16_bits(lo) | lax.shift_left(_bf16_bits(hi), 16)

    @pl.kernel(out_type=jax.ShapeDtypeStruct((rows // 2, cols), jnp.int32), mesh=mesh, scratch_types=[],
               compiler_params=pltpu.CompilerParams(use_tc_tiling_on_sc=True))
    def pack_kernel(w_hbm, after_hbm, o_hbm):
        del after_hbm
        pltpu.emit_pipeline(
            body,
            grid=(rows // blk_r, cols // blk_c),
            in_specs=[pl.BlockSpec((blk_r, blk_c), lambda i, j: (i, j0 + j))],
            out_specs=[pl.BlockSpec((blk_r // 2, blk_c), lambda i, j: (i, j))],
            core_axis_name=("core", "subcore"),
            dimension_semantics=(pltpu.PARALLEL, pltpu.PARALLEL),
        )(w_hbm, o_hbm)

    return pack_kernel(w, after)


def _ffn_x_block(f, n_f, n_x):
    return jnp.clip(f - (n_f - n_x), 0, n_x - 1)


def _ffn_kernel(x_ref, h_ref, mod_ref, gpost_ref, wg_ref, wu_ref, wd_ref, o_ref, acc_ref, *, n_x, row_parts):
    f = pl.program_id(1)
    x_rows = x_ref.shape[0]
    tm = h_ref.shape[0]

    @pl.when(f == 0)
    def _():
        acc_ref[...] = jnp.zeros_like(acc_ref)

    row0 = pl.multiple_of(_ffn_x_block(f, pl.num_programs(1), n_x) * x_rows, x_rows)
    o_ref[pl.ds(row0, x_rows), :] = x_ref[...]

    for part in range(row_parts):
        rows = slice(part * (tm // row_parts), (part + 1) * (tm // row_parts))
        h = h_ref[rows, :]
        gate = jnp.dot(h, pltpu.bitcast(wg_ref[...], BF16), preferred_element_type=F32)
        up = jnp.dot(h, pltpu.bitcast(wu_ref[...], BF16), preferred_element_type=F32)
        act = (jax.nn.silu(gate) * up).astype(BF16)
        acc_ref[rows, :] += jnp.dot(act, pltpu.bitcast(wd_ref[...], BF16), preferred_element_type=F32)

    @pl.when(f == pl.num_programs(1) - 1)
    def _():
        scale = mod_ref[0, 5:6, :] * gpost_ref[...]

        def residual_rows(c, carry):
            rows = pl.ds(pl.multiple_of(c * SUBLANES, SUBLANES), SUBLANES)
            a = acc_ref[rows, :]
            r = lax.rsqrt(jnp.mean(a * a, axis=-1, keepdims=True) + EPS)
            o_ref[rows, :] = o_ref[rows, :] + (a * r) * scale
            return carry

        lax.fori_loop(0, tm // SUBLANES, residual_rows, 0, unroll=32)


def _ffn(x, h, mods, seq, g_post, w_gate, w_up, w_down, *, tm=1024, tf=512, x_rows=128, row_parts=1):
    m = x.shape[0]
    n_f = D_FF // tf
    n_x = tm // x_rows
    assert n_x <= n_f
    return pl.pallas_call(
        functools.partial(_ffn_kernel, n_x=n_x, row_parts=row_parts),
        grid=(m // tm, n_f),
        in_specs=[
            pl.BlockSpec((x_rows, D_MODEL), lambda i, f: (i * n_x + _ffn_x_block(f, n_f, n_x), 0)),
            pl.BlockSpec((tm, D_MODEL), lambda i, f: (i, 0)),
            pl.BlockSpec((1, N_MOD, D_MODEL), lambda i, f: (i * tm // seq, 0, 0)),
            pl.BlockSpec((1, D_MODEL), lambda i, f: (0, 0)),
            pl.BlockSpec((D_MODEL // 2, tf), lambda i, f: (0, f)),
            pl.BlockSpec((D_MODEL // 2, tf), lambda i, f: (0, f)),
            pl.BlockSpec((tf // 2, D_MODEL), lambda i, f: (f, 0)),
        ],
        out_specs=pl.BlockSpec((tm, D_MODEL), lambda i, f: (i, 0)),
        out_shape=jax.ShapeDtypeStruct((m, D_MODEL), F32),
        scratch_shapes=[pltpu.VMEM((tm, D_MODEL), F32)],
        compiler_params=_params(("parallel", "arbitrary"), 58),
        name="ffn",
    )(x, h, mods, g_post, w_gate, w_up, w_down)


def kernel(x_prompt, x_sample, cache_k, cache_v, c, c_ctx, w_ada, b_ada, norm_mix_pre, norm_mix_post,
           norm_ffn_pre, norm_ffn_post, w_in, rpb, ln_v, w_s, b_s, w_pa, w_pb, w_o, w_gate, w_up, w_down):
    assert w_ada.shape[0] == DEPTH == 1
    batch, seq, _ = x_prompt.shape
    dec_batch, dec_seq, _ = x_sample.shape
    past = cache_k.shape[2]

    row = lambda a: a[0].reshape(1, -1)
    bf = lambda a: a[0].astype(BF16)

    cvecs = jnp.concatenate(
        [c, c_ctx[None], jnp.zeros((MOD_ROWS - dec_batch - 1, D_MODEL), F32)], axis=0)
    mods = _modulation(cvecs, w_ada[0], b_ada[0])
    mods_lat = mods[:dec_batch].reshape(dec_batch, N_MOD, D_MODEL)
    mods_ctx = mods[dec_batch:dec_batch + 1].reshape(1, N_MOD, D_MODEL)
    table = _bias_table(rpb[0])

    w_in_f = w_in.reshape(D_MODEL, D_IN)
    mix_head = (bf(w_s), b_s[0].T)
    mix_tail = (row(norm_mix_post), row(norm_ffn_pre))

    xp = x_prompt.reshape(batch * seq, D_MODEL)
    xs = x_sample.reshape(dec_batch * dec_seq, D_MODEL)
    hp, proj_p, k_p, v_p, w_qkvu = _proj_qkvu(xp, mods_ctx, batch * seq, row(norm_mix_pre), w_in_f, is_ctx=True)
    hs, proj_s = _proj_qkvu(xs, mods_lat, dec_seq, row(norm_mix_pre), w_qkvu, is_ctx=False)
    gates = _proj_gates(hp, hs, row(ln_v), _sc_pack_bf16(w_in_f, mods, N_QKVU, D_IN - N_QKVU))

    mix_w = [_sc_pack_bf16(w[0], hp) for w in (w_pa, w_pb, w_o)]
    ffn = (row(norm_ffn_post), *[_sc_pack_bf16(w[0], hp) for w in (w_gate, w_up, w_down)])

    oa_p = _ctx_attention(proj_p, seq)
    xp, hp = _merge(xp, mods_ctx, batch * seq, oa_p, proj_p, gates, 0, *mix_head, *mix_w, *mix_tail)
    y_prompt = _ffn(xp, hp, mods_ctx, batch * seq, *ffn).reshape(batch, seq, D_MODEL)

    oa_s = _lat_attention(proj_s, cache_k.reshape(dec_batch, past * N_HEADS_A, HEAD_DIM),
                          cache_v.reshape(dec_batch, past * N_HEADS_A, HEAD_DIM), table, dec_seq)
    xs, hs = _merge(xs, mods_lat, dec_seq, oa_s, proj_s, gates, batch * seq, *mix_head, *mix_w, *mix_tail)
    y_sample = _ffn(xs, hs, mods_lat, dec_seq, *ffn).reshape(dec_batch, dec_seq, D_MODEL)

    state_shape = (batch, DEPTH, seq, N_HEADS_A, HEAD_DIM)
    return y_prompt, y_sample, k_p.reshape(state_shape), v_p.reshape(state_shape)
```

```python
import functools

import jax
import jax.numpy as jnp
from jax import lax
from jax.experimental import pallas as pl
from jax.experimental.pallas import tpu as pltpu
from jax.experimental.pallas import tpu_sc as plsc

D_MODEL = 2048
DEPTH = 1
GRID_W = 64
N_HEADS_A = 8
HEAD_DIM = 128
D_ATTN = N_HEADS_A * HEAD_DIM
KH_MAX = 8
KW = 16
CHUNK = 128
N_GROUPS_B = 8
D_GMLP = 1024
GROUP_CH = D_GMLP // N_GROUPS_B
D_FF = ((8 * D_MODEL // 3 + 255) // 256) * 256
N_MOD = 6
EPS = 1e-6
ATTN_SCALE = HEAD_DIM ** -0.5
LOG2E = 1.4426950408889634
D_IN = 3 * D_ATTN + 2 * D_GMLP + 2 * D_MODEL

N_DR = 2 * KH_MAX - 1
N_DC = 2 * KW - 1
COL_BLOCK = 1024
SUBLANES = 8
MOD_ROWS = SUBLANES

F32 = jnp.float32
BF16 = jnp.bfloat16

MIB = 1024 * 1024


def _params(semantics, vmem_mib):
    return pltpu.CompilerParams(dimension_semantics=semantics, vmem_limit_bytes=vmem_mib * MIB)


def _rms_norm(x, g):
    return x * lax.rsqrt(jnp.mean(x * x, axis=-1, keepdims=True) + EPS) * g


def _layer_norm(x, g):
    xc = x - jnp.mean(x, axis=-1, keepdims=True)
    return xc * lax.rsqrt(jnp.mean(xc * xc, axis=-1, keepdims=True) + EPS) * g


def _modulation_kernel(c_ref, w_ref, b_ref, o_ref):
    s = jax.nn.silu(c_ref[...]).astype(BF16)
    o_ref[...] = jnp.dot(s, w_ref[...].astype(BF16), preferred_element_type=F32) + b_ref[...]


def _modulation(cvecs, w_ada, b_ada):
    tn = 1024
    n = N_MOD * D_MODEL
    return pl.pallas_call(
        _modulation_kernel,
        grid=(n // tn,),
        in_specs=[
            pl.BlockSpec((MOD_ROWS, D_MODEL), lambda j: (0, 0)),
            pl.BlockSpec((D_MODEL, tn), lambda j: (0, j)),
            pl.BlockSpec((1, tn), lambda j: (0, j)),
        ],
        out_specs=pl.BlockSpec((MOD_ROWS, tn), lambda j: (0, j)),
        out_shape=jax.ShapeDtypeStruct((MOD_ROWS, n), F32),
        compiler_params=_params(("parallel",), 40),
        name="modulation",
    )(cvecs, w_ada, b_ada.reshape(1, n))


def _bias_table_kernel(rpb_ref, o_ref):
    qc = lax.broadcasted_iota(jnp.int32, (GRID_W, GRID_W), 0)
    kc = lax.broadcasted_iota(jnp.int32, (GRID_W, GRID_W), 1)
    cs = jnp.clip(qc - KW // 2, 0, GRID_W - KW)
    valid = (kc >= cs) & (kc < cs + KW)
    lanes = rpb_ref.shape[-1]
    tiles = []
    for dr in range(N_DR):
        row = jnp.broadcast_to(rpb_ref[0, dr:dr + 1, :], (GRID_W, lanes))
        t = pltpu.roll(row, lanes - (KW - 1), 1, stride=1, stride_axis=0)[:, :GRID_W]
        tiles.append(jnp.where(valid, t * LOG2E, -jnp.inf))
    pad = jnp.zeros((GRID_W, GRID_W), F32)
    o_ref[0, 0] = jnp.concatenate(tiles + [pad], axis=-1)
    o_ref[0, 1] = jnp.concatenate([pad] + tiles, axis=-1)


def _bias_table(rpb):
    width = (N_DR + 1) * GRID_W
    lanes = 128
    rpb_rows = jnp.pad(rpb, ((0, 0), (0, 0), (0, lanes - N_DC)))
    return pl.pallas_call(
        _bias_table_kernel,
        grid=(N_HEADS_A,),
        in_specs=[pl.BlockSpec((1, N_DR, lanes), lambda h: (h, 0, 0))],
        out_specs=pl.BlockSpec((1, 2, GRID_W, width), lambda h: (h, 0, 0, 0)),
        out_shape=jax.ShapeDtypeStruct((N_HEADS_A, 2, GRID_W, width), F32),
        compiler_params=_params(("parallel",), 16),
        name="bias_table",
    )(rpb_rows)


SC_LANES = 16
SC_BLOCK = (32, 512)
SC_UNROLL = 8


def _bf16_bits(u):
    return lax.shift_right_logical(u + 0x7FFF + (lax.shift_right_logical(u, 16) & 1), 16)


def _sc_pack_bf16(w, after, col0=0, cols=None):
    rows = w.shape[0]
    cols = w.shape[1] if cols is None else cols
    blk_r, blk_c = SC_BLOCK
    assert rows % blk_r == 0 and cols % blk_c == 0 and col0 % blk_c == 0 and blk_c % (SC_UNROLL * SC_LANES) == 0
    j0 = col0 // blk_c
    mesh = plsc.VectorSubcoreMesh(core_axis_name="core", subcore_axis_name="subcore")

    def body(in_vmem, out_vmem):
        in_vmem = in_vmem.bitcast(jnp.int32)

        @pl.loop(0, blk_r // 2)
        def _(r):
            @pl.loop(0, blk_c, step=SC_UNROLL * SC_LANES)
            def _(c0):
                for k in range(SC_UNROLL):
                    lanes = pl.ds(c0 + k * SC_LANES, SC_LANES)
                    lo = in_vmem[2 * r, lanes]
                    hi = in_vmem[2 * r + 1, lanes]
                    out_vmem[r, lanes] = _bf16_bits(lo) | lax.shift_left(_bf16_bits(hi), 16)

    @pl.kernel(out_type=jax.ShapeDtypeStruct((rows // 2, cols), jnp.int32), mesh=mesh, scratch_types=[],
               compiler_params=pltpu.CompilerParams(use_tc_tiling_on_sc=True))
    def pack_kernel(w_hbm, after_hbm, o_hbm):
        del after_hbm
        pltpu.emit_pipeline(
            body,
            grid=(rows // blk_r, cols // blk_c),
            in_specs=[pl.BlockSpec((blk_r, blk_c), lambda i, j: (i, j0 + j))],
            out_specs=[pl.BlockSpec((blk_r // 2, blk_c), lambda i, j: (i, j))],
            core_axis_name=("core", "subcore"),
            dimension_semantics=(pltpu.PARALLEL, pltpu.PARALLEL),
        )(w_hbm, o_hbm)

    return pack_kernel(w, after)


LOAD_ROWS = 256
N_QKVU = 3 * D_ATTN + D_GMLP


def _row_parts(tm, parts):
    return [slice(p * (tm // parts), (p + 1) * (tm // parts)) for p in range(parts)]


def _load_cast_weights(w_hbm, col0, w_res, stage, sems):
    rows, width = w_res.shape
    chunk = stage.shape[1]
    n_chunks = rows // chunk

    def copy(c):
        src = w_hbm.at[pl.ds(c * chunk, chunk), pl.ds(col0, width)]
        return pltpu.make_async_copy(src, stage.at[c % 2], sems.at[c % 2])

    copy(0).start()
    for c in range(n_chunks):
        if c + 1 < n_chunks:
            copy(c + 1).start()
        copy(c).wait()
        w_res[c * chunk:(c + 1) * chunk, :] = stage[c % 2].astype(BF16)


def _proj_qkvu_kernel(x_ref, mod_ref, g_ref, w_ref, h_ref, proj_ref, *rest, is_ctx):
    if is_ctx:
        k_ref, v_ref, wpub_ref, w_res, stage, sems = rest
        step = pl.program_id(0)
        publish = pltpu.make_async_copy(w_res, wpub_ref, sems.at[2])

        @pl.when(step == 0)
        def _():
            _load_cast_weights(w_ref, 0, w_res, stage, sems)
            publish.start()
    else:
        w_res = w_ref
    y = _rms_norm(x_ref[...], g_ref[...])
    h = (y * (1.0 + mod_ref[0, 1:2, :]) + mod_ref[0, 0:1, :]).astype(BF16)
    h_ref[...] = h
    for blk in (3, 0, 1, 2):
        cols = slice(blk * COL_BLOCK, (blk + 1) * COL_BLOCK)
        acc = jnp.dot(h, w_res[:, cols], preferred_element_type=F32)
        if is_ctx and blk in (1, 2):
            kv_ref = (k_ref, v_ref)[blk - 1]
            for head in range(N_HEADS_A):
                dst = pl.ds(head, x_ref.shape[0], stride=N_HEADS_A)
                kv_ref[dst, :] = acc[:, head * HEAD_DIM:(head + 1) * HEAD_DIM]
        if blk == 3:
            acc = jax.nn.gelu(acc)
        proj_ref[:, cols] = acc.astype(BF16)
    if is_ctx:
        @pl.when(step == pl.num_programs(0) - 1)
        def _():
            publish.wait()


def _proj_qkvu(x, mods, seq, g, w, *, is_ctx, tm=512):
    m = x.shape[0]
    out_shape = [jax.ShapeDtypeStruct((m, D_MODEL), BF16), jax.ShapeDtypeStruct((m, N_QKVU), BF16)]
    out_specs = [pl.BlockSpec((tm, D_MODEL), lambda i: (i, 0)), pl.BlockSpec((tm, N_QKVU), lambda i: (i, 0))]
    scratch = []
    if is_ctx:
        out_shape += [jax.ShapeDtypeStruct((m * N_HEADS_A, HEAD_DIM), F32)] * 2
        out_specs += [pl.BlockSpec((tm * N_HEADS_A, HEAD_DIM), lambda i: (i, 0))] * 2
        out_shape += [jax.ShapeDtypeStruct((D_MODEL, N_QKVU), BF16)]
        out_specs += [pl.BlockSpec(memory_space=pl.ANY)]
        w_spec = pl.BlockSpec(memory_space=pl.ANY)
        scratch = [pltpu.VMEM((D_MODEL, N_QKVU), BF16), pltpu.VMEM((2, LOAD_ROWS, N_QKVU), F32),
                   pltpu.SemaphoreType.DMA((3,))]
    else:
        w_spec = pl.BlockSpec((D_MODEL, N_QKVU), lambda i: (0, 0), pipeline_mode=pl.Buffered(1))
    return pl.pallas_call(
        functools.partial(_proj_qkvu_kernel, is_ctx=is_ctx),
        grid=(m // tm,),
        in_specs=[
            pl.BlockSpec((tm, D_MODEL), lambda i: (i, 0)),
            pl.BlockSpec((1, N_MOD, D_MODEL), lambda i: (i * tm // seq, 0, 0)),
            pl.BlockSpec((1, D_MODEL), lambda i: (0, 0)),
            w_spec,
        ],
        out_specs=out_specs,
        out_shape=out_shape,
        scratch_shapes=scratch,
        compiler_params=_params(("arbitrary",), 58),
        name="proj_qkvu_ctx" if is_ctx else "proj_qkvu_lat",
    )(x, mods, g, w)


def _proj_gates_kernel(hp_ref, hs_ref, lnv_ref, w_ref, proj_ref, *, n_ctx, row_parts):
    step = pl.program_id(0)
    w_res = pltpu.bitcast(w_ref[...], BF16)
    n_gate = w_res.shape[1] - D_GMLP
    for rows in _row_parts(hp_ref.shape[0], row_parts):
        h = jnp.where(step < n_ctx, hp_ref[rows, :], hs_ref[rows, :])
        acc = jnp.dot(h, w_res[:, :D_GMLP], preferred_element_type=F32)
        proj_ref[rows, n_gate:] = _layer_norm(jax.nn.gelu(acc), lnv_ref[...]).astype(BF16)
        for blk in range(n_gate // COL_BLOCK):
            cols = slice(blk * COL_BLOCK, (blk + 1) * COL_BLOCK)
            acc = jnp.dot(h, w_res[:, D_GMLP + cols.start:D_GMLP + cols.stop], preferred_element_type=F32)
            proj_ref[rows, cols] = jax.nn.sigmoid(acc).astype(BF16)


def _proj_gates(h_ctx, h_lat, ln_v, w, *, tm=512, row_parts=2):
    n_ctx, n_lat = h_ctx.shape[0] // tm, h_lat.shape[0] // tm
    n = w.shape[1]
    return pl.pallas_call(
        functools.partial(_proj_gates_kernel, n_ctx=n_ctx, row_parts=row_parts),
        grid=(n_ctx + n_lat,),
        in_specs=[
            pl.BlockSpec((tm, D_MODEL), lambda i: (jnp.minimum(i, n_ctx - 1), 0)),
            pl.BlockSpec((tm, D_MODEL), lambda i: (jnp.maximum(i - n_ctx, 0), 0)),
            pl.BlockSpec((1, D_GMLP), lambda i: (0, 0)),
            pl.BlockSpec(w.shape, lambda i: (0, 0), pipeline_mode=pl.Buffered(1)),
        ],
        out_specs=pl.BlockSpec((tm, n), lambda i: (i, 0)),
        out_shape=jax.ShapeDtypeStruct(((n_ctx + n_lat) * tm, n), BF16),
        compiler_params=_params(("parallel",), 56),
        name="proj_gates",
    )(h_ctx, h_lat, ln_v, w)


def _softmax_parts(logits2):
    m = functools.reduce(jnp.maximum, [jnp.max(t, axis=-1, keepdims=True) for t in logits2])
    es = [jnp.exp2(t - m) for t in logits2]
    inv = 1.0 / functools.reduce(jnp.add, [jnp.sum(e, axis=-1, keepdims=True) for e in es])
    return [e.astype(BF16) for e in es], inv


def _qk(q, k):
    return lax.dot_general(q, k, (((1,), (1,)), ((), ())), preferred_element_type=F32) * (ATTN_SCALE * LOG2E)


def _ctx_attn_kernel(q_ref, k_ref, v_ref, o_ref):
    for h in range(N_HEADS_A):
        cols = slice(h * HEAD_DIM, (h + 1) * HEAD_DIM)
        (p,), inv = _softmax_parts([_qk(q_ref[:, cols], k_ref[:, cols])])
        o_ref[:, cols] = (jnp.dot(p, v_ref[:, cols], preferred_element_type=F32) * inv).astype(BF16)


def _ctx_attention(proj, seq):
    m = proj.shape[0]
    spec = lambda col: pl.BlockSpec((seq, D_ATTN), lambda b: (b, col))
    return pl.pallas_call(
        _ctx_attn_kernel,
        grid=(m // seq,),
        in_specs=[spec(0), spec(1), spec(2)],
        out_specs=spec(0),
        out_shape=jax.ShapeDtypeStruct((m, D_ATTN), BF16),
        compiler_params=_params(("parallel",), 32),
        name="ctx_attention",
    )(proj, proj, proj)


def _lat_attn_kernel(q_ref, k_ref, v_ref, kc_ref, vc_ref, tab_ref, o_ref, sw_ref, pw_ref, *, rows, kh):
    heads = tab_ref.shape[0]
    past = kc_ref.shape[1] // N_HEADS_A
    win = kh * GRID_W
    starts = [min(max(r - kh // 2, 0), rows - kh) * GRID_W for r in range(rows)]
    for j in range(heads):
        head = pl.program_id(1) * heads + j
        cols = slice(j * HEAD_DIM, (j + 1) * HEAD_DIM)
        sw, pw = sw_ref.at[j % 2], pw_ref.at[j % 2]
        kc = kc_ref[0, pl.ds(head, past, stride=N_HEADS_A), :].astype(BF16)
        vc = vc_ref[0, pl.ds(head, past, stride=N_HEADS_A), :].astype(BF16)
        s_c = _qk(q_ref[:, cols], kc)
        for r, start in enumerate(starts):
            off = start // GRID_W - r + (KH_MAX - 1)
            lane0 = (off + off % 2) * GRID_W
            bias = tab_ref[j, off % 2, :, lane0:lane0 + win]
            q = q_ref[r * GRID_W:(r + 1) * GRID_W, cols]
            sw[r * GRID_W:(r + 1) * GRID_W, :] = _qk(q, k_ref[start:start + win, cols]) + bias
        (p_w, p_c), inv = _softmax_parts([sw[...], s_c])
        pw[...] = p_w
        o_c = jnp.dot(p_c, vc, preferred_element_type=F32)
        for r, start in enumerate(starts):
            q_rows = slice(r * GRID_W, (r + 1) * GRID_W)
            o_w = jnp.dot(pw[q_rows, :], v_ref[start:start + win, cols], preferred_element_type=F32)
            o_ref[q_rows, cols] = ((o_w + o_c[q_rows, :]) * inv[q_rows, :]).astype(BF16)


def _lat_attention(proj, cache_k, cache_v, table, seq, *, heads_per_step=4):
    m = proj.shape[0]
    rows = seq // GRID_W
    kh = min(KH_MAX, rows)
    n_hp = N_HEADS_A // heads_per_step
    width = heads_per_step * HEAD_DIM
    qkv = lambda part: pl.BlockSpec((seq, width), lambda b, hp: (b, part * n_hp + hp))
    cache = pl.BlockSpec((1,) + cache_k.shape[1:], lambda b, hp: (b, 0, 0))
    return pl.pallas_call(
        functools.partial(_lat_attn_kernel, rows=rows, kh=kh),
        grid=(m // seq, n_hp),
        in_specs=[qkv(0), qkv(1), qkv(2), cache, cache,
                  pl.BlockSpec((heads_per_step,) + table.shape[1:], lambda b, hp: (hp, 0, 0, 0))],
        out_specs=pl.BlockSpec((seq, width), lambda b, hp: (b, hp)),
        out_shape=jax.ShapeDtypeStruct((m, D_ATTN), BF16),
        scratch_shapes=[pltpu.VMEM((2, seq, kh * GRID_W), F32), pltpu.VMEM((2, seq, kh * GRID_W), BF16)],
        compiler_params=_params(("parallel", "arbitrary"), 48),
        name="lat_attention",
    )(proj, proj, proj, cache_k, cache_v, table)


def _merge_kernel(x_ref, mod_ref, oa_ref, gu_ref, vn_ref, ga_ref, gb_ref,
                  ws_ref, bst_ref, wpa_ref, wpb_ref, wo_ref, g_ref, gffn_ref, o_ref, h2_ref,
                  ob_ref, *, row_parts):
    w_pa, w_pb, w_o = (pltpu.bitcast(w[...], BF16) for w in (wpa_ref, wpb_ref, wo_ref))
    for part in _row_parts(x_ref.shape[0], row_parts):
        for c in range(part.start // CHUNK, part.stop // CHUNK):
            rows = slice(c * CHUNK, (c + 1) * CHUNK)
            for g in range(N_GROUPS_B):
                cols = slice(g * GROUP_CH, (g + 1) * GROUP_CH)
                s = jnp.dot(ws_ref[g], vn_ref[rows, cols], preferred_element_type=F32) + bst_ref[:, g:g + 1]
                ob_ref[rows, cols] = (gu_ref[rows, cols].astype(F32) * s).astype(BF16)
        pa = jnp.dot(oa_ref[part, :], w_pa, preferred_element_type=F32)
        pb = jnp.dot(ob_ref[part, :], w_pb, preferred_element_type=F32)
        mixed = (ga_ref[part, :].astype(F32) * pa + gb_ref[part, :].astype(F32) * pb).astype(BF16)
        y = jnp.dot(mixed, w_o, preferred_element_type=F32)
        x1 = x_ref[part, :] + mod_ref[0, 2:3, :] * _rms_norm(y, g_ref[...])
        o_ref[part, :] = x1
        h2 = _rms_norm(x1, gffn_ref[...]) * (1.0 + mod_ref[0, 4:5, :]) + mod_ref[0, 3:4, :]
        h2_ref[part, :] = h2.astype(BF16)


def _merge(x, mods, seq, o_a, proj_qkvu, proj_gates, gates_row0, w_s, b_s_t, w_pa, w_pb, w_o, g, g_ffn, *,
           tm=512, row_parts=2):
    m = x.shape[0]
    assert (tm // row_parts) % CHUNK == 0 and gates_row0 % tm == 0
    g0 = gates_row0 // tm
    whole = lambda a: pl.BlockSpec(a.shape, lambda i: (0,) * a.ndim, pipeline_mode=pl.Buffered(1))
    return pl.pallas_call(
        functools.partial(_merge_kernel, row_parts=row_parts),
        grid=(m // tm,),
        in_specs=[
            pl.BlockSpec((tm, D_MODEL), lambda i: (i, 0)),
            pl.BlockSpec((1, N_MOD, D_MODEL), lambda i: (i * tm // seq, 0, 0)),
            pl.BlockSpec((tm, D_ATTN), lambda i: (i, 0)),
            pl.BlockSpec((tm, D_GMLP), lambda i: (i, 3)),
            pl.BlockSpec((tm, D_GMLP), lambda i: (g0 + i, 4)),
            pl.BlockSpec((tm, D_MODEL), lambda i: (g0 + i, 0)),
            pl.BlockSpec((tm, D_MODEL), lambda i: (g0 + i, 1)),
            whole(w_s), whole(b_s_t), whole(w_pa), whole(w_pb), whole(w_o), whole(g), whole(g_ffn),
        ],
        out_specs=[pl.BlockSpec((tm, D_MODEL), lambda i: (i, 0))] * 2,
        out_shape=[jax.ShapeDtypeStruct((m, D_MODEL), F32), jax.ShapeDtypeStruct((m, D_MODEL), BF16)],
        scratch_shapes=[pltpu.VMEM((tm, D_GMLP), BF16)],
        compiler_params=_params(("parallel",), 60),
        name="merge",
    )(x, mods, o_a, proj_qkvu, proj_gates, proj_gates, proj_gates,
      w_s, b_s_t, w_pa, w_pb, w_o, g, g_ffn)


def _ffn_x_block(f, n_f, n_x):
    return jnp.clip(f - (n_f - n_x), 0, n_x - 1)


def _ffn_kernel(x_ref, h_ref, mod_ref, gpost_ref, wg_ref, wu_ref, wd_ref, o_ref, acc_ref, *, n_x, row_parts):
    f = pl.program_id(1)
    x_rows = x_ref.shape[0]
    tm = h_ref.shape[0]

    row0 = pl.multiple_of(_ffn_x_block(f, pl.num_programs(1), n_x) * x_rows, x_rows)
    o_ref[pl.ds(row0, x_rows), :] = x_ref[...]

    def reduce_step(first):
        for rows in _row_parts(tm, row_parts):
            h = h_ref[rows, :]
            gate = jnp.dot(h, pltpu.bitcast(wg_ref[...], BF16), preferred_element_type=F32)
            up = jnp.dot(h, pltpu.bitcast(wu_ref[...], BF16), preferred_element_type=F32)
            act = (jax.nn.silu(gate) * up).astype(BF16)
            down = jnp.dot(act, pltpu.bitcast(wd_ref[...], BF16), preferred_element_type=F32)
            acc_ref[rows, :] = down if first else acc_ref[rows, :] + down

    pl.when(f == 0)(functools.partial(reduce_step, True))
    pl.when(f > 0)(functools.partial(reduce_step, False))

    @pl.when(f == pl.num_programs(1) - 1)
    def _():
        scale = mod_ref[0, 5:6, :] * gpost_ref[...]

        def residual_rows(c, carry):
            rows = pl.ds(pl.multiple_of(c * SUBLANES, SUBLANES), SUBLANES)
            a = acc_ref[rows, :]
            r = lax.rsqrt(jnp.mean(a * a, axis=-1, keepdims=True) + EPS)
            o_ref[rows, :] = o_ref[rows, :] + (a * r) * scale
            return carry

        lax.fori_loop(0, tm // SUBLANES, residual_rows, 0, unroll=32)


def _ffn(x, h, mods, seq, g_post, w_gate, w_up, w_down, *, tm=1024, tf=512, x_rows=128, row_parts=1):
    m = x.shape[0]
    n_f = D_FF // tf
    n_x = tm // x_rows
    assert n_x <= n_f
    return pl.pallas_call(
        functools.partial(_ffn_kernel, n_x=n_x, row_parts=row_parts),
        grid=(m // tm, n_f),
        in_specs=[
            pl.BlockSpec((x_rows, D_MODEL), lambda i, f: (i * n_x + _ffn_x_block(f, n_f, n_x), 0)),
            pl.BlockSpec((tm, D_MODEL), lambda i, f: (i, 0)),
            pl.BlockSpec((1, N_MOD, D_MODEL), lambda i, f: (i * tm // seq, 0, 0)),
            pl.BlockSpec((1, D_MODEL), lambda i, f: (0, 0)),
            pl.BlockSpec((D_MODEL // 2, tf), lambda i, f: (0, f)),
            pl.BlockSpec((D_MODEL // 2, tf), lambda i, f: (0, f)),
            pl.BlockSpec((tf // 2, D_MODEL), lambda i, f: (f, 0)),
        ],
        out_specs=pl.BlockSpec((tm, D_MODEL), lambda i, f: (i, 0)),
        out_shape=jax.ShapeDtypeStruct((m, D_MODEL), F32),
        scratch_shapes=[pltpu.VMEM((tm, D_MODEL), F32)],
        compiler_params=_params(("parallel", "arbitrary"), 58),
        name="ffn",
    )(x, h, mods, g_post, w_gate, w_up, w_down)


def kernel(x_prompt, x_sample, cache_k, cache_v, c, c_ctx, w_ada, b_ada, norm_mix_pre, norm_mix_post,
           norm_ffn_pre, norm_ffn_post, w_in, rpb, ln_v, w_s, b_s, w_pa, w_pb, w_o, w_gate, w_up, w_down):
    assert w_ada.shape[0] == DEPTH == 1
    batch, seq, _ = x_prompt.shape
    dec_batch, dec_seq, _ = x_sample.shape
    past = cache_k.shape[2]

    row = lambda a: a[0].reshape(1, -1)
    bf = lambda a: a[0].astype(BF16)

    cvecs = jnp.concatenate(
        [c, c_ctx[None], jnp.zeros((MOD_ROWS - dec_batch - 1, D_MODEL), F32)], axis=0)
    mods = _modulation(cvecs, w_ada[0], b_ada[0])
    mods_lat = mods[:dec_batch].reshape(dec_batch, N_MOD, D_MODEL)
    mods_ctx = mods[dec_batch:dec_batch + 1].reshape(1, N_MOD, D_MODEL)
    table = _bias_table(rpb[0])

    w_in_f = w_in.reshape(D_MODEL, D_IN)
    mix_head = (bf(w_s), b_s[0].T)
    mix_tail = (row(norm_mix_post), row(norm_ffn_pre))

    xp = x_prompt.reshape(batch * seq, D_MODEL)
    xs = x_sample.reshape(dec_batch * dec_seq, D_MODEL)
    hp, proj_p, k_p, v_p, w_qkvu = _proj_qkvu(xp, mods_ctx, batch * seq, row(norm_mix_pre), w_in_f, is_ctx=True)
    hs, proj_s = _proj_qkvu(xs, mods_lat, dec_seq, row(norm_mix_pre), w_qkvu, is_ctx=False)
    gates = _proj_gates(hp, hs, row(ln_v), _sc_pack_bf16(w_in_f, mods, N_QKVU, D_IN - N_QKVU))

    mix_w = [_sc_pack_bf16(w[0], hp) for w in (w_pa, w_pb, w_o)]
    ffn = (row(norm_ffn_post), *[_sc_pack_bf16(w[0], hp) for w in (w_gate, w_up, w_down)])

    oa_p = _ctx_attention(proj_p, seq)
    xp, hp = _merge(xp, mods_ctx, batch * seq, oa_p, proj_p, gates, 0, *mix_head, *mix_w, *mix_tail)
    y_prompt = _ffn(xp, hp, mods_ctx, batch * seq, *ffn).reshape(batch, seq, D_MODEL)

    oa_s = _lat_attention(proj_s, cache_k.reshape(dec_batch, past * N_HEADS_A, HEAD_DIM),
                          cache_v.reshape(dec_batch, past * N_HEADS_A, HEAD_DIM), table, dec_seq)
    xs, hs = _merge(xs, mods_lat, dec_seq, oa_s, proj_s, gates, batch * seq, *mix_head, *mix_w, *mix_tail)
    y_sample = _ffn(xs, hs, mods_lat, dec_seq, *ffn).reshape(dec_batch, dec_seq, D_MODEL)

    state_shape = (batch, DEPTH, seq, N_HEADS_A, HEAD_DIM)
    return y_prompt, y_sample, k_p.reshape(state_shape), v_p.reshape(state_shape)
```

```python
import functools

import jax
import jax.numpy as jnp
from jax import lax
from jax.experimental import pallas as pl
from jax.experimental.pallas import tpu as pltpu
from jax.experimental.pallas import tpu_sc as plsc

D_MODEL = 2048
DEPTH = 1
GRID_W = 64
N_HEADS_A = 8
HEAD_DIM = 128
D_ATTN = N_HEADS_A * HEAD_DIM
KH_MAX = 8
KW = 16
CHUNK = 128
N_GROUPS_B = 8
D_GMLP = 1024
GROUP_CH = D_GMLP // N_GROUPS_B
D_FF = ((8 * D_MODEL // 3 + 255) // 256) * 256
N_MOD = 6
EPS = 1e-6
ATTN_SCALE = HEAD_DIM ** -0.5
LOG2E = 1.4426950408889634
D_IN = 3 * D_ATTN + 2 * D_GMLP + 2 * D_MODEL

N_DR = 2 * KH_MAX - 1
N_DC = 2 * KW - 1
COL_BLOCK = 1024
SUBLANES = 8
MOD_ROWS = SUBLANES

F32 = jnp.float32
BF16 = jnp.bfloat16

MIB = 1024 * 1024


def _params(semantics, vmem_mib):
    return pltpu.CompilerParams(dimension_semantics=semantics, vmem_limit_bytes=vmem_mib * MIB)


def _rms_norm(x, g):
    return x * lax.rsqrt(jnp.mean(x * x, axis=-1, keepdims=True) + EPS) * g


def _layer_norm(x, g):
    xc = x - jnp.mean(x, axis=-1, keepdims=True)
    return xc * lax.rsqrt(jnp.mean(xc * xc, axis=-1, keepdims=True) + EPS) * g


def _modulation_kernel(c_ref, w_ref, b_ref, o_ref):
    s = jax.nn.silu(c_ref[...]).astype(BF16)
    o_ref[...] = jnp.dot(s, w_ref[...].astype(BF16), preferred_element_type=F32) + b_ref[...]


def _modulation(cvecs, w_ada, b_ada):
    tn = 1024
    n = N_MOD * D_MODEL
    return pl.pallas_call(
        _modulation_kernel,
        grid=(n // tn,),
        in_specs=[
            pl.BlockSpec((MOD_ROWS, D_MODEL), lambda j: (0, 0)),
            pl.BlockSpec((D_MODEL, tn), lambda j: (0, j)),
            pl.BlockSpec((1, tn), lambda j: (0, j)),
        ],
        out_specs=pl.BlockSpec((MOD_ROWS, tn), lambda j: (0, j)),
        out_shape=jax.ShapeDtypeStruct((MOD_ROWS, n), F32),
        compiler_params=_params(("parallel",), 40),
        name="modulation",
    )(cvecs, w_ada, b_ada.reshape(1, n))


def _bias_table_kernel(rpb_ref, o_ref):
    qc = lax.broadcasted_iota(jnp.int32, (GRID_W, GRID_W), 0)
    kc = lax.broadcasted_iota(jnp.int32, (GRID_W, GRID_W), 1)
    cs = jnp.clip(qc - KW // 2, 0, GRID_W - KW)
    valid = (kc >= cs) & (kc < cs + KW)
    lanes = rpb_ref.shape[-1]
    tiles = []
    for dr in range(N_DR):
        row = jnp.broadcast_to(rpb_ref[0, dr:dr + 1, :], (GRID_W, lanes))
        t = pltpu.roll(row, lanes - (KW - 1), 1, stride=1, stride_axis=0)[:, :GRID_W]
        tiles.append(jnp.where(valid, t * LOG2E, -jnp.inf))
    pad = jnp.zeros((GRID_W, GRID_W), F32)
    o_ref[0, 0] = jnp.concatenate(tiles + [pad], axis=-1)
    o_ref[0, 1] = jnp.concatenate([pad] + tiles, axis=-1)


def _bias_table(rpb):
    width = (N_DR + 1) * GRID_W
    lanes = 128
    rpb_rows = jnp.pad(rpb, ((0, 0), (0, 0), (0, lanes - N_DC)))
    return pl.pallas_call(
        _bias_table_kernel,
        grid=(N_HEADS_A,),
        in_specs=[pl.BlockSpec((1, N_DR, lanes), lambda h: (h, 0, 0))],
        out_specs=pl.BlockSpec((1, 2, GRID_W, width), lambda h: (h, 0, 0, 0)),
        out_shape=jax.ShapeDtypeStruct((N_HEADS_A, 2, GRID_W, width), F32),
        compiler_params=_params(("parallel",), 16),
        name="bias_table",
    )(rpb_rows)


SC_LANES = 16
SC_BLOCK = (32, 512)
SC_UNROLL = 8


def _bf16_bits(u):
    return lax.shift_right_logical(u + 0x7FFF + (lax.shift_right_logical(u, 16) & 1), 16)


def _sc_pack_bf16(w, after, col0=0, cols=None):
    rows = w.shape[0]
    cols = w.shape[1] if cols is None else cols
    blk_r, blk_c = SC_BLOCK
    assert rows % blk_r == 0 and cols % blk_c == 0 and col0 % blk_c == 0 and blk_c % (SC_UNROLL * SC_LANES) == 0
    j0 = col0 // blk_c
    mesh = plsc.VectorSubcoreMesh(core_axis_name="core", subcore_axis_name="subcore")

    def body(in_vmem, out_vmem):
        in_vmem = in_vmem.bitcast(jnp.int32)

        @pl.loop(0, blk_r // 2)
        def _(r):
            @pl.loop(0, blk_c, step=SC_UNROLL * SC_LANES)
            def _(c0):
                for k in range(SC_UNROLL):
                    lanes = pl.ds(c0 + k * SC_LANES, SC_LANES)
                    lo = in_vmem[2 * r, lanes]
                    hi = in_vmem[2 * r + 1, lanes]
                    out_vmem[r, lanes] = _bf16_bits(lo) | lax.shift_left(_bf16_bits(hi), 16)

    @pl.kernel(out_type=jax.ShapeDtypeStruct((rows // 2, cols), jnp.int32), mesh=mesh, scratch_types=[],
               compiler_params=pltpu.CompilerParams(use_tc_tiling_on_sc=True))
    def pack_kernel(w_hbm, after_hbm, o_hbm):
        del after_hbm
        pltpu.emit_pipeline(
            body,
            grid=(rows // blk_r, cols // blk_c),
            in_specs=[pl.BlockSpec((blk_r, blk_c), lambda i, j: (i, j0 + j))],
            out_specs=[pl.BlockSpec((blk_r // 2, blk_c), lambda i, j: (i, j))],
            core_axis_name=("core", "subcore"),
            dimension_semantics=(pltpu.PARALLEL, pltpu.PARALLEL),
        )(w_hbm, o_hbm)

    return pack_kernel(w, after)


def _sc_pack_cache(cache, after):
    rows, dh = cache.shape
    pair = 2 * N_HEADS_A
    blk = 16 * pair
    assert rows % blk == 0 and dh % SC_LANES == 0
    mesh = plsc.VectorSubcoreMesh(core_axis_name="core", subcore_axis_name="subcore")

    def body(in_vmem, out_vmem):
        in_vmem = in_vmem.bitcast(jnp.int32)

        @pl.loop(0, blk // pair)
        def _(g):
            for head in range(N_HEADS_A):
                for k in range(dh // SC_LANES):
                    lanes = pl.ds(k * SC_LANES, SC_LANES)
                    lo = in_vmem[g * pair + head, lanes]
                    hi = in_vmem[g * pair + N_HEADS_A + head, lanes]
                    out_vmem[g * N_HEADS_A + head, lanes] = _bf16_bits(lo) | lax.shift_left(_bf16_bits(hi), 16)

    @pl.kernel(out_type=jax.ShapeDtypeStruct((rows // 2, dh), jnp.int32), mesh=mesh, scratch_types=[],
               compiler_params=pltpu.CompilerParams(use_tc_tiling_on_sc=True))
    def pack_kernel(c_hbm, after_hbm, o_hbm):
        del after_hbm
        pltpu.emit_pipeline(
            body,
            grid=(rows // blk, 1),
            in_specs=[pl.BlockSpec((blk, dh), lambda i, j: (i, j))],
            out_specs=[pl.BlockSpec((blk // 2, dh), lambda i, j: (i, j))],
            core_axis_name=("core", "subcore"),
            dimension_semantics=(pltpu.PARALLEL, pltpu.PARALLEL),
        )(c_hbm, o_hbm)

    return pack_kernel(cache, after)


LOAD_ROWS = 256
N_QKVU = 3 * D_ATTN + D_GMLP


def _row_parts(tm, parts):
    return [slice(p * (tm // parts), (p + 1) * (tm // parts)) for p in range(parts)]


def _load_cast_weights(w_hbm, col0, w_res, stage, sems):
    rows, width = w_res.shape
    chunk = stage.shape[1]
    n_chunks = rows // chunk

    def copy(c):
        src = w_hbm.at[pl.ds(c * chunk, chunk), pl.ds(col0, width)]
        return pltpu.make_async_copy(src, stage.at[c % 2], sems.at[c % 2])

    copy(0).start()
    for c in range(n_chunks):
        if c + 1 < n_chunks:
            copy(c + 1).start()
        copy(c).wait()
        w_res[c * chunk:(c + 1) * chunk, :] = stage[c % 2].astype(BF16)


def _proj_qkvu_kernel(x_ref, mod_ref, g_ref, w_ref, h_ref, proj_ref, *rest, is_ctx):
    if is_ctx:
        k_ref, v_ref, wpub_ref, w_res, stage, sems = rest
        step = pl.program_id(0)
        publish = pltpu.make_async_copy(w_res, wpub_ref, sems.at[2])

        @pl.when(step == 0)
        def _():
            _load_cast_weights(w_ref, 0, w_res, stage, sems)
            publish.start()
    else:
        w_res = w_ref
    y = _rms_norm(x_ref[...], g_ref[...])
    h = (y * (1.0 + mod_ref[0, 1:2, :]) + mod_ref[0, 0:1, :]).astype(BF16)
    h_ref[...] = h
    for blk in (3, 0, 1, 2):
        cols = slice(blk * COL_BLOCK, (blk + 1) * COL_BLOCK)
        acc = jnp.dot(h, w_res[:, cols], preferred_element_type=F32)
        if is_ctx and blk in (1, 2):
            kv_ref = (k_ref, v_ref)[blk - 1]
            for head in range(N_HEADS_A):
                dst = pl.ds(head, x_ref.shape[0], stride=N_HEADS_A)
                kv_ref[dst, :] = acc[:, head * HEAD_DIM:(head + 1) * HEAD_DIM]
        if blk == 3:
            acc = jax.nn.gelu(acc)
        proj_ref[:, cols] = acc.astype(BF16)
    if is_ctx:
        @pl.when(step == pl.num_programs(0) - 1)
        def _():
            publish.wait()


def _proj_qkvu(x, mods, seq, g, w, *, is_ctx, tm=512):
    m = x.shape[0]
    out_shape = [jax.ShapeDtypeStruct((m, D_MODEL), BF16), jax.ShapeDtypeStruct((m, N_QKVU), BF16)]
    out_specs = [pl.BlockSpec((tm, D_MODEL), lambda i: (i, 0)), pl.BlockSpec((tm, N_QKVU), lambda i: (i, 0))]
    scratch = []
    if is_ctx:
        out_shape += [jax.ShapeDtypeStruct((m * N_HEADS_A, HEAD_DIM), F32)] * 2
        out_specs += [pl.BlockSpec((tm * N_HEADS_A, HEAD_DIM), lambda i: (i, 0))] * 2
        out_shape += [jax.ShapeDtypeStruct((D_MODEL, N_QKVU), BF16)]
        out_specs += [pl.BlockSpec(memory_space=pl.ANY)]
        w_spec = pl.BlockSpec(memory_space=pl.ANY)
        scratch = [pltpu.VMEM((D_MODEL, N_QKVU), BF16), pltpu.VMEM((2, LOAD_ROWS, N_QKVU), F32),
                   pltpu.SemaphoreType.DMA((3,))]
    else:
        w_spec = pl.BlockSpec((D_MODEL, N_QKVU), lambda i: (0, 0), pipeline_mode=pl.Buffered(1))
    return pl.pallas_call(
        functools.partial(_proj_qkvu_kernel, is_ctx=is_ctx),
        grid=(m // tm,),
        in_specs=[
            pl.BlockSpec((tm, D_MODEL), lambda i: (i, 0)),
            pl.BlockSpec((1, N_MOD, D_MODEL), lambda i: (i * tm // seq, 0, 0)),
            pl.BlockSpec((1, D_MODEL), lambda i: (0, 0)),
            w_spec,
        ],
        out_specs=out_specs,
        out_shape=out_shape,
        scratch_shapes=scratch,
        compiler_params=_params(("arbitrary",), 58),
        name="proj_qkvu_ctx" if is_ctx else "proj_qkvu_lat",
    )(x, mods, g, w)


def _proj_gates_kernel(hp_ref, hs_ref, lnv_ref, w_ref, proj_ref, *, n_ctx, row_parts):
    step = pl.program_id(0)
    w_res = pltpu.bitcast(w_ref[...], BF16)
    n_gate = w_res.shape[1] - D_GMLP
    for rows in _row_parts(hp_ref.shape[0], row_parts):
        h = jnp.where(step < n_ctx, hp_ref[rows, :], hs_ref[rows, :])
        acc = jnp.dot(h, w_res[:, :D_GMLP], preferred_element_type=F32)
        proj_ref[rows, n_gate:] = _layer_norm(jax.nn.gelu(acc), lnv_ref[...]).astype(BF16)
        for blk in range(n_gate // COL_BLOCK):
            cols = slice(blk * COL_BLOCK, (blk + 1) * COL_BLOCK)
            acc = jnp.dot(h, w_res[:, D_GMLP + cols.start:D_GMLP + cols.stop], preferred_element_type=F32)
            proj_ref[rows, cols] = jax.nn.sigmoid(acc).astype(BF16)


def _proj_gates(h_ctx, h_lat, ln_v, w, *, tm=512, row_parts=2):
    n_ctx, n_lat = h_ctx.shape[0] // tm, h_lat.shape[0] // tm
    n = w.shape[1]
    return pl.pallas_call(
        functools.partial(_proj_gates_kernel, n_ctx=n_ctx, row_parts=row_parts),
        grid=(n_ctx + n_lat,),
        in_specs=[
            pl.BlockSpec((tm, D_MODEL), lambda i: (jnp.minimum(i, n_ctx - 1), 0)),
            pl.BlockSpec((tm, D_MODEL), lambda i: (jnp.maximum(i - n_ctx, 0), 0)),
            pl.BlockSpec((1, D_GMLP), lambda i: (0, 0)),
            pl.BlockSpec(w.shape, lambda i: (0, 0), pipeline_mode=pl.Buffered(1)),
        ],
        out_specs=pl.BlockSpec((tm, n), lambda i: (i, 0)),
        out_shape=jax.ShapeDtypeStruct(((n_ctx + n_lat) * tm, n), BF16),
        compiler_params=_params(("parallel",), 56),
        name="proj_gates",
    )(h_ctx, h_lat, ln_v, w)


def _softmax_parts(logits2):
    m = functools.reduce(jnp.maximum, [jnp.max(t, axis=-1, keepdims=True) for t in logits2])
    es = [jnp.exp2(t - m) for t in logits2]
    inv = 1.0 / functools.reduce(jnp.add, [jnp.sum(e, axis=-1, keepdims=True) for e in es])
    return [e.astype(BF16) for e in es], inv


def _qk(q, k):
    return lax.dot_general(q, k, (((1,), (1,)), ((), ())), preferred_element_type=F32) * (ATTN_SCALE * LOG2E)


def _ctx_attn_kernel(q_ref, k_ref, v_ref, o_ref):
    for h in range(N_HEADS_A):
        cols = slice(h * HEAD_DIM, (h + 1) * HEAD_DIM)
        (p,), inv = _softmax_parts([_qk(q_ref[:, cols], k_ref[:, cols])])
        o_ref[:, cols] = (jnp.dot(p, v_ref[:, cols], preferred_element_type=F32) * inv).astype(BF16)


def _ctx_attention(proj, seq):
    m = proj.shape[0]
    spec = lambda col: pl.BlockSpec((seq, D_ATTN), lambda b: (b, col))
    return pl.pallas_call(
        _ctx_attn_kernel,
        grid=(m // seq,),
        in_specs=[spec(0), spec(1), spec(2)],
        out_specs=spec(0),
        out_shape=jax.ShapeDtypeStruct((m, D_ATTN), BF16),
        compiler_params=_params(("parallel",), 32),
        name="ctx_attention",
    )(proj, proj, proj)


def _lat_attn_kernel(q_ref, k_ref, v_ref, kc_ref, vc_ref, tab_ref, o_ref, sw_ref, pw_ref, *, rows, kh):
    heads = tab_ref.shape[0]
    pairs = kc_ref.shape[1] // N_HEADS_A
    win = kh * GRID_W
    starts = [min(max(r - kh // 2, 0), rows - kh) * GRID_W for r in range(rows)]
    for j in range(heads):
        head = pl.program_id(1) * heads + j
        cols = slice(j * HEAD_DIM, (j + 1) * HEAD_DIM)
        sw, pw = sw_ref.at[j % 2], pw_ref.at[j % 2]
        kc = pltpu.bitcast(kc_ref[0, pl.ds(head, pairs, stride=N_HEADS_A), :], BF16)
        vc = pltpu.bitcast(vc_ref[0, pl.ds(head, pairs, stride=N_HEADS_A), :], BF16)
        s_c = _qk(q_ref[:, cols], kc)
        for r, start in enumerate(starts):
            off = start // GRID_W - r + (KH_MAX - 1)
            lane0 = (off + off % 2) * GRID_W
            bias = tab_ref[j, off % 2, :, lane0:lane0 + win]
            q = q_ref[r * GRID_W:(r + 1) * GRID_W, cols]
            sw[r * GRID_W:(r + 1) * GRID_W, :] = _qk(q, k_ref[start:start + win, cols]) + bias
        (p_w, p_c), inv = _softmax_parts([sw[...], s_c])
        pw[...] = p_w
        o_c = jnp.dot(p_c, vc, preferred_element_type=F32)
        for r, start in enumerate(starts):
            q_rows = slice(r * GRID_W, (r + 1) * GRID_W)
            o_w = jnp.dot(pw[q_rows, :], v_ref[start:start + win, cols], preferred_element_type=F32)
            o_ref[q_rows, cols] = ((o_w + o_c[q_rows, :]) * inv[q_rows, :]).astype(BF16)


def _lat_attention(proj, cache_k, cache_v, table, seq, *, heads_per_step=4):
    m = proj.shape[0]
    rows = seq // GRID_W
    kh = min(KH_MAX, rows)
    n_hp = N_HEADS_A // heads_per_step
    width = heads_per_step * HEAD_DIM
    qkv = lambda part: pl.BlockSpec((seq, width), lambda b, hp: (b, part * n_hp + hp))
    cache = pl.BlockSpec((1,) + cache_k.shape[1:], lambda b, hp: (b, 0, 0))
    return pl.pallas_call(
        functools.partial(_lat_attn_kernel, rows=rows, kh=kh),
        grid=(m // seq, n_hp),
        in_specs=[qkv(0), qkv(1), qkv(2), cache, cache,
                  pl.BlockSpec((heads_per_step,) + table.shape[1:], lambda b, hp: (hp, 0, 0, 0))],
        out_specs=pl.BlockSpec((seq, width), lambda b, hp: (b, hp)),
        out_shape=jax.ShapeDtypeStruct((m, D_ATTN), BF16),
        scratch_shapes=[pltpu.VMEM((2, seq, kh * GRID_W), F32), pltpu.VMEM((2, seq, kh * GRID_W), BF16)],
        compiler_params=_params(("parallel", "arbitrary"), 48),
        name="lat_attention",
    )(proj, proj, proj, cache_k, cache_v, table)


def _merge_kernel(x_ref, mod_ref, oa_ref, gu_ref, gates_ref,
                  ws_ref, bst_ref, wpa_ref, wpb_ref, wo_ref, g_ref, gffn_ref, o_ref, h2_ref,
                  ob_ref, *, row_parts):
    w_pa, w_pb, w_o = (pltpu.bitcast(w[...], BF16) for w in (wpa_ref, wpb_ref, wo_ref))
    ga_ref = gates_ref.at[:, pl.ds(0, D_MODEL)]
    gb_ref = gates_ref.at[:, pl.ds(D_MODEL, D_MODEL)]
    vn_ref = gates_ref.at[:, pl.ds(2 * D_MODEL, D_GMLP)]
    for part in _row_parts(x_ref.shape[0], row_parts):
        for c in range(part.start // CHUNK, part.stop // CHUNK):
            rows = slice(c * CHUNK, (c + 1) * CHUNK)
            for g in range(N_GROUPS_B):
                cols = slice(g * GROUP_CH, (g + 1) * GROUP_CH)
                s = jnp.dot(ws_ref[g], vn_ref[rows, cols], preferred_element_type=F32) + bst_ref[:, g:g + 1]
                ob_ref[rows, cols] = (gu_ref[rows, cols].astype(F32) * s).astype(BF16)
        pa = jnp.dot(oa_ref[part, :], w_pa, preferred_element_type=F32)
        pb = jnp.dot(ob_ref[part, :], w_pb, preferred_element_type=F32)
        mixed = (ga_ref[part, :].astype(F32) * pa + gb_ref[part, :].astype(F32) * pb).astype(BF16)
        y = jnp.dot(mixed, w_o, preferred_element_type=F32)
        x1 = x_ref[part, :] + mod_ref[0, 2:3, :] * _rms_norm(y, g_ref[...])
        o_ref[part, :] = x1
        h2 = _rms_norm(x1, gffn_ref[...]) * (1.0 + mod_ref[0, 4:5, :]) + mod_ref[0, 3:4, :]
        h2_ref[part, :] = h2.astype(BF16)


def _merge(x, mods, seq, o_a, proj_qkvu, proj_gates, gates_row0, w_s, b_s_t, w_pa, w_pb, w_o, g, g_ffn, *,
           tm=512, row_parts=2):
    m = x.shape[0]
    assert (tm // row_parts) % CHUNK == 0 and gates_row0 % tm == 0
    g0 = gates_row0 // tm
    whole = lambda a: pl.BlockSpec(a.shape, lambda i: (0,) * a.ndim, pipeline_mode=pl.Buffered(1))
    return pl.pallas_call(
        functools.partial(_merge_kernel, row_parts=row_parts),
        grid=(m // tm,),
        in_specs=[
            pl.BlockSpec((tm, D_MODEL), lambda i: (i, 0)),
            pl.BlockSpec((1, N_MOD, D_MODEL), lambda i: (i * tm // seq, 0, 0)),
            pl.BlockSpec((tm, D_ATTN), lambda i: (i, 0)),
            pl.BlockSpec((tm, D_GMLP), lambda i: (i, 3)),
            pl.BlockSpec((tm, proj_gates.shape[1]), lambda i: (g0 + i, 0)),
            whole(w_s), whole(b_s_t), whole(w_pa), whole(w_pb), whole(w_o), whole(g), whole(g_ffn),
        ],
        out_specs=[pl.BlockSpec((tm, D_MODEL), lambda i: (i, 0))] * 2,
        out_shape=[jax.ShapeDtypeStruct((m, D_MODEL), F32), jax.ShapeDtypeStruct((m, D_MODEL), BF16)],
        scratch_shapes=[pltpu.VMEM((tm, D_GMLP), BF16)],
        compiler_params=_params(("parallel",), 60),
        name="merge",
    )(x, mods, o_a, proj_qkvu, proj_gates,
      w_s, b_s_t, w_pa, w_pb, w_o, g, g_ffn)


def _ffn_x_block(f, n_f, n_x):
    return jnp.clip(f - (n_f - n_x), 0, n_x - 1)


def _ffn_kernel(x_ref, h_ref, mod_ref, gpost_ref, wg_ref, wu_ref, wd_ref, o_ref, acc_ref, *, n_x, row_parts):
    f = pl.program_id(1)
    x_rows = x_ref.shape[0]
    tm = h_ref.shape[0]

    row0 = pl.multiple_of(_ffn_x_block(f, pl.num_programs(1), n_x) * x_rows, x_rows)
    o_ref[pl.ds(row0, x_rows), :] = x_ref[...]

    def reduce_step(first):
        for rows in _row_parts(tm, row_parts):
            h = h_ref[rows, :]
            gate = jnp.dot(h, pltpu.bitcast(wg_ref[...], BF16), preferred_element_type=F32)
            up = jnp.dot(h, pltpu.bitcast(wu_ref[...], BF16), preferred_element_type=F32)
            act = (jax.nn.silu(gate) * up).astype(BF16)
            down = jnp.dot(act, pltpu.bitcast(wd_ref[...], BF16), preferred_element_type=F32)
            acc_ref[rows, :] = down if first else acc_ref[rows, :] + down

    pl.when(f == 0)(functools.partial(reduce_step, True))
    pl.when(f > 0)(functools.partial(reduce_step, False))

    @pl.when(f == pl.num_programs(1) - 1)
    def _():
        scale = mod_ref[0, 5:6, :] * gpost_ref[...]

        def residual_rows(c, carry):
            rows = pl.ds(pl.multiple_of(c * SUBLANES, SUBLANES), SUBLANES)
            a = acc_ref[rows, :]
            r = lax.rsqrt(jnp.mean(a * a, axis=-1, keepdims=True) + EPS)
            o_ref[rows, :] = o_ref[rows, :] + (a * r) * scale
            return carry

        lax.fori_loop(0, tm // SUBLANES, residual_rows, 0, unroll=32)


def _ffn(x, h, mods, seq, g_post, w_gate, w_up, w_down, *, tm=1024, tf=512, x_rows=128, row_parts=1):
    m = x.shape[0]
    n_f = D_FF // tf
    n_x = tm // x_rows
    assert n_x <= n_f
    return pl.pallas_call(
        functools.partial(_ffn_kernel, n_x=n_x, row_parts=row_parts),
        grid=(m // tm, n_f),
        in_specs=[
            pl.BlockSpec((x_rows, D_MODEL), lambda i, f: (i * n_x + _ffn_x_block(f, n_f, n_x), 0)),
            pl.BlockSpec((tm, D_MODEL), lambda i, f: (i, 0)),
            pl.BlockSpec((1, N_MOD, D_MODEL), lambda i, f: (i * tm // seq, 0, 0)),
            pl.BlockSpec((1, D_MODEL), lambda i, f: (0, 0)),
            pl.BlockSpec((D_MODEL // 2, tf), lambda i, f: (0, f)),
            pl.BlockSpec((D_MODEL // 2, tf), lambda i, f: (0, f)),
            pl.BlockSpec((tf // 2, D_MODEL), lambda i, f: (f, 0)),
        ],
        out_specs=pl.BlockSpec((tm, D_MODEL), lambda i, f: (i, 0)),
        out_shape=jax.ShapeDtypeStruct((m, D_MODEL), F32),
        scratch_shapes=[pltpu.VMEM((tm, D_MODEL), F32)],
        compiler_params=_params(("parallel", "arbitrary"), 58),
        name="ffn",
    )(x, h, mods, g_post, w_gate, w_up, w_down)


def kernel(x_prompt, x_sample, cache_k, cache_v, c, c_ctx, w_ada, b_ada, norm_mix_pre, norm_mix_post,
           norm_ffn_pre, norm_ffn_post, w_in, rpb, ln_v, w_s, b_s, w_pa, w_pb, w_o, w_gate, w_up, w_down):
    assert w_ada.shape[0] == DEPTH == 1
    batch, seq, _ = x_prompt.shape
    dec_batch, dec_seq, _ = x_sample.shape
    past = cache_k.shape[2]

    row = lambda a: a[0].reshape(1, -1)
    bf = lambda a: a[0].astype(BF16)

    cvecs = jnp.concatenate(
        [c, c_ctx[None], jnp.zeros((MOD_ROWS - dec_batch - 1, D_MODEL), F32)], axis=0)
    mods = _modulation(cvecs, w_ada[0], b_ada[0])
    mods_lat = mods[:dec_batch].reshape(dec_batch, N_MOD, D_MODEL)
    mods_ctx = mods[dec_batch:dec_batch + 1].reshape(1, N_MOD, D_MODEL)
    table = _bias_table(rpb[0])

    w_in_f = w_in.reshape(D_MODEL, D_IN)
    mix_head = (bf(w_s), b_s[0].T)
    mix_tail = (row(norm_mix_post), row(norm_ffn_pre))

    xp = x_prompt.reshape(batch * seq, D_MODEL)
    xs = x_sample.reshape(dec_batch * dec_seq, D_MODEL)
    hp, proj_p, k_p, v_p, w_qkvu = _proj_qkvu(xp, mods_ctx, batch * seq, row(norm_mix_pre), w_in_f, is_ctx=True)
    hs, proj_s = _proj_qkvu(xs, mods_lat, dec_seq, row(norm_mix_pre), w_qkvu, is_ctx=False)
    gates = _proj_gates(hp, hs, row(ln_v), _sc_pack_bf16(w_in_f, mods, N_QKVU, D_IN - N_QKVU))

    mix_w = [_sc_pack_bf16(w[0], hp) for w in (w_pa, w_pb, w_o)]
    ffn = (row(norm_ffn_post), *[_sc_pack_bf16(w[0], hp) for w in (w_gate, w_up, w_down)])

    oa_p = _ctx_attention(proj_p, seq)
    xp, hp = _merge(xp, mods_ctx, batch * seq, oa_p, proj_p, gates, 0, *mix_head, *mix_w, *mix_tail)
    y_prompt = _ffn(xp, hp, mods_ctx, batch * seq, *ffn).reshape(batch, seq, D_MODEL)

    cache_rows = dec_batch * past * N_HEADS_A
    packed_cache = [_sc_pack_cache(c.reshape(cache_rows, HEAD_DIM), mods).reshape(dec_batch, -1, HEAD_DIM)
                    for c in (cache_k, cache_v)]
    oa_s = _lat_attention(proj_s, *packed_cache, table, dec_seq)
    xs, hs = _merge(xs, mods_lat, dec_seq, oa_s, proj_s, gates, batch * seq, *mix_head, *mix_w, *mix_tail)
    y_sample = _ffn(xs, hs, mods_lat, dec_seq, *ffn).reshape(dec_batch, dec_seq, D_MODEL)

    state_shape = (batch, DEPTH, seq, N_HEADS_A, HEAD_DIM)
    return y_prompt, y_sample, k_p.reshape(state_shape), v_p.reshape(state_shape)
```

```python
import functools

import jax
import jax.numpy as jnp
from jax import lax
from jax.experimental import pallas as pl
from jax.experimental.pallas import tpu as pltpu
from jax.experimental.pallas import tpu_sc as plsc

D_MODEL = 2048
DEPTH = 1
GRID_W = 64
N_HEADS_A = 8
HEAD_DIM = 128
D_ATTN = N_HEADS_A * HEAD_DIM
KH_MAX = 8
KW = 16
CHUNK = 128
N_GROUPS_B = 8
D_GMLP = 1024
GROUP_CH = D_GMLP // N_GROUPS_B
D_FF = ((8 * D_MODEL // 3 + 255) // 256) * 256
N_MOD = 6
EPS = 1e-6
ATTN_SCALE = HEAD_DIM ** -0.5
LOG2E = 1.4426950408889634
D_IN = 3 * D_ATTN + 2 * D_GMLP + 2 * D_MODEL

N_DR = 2 * KH_MAX - 1
N_DC = 2 * KW - 1
COL_BLOCK = 1024
SUBLANES = 8
MOD_ROWS = SUBLANES

F32 = jnp.float32
BF16 = jnp.bfloat16

MIB = 1024 * 1024


def _params(semantics, vmem_mib):
    return pltpu.CompilerParams(dimension_semantics=semantics, vmem_limit_bytes=vmem_mib * MIB)


def _rms_norm(x, g):
    return x * lax.rsqrt(jnp.mean(x * x, axis=-1, keepdims=True) + EPS) * g


def _layer_norm(x, g):
    xc = x - jnp.mean(x, axis=-1, keepdims=True)
    return xc * lax.rsqrt(jnp.mean(xc * xc, axis=-1, keepdims=True) + EPS) * g


def _modulation_kernel(c_ref, w_ref, b_ref, o_ref):
    s = jax.nn.silu(c_ref[...]).astype(BF16)
    o_ref[...] = jnp.dot(s, w_ref[...].astype(BF16), preferred_element_type=F32) + b_ref[...]


def _modulation(cvecs, w_ada, b_ada):
    tn = 1024
    n = N_MOD * D_MODEL
    return pl.pallas_call(
        _modulation_kernel,
        grid=(n // tn,),
        in_specs=[
            pl.BlockSpec((MOD_ROWS, D_MODEL), lambda j: (0, 0)),
            pl.BlockSpec((D_MODEL, tn), lambda j: (0, j)),
            pl.BlockSpec((1, tn), lambda j: (0, j)),
        ],
        out_specs=pl.BlockSpec((MOD_ROWS, tn), lambda j: (0, j)),
        out_shape=jax.ShapeDtypeStruct((MOD_ROWS, n), F32),
        compiler_params=_params(("parallel",), 40),
        name="modulation",
    )(cvecs, w_ada, b_ada.reshape(1, n))


def _bias_table_kernel(rpb_ref, o_ref):
    qc = lax.broadcasted_iota(jnp.int32, (GRID_W, GRID_W), 0)
    kc = lax.broadcasted_iota(jnp.int32, (GRID_W, GRID_W), 1)
    cs = jnp.clip(qc - KW // 2, 0, GRID_W - KW)
    valid = (kc >= cs) & (kc < cs + KW)
    lanes = rpb_ref.shape[-1]
    tiles = []
    for dr in range(N_DR):
        row = jnp.broadcast_to(rpb_ref[0, dr:dr + 1, :], (GRID_W, lanes))
        t = pltpu.roll(row, lanes - (KW - 1), 1, stride=1, stride_axis=0)[:, :GRID_W]
        tiles.append(jnp.where(valid, t * LOG2E, -jnp.inf))
    pad = jnp.zeros((GRID_W, GRID_W), F32)
    o_ref[0, 0] = jnp.concatenate(tiles + [pad], axis=-1)
    o_ref[0, 1] = jnp.concatenate([pad] + tiles, axis=-1)


def _bias_table(rpb):
    width = (N_DR + 1) * GRID_W
    lanes = 128
    rpb_rows = jnp.pad(rpb, ((0, 0), (0, 0), (0, lanes - N_DC)))
    return pl.pallas_call(
        _bias_table_kernel,
        grid=(N_HEADS_A,),
        in_specs=[pl.BlockSpec((1, N_DR, lanes), lambda h: (h, 0, 0))],
        out_specs=pl.BlockSpec((1, 2, GRID_W, width), lambda h: (h, 0, 0, 0)),
        out_shape=jax.ShapeDtypeStruct((N_HEADS_A, 2, GRID_W, width), F32),
        compiler_params=_params(("parallel",), 16),
        name="bias_table",
    )(rpb_rows)


SC_LANES = 16
SC_BLOCK = (32, 512)
SC_UNROLL = 8


def _bf16_bits(u):
    return lax.shift_right_logical(u + 0x7FFF + (lax.shift_right_logical(u, 16) & 1), 16)


def _sc_pack_bf16(w, after, col0=0, cols=None):
    rows = w.shape[0]
    cols = w.shape[1] if cols is None else cols
    blk_r, blk_c = SC_BLOCK
    assert rows % blk_r == 0 and cols % blk_c == 0 and col0 % blk_c == 0 and blk_c % (SC_UNROLL * SC_LANES) == 0
    j0 = col0 // blk_c
    mesh = plsc.VectorSubcoreMesh(core_axis_name="core", subcore_axis_name="subcore")

    def body(in_vmem, out_vmem):
        in_vmem = in_vmem.bitcast(jnp.int32)

        @pl.loop(0, blk_r // 2)
        def _(r):
            @pl.loop(0, blk_c, step=SC_UNROLL * SC_LANES)
            def _(c0):
                for k in range(SC_UNROLL):
                    lanes = pl.ds(c0 + k * SC_LANES, SC_LANES)
                    lo = in_vmem[2 * r, lanes]
                    hi = in_vmem[2 * r + 1, lanes]
                    out_vmem[r, lanes] = _bf16_bits(lo) | lax.shift_left(_bf16_bits(hi), 16)

    @pl.kernel(out_type=jax.ShapeDtypeStruct((rows // 2, cols), jnp.int32), mesh=mesh, scratch_types=[],
               compiler_params=pltpu.CompilerParams(use_tc_tiling_on_sc=True))
    def pack_kernel(w_hbm, after_hbm, o_hbm):
        del after_hbm
        pltpu.emit_pipeline(
            body,
            grid=(rows // blk_r, cols // blk_c),
            in_specs=[pl.BlockSpec((blk_r, blk_c), lambda i, j: (i, j0 + j))],
            out_specs=[pl.BlockSpec((blk_r // 2, blk_c), lambda i, j: (i, j))],
            core_axis_name=("core", "subcore"),
            dimension_semantics=(pltpu.PARALLEL, pltpu.PARALLEL),
        )(w_hbm, o_hbm)

    return pack_kernel(w, after)


LOAD_ROWS = 256
N_QKVU = 3 * D_ATTN + D_GMLP


def _row_parts(tm, parts):
    return [slice(p * (tm // parts), (p + 1) * (tm // parts)) for p in range(parts)]


def _load_cast_weights(w_hbm, col0, w_res, stage, sems):
    rows, width = w_res.shape
    chunk = stage.shape[1]
    n_chunks = rows // chunk

    def copy(c):
        src = w_hbm.at[pl.ds(c * chunk, chunk), pl.ds(col0, width)]
        return pltpu.make_async_copy(src, stage.at[c % 2], sems.at[c % 2])

    copy(0).start()
    for c in range(n_chunks):
        if c + 1 < n_chunks:
            copy(c + 1).start()
        copy(c).wait()
        w_res[c * chunk:(c + 1) * chunk, :] = stage[c % 2].astype(BF16)


def _proj_qkvu_kernel(x_ref, mod_ref, g_ref, w_ref, h_ref, proj_ref, *rest, is_ctx):
    if is_ctx:
        k_ref, v_ref, wpub_ref, w_res, stage, sems = rest
        step = pl.program_id(0)
        publish = pltpu.make_async_copy(w_res, wpub_ref, sems.at[2])

        @pl.when(step == 0)
        def _():
            _load_cast_weights(w_ref, 0, w_res, stage, sems)
            publish.start()
    else:
        w_res = w_ref
    y = _rms_norm(x_ref[...], g_ref[...])
    h = (y * (1.0 + mod_ref[0, 1:2, :]) + mod_ref[0, 0:1, :]).astype(BF16)
    h_ref[...] = h
    for blk in (3, 0, 1, 2):
        cols = slice(blk * COL_BLOCK, (blk + 1) * COL_BLOCK)
        acc = jnp.dot(h, w_res[:, cols], preferred_element_type=F32)
        if is_ctx and blk in (1, 2):
            kv_ref = (k_ref, v_ref)[blk - 1]
            for head in range(N_HEADS_A):
                dst = pl.ds(head, x_ref.shape[0], stride=N_HEADS_A)
                kv_ref[dst, :] = acc[:, head * HEAD_DIM:(head + 1) * HEAD_DIM]
        if blk == 3:
            acc = jax.nn.gelu(acc)
        proj_ref[:, cols] = acc.astype(BF16)
    if is_ctx:
        @pl.when(step == pl.num_programs(0) - 1)
        def _():
            publish.wait()


def _proj_qkvu(x, mods, seq, g, w, *, is_ctx, tm=512):
    m = x.shape[0]
    out_shape = [jax.ShapeDtypeStruct((m, D_MODEL), BF16), jax.ShapeDtypeStruct((m, N_QKVU), BF16)]
    out_specs = [pl.BlockSpec((tm, D_MODEL), lambda i: (i, 0)), pl.BlockSpec((tm, N_QKVU), lambda i: (i, 0))]
    scratch = []
    if is_ctx:
        out_shape += [jax.ShapeDtypeStruct((m * N_HEADS_A, HEAD_DIM), F32)] * 2
        out_specs += [pl.BlockSpec((tm * N_HEADS_A, HEAD_DIM), lambda i: (i, 0))] * 2
        out_shape += [jax.ShapeDtypeStruct((D_MODEL, N_QKVU), BF16)]
        out_specs += [pl.BlockSpec(memory_space=pl.ANY)]
        w_spec = pl.BlockSpec(memory_space=pl.ANY)
        scratch = [pltpu.VMEM((D_MODEL, N_QKVU), BF16), pltpu.VMEM((2, LOAD_ROWS, N_QKVU), F32),
                   pltpu.SemaphoreType.DMA((3,))]
    else:
        w_spec = pl.BlockSpec((D_MODEL, N_QKVU), lambda i: (0, 0), pipeline_mode=pl.Buffered(1))
    return pl.pallas_call(
        functools.partial(_proj_qkvu_kernel, is_ctx=is_ctx),
        grid=(m // tm,),
        in_specs=[
            pl.BlockSpec((tm, D_MODEL), lambda i: (i, 0)),
            pl.BlockSpec((1, N_MOD, D_MODEL), lambda i: (i * tm // seq, 0, 0)),
            pl.BlockSpec((1, D_MODEL), lambda i: (0, 0)),
            w_spec,
        ],
        out_specs=out_specs,
        out_shape=out_shape,
        scratch_shapes=scratch,
        compiler_params=_params(("arbitrary",), 58),
        name="proj_qkvu_ctx" if is_ctx else "proj_qkvu_lat",
    )(x, mods, g, w)


def _proj_gates_kernel(hp_ref, hs_ref, lnv_ref, w_ref, proj_ref, *, n_ctx, row_parts):
    step = pl.program_id(0)
    w_res = pltpu.bitcast(w_ref[...], BF16)
    n_gate = w_res.shape[1] - D_GMLP
    for rows in _row_parts(hp_ref.shape[0], row_parts):
        h = jnp.where(step < n_ctx, hp_ref[rows, :], hs_ref[rows, :])
        acc = jnp.dot(h, w_res[:, :D_GMLP], preferred_element_type=F32)
        proj_ref[rows, n_gate:] = _layer_norm(jax.nn.gelu(acc), lnv_ref[...]).astype(BF16)
        for blk in range(n_gate // COL_BLOCK):
            cols = slice(blk * COL_BLOCK, (blk + 1) * COL_BLOCK)
            acc = jnp.dot(h, w_res[:, D_GMLP + cols.start:D_GMLP + cols.stop], preferred_element_type=F32)
            proj_ref[rows, cols] = jax.nn.sigmoid(acc).astype(BF16)


def _proj_gates(h_ctx, h_lat, ln_v, w, *, tm=512, row_parts=2):
    n_ctx, n_lat = h_ctx.shape[0] // tm, h_lat.shape[0] // tm
    n = w.shape[1]
    return pl.pallas_call(
        functools.partial(_proj_gates_kernel, n_ctx=n_ctx, row_parts=row_parts),
        grid=(n_ctx + n_lat,),
        in_specs=[
            pl.BlockSpec((tm, D_MODEL), lambda i: (jnp.minimum(i, n_ctx - 1), 0)),
            pl.BlockSpec((tm, D_MODEL), lambda i: (jnp.maximum(i - n_ctx, 0), 0)),
            pl.BlockSpec((1, D_GMLP), lambda i: (0, 0)),
            pl.BlockSpec(w.shape, lambda i: (0, 0), pipeline_mode=pl.Buffered(1)),
        ],
        out_specs=pl.BlockSpec((tm, n), lambda i: (i, 0)),
        out_shape=jax.ShapeDtypeStruct(((n_ctx + n_lat) * tm, n), BF16),
        compiler_params=_params(("parallel",), 56),
        name="proj_gates",
    )(h_ctx, h_lat, ln_v, w)


def _softmax_parts(logits2):
    m = functools.reduce(jnp.maximum, [jnp.max(t, axis=-1, keepdims=True) for t in logits2])
    es = [jnp.exp2(t - m) for t in logits2]
    inv = 1.0 / functools.reduce(jnp.add, [jnp.sum(e, axis=-1, keepdims=True) for e in es])
    return [e.astype(BF16) for e in es], inv


def _qk(q, k):
    return lax.dot_general(q, k, (((1,), (1,)), ((), ())), preferred_element_type=F32) * (ATTN_SCALE * LOG2E)


def _ctx_attn_kernel(q_ref, k_ref, v_ref, o_ref):
    for h in range(N_HEADS_A):
        cols = slice(h * HEAD_DIM, (h + 1) * HEAD_DIM)
        (p,), inv = _softmax_parts([_qk(q_ref[:, cols], k_ref[:, cols])])
        o_ref[:, cols] = (jnp.dot(p, v_ref[:, cols], preferred_element_type=F32) * inv).astype(BF16)


def _ctx_attention(proj, seq):
    m = proj.shape[0]
    spec = lambda col: pl.BlockSpec((seq, D_ATTN), lambda b: (b, col))
    return pl.pallas_call(
        _ctx_attn_kernel,
        grid=(m // seq,),
        in_specs=[spec(0), spec(1), spec(2)],
        out_specs=spec(0),
        out_shape=jax.ShapeDtypeStruct((m, D_ATTN), BF16),
        compiler_params=_params(("parallel",), 32),
        name="ctx_attention",
    )(proj, proj, proj)


def _lat_attn_kernel(q_ref, k_ref, v_ref, kc_ref, vc_ref, tab_ref, o_ref, sw_ref, pw_ref, *, rows, kh):
    heads = tab_ref.shape[0]
    past = kc_ref.shape[1] // N_HEADS_A
    win = kh * GRID_W
    starts = [min(max(r - kh // 2, 0), rows - kh) * GRID_W for r in range(rows)]
    for j in range(heads):
        head = pl.program_id(1) * heads + j
        cols = slice(j * HEAD_DIM, (j + 1) * HEAD_DIM)
        sw, pw = sw_ref.at[j % 2], pw_ref.at[j % 2]
        kc = kc_ref[0, pl.ds(head, past, stride=N_HEADS_A), :].astype(BF16)
        vc = vc_ref[0, pl.ds(head, past, stride=N_HEADS_A), :].astype(BF16)
        s_c = _qk(q_ref[:, cols], kc)
        for r, start in enumerate(starts):
            off = start // GRID_W - r + (KH_MAX - 1)
            lane0 = (off + off % 2) * GRID_W
            bias = tab_ref[j, off % 2, :, lane0:lane0 + win]
            q = q_ref[r * GRID_W:(r + 1) * GRID_W, cols]
            sw[r * GRID_W:(r + 1) * GRID_W, :] = _qk(q, k_ref[start:start + win, cols]) + bias
        (p_w, p_c), inv = _softmax_parts([sw[...], s_c])
        pw[...] = p_w
        o_c = jnp.dot(p_c, vc, preferred_element_type=F32)
        for r, start in enumerate(starts):
            q_rows = slice(r * GRID_W, (r + 1) * GRID_W)
            o_w = jnp.dot(pw[q_rows, :], v_ref[start:start + win, cols], preferred_element_type=F32)
            o_ref[q_rows, cols] = ((o_w + o_c[q_rows, :]) * inv[q_rows, :]).astype(BF16)


def _lat_attention(proj, cache_k, cache_v, table, seq, *, heads_per_step=4):
    m = proj.shape[0]
    rows = seq // GRID_W
    kh = min(KH_MAX, rows)
    n_hp = N_HEADS_A // heads_per_step
    width = heads_per_step * HEAD_DIM
    qkv = lambda part: pl.BlockSpec((seq, width), lambda b, hp: (b, part * n_hp + hp))
    cache = pl.BlockSpec((1,) + cache_k.shape[1:], lambda b, hp: (b, 0, 0))
    return pl.pallas_call(
        functools.partial(_lat_attn_kernel, rows=rows, kh=kh),
        grid=(m // seq, n_hp),
        in_specs=[qkv(0), qkv(1), qkv(2), cache, cache,
                  pl.BlockSpec((heads_per_step,) + table.shape[1:], lambda b, hp: (hp, 0, 0, 0))],
        out_specs=pl.BlockSpec((seq, width), lambda b, hp: (b, hp)),
        out_shape=jax.ShapeDtypeStruct((m, D_ATTN), BF16),
        scratch_shapes=[pltpu.VMEM((2, seq, kh * GRID_W), F32), pltpu.VMEM((2, seq, kh * GRID_W), BF16)],
        compiler_params=_params(("parallel", "arbitrary"), 48),
        name="lat_attention",
    )(proj, proj, proj, cache_k, cache_v, table)


def _merge_kernel(x_ref, mod_ref, oa_ref, gu_ref, gates_ref,
                  ws_ref, bst_ref, wpa_ref, wpb_ref, wo_ref, g_ref, gffn_ref, o_ref, h2_ref,
                  ob_ref, *, row_parts):
    w_pa, w_pb, w_o = (pltpu.bitcast(w[...], BF16) for w in (wpa_ref, wpb_ref, wo_ref))
    ga_ref = gates_ref.at[:, pl.ds(0, D_MODEL)]
    gb_ref = gates_ref.at[:, pl.ds(D_MODEL, D_MODEL)]
    vn_ref = gates_ref.at[:, pl.ds(2 * D_MODEL, D_GMLP)]
    for part in _row_parts(x_ref.shape[0], row_parts):
        for c in range(part.start // CHUNK, part.stop // CHUNK):
            rows = slice(c * CHUNK, (c + 1) * CHUNK)
            for g in range(N_GROUPS_B):
                cols = slice(g * GROUP_CH, (g + 1) * GROUP_CH)
                s = jnp.dot(ws_ref[g], vn_ref[rows, cols], preferred_element_type=F32) + bst_ref[:, g:g + 1]
                ob_ref[rows, cols] = (gu_ref[rows, cols].astype(F32) * s).astype(BF16)
        pa = jnp.dot(oa_ref[part, :], w_pa, preferred_element_type=F32)
        pb = jnp.dot(ob_ref[part, :], w_pb, preferred_element_type=F32)
        mixed = (ga_ref[part, :].astype(F32) * pa + gb_ref[part, :].astype(F32) * pb).astype(BF16)
        y = jnp.dot(mixed, w_o, preferred_element_type=F32)
        x1 = x_ref[part, :] + mod_ref[0, 2:3, :] * _rms_norm(y, g_ref[...])
        o_ref[part, :] = x1
        h2 = _rms_norm(x1, gffn_ref[...]) * (1.0 + mod_ref[0, 4:5, :]) + mod_ref[0, 3:4, :]
        h2_ref[part, :] = h2.astype(BF16)


def _merge(x, mods, seq, o_a, proj_qkvu, proj_gates, gates_row0, w_s, b_s_t, w_pa, w_pb, w_o, g, g_ffn, *,
           tm=512, row_parts=2):
    m = x.shape[0]
    assert (tm // row_parts) % CHUNK == 0 and gates_row0 % tm == 0
    g0 = gates_row0 // tm
    whole = lambda a: pl.BlockSpec(a.shape, lambda i: (0,) * a.ndim, pipeline_mode=pl.Buffered(1))
    return pl.pallas_call(
        functools.partial(_merge_kernel, row_parts=row_parts),
        grid=(m // tm,),
        in_specs=[
            pl.BlockSpec((tm, D_MODEL), lambda i: (i, 0)),
            pl.BlockSpec((1, N_MOD, D_MODEL), lambda i: (i * tm // seq, 0, 0)),
            pl.BlockSpec((tm, D_ATTN), lambda i: (i, 0)),
            pl.BlockSpec((tm, D_GMLP), lambda i: (i, 3)),
            pl.BlockSpec((tm, proj_gates.shape[1]), lambda i: (g0 + i, 0)),
            whole(w_s), whole(b_s_t), whole(w_pa), whole(w_pb), whole(w_o), whole(g), whole(g_ffn),
        ],
        out_specs=[pl.BlockSpec((tm, D_MODEL), lambda i: (i, 0))] * 2,
        out_shape=[jax.ShapeDtypeStruct((m, D_MODEL), F32), jax.ShapeDtypeStruct((m, D_MODEL), BF16)],
        scratch_shapes=[pltpu.VMEM((tm, D_GMLP), BF16)],
        compiler_params=_params(("parallel",), 60),
        name="merge",
    )(x, mods, o_a, proj_qkvu, proj_gates,
      w_s, b_s_t, w_pa, w_pb, w_o, g, g_ffn)


def _ffn_x_block(f, n_f, n_x):
    return jnp.clip(f - (n_f - n_x), 0, n_x - 1)


def _ffn_kernel(x_ref, h_ref, mod_ref, gpost_ref, wg_ref, wu_ref, wd_ref, o_ref, acc_ref, *, n_x, row_parts):
    f = pl.program_id(1)
    x_rows = x_ref.shape[0]
    tm = h_ref.shape[0]

    row0 = pl.multiple_of(_ffn_x_block(f, pl.num_programs(1), n_x) * x_rows, x_rows)
    o_ref[pl.ds(row0, x_rows), :] = x_ref[...]

    def reduce_step(first):
        for rows in _row_parts(tm, row_parts):
            h = h_ref[rows, :]
            gate = jnp.dot(h, pltpu.bitcast(wg_ref[...], BF16), preferred_element_type=F32)
            up = jnp.dot(h, pltpu.bitcast(wu_ref[...], BF16), preferred_element_type=F32)
            act = (jax.nn.silu(gate) * up).astype(BF16)
            down = jnp.dot(act, pltpu.bitcast(wd_ref[...], BF16), preferred_element_type=F32)
            acc_ref[rows, :] = down if first else acc_ref[rows, :] + down

    pl.when(f == 0)(functools.partial(reduce_step, True))
    pl.when(f > 0)(functools.partial(reduce_step, False))

    @pl.when(f == pl.num_programs(1) - 1)
    def _():
        scale = mod_ref[0, 5:6, :] * gpost_ref[...]

        def residual_rows(c, carry):
            rows = pl.ds(pl.multiple_of(c * SUBLANES, SUBLANES), SUBLANES)
            a = acc_ref[rows, :]
            r = lax.rsqrt(jnp.mean(a * a, axis=-1, keepdims=True) + EPS)
            o_ref[rows, :] = o_ref[rows, :] + (a * r) * scale
            return carry

        lax.fori_loop(0, tm // SUBLANES, residual_rows, 0, unroll=32)


def _ffn(x, h, mods, seq, g_post, w_gate, w_up, w_down, *, tm=1024, tf=512, x_rows=128, row_parts=1):
    m = x.shape[0]
    n_f = D_FF // tf
    n_x = tm // x_rows
    assert n_x <= n_f
    return pl.pallas_call(
        functools.partial(_ffn_kernel, n_x=n_x, row_parts=row_parts),
        grid=(m // tm, n_f),
        in_specs=[
            pl.BlockSpec((x_rows, D_MODEL), lambda i, f: (i * n_x + _ffn_x_block(f, n_f, n_x), 0)),
            pl.BlockSpec((tm, D_MODEL), lambda i, f: (i, 0)),
            pl.BlockSpec((1, N_MOD, D_MODEL), lambda i, f: (i * tm // seq, 0, 0)),
            pl.BlockSpec((1, D_MODEL), lambda i, f: (0, 0)),
            pl.BlockSpec((D_MODEL // 2, tf), lambda i, f: (0, f)),
            pl.BlockSpec((D_MODEL // 2, tf), lambda i, f: (0, f)),
            pl.BlockSpec((tf // 2, D_MODEL), lambda i, f: (f, 0)),
        ],
        out_specs=pl.BlockSpec((tm, D_MODEL), lambda i, f: (i, 0)),
        out_shape=jax.ShapeDtypeStruct((m, D_MODEL), F32),
        scratch_shapes=[pltpu.VMEM((tm, D_MODEL), F32)],
        compiler_params=_params(("parallel", "arbitrary"), 58),
        name="ffn",
    )(x, h, mods, g_post, w_gate, w_up, w_down)


def kernel(x_prompt, x_sample, cache_k, cache_v, c, c_ctx, w_ada, b_ada, norm_mix_pre, norm_mix_post,
           norm_ffn_pre, norm_ffn_post, w_in, rpb, ln_v, w_s, b_s, w_pa, w_pb, w_o, w_gate, w_up, w_down):
    assert w_ada.shape[0] == DEPTH == 1
    batch, seq, _ = x_prompt.shape
    dec_batch, dec_seq, _ = x_sample.shape
    past = cache_k.shape[2]

    row = lambda a: a[0].reshape(1, -1)
    bf = lambda a: a[0].astype(BF16)

    cvecs = jnp.concatenate(
        [c, c_ctx[None], jnp.zeros((MOD_ROWS - dec_batch - 1, D_MODEL), F32)], axis=0)
    mods = _modulation(cvecs, w_ada[0], b_ada[0])
    mods_lat = mods[:dec_batch].reshape(dec_batch, N_MOD, D_MODEL)
    mods_ctx = mods[dec_batch:dec_batch + 1].reshape(1, N_MOD, D_MODEL)
    table = _bias_table(rpb[0])

    w_in_f = w_in.reshape(D_MODEL, D_IN)
    mix_head = (bf(w_s), b_s[0].T)
    mix_tail = (row(norm_mix_post), row(norm_ffn_pre))

    xp = x_prompt.reshape(batch * seq, D_MODEL)
    xs = x_sample.reshape(dec_batch * dec_seq, D_MODEL)
    hp, proj_p, k_p, v_p, w_qkvu = _proj_qkvu(xp, mods_ctx, batch * seq, row(norm_mix_pre), w_in_f, is_ctx=True)
    hs, proj_s = _proj_qkvu(xs, mods_lat, dec_seq, row(norm_mix_pre), w_qkvu, is_ctx=False)
    gates = _proj_gates(hp, hs, row(ln_v), _sc_pack_bf16(w_in_f, mods, N_QKVU, D_IN - N_QKVU))

    mix_w = [_sc_pack_bf16(w[0], hs) for w in (w_pa, w_pb, w_o)]
    ffn = (row(norm_ffn_post), *[_sc_pack_bf16(w[0], hs) for w in (w_gate, w_up, w_down)])

    oa_p = _ctx_attention(proj_p, seq)
    xp, hp = _merge(xp, mods_ctx, batch * seq, oa_p, proj_p, gates, 0, *mix_head, *mix_w, *mix_tail)
    y_prompt = _ffn(xp, hp, mods_ctx, batch * seq, *ffn).reshape(batch, seq, D_MODEL)

    oa_s = _lat_attention(proj_s, cache_k.reshape(dec_batch, past * N_HEADS_A, HEAD_DIM),
                          cache_v.reshape(dec_batch, past * N_HEADS_A, HEAD_DIM), table, dec_seq)
    xs, hs = _merge(xs, mods_lat, dec_seq, oa_s, proj_s, gates, batch * seq, *mix_head, *mix_w, *mix_tail)
    y_sample = _ffn(xs, hs, mods_lat, dec_seq, *ffn).reshape(dec_batch, dec_seq, D_MODEL)

    state_shape = (batch, DEPTH, seq, N_HEADS_A, HEAD_DIM)
    return y_prompt, y_sample, k_p.reshape(state_shape), v_p.reshape(state_shape)
```

```python
import functools

import jax
import jax.numpy as jnp
from jax import lax
from jax.experimental import pallas as pl
from jax.experimental.pallas import tpu as pltpu
from jax.experimental.pallas import tpu_sc as plsc

D_MODEL = 2048
DEPTH = 1
GRID_W = 64
N_HEADS_A = 8
HEAD_DIM = 128
D_ATTN = N_HEADS_A * HEAD_DIM
KH_MAX = 8
KW = 16
CHUNK = 128
N_GROUPS_B = 8
D_GMLP = 1024
GROUP_CH = D_GMLP // N_GROUPS_B
D_FF = ((8 * D_MODEL // 3 + 255) // 256) * 256
N_MOD = 6
EPS = 1e-6
ATTN_SCALE = HEAD_DIM ** -0.5
LOG2E = 1.4426950408889634
D_IN = 3 * D_ATTN + 2 * D_GMLP + 2 * D_MODEL

N_DR = 2 * KH_MAX - 1
N_DC = 2 * KW - 1
COL_BLOCK = 1024
SUBLANES = 8
MOD_ROWS = SUBLANES

F32 = jnp.float32
BF16 = jnp.bfloat16

MIB = 1024 * 1024


def _params(semantics, vmem_mib):
    return pltpu.CompilerParams(dimension_semantics=semantics, vmem_limit_bytes=vmem_mib * MIB)


def _rms_norm(x, g):
    return x * lax.rsqrt(jnp.mean(x * x, axis=-1, keepdims=True) + EPS) * g


def _layer_norm(x, g):
    xc = x - jnp.mean(x, axis=-1, keepdims=True)
    return xc * lax.rsqrt(jnp.mean(xc * xc, axis=-1, keepdims=True) + EPS) * g


def _modulation_kernel(c_ref, w_ref, b_ref, o_ref):
    s = jax.nn.silu(c_ref[...]).astype(BF16)
    o_ref[...] = jnp.dot(s, w_ref[...].astype(BF16), preferred_element_type=F32) + b_ref[...]


def _modulation(cvecs, w_ada, b_ada):
    tn = 1024
    n = N_MOD * D_MODEL
    return pl.pallas_call(
        _modulation_kernel,
        grid=(n // tn,),
        in_specs=[
            pl.BlockSpec((MOD_ROWS, D_MODEL), lambda j: (0, 0)),
            pl.BlockSpec((D_MODEL, tn), lambda j: (0, j)),
            pl.BlockSpec((1, tn), lambda j: (0, j)),
        ],
        out_specs=pl.BlockSpec((MOD_ROWS, tn), lambda j: (0, j)),
        out_shape=jax.ShapeDtypeStruct((MOD_ROWS, n), F32),
        compiler_params=_params(("parallel",), 40),
        name="modulation",
    )(cvecs, w_ada, b_ada.reshape(1, n))


def _bias_table_kernel(rpb_ref, o_ref):
    qc = lax.broadcasted_iota(jnp.int32, (GRID_W, GRID_W), 0)
    kc = lax.broadcasted_iota(jnp.int32, (GRID_W, GRID_W), 1)
    cs = jnp.clip(qc - KW // 2, 0, GRID_W - KW)
    valid = (kc >= cs) & (kc < cs + KW)
    lanes = rpb_ref.shape[-1]
    tiles = []
    for dr in range(N_DR):
        row = jnp.broadcast_to(rpb_ref[0, dr:dr + 1, :], (GRID_W, lanes))
        t = pltpu.roll(row, lanes - (KW - 1), 1, stride=1, stride_axis=0)[:, :GRID_W]
        tiles.append(jnp.where(valid, t * LOG2E, -jnp.inf))
    pad = jnp.zeros((GRID_W, GRID_W), F32)
    o_ref[0, 0] = jnp.concatenate(tiles + [pad], axis=-1)
    o_ref[0, 1] = jnp.concatenate([pad] + tiles, axis=-1)


def _bias_table(rpb):
    width = (N_DR + 1) * GRID_W
    lanes = 128
    rpb_rows = jnp.pad(rpb, ((0, 0), (0, 0), (0, lanes - N_DC)))
    return pl.pallas_call(
        _bias_table_kernel,
        grid=(N_HEADS_A,),
        in_specs=[pl.BlockSpec((1, N_DR, lanes), lambda h: (h, 0, 0))],
        out_specs=pl.BlockSpec((1, 2, GRID_W, width), lambda h: (h, 0, 0, 0)),
        out_shape=jax.ShapeDtypeStruct((N_HEADS_A, 2, GRID_W, width), F32),
        compiler_params=_params(("parallel",), 16),
        name="bias_table",
    )(rpb_rows)


SC_LANES = 16
SC_BLOCK = (32, 512)
SC_UNROLL = 8


def _bf16_bits(u):
    return lax.shift_right_logical(u + 0x7FFF + (lax.shift_right_logical(u, 16) & 1), 16)


def _sc_pack_bf16(w, after, col0=0, cols=None):
    rows = w.shape[0]
    cols = w.shape[1] if cols is None else cols
    blk_r, blk_c = SC_BLOCK
    assert rows % blk_r == 0 and cols % blk_c == 0 and col0 % blk_c == 0 and blk_c % (SC_UNROLL * SC_LANES) == 0
    j0 = col0 // blk_c
    mesh = plsc.VectorSubcoreMesh(core_axis_name="core", subcore_axis_name="subcore")

    def body(in_vmem, out_vmem):
        in_vmem = in_vmem.bitcast(jnp.int32)

        @pl.loop(0, blk_r // 2)
        def _(r):
            @pl.loop(0, blk_c, step=SC_UNROLL * SC_LANES)
            def _(c0):
                for k in range(SC_UNROLL):
                    lanes = pl.ds(c0 + k * SC_LANES, SC_LANES)
                    lo = in_vmem[2 * r, lanes]
                    hi = in_vmem[2 * r + 1, lanes]
                    out_vmem[r, lanes] = _bf16_bits(lo) | lax.shift_left(_bf16_bits(hi), 16)

    @pl.kernel(out_type=jax.ShapeDtypeStruct((rows // 2, cols), jnp.int32), mesh=mesh, scratch_types=[],
               compiler_params=pltpu.CompilerParams(use_tc_tiling_on_sc=True))
    def pack_kernel(w_hbm, after_hbm, o_hbm):
        del after_hbm
        pltpu.emit_pipeline(
            body,
            grid=(rows // blk_r, cols // blk_c),
            in_specs=[pl.BlockSpec((blk_r, blk_c), lambda i, j: (i, j0 + j))],
            out_specs=[pl.BlockSpec((blk_r // 2, blk_c), lambda i, j: (i, j))],
            core_axis_name=("core", "subcore"),
            dimension_semantics=(pltpu.PARALLEL, pltpu.PARALLEL),
        )(w_hbm, o_hbm)

    return pack_kernel(w, after)


LOAD_ROWS = 256
N_QKVU = 3 * D_ATTN + D_GMLP


def _row_parts(tm, parts):
    return [slice(p * (tm // parts), (p + 1) * (tm // parts)) for p in range(parts)]


def _load_cast_weights(w_hbm, col0, w_res, stage, sems):
    rows, width = w_res.shape
    chunk = stage.shape[1]
    n_chunks = rows // chunk

    def copy(c):
        src = w_hbm.at[pl.ds(c * chunk, chunk), pl.ds(col0, width)]
        return pltpu.make_async_copy(src, stage.at[c % 2], sems.at[c % 2])

    copy(0).start()
    for c in range(n_chunks):
        if c + 1 < n_chunks:
            copy(c + 1).start()
        copy(c).wait()
        w_res[c * chunk:(c + 1) * chunk, :] = stage[c % 2].astype(BF16)


def _proj_qkvu_kernel(x_ref, mod_ref, g_ref, w_ref, h_ref, proj_ref, *rest, is_ctx):
    if is_ctx:
        k_ref, v_ref, wpub_ref, w_res, stage, sems = rest
        step = pl.program_id(0)
        publish = pltpu.make_async_copy(w_res, wpub_ref, sems.at[2])

        @pl.when(step == 0)
        def _():
            _load_cast_weights(w_ref, 0, w_res, stage, sems)
            publish.start()
    else:
        w_res = w_ref
    y = _rms_norm(x_ref[...], g_ref[...])
    h = (y * (1.0 + mod_ref[0, 1:2, :]) + mod_ref[0, 0:1, :]).astype(BF16)
    h_ref[...] = h
    for blk in (3, 0, 1, 2):
        cols = slice(blk * COL_BLOCK, (blk + 1) * COL_BLOCK)
        acc = jnp.dot(h, w_res[:, cols], preferred_element_type=F32)
        if is_ctx and blk in (1, 2):
            kv_ref = (k_ref, v_ref)[blk - 1]
            for head in range(N_HEADS_A):
                dst = pl.ds(head, x_ref.shape[0], stride=N_HEADS_A)
                kv_ref[dst, :] = acc[:, head * HEAD_DIM:(head + 1) * HEAD_DIM]
        if blk == 3:
            acc = jax.nn.gelu(acc)
        proj_ref[:, cols] = acc.astype(BF16)
    if is_ctx:
        @pl.when(step == pl.num_programs(0) - 1)
        def _():
            publish.wait()


def _proj_qkvu(x, mods, seq, g, w, *, is_ctx, tm=512):
    m = x.shape[0]
    out_shape = [jax.ShapeDtypeStruct((m, D_MODEL), BF16), jax.ShapeDtypeStruct((m, N_QKVU), BF16)]
    out_specs = [pl.BlockSpec((tm, D_MODEL), lambda i: (i, 0)), pl.BlockSpec((tm, N_QKVU), lambda i: (i, 0))]
    scratch = []
    if is_ctx:
        out_shape += [jax.ShapeDtypeStruct((m * N_HEADS_A, HEAD_DIM), F32)] * 2
        out_specs += [pl.BlockSpec((tm * N_HEADS_A, HEAD_DIM), lambda i: (i, 0))] * 2
        out_shape += [jax.ShapeDtypeStruct((D_MODEL, N_QKVU), BF16)]
        out_specs += [pl.BlockSpec(memory_space=pl.ANY)]
        w_spec = pl.BlockSpec(memory_space=pl.ANY)
        scratch = [pltpu.VMEM((D_MODEL, N_QKVU), BF16), pltpu.VMEM((2, LOAD_ROWS, N_QKVU), F32),
                   pltpu.SemaphoreType.DMA((3,))]
    else:
        w_spec = pl.BlockSpec((D_MODEL, N_QKVU), lambda i: (0, 0), pipeline_mode=pl.Buffered(1))
    return pl.pallas_call(
        functools.partial(_proj_qkvu_kernel, is_ctx=is_ctx),
        grid=(m // tm,),
        in_specs=[
            pl.BlockSpec((tm, D_MODEL), lambda i: (i, 0)),
            pl.BlockSpec((1, N_MOD, D_MODEL), lambda i: (i * tm // seq, 0, 0)),
            pl.BlockSpec((1, D_MODEL), lambda i: (0, 0)),
            w_spec,
        ],
        out_specs=out_specs,
        out_shape=out_shape,
        scratch_shapes=scratch,
        compiler_params=_params(("arbitrary",), 58),
        name="proj_qkvu_ctx" if is_ctx else "proj_qkvu_lat",
    )(x, mods, g, w)


def _proj_gates_kernel(hp_ref, hs_ref, lnv_ref, w_ref, proj_ref, *, n_ctx, row_parts):
    step = pl.program_id(0)
    w_res = pltpu.bitcast(w_ref[...], BF16)
    n_gate = w_res.shape[1] - D_GMLP
    for rows in _row_parts(hp_ref.shape[0], row_parts):
        h = jnp.where(step < n_ctx, hp_ref[rows, :], hs_ref[rows, :])
        acc = jnp.dot(h, w_res[:, :D_GMLP], preferred_element_type=F32)
        proj_ref[rows, n_gate:] = _layer_norm(jax.nn.gelu(acc), lnv_ref[...]).astype(BF16)
        for blk in range(n_gate // COL_BLOCK):
            cols = slice(blk * COL_BLOCK, (blk + 1) * COL_BLOCK)
            acc = jnp.dot(h, w_res[:, D_GMLP + cols.start:D_GMLP + cols.stop], preferred_element_type=F32)
            proj_ref[rows, cols] = jax.nn.sigmoid(acc).astype(BF16)


def _proj_gates(h_ctx, h_lat, ln_v, w, *, tm=512, row_parts=2):
    n_ctx, n_lat = h_ctx.shape[0] // tm, h_lat.shape[0] // tm
    n = w.shape[1]
    return pl.pallas_call(
        functools.partial(_proj_gates_kernel, n_ctx=n_ctx, row_parts=row_parts),
        grid=(n_ctx + n_lat,),
        in_specs=[
            pl.BlockSpec((tm, D_MODEL), lambda i: (jnp.minimum(i, n_ctx - 1), 0)),
            pl.BlockSpec((tm, D_MODEL), lambda i: (jnp.maximum(i - n_ctx, 0), 0)),
            pl.BlockSpec((1, D_GMLP), lambda i: (0, 0)),
            pl.BlockSpec(w.shape, lambda i: (0, 0), pipeline_mode=pl.Buffered(1)),
        ],
        out_specs=pl.BlockSpec((tm, n), lambda i: (i, 0)),
        out_shape=jax.ShapeDtypeStruct(((n_ctx + n_lat) * tm, n), BF16),
        compiler_params=_params(("parallel",), 56),
        name="proj_gates",
    )(h_ctx, h_lat, ln_v, w)


def _softmax_parts(logits2):
    m = functools.reduce(jnp.maximum, [jnp.max(t, axis=-1, keepdims=True) for t in logits2])
    es = [jnp.exp2(t - m) for t in logits2]
    inv = 1.0 / functools.reduce(jnp.add, [jnp.sum(e, axis=-1, keepdims=True) for e in es])
    return [e.astype(BF16) for e in es], inv


def _qk(q, k):
    return lax.dot_general(q, k, (((1,), (1,)), ((), ())), preferred_element_type=F32) * (ATTN_SCALE * LOG2E)


def _ctx_attn_kernel(q_ref, k_ref, v_ref, o_ref):
    for h in range(N_HEADS_A):
        cols = slice(h * HEAD_DIM, (h + 1) * HEAD_DIM)
        (p,), inv = _softmax_parts([_qk(q_ref[:, cols], k_ref[:, cols])])
        o_ref[:, cols] = (jnp.dot(p, v_ref[:, cols], preferred_element_type=F32) * inv).astype(BF16)


def _ctx_attention(proj, seq):
    m = proj.shape[0]
    spec = lambda col: pl.BlockSpec((seq, D_ATTN), lambda b: (b, col))
    return pl.pallas_call(
        _ctx_attn_kernel,
        grid=(m // seq,),
        in_specs=[spec(0), spec(1), spec(2)],
        out_specs=spec(0),
        out_shape=jax.ShapeDtypeStruct((m, D_ATTN), BF16),
        compiler_params=_params(("parallel",), 56),
        name="ctx_attention",
    )(proj, proj, proj)


def _lat_attn_kernel(q_ref, k_ref, v_ref, kc_ref, vc_ref, tab_ref, o_ref, sw_ref, pw_ref, *, rows, kh):
    heads = tab_ref.shape[0]
    past = kc_ref.shape[1] // N_HEADS_A
    win = kh * GRID_W
    starts = [min(max(r - kh // 2, 0), rows - kh) * GRID_W for r in range(rows)]
    for j in range(heads):
        head = pl.program_id(1) * heads + j
        cols = slice(j * HEAD_DIM, (j + 1) * HEAD_DIM)
        sw, pw = sw_ref.at[j % 2], pw_ref.at[j % 2]
        kc = kc_ref[0, pl.ds(head, past, stride=N_HEADS_A), :].astype(BF16)
        vc = vc_ref[0, pl.ds(head, past, stride=N_HEADS_A), :].astype(BF16)
        s_c = _qk(q_ref[:, cols], kc)
        for r, start in enumerate(starts):
            off = start // GRID_W - r + (KH_MAX - 1)
            lane0 = (off + off % 2) * GRID_W
            bias = tab_ref[j, off % 2, :, lane0:lane0 + win]
            q = q_ref[r * GRID_W:(r + 1) * GRID_W, cols]
            sw[r * GRID_W:(r + 1) * GRID_W, :] = _qk(q, k_ref[start:start + win, cols]) + bias
        (p_w, p_c), inv = _softmax_parts([sw[...], s_c])
        pw[...] = p_w
        o_c = jnp.dot(p_c, vc, preferred_element_type=F32)
        for r, start in enumerate(starts):
            q_rows = slice(r * GRID_W, (r + 1) * GRID_W)
            o_w = jnp.dot(pw[q_rows, :], v_ref[start:start + win, cols], preferred_element_type=F32)
            o_ref[q_rows, cols] = ((o_w + o_c[q_rows, :]) * inv[q_rows, :]).astype(BF16)


def _lat_attention(proj, cache_k, cache_v, table, seq, *, heads_per_step=4):
    m = proj.shape[0]
    rows = seq // GRID_W
    kh = min(KH_MAX, rows)
    n_hp = N_HEADS_A // heads_per_step
    width = heads_per_step * HEAD_DIM
    qkv = lambda part: pl.BlockSpec((seq, width), lambda b, hp: (b, part * n_hp + hp))
    cache = pl.BlockSpec((1,) + cache_k.shape[1:], lambda b, hp: (b, 0, 0))
    return pl.pallas_call(
        functools.partial(_lat_attn_kernel, rows=rows, kh=kh),
        grid=(m // seq, n_hp),
        in_specs=[qkv(0), qkv(1), qkv(2), cache, cache,
                  pl.BlockSpec((heads_per_step,) + table.shape[1:], lambda b, hp: (hp, 0, 0, 0))],
        out_specs=pl.BlockSpec((seq, width), lambda b, hp: (b, hp)),
        out_shape=jax.ShapeDtypeStruct((m, D_ATTN), BF16),
        scratch_shapes=[pltpu.VMEM((2, seq, kh * GRID_W), F32), pltpu.VMEM((2, seq, kh * GRID_W), BF16)],
        compiler_params=_params(("parallel", "arbitrary"), 48),
        name="lat_attention",
    )(proj, proj, proj, cache_k, cache_v, table)


def _merge_kernel(x_ref, mod_ref, oa_ref, gu_ref, gates_ref,
                  ws_ref, bst_ref, wpa_ref, wpb_ref, wo_ref, g_ref, gffn_ref, o_ref, h2_ref,
                  ob_ref, *, row_parts):
    w_pa, w_pb, w_o = (pltpu.bitcast(w[...], BF16) for w in (wpa_ref, wpb_ref, wo_ref))
    ga_ref = gates_ref.at[:, pl.ds(0, D_MODEL)]
    gb_ref = gates_ref.at[:, pl.ds(D_MODEL, D_MODEL)]
    vn_ref = gates_ref.at[:, pl.ds(2 * D_MODEL, D_GMLP)]
    for part in _row_parts(x_ref.shape[0], row_parts):
        for c in range(part.start // CHUNK, part.stop // CHUNK):
            rows = slice(c * CHUNK, (c + 1) * CHUNK)
            for g in range(N_GROUPS_B):
                cols = slice(g * GROUP_CH, (g + 1) * GROUP_CH)
                s = jnp.dot(ws_ref[g], vn_ref[rows, cols], preferred_element_type=F32) + bst_ref[:, g:g + 1]
                ob_ref[rows, cols] = (gu_ref[rows, cols].astype(F32) * s).astype(BF16)
        pa = jnp.dot(oa_ref[part, :], w_pa, preferred_element_type=F32)
        pb = jnp.dot(ob_ref[part, :], w_pb, preferred_element_type=F32)
        mixed = (ga_ref[part, :].astype(F32) * pa + gb_ref[part, :].astype(F32) * pb).astype(BF16)
        y = jnp.dot(mixed, w_o, preferred_element_type=F32)
        x1 = x_ref[part, :] + mod_ref[0, 2:3, :] * _rms_norm(y, g_ref[...])
        o_ref[part, :] = x1
        h2 = _rms_norm(x1, gffn_ref[...]) * (1.0 + mod_ref[0, 4:5, :]) + mod_ref[0, 3:4, :]
        h2_ref[part, :] = h2.astype(BF16)


def _merge(x, mods, seq, o_a, proj_qkvu, proj_gates, gates_row0, w_s, b_s_t, w_pa, w_pb, w_o, g, g_ffn, *,
           tm=512, row_parts=2):
    m = x.shape[0]
    assert (tm // row_parts) % CHUNK == 0 and gates_row0 % tm == 0
    g0 = gates_row0 // tm
    whole = lambda a: pl.BlockSpec(a.shape, lambda i: (0,) * a.ndim, pipeline_mode=pl.Buffered(1))
    return pl.pallas_call(
        functools.partial(_merge_kernel, row_parts=row_parts),
        grid=(m // tm,),
        in_specs=[
            pl.BlockSpec((tm, D_MODEL), lambda i: (i, 0)),
            pl.BlockSpec((1, N_MOD, D_MODEL), lambda i: (i * tm // seq, 0, 0)),
            pl.BlockSpec((tm, D_ATTN), lambda i: (i, 0)),
            pl.BlockSpec((tm, D_GMLP), lambda i: (i, 3)),
            pl.BlockSpec((tm, proj_gates.shape[1]), lambda i: (g0 + i, 0)),
            whole(w_s), whole(b_s_t), whole(w_pa), whole(w_pb), whole(w_o), whole(g), whole(g_ffn),
        ],
        out_specs=[pl.BlockSpec((tm, D_MODEL), lambda i: (i, 0))] * 2,
        out_shape=[jax.ShapeDtypeStruct((m, D_MODEL), F32), jax.ShapeDtypeStruct((m, D_MODEL), BF16)],
        scratch_shapes=[pltpu.VMEM((tm, D_GMLP), BF16)],
        compiler_params=_params(("parallel",), 60),
        name="merge",
    )(x, mods, o_a, proj_qkvu, proj_gates,
      w_s, b_s_t, w_pa, w_pb, w_o, g, g_ffn)


def _ffn_x_block(f, n_f, n_x):
    return jnp.clip(f - (n_f - n_x), 0, n_x - 1)


def _ffn_kernel(x_ref, h_ref, mod_ref, gpost_ref, wg_ref, wu_ref, wd_ref, o_ref, acc_ref, *, n_x, row_parts):
    f = pl.program_id(1)
    x_rows = x_ref.shape[0]
    tm = h_ref.shape[0]

    row0 = pl.multiple_of(_ffn_x_block(f, pl.num_programs(1), n_x) * x_rows, x_rows)
    o_ref[pl.ds(row0, x_rows), :] = x_ref[...]

    def reduce_step(first):
        for rows in _row_parts(tm, row_parts):
            h = h_ref[rows, :]
            gate = jnp.dot(h, pltpu.bitcast(wg_ref[...], BF16), preferred_element_type=F32)
            up = jnp.dot(h, pltpu.bitcast(wu_ref[...], BF16), preferred_element_type=F32)
            act = (jax.nn.silu(gate) * up).astype(BF16)
            down = jnp.dot(act, pltpu.bitcast(wd_ref[...], BF16), preferred_element_type=F32)
            acc_ref[rows, :] = down if first else acc_ref[rows, :] + down

    pl.when(f == 0)(functools.partial(reduce_step, True))
    pl.when(f > 0)(functools.partial(reduce_step, False))

    @pl.when(f == pl.num_programs(1) - 1)
    def _():
        scale = mod_ref[0, 5:6, :] * gpost_ref[...]

        def residual_rows(c, carry):
            rows = pl.ds(pl.multiple_of(c * SUBLANES, SUBLANES), SUBLANES)
            a = acc_ref[rows, :]
            r = lax.rsqrt(jnp.mean(a * a, axis=-1, keepdims=True) + EPS)
            o_ref[rows, :] = o_ref[rows, :] + (a * r) * scale
            return carry

        lax.fori_loop(0, tm // SUBLANES, residual_rows, 0, unroll=32)


def _ffn(x, h, mods, seq, g_post, w_gate, w_up, w_down, *, tm=1024, tf=512, x_rows=128, row_parts=1):
    m = x.shape[0]
    n_f = D_FF // tf
    n_x = tm // x_rows
    assert n_x <= n_f
    return pl.pallas_call(
        functools.partial(_ffn_kernel, n_x=n_x, row_parts=row_parts),
        grid=(m // tm, n_f),
        in_specs=[
            pl.BlockSpec((x_rows, D_MODEL), lambda i, f: (i * n_x + _ffn_x_block(f, n_f, n_x), 0)),
            pl.BlockSpec((tm, D_MODEL), lambda i, f: (i, 0)),
            pl.BlockSpec((1, N_MOD, D_MODEL), lambda i, f: (i * tm // seq, 0, 0)),
            pl.BlockSpec((1, D_MODEL), lambda i, f: (0, 0)),
            pl.BlockSpec((D_MODEL // 2, tf), lambda i, f: (0, f)),
            pl.BlockSpec((D_MODEL // 2, tf), lambda i, f: (0, f)),
            pl.BlockSpec((tf // 2, D_MODEL), lambda i, f: (f, 0)),
        ],
        out_specs=pl.BlockSpec((tm, D_MODEL), lambda i, f: (i, 0)),
        out_shape=jax.ShapeDtypeStruct((m, D_MODEL), F32),
        scratch_shapes=[pltpu.VMEM((tm, D_MODEL), F32)],
        compiler_params=_params(("parallel", "arbitrary"), 58),
        name="ffn",
    )(x, h, mods, g_post, w_gate, w_up, w_down)


def kernel(x_prompt, x_sample, cache_k, cache_v, c, c_ctx, w_ada, b_ada, norm_mix_pre, norm_mix_post,
           norm_ffn_pre, norm_ffn_post, w_in, rpb, ln_v, w_s, b_s, w_pa, w_pb, w_o, w_gate, w_up, w_down):
    assert w_ada.shape[0] == DEPTH == 1
    batch, seq, _ = x_prompt.shape
    dec_batch, dec_seq, _ = x_sample.shape
    past = cache_k.shape[2]

    row = lambda a: a[0].reshape(1, -1)
    bf = lambda a: a[0].astype(BF16)

    cvecs = jnp.concatenate(
        [c, c_ctx[None], jnp.zeros((MOD_ROWS - dec_batch - 1, D_MODEL), F32)], axis=0)
    mods = _modulation(cvecs, w_ada[0], b_ada[0])
    mods_lat = mods[:dec_batch].reshape(dec_batch, N_MOD, D_MODEL)
    mods_ctx = mods[dec_batch:dec_batch + 1].reshape(1, N_MOD, D_MODEL)
    table = _bias_table(rpb[0])

    w_in_f = w_in.reshape(D_MODEL, D_IN)
    mix_head = (bf(w_s), b_s[0].T)
    mix_tail = (row(norm_mix_post), row(norm_ffn_pre))

    xp = x_prompt.reshape(batch * seq, D_MODEL)
    xs = x_sample.reshape(dec_batch * dec_seq, D_MODEL)
    hp, proj_p, k_p, v_p, w_qkvu = _proj_qkvu(xp, mods_ctx, batch * seq, row(norm_mix_pre), w_in_f, is_ctx=True)
    hs, proj_s = _proj_qkvu(xs, mods_lat, dec_seq, row(norm_mix_pre), w_qkvu, is_ctx=False)
    gates = _proj_gates(hp, hs, row(ln_v), _sc_pack_bf16(w_in_f, mods, N_QKVU, D_IN - N_QKVU))

    mix_w = [_sc_pack_bf16(w[0], hp) for w in (w_pa, w_pb, w_o)]
    ffn = (row(norm_ffn_post), *[_sc_pack_bf16(w[0], hp) for w in (w_gate, w_up, w_down)])

    oa_p = _ctx_attention(proj_p, seq)
    xp, hp = _merge(xp, mods_ctx, batch * seq, oa_p, proj_p, gates, 0, *mix_head, *mix_w, *mix_tail)
    y_prompt = _ffn(xp, hp, mods_ctx, batch * seq, *ffn).reshape(batch, seq, D_MODEL)

    oa_s = _lat_attention(proj_s, cache_k.reshape(dec_batch, past * N_HEADS_A, HEAD_DIM),
                          cache_v.reshape(dec_batch, past * N_HEADS_A, HEAD_DIM), table, dec_seq)
    xs, hs = _merge(xs, mods_lat, dec_seq, oa_s, proj_s, gates, batch * seq, *mix_head, *mix_w, *mix_tail)
    y_sample = _ffn(xs, hs, mods_lat, dec_seq, *ffn).reshape(dec_batch, dec_seq, D_MODEL)

    state_shape = (batch, DEPTH, seq, N_HEADS_A, HEAD_DIM)
    return y_prompt, y_sample, k_p.reshape(state_shape), v_p.reshape(state_shape)
```

```python
import functools

import jax
import jax.numpy as jnp
from jax import lax
from jax.experimental import pallas as pl
from jax.experimental.pallas import tpu as pltpu
from jax.experimental.pallas import tpu_sc as plsc

D_MODEL = 2048
DEPTH = 1
GRID_W = 64
N_HEADS_A = 8
HEAD_DIM = 128
D_ATTN = N_HEADS_A * HEAD_DIM
KH_MAX = 8
KW = 16
CHUNK = 128
N_GROUPS_B = 8
D_GMLP = 1024
GROUP_CH = D_GMLP // N_GROUPS_B
D_FF = ((8 * D_MODEL // 3 + 255) // 256) * 256
N_MOD = 6
EPS = 1e-6
ATTN_SCALE = HEAD_DIM ** -0.5
LOG2E = 1.4426950408889634
D_IN = 3 * D_ATTN + 2 * D_GMLP + 2 * D_MODEL

N_DR = 2 * KH_MAX - 1
N_DC = 2 * KW - 1
COL_BLOCK = 1024
SUBLANES = 8
MOD_ROWS = SUBLANES

F32 = jnp.float32
BF16 = jnp.bfloat16

MIB = 1024 * 1024


def _params(semantics, vmem_mib):
    return pltpu.CompilerParams(dimension_semantics=semantics, vmem_limit_bytes=vmem_mib * MIB)


def _rms_norm(x, g):
    return x * lax.rsqrt(jnp.mean(x * x, axis=-1, keepdims=True) + EPS) * g


def _layer_norm(x, g):
    xc = x - jnp.mean(x, axis=-1, keepdims=True)
    return xc * lax.rsqrt(jnp.mean(xc * xc, axis=-1, keepdims=True) + EPS) * g


def _modulation_kernel(c_ref, w_ref, b_ref, o_ref):
    s = jax.nn.silu(c_ref[...]).astype(BF16)
    o_ref[...] = jnp.dot(s, w_ref[...].astype(BF16), preferred_element_type=F32) + b_ref[...]


def _modulation(cvecs, w_ada, b_ada):
    tn = 1024
    n = N_MOD * D_MODEL
    return pl.pallas_call(
        _modulation_kernel,
        grid=(n // tn,),
        in_specs=[
            pl.BlockSpec((MOD_ROWS, D_MODEL), lambda j: (0, 0)),
            pl.BlockSpec((D_MODEL, tn), lambda j: (0, j)),
            pl.BlockSpec((1, tn), lambda j: (0, j)),
        ],
        out_specs=pl.BlockSpec((MOD_ROWS, tn), lambda j: (0, j)),
        out_shape=jax.ShapeDtypeStruct((MOD_ROWS, n), F32),
        compiler_params=_params(("parallel",), 56),
        name="modulation",
    )(cvecs, w_ada, b_ada.reshape(1, n))


def _bias_table_kernel(rpb_ref, o_ref):
    qc = lax.broadcasted_iota(jnp.int32, (GRID_W, GRID_W), 0)
    kc = lax.broadcasted_iota(jnp.int32, (GRID_W, GRID_W), 1)
    cs = jnp.clip(qc - KW // 2, 0, GRID_W - KW)
    valid = (kc >= cs) & (kc < cs + KW)
    lanes = rpb_ref.shape[-1]
    tiles = []
    for dr in range(N_DR):
        row = jnp.broadcast_to(rpb_ref[0, dr:dr + 1, :], (GRID_W, lanes))
        t = pltpu.roll(row, lanes - (KW - 1), 1, stride=1, stride_axis=0)[:, :GRID_W]
        tiles.append(jnp.where(valid, t * LOG2E, -jnp.inf))
    pad = jnp.zeros((GRID_W, GRID_W), F32)
    o_ref[0, 0] = jnp.concatenate(tiles + [pad], axis=-1)
    o_ref[0, 1] = jnp.concatenate([pad] + tiles, axis=-1)


def _bias_table(rpb):
    width = (N_DR + 1) * GRID_W
    lanes = 128
    rpb_rows = jnp.pad(rpb, ((0, 0), (0, 0), (0, lanes - N_DC)))
    return pl.pallas_call(
        _bias_table_kernel,
        grid=(N_HEADS_A,),
        in_specs=[pl.BlockSpec((1, N_DR, lanes), lambda h: (h, 0, 0))],
        out_specs=pl.BlockSpec((1, 2, GRID_W, width), lambda h: (h, 0, 0, 0)),
        out_shape=jax.ShapeDtypeStruct((N_HEADS_A, 2, GRID_W, width), F32),
        compiler_params=_params(("parallel",), 56),
        name="bias_table",
    )(rpb_rows)


SC_LANES = 16
SC_BLOCK = (32, 512)
SC_UNROLL = 8


def _bf16_bits(u):
    return lax.shift_right_logical(u + 0x7FFF + (lax.shift_right_logical(u, 16) & 1), 16)


def _sc_pack_bf16(w, after, col0=0, cols=None):
    rows = w.shape[0]
    cols = w.shape[1] if cols is None else cols
    blk_r, blk_c = SC_BLOCK
    assert rows % blk_r == 0 and cols % blk_c == 0 and col0 % blk_c == 0 and blk_c % (SC_UNROLL * SC_LANES) == 0
    j0 = col0 // blk_c
    mesh = plsc.VectorSubcoreMesh(core_axis_name="core", subcore_axis_name="subcore")

    def body(in_vmem, out_vmem):
        in_vmem = in_vmem.bitcast(jnp.int32)

        @pl.loop(0, blk_r // 2)
        def _(r):
            @pl.loop(0, blk_c, step=SC_UNROLL * SC_LANES)
            def _(c0):
                for k in range(SC_UNROLL):
                    lanes = pl.ds(c0 + k * SC_LANES, SC_LANES)
                    lo = in_vmem[2 * r, lanes]
                    hi = in_vmem[2 * r + 1, lanes]
                    out_vmem[r, lanes] = _bf16_bits(lo) | lax.shift_left(_bf16_bits(hi), 16)

    @pl.kernel(out_type=jax.ShapeDtypeStruct((rows // 2, cols), jnp.int32), mesh=mesh, scratch_types=[],
               compiler_params=pltpu.CompilerParams(use_tc_tiling_on_sc=True))
    def pack_kernel(w_hbm, after_hbm, o_hbm):
        del after_hbm
        pltpu.emit_pipeline(
            body,
            grid=(rows // blk_r, cols // blk_c),
            in_specs=[pl.BlockSpec((blk_r, blk_c), lambda i, j: (i, j0 + j))],
            out_specs=[pl.BlockSpec((blk_r // 2, blk_c), lambda i, j: (i, j))],
            core_axis_name=("core", "subcore"),
            dimension_semantics=(pltpu.PARALLEL, pltpu.PARALLEL),
        )(w_hbm, o_hbm)

    return pack_kernel(w, after)


LOAD_ROWS = 256
N_QKVU = 3 * D_ATTN + D_GMLP


def _row_parts(tm, parts):
    return [slice(p * (tm // parts), (p + 1) * (tm // parts)) for p in range(parts)]


def _load_cast_weights(w_hbm, col0, w_res, stage, sems):
    rows, width = w_res.shape
    chunk = stage.shape[1]
    n_chunks = rows // chunk

    def copy(c):
        src = w_hbm.at[pl.ds(c * chunk, chunk), pl.ds(col0, width)]
        return pltpu.make_async_copy(src, stage.at[c % 2], sems.at[c % 2])

    copy(0).start()
    for c in range(n_chunks):
        if c + 1 < n_chunks:
            copy(c + 1).start()
        copy(c).wait()
        w_res[c * chunk:(c + 1) * chunk, :] = stage[c % 2].astype(BF16)


def _proj_qkvu_kernel(x_ref, mod_ref, g_ref, w_ref, h_ref, proj_ref, *rest, is_ctx):
    if is_ctx:
        k_ref, v_ref, wpub_ref, w_res, stage, sems = rest
        step = pl.program_id(0)
        publish = pltpu.make_async_copy(w_res, wpub_ref, sems.at[2])

        @pl.when(step == 0)
        def _():
            _load_cast_weights(w_ref, 0, w_res, stage, sems)
            publish.start()
    else:
        w_res = w_ref
    y = _rms_norm(x_ref[...], g_ref[...])
    h = (y * (1.0 + mod_ref[0, 1:2, :]) + mod_ref[0, 0:1, :]).astype(BF16)
    h_ref[...] = h
    for blk in (3, 0, 1, 2):
        cols = slice(blk * COL_BLOCK, (blk + 1) * COL_BLOCK)
        acc = jnp.dot(h, w_res[:, cols], preferred_element_type=F32)
        if is_ctx and blk in (1, 2):
            kv_ref = (k_ref, v_ref)[blk - 1]
            for head in range(N_HEADS_A):
                dst = pl.ds(head, x_ref.shape[0], stride=N_HEADS_A)
                kv_ref[dst, :] = acc[:, head * HEAD_DIM:(head + 1) * HEAD_DIM]
        if blk == 3:
            acc = jax.nn.gelu(acc)
        proj_ref[:, cols] = acc.astype(BF16)
    if is_ctx:
        @pl.when(step == pl.num_programs(0) - 1)
        def _():
            publish.wait()


def _proj_qkvu(x, mods, seq, g, w, *, is_ctx, tm=512):
    m = x.shape[0]
    out_shape = [jax.ShapeDtypeStruct((m, D_MODEL), BF16), jax.ShapeDtypeStruct((m, N_QKVU), BF16)]
    out_specs = [pl.BlockSpec((tm, D_MODEL), lambda i: (i, 0)), pl.BlockSpec((tm, N_QKVU), lambda i: (i, 0))]
    scratch = []
    if is_ctx:
        out_shape += [jax.ShapeDtypeStruct((m * N_HEADS_A, HEAD_DIM), F32)] * 2
        out_specs += [pl.BlockSpec((tm * N_HEADS_A, HEAD_DIM), lambda i: (i, 0))] * 2
        out_shape += [jax.ShapeDtypeStruct((D_MODEL, N_QKVU), BF16)]
        out_specs += [pl.BlockSpec(memory_space=pl.ANY)]
        w_spec = pl.BlockSpec(memory_space=pl.ANY)
        scratch = [pltpu.VMEM((D_MODEL, N_QKVU), BF16), pltpu.VMEM((2, LOAD_ROWS, N_QKVU), F32),
                   pltpu.SemaphoreType.DMA((3,))]
    else:
        w_spec = pl.BlockSpec((D_MODEL, N_QKVU), lambda i: (0, 0), pipeline_mode=pl.Buffered(1))
    return pl.pallas_call(
        functools.partial(_proj_qkvu_kernel, is_ctx=is_ctx),
        grid=(m // tm,),
        in_specs=[
            pl.BlockSpec((tm, D_MODEL), lambda i: (i, 0)),
            pl.BlockSpec((1, N_MOD, D_MODEL), lambda i: (i * tm // seq, 0, 0)),
            pl.BlockSpec((1, D_MODEL), lambda i: (0, 0)),
            w_spec,
        ],
        out_specs=out_specs,
        out_shape=out_shape,
        scratch_shapes=scratch,
        compiler_params=_params(("arbitrary",), 58),
        name="proj_qkvu_ctx" if is_ctx else "proj_qkvu_lat",
    )(x, mods, g, w)


def _proj_gates_kernel(hp_ref, hs_ref, lnv_ref, w_ref, proj_ref, *, n_ctx, row_parts):
    step = pl.program_id(0)
    w_res = pltpu.bitcast(w_ref[...], BF16)
    n_gate = w_res.shape[1] - D_GMLP
    for rows in _row_parts(hp_ref.shape[0], row_parts):
        h = jnp.where(step < n_ctx, hp_ref[rows, :], hs_ref[rows, :])
        acc = jnp.dot(h, w_res[:, :D_GMLP], preferred_element_type=F32)
        proj_ref[rows, n_gate:] = _layer_norm(jax.nn.gelu(acc), lnv_ref[...]).astype(BF16)
        for blk in range(n_gate // COL_BLOCK):
            cols = slice(blk * COL_BLOCK, (blk + 1) * COL_BLOCK)
            acc = jnp.dot(h, w_res[:, D_GMLP + cols.start:D_GMLP + cols.stop], preferred_element_type=F32)
            proj_ref[rows, cols] = jax.nn.sigmoid(acc).astype(BF16)


def _proj_gates(h_ctx, h_lat, ln_v, w, *, tm=512, row_parts=2):
    n_ctx, n_lat = h_ctx.shape[0] // tm, h_lat.shape[0] // tm
    n = w.shape[1]
    return pl.pallas_call(
        functools.partial(_proj_gates_kernel, n_ctx=n_ctx, row_parts=row_parts),
        grid=(n_ctx + n_lat,),
        in_specs=[
            pl.BlockSpec((tm, D_MODEL), lambda i: (jnp.minimum(i, n_ctx - 1), 0)),
            pl.BlockSpec((tm, D_MODEL), lambda i: (jnp.maximum(i - n_ctx, 0), 0)),
            pl.BlockSpec((1, D_GMLP), lambda i: (0, 0)),
            pl.BlockSpec(w.shape, lambda i: (0, 0), pipeline_mode=pl.Buffered(1)),
        ],
        out_specs=pl.BlockSpec((tm, n), lambda i: (i, 0)),
        out_shape=jax.ShapeDtypeStruct(((n_ctx + n_lat) * tm, n), BF16),
        compiler_params=_params(("parallel",), 56),
        name="proj_gates",
    )(h_ctx, h_lat, ln_v, w)


def _softmax_parts(logits2):
    m = functools.reduce(jnp.maximum, [jnp.max(t, axis=-1, keepdims=True) for t in logits2])
    es = [jnp.exp2(t - m) for t in logits2]
    inv = 1.0 / functools.reduce(jnp.add, [jnp.sum(e, axis=-1, keepdims=True) for e in es])
    return [e.astype(BF16) for e in es], inv


def _qk(q, k):
    return lax.dot_general(q, k, (((1,), (1,)), ((), ())), preferred_element_type=F32) * (ATTN_SCALE * LOG2E)


def _ctx_attn_kernel(q_ref, k_ref, v_ref, o_ref):
    for h in range(N_HEADS_A):
        cols = slice(h * HEAD_DIM, (h + 1) * HEAD_DIM)
        (p,), inv = _softmax_parts([_qk(q_ref[:, cols], k_ref[:, cols])])
        o_ref[:, cols] = (jnp.dot(p, v_ref[:, cols], preferred_element_type=F32) * inv).astype(BF16)


def _ctx_attention(proj, seq):
    m = proj.shape[0]
    spec = lambda col: pl.BlockSpec((seq, D_ATTN), lambda b: (b, col))
    return pl.pallas_call(
        _ctx_attn_kernel,
        grid=(m // seq,),
        in_specs=[spec(0), spec(1), spec(2)],
        out_specs=spec(0),
        out_shape=jax.ShapeDtypeStruct((m, D_ATTN), BF16),
        compiler_params=_params(("parallel",), 56),
        name="ctx_attention",
    )(proj, proj, proj)


def _lat_attn_kernel(q_ref, k_ref, v_ref, kc_ref, vc_ref, tab_ref, o_ref, sw_ref, pw_ref, *, rows, kh):
    heads = tab_ref.shape[0]
    past = kc_ref.shape[1] // N_HEADS_A
    win = kh * GRID_W
    starts = [min(max(r - kh // 2, 0), rows - kh) * GRID_W for r in range(rows)]
    for j in range(heads):
        head = pl.program_id(1) * heads + j
        cols = slice(j * HEAD_DIM, (j + 1) * HEAD_DIM)
        sw, pw = sw_ref.at[j % 2], pw_ref.at[j % 2]
        kc = kc_ref[0, pl.ds(head, past, stride=N_HEADS_A), :].astype(BF16)
        vc = vc_ref[0, pl.ds(head, past, stride=N_HEADS_A), :].astype(BF16)
        s_c = _qk(q_ref[:, cols], kc)
        for r, start in enumerate(starts):
            off = start // GRID_W - r + (KH_MAX - 1)
            lane0 = (off + off % 2) * GRID_W
            bias = tab_ref[j, off % 2, :, lane0:lane0 + win]
            q = q_ref[r * GRID_W:(r + 1) * GRID_W, cols]
            sw[r * GRID_W:(r + 1) * GRID_W, :] = _qk(q, k_ref[start:start + win, cols]) + bias
        (p_w, p_c), inv = _softmax_parts([sw[...], s_c])
        pw[...] = p_w
        o_c = jnp.dot(p_c, vc, preferred_element_type=F32)
        for r, start in enumerate(starts):
            q_rows = slice(r * GRID_W, (r + 1) * GRID_W)
            o_w = jnp.dot(pw[q_rows, :], v_ref[start:start + win, cols], preferred_element_type=F32)
            o_ref[q_rows, cols] = ((o_w + o_c[q_rows, :]) * inv[q_rows, :]).astype(BF16)


def _lat_attention(proj, cache_k, cache_v, table, seq, *, heads_per_step=4):
    m = proj.shape[0]
    rows = seq // GRID_W
    kh = min(KH_MAX, rows)
    n_hp = N_HEADS_A // heads_per_step
    width = heads_per_step * HEAD_DIM
    qkv = lambda part: pl.BlockSpec((seq, width), lambda b, hp: (b, part * n_hp + hp))
    cache = pl.BlockSpec((1,) + cache_k.shape[1:], lambda b, hp: (b, 0, 0))
    return pl.pallas_call(
        functools.partial(_lat_attn_kernel, rows=rows, kh=kh),
        grid=(m // seq, n_hp),
        in_specs=[qkv(0), qkv(1), qkv(2), cache, cache,
                  pl.BlockSpec((heads_per_step,) + table.shape[1:], lambda b, hp: (hp, 0, 0, 0))],
        out_specs=pl.BlockSpec((seq, width), lambda b, hp: (b, hp)),
        out_shape=jax.ShapeDtypeStruct((m, D_ATTN), BF16),
        scratch_shapes=[pltpu.VMEM((2, seq, kh * GRID_W), F32), pltpu.VMEM((2, seq, kh * GRID_W), BF16)],
        compiler_params=_params(("parallel", "arbitrary"), 48),
        name="lat_attention",
    )(proj, proj, proj, cache_k, cache_v, table)


def _merge_kernel(x_ref, mod_ref, oa_ref, gu_ref, gates_ref,
                  ws_ref, bst_ref, wpa_ref, wpb_ref, wo_ref, g_ref, gffn_ref, o_ref, h2_ref,
                  ob_ref, *, row_parts):
    w_pa, w_pb, w_o = (pltpu.bitcast(w[...], BF16) for w in (wpa_ref, wpb_ref, wo_ref))
    ga_ref = gates_ref.at[:, pl.ds(0, D_MODEL)]
    gb_ref = gates_ref.at[:, pl.ds(D_MODEL, D_MODEL)]
    vn_ref = gates_ref.at[:, pl.ds(2 * D_MODEL, D_GMLP)]
    for part in _row_parts(x_ref.shape[0], row_parts):
        for c in range(part.start // CHUNK, part.stop // CHUNK):
            rows = slice(c * CHUNK, (c + 1) * CHUNK)
            for g in range(N_GROUPS_B):
                cols = slice(g * GROUP_CH, (g + 1) * GROUP_CH)
                s = jnp.dot(ws_ref[g], vn_ref[rows, cols], preferred_element_type=F32) + bst_ref[:, g:g + 1]
                ob_ref[rows, cols] = (gu_ref[rows, cols].astype(F32) * s).astype(BF16)
        pa = jnp.dot(oa_ref[part, :], w_pa, preferred_element_type=F32)
        pb = jnp.dot(ob_ref[part, :], w_pb, preferred_element_type=F32)
        mixed = (ga_ref[part, :].astype(F32) * pa + gb_ref[part, :].astype(F32) * pb).astype(BF16)
        y = jnp.dot(mixed, w_o, preferred_element_type=F32)
        x1 = x_ref[part, :] + mod_ref[0, 2:3, :] * _rms_norm(y, g_ref[...])
        o_ref[part, :] = x1
        h2 = _rms_norm(x1, gffn_ref[...]) * (1.0 + mod_ref[0, 4:5, :]) + mod_ref[0, 3:4, :]
        h2_ref[part, :] = h2.astype(BF16)


def _merge(x, mods, seq, o_a, proj_qkvu, proj_gates, gates_row0, w_s, b_s_t, w_pa, w_pb, w_o, g, g_ffn, *,
           tm=512, row_parts=2):
    m = x.shape[0]
    assert (tm // row_parts) % CHUNK == 0 and gates_row0 % tm == 0
    g0 = gates_row0 // tm
    whole = lambda a: pl.BlockSpec(a.shape, lambda i: (0,) * a.ndim, pipeline_mode=pl.Buffered(1))
    return pl.pallas_call(
        functools.partial(_merge_kernel, row_parts=row_parts),
        grid=(m // tm,),
        in_specs=[
            pl.BlockSpec((tm, D_MODEL), lambda i: (i, 0)),
            pl.BlockSpec((1, N_MOD, D_MODEL), lambda i: (i * tm // seq, 0, 0)),
            pl.BlockSpec((tm, D_ATTN), lambda i: (i, 0)),
            pl.BlockSpec((tm, D_GMLP), lambda i: (i, 3)),
            pl.BlockSpec((tm, proj_gates.shape[1]), lambda i: (g0 + i, 0)),
            whole(w_s), whole(b_s_t), whole(w_pa), whole(w_pb), whole(w_o), whole(g), whole(g_ffn),
        ],
        out_specs=[pl.BlockSpec((tm, D_MODEL), lambda i: (i, 0))] * 2,
        out_shape=[jax.ShapeDtypeStruct((m, D_MODEL), F32), jax.ShapeDtypeStruct((m, D_MODEL), BF16)],
        scratch_shapes=[pltpu.VMEM((tm, D_GMLP), BF16)],
        compiler_params=_params(("parallel",), 60),
        name="merge",
    )(x, mods, o_a, proj_qkvu, proj_gates,
      w_s, b_s_t, w_pa, w_pb, w_o, g, g_ffn)


def _ffn_x_block(f, n_f, n_x):
    return jnp.clip(f - (n_f - n_x), 0, n_x - 1)


def _ffn_kernel(x_ref, h_ref, mod_ref, gpost_ref, wg_ref, wu_ref, wd_ref, o_ref, acc_ref, *, n_x, row_parts):
    f = pl.program_id(1)
    x_rows = x_ref.shape[0]
    tm = h_ref.shape[0]

    row0 = pl.multiple_of(_ffn_x_block(f, pl.num_programs(1), n_x) * x_rows, x_rows)
    o_ref[pl.ds(row0, x_rows), :] = x_ref[...]

    def reduce_step(first):
        for rows in _row_parts(tm, row_parts):
            h = h_ref[rows, :]
            gate = jnp.dot(h, pltpu.bitcast(wg_ref[...], BF16), preferred_element_type=F32)
            up = jnp.dot(h, pltpu.bitcast(wu_ref[...], BF16), preferred_element_type=F32)
            act = (jax.nn.silu(gate) * up).astype(BF16)
            down = jnp.dot(act, pltpu.bitcast(wd_ref[...], BF16), preferred_element_type=F32)
            acc_ref[rows, :] = down if first else acc_ref[rows, :] + down

    pl.when(f == 0)(functools.partial(reduce_step, True))
    pl.when(f > 0)(functools.partial(reduce_step, False))

    @pl.when(f == pl.num_programs(1) - 1)
    def _():
        scale = mod_ref[0, 5:6, :] * gpost_ref[...]

        def residual_rows(c, carry):
            rows = pl.ds(pl.multiple_of(c * SUBLANES, SUBLANES), SUBLANES)
            a = acc_ref[rows, :]
            r = lax.rsqrt(jnp.mean(a * a, axis=-1, keepdims=True) + EPS)
            o_ref[rows, :] = o_ref[rows, :] + (a * r) * scale
            return carry

        lax.fori_loop(0, tm // SUBLANES, residual_rows, 0, unroll=32)


def _ffn(x, h, mods, seq, g_post, w_gate, w_up, w_down, *, tm=1024, tf=512, x_rows=128, row_parts=1):
    m = x.shape[0]
    n_f = D_FF // tf
    n_x = tm // x_rows
    assert n_x <= n_f
    return pl.pallas_call(
        functools.partial(_ffn_kernel, n_x=n_x, row_parts=row_parts),
        grid=(m // tm, n_f),
        in_specs=[
            pl.BlockSpec((x_rows, D_MODEL), lambda i, f: (i * n_x + _ffn_x_block(f, n_f, n_x), 0)),
            pl.BlockSpec((tm, D_MODEL), lambda i, f: (i, 0)),
            pl.BlockSpec((1, N_MOD, D_MODEL), lambda i, f: (i * tm // seq, 0, 0)),
            pl.BlockSpec((1, D_MODEL), lambda i, f: (0, 0)),
            pl.BlockSpec((D_MODEL // 2, tf), lambda i, f: (0, f)),
            pl.BlockSpec((D_MODEL // 2, tf), lambda i, f: (0, f)),
            pl.BlockSpec((tf // 2, D_MODEL), lambda i, f: (f, 0)),
        ],
        out_specs=pl.BlockSpec((tm, D_MODEL), lambda i, f: (i, 0)),
        out_shape=jax.ShapeDtypeStruct((m, D_MODEL), F32),
        scratch_shapes=[pltpu.VMEM((tm, D_MODEL), F32)],
        compiler_params=_params(("parallel", "arbitrary"), 58),
        name="ffn",
    )(x, h, mods, g_post, w_gate, w_up, w_down)


def kernel(x_prompt, x_sample, cache_k, cache_v, c, c_ctx, w_ada, b_ada, norm_mix_pre, norm_mix_post,
           norm_ffn_pre, norm_ffn_post, w_in, rpb, ln_v, w_s, b_s, w_pa, w_pb, w_o, w_gate, w_up, w_down):
    assert w_ada.shape[0] == DEPTH == 1
    batch, seq, _ = x_prompt.shape
    dec_batch, dec_seq, _ = x_sample.shape
    past = cache_k.shape[2]

    row = lambda a: a[0].reshape(1, -1)
    bf = lambda a: a[0].astype(BF16)

    cvecs = jnp.concatenate(
        [c, c_ctx[None], jnp.zeros((MOD_ROWS - dec_batch - 1, D_MODEL), F32)], axis=0)
    mods = _modulation(cvecs, w_ada[0], b_ada[0])
    mods_lat = mods[:dec_batch].reshape(dec_batch, N_MOD, D_MODEL)
    mods_ctx = mods[dec_batch:dec_batch + 1].reshape(1, N_MOD, D_MODEL)
    table = _bias_table(rpb[0])

    w_in_f = w_in.reshape(D_MODEL, D_IN)
    mix_head = (bf(w_s), b_s[0].T)
    mix_tail = (row(norm_mix_post), row(norm_ffn_pre))

    xp = x_prompt.reshape(batch * seq, D_MODEL)
    xs = x_sample.reshape(dec_batch * dec_seq, D_MODEL)
    hp, proj_p, k_p, v_p, w_qkvu = _proj_qkvu(xp, mods_ctx, batch * seq, row(norm_mix_pre), w_in_f, is_ctx=True)
    hs, proj_s = _proj_qkvu(xs, mods_lat, dec_seq, row(norm_mix_pre), w_qkvu, is_ctx=False)
    gates = _proj_gates(hp, hs, row(ln_v), _sc_pack_bf16(w_in_f, mods, N_QKVU, D_IN - N_QKVU))

    mix_w = [_sc_pack_bf16(w[0], hp) for w in (w_pa, w_pb, w_o)]
    ffn = (row(norm_ffn_post), *[_sc_pack_bf16(w[0], hp) for w in (w_gate, w_up, w_down)])

    oa_p = _ctx_attention(proj_p, seq)
    xp, hp = _merge(xp, mods_ctx, batch * seq, oa_p, proj_p, gates, 0, *mix_head, *mix_w, *mix_tail)
    y_prompt = _ffn(xp, hp, mods_ctx, batch * seq, *ffn).reshape(batch, seq, D_MODEL)

    oa_s = _lat_attention(proj_s, cache_k.reshape(dec_batch, past * N_HEADS_A, HEAD_DIM),
                          cache_v.reshape(dec_batch, past * N_HEADS_A, HEAD_DIM), table, dec_seq)
    xs, hs = _merge(xs, mods_lat, dec_seq, oa_s, proj_s, gates, batch * seq, *mix_head, *mix_w, *mix_tail)
    y_sample = _ffn(xs, hs, mods_lat, dec_seq, *ffn).reshape(dec_batch, dec_seq, D_MODEL)

    state_shape = (batch, DEPTH, seq, N_HEADS_A, HEAD_DIM)
    return y_prompt, y_sample, k_p.reshape(state_shape), v_p.reshape(state_shape)
```
